```python
import math
import jax, jax.numpy as jnp
from jax import lax
import numpy as np

D_MODEL = 2048
BATCH = 1
SEQ = 8192
DEPTH = 2

HEAD_DIM = 128
A_HEADS = 4
B_HEADS = 6
C_HEADS = 6
C_HALF = HEAD_DIM // 2
IDX_HEADS = 16
IDX_DIM = 64
TOPK_MAX = 256
DILATED_PAIRS = ((128, 1), (512, 4), (2048, 16))
N_BUCKETS = 32
MAX_DISTANCE = 2048
Q_BLOCK = 128
D_FF = 5632
N_EXPERTS = 8
TOP_K_EXPERTS = 2
D_FF_EXPERT = 7168
N_DENSE = (DEPTH + 1) // 2
N_MOE = DEPTH // 2
RMS_EPS = 1e-6
SPLIT_SIZES = (A_HEADS * HEAD_DIM, A_HEADS * HEAD_DIM, A_HEADS * HEAD_DIM,
               IDX_HEADS * IDX_DIM, IDX_DIM, IDX_HEADS,
               B_HEADS * HEAD_DIM, B_HEADS * HEAD_DIM, B_HEADS * HEAD_DIM,
               C_HEADS * HEAD_DIM, C_HEADS * HEAD_DIM, C_HEADS * HEAD_DIM)
N_IN = sum(SPLIT_SIZES)

kernel_name = 'hybrid_dsa_dilated_diff_moe_trunk'


def rms_norm(x, g):
    xf = x.astype(jnp.float32)
    y = xf * lax.rsqrt(jnp.mean(xf * xf, axis=-1, keepdims=True) + RMS_EPS)
    return (y * g.astype(jnp.float32)).astype(x.dtype)


def t5_bucket(dist):
    n = jnp.maximum(dist, 0)
    max_exact = N_BUCKETS // 2
    nf = jnp.maximum(n, 1).astype(jnp.float32)
    large = max_exact + (jnp.log(nf / max_exact) / math.log(MAX_DISTANCE / max_exact)
                         * (N_BUCKETS - max_exact)).astype(jnp.int32)
    return jnp.where(n < max_exact, n, jnp.minimum(large, N_BUCKETS - 1))


def dsa_attention(q, k, v, q_idx, k_idx, w_idx, bias_tab):
    B, S, H, Dh = q.shape
    topk = min(TOPK_MAX, S // 4)
    nblk = S // Q_BLOCK
    pos = jnp.arange(S, dtype=jnp.int32)
    gather = jax.vmap(lambda t, i: t[i])

    def blocks(t):
        return t.reshape(B, nblk, Q_BLOCK, *t.shape[2:]).swapaxes(0, 1)

    def one_block(args):
        qb, qib, wib, t = args
        rel = jax.nn.relu(jnp.einsum('bqhd,bkd->bqhk', qib, k_idx).astype(jnp.float32) * IDX_DIM ** -0.5)
        score = jnp.einsum('bqh,bqhk->bqk', wib.astype(jnp.float32) * IDX_HEADS ** -0.5, rel)
        score = jnp.where(pos[None, None, :] <= t[None, :, None], score, -jnp.inf)
        _, idx = lax.top_k(score, topk)
        valid = idx <= t[None, :, None]
        ks = gather(k, idx)
        vs = gather(v, idx)
        logits = jnp.einsum('bqhd,bqkhd->bhqk', qb, ks).astype(jnp.float32) * Dh ** -0.5
        logits = logits + bias_tab[t5_bucket(t[None, :, None] - idx)].transpose(0, 3, 1, 2)
        logits = jnp.where(valid[:, None], logits, -jnp.inf)
        p = jax.nn.softmax(logits, axis=-1).astype(v.dtype)
        return jnp.einsum('bhqk,bqkhd->bqhd', p, vs)

    out = lax.map(one_block, (blocks(q), blocks(q_idx), blocks(w_idx), pos.reshape(nblk, Q_BLOCK)))
    return out.swapaxes(0, 1).reshape(B, S, H, Dh)


def dilated_branch(q, k, v, bias_tab, window, dil):
    B, S, H, Dh = q.shape
    steps = window // dil
    span = Q_BLOCK * dil
    Sp = -(-S // span) * span
    M = Sp // dil
    nb = M // Q_BLOCK

    def stride_view(t):
        t = jnp.pad(t, ((0, 0), (0, Sp - S), (0, 0), (0, 0)))
        return t.reshape(B, M, dil, H, Dh).transpose(0, 2, 1, 3, 4).reshape(B, dil, nb, Q_BLOCK, H, Dh)

    def with_prev(t):
        prev = jnp.pad(t, ((0, 0), (0, 0), (1, 0), (0, 0), (0, 0), (0, 0)))[:, :, :-1]
        return jnp.concatenate([prev, t], axis=3)

    qs = stride_view(q)
    kk = with_prev(stride_view(k))
    vv = with_prev(stride_view(v))
    logits = jnp.einsum('brnqhd,brnkhd->brnhqk', qs, kk).astype(jnp.float32) * Dh ** -0.5
    qi = jnp.arange(Q_BLOCK, dtype=jnp.int32)[:, None]
    ki = jnp.arange(2 * Q_BLOCK, dtype=jnp.int32)[None, :] - Q_BLOCK
    delta = qi - ki
    blk = jnp.arange(nb, dtype=jnp.int32)
    valid = ((delta >= 0) & (delta <= steps))[None] & ((ki >= 0)[None] | (blk[:, None, None] > 0))
    logits = logits + bias_tab[t5_bucket(delta * dil)].transpose(2, 0, 1)
    logits = jnp.where(valid[None, None, :, None], logits, -jnp.inf)
    lse = jax.nn.logsumexp(logits, axis=-1)
    p = jnp.exp(logits - lse[..., None]).astype(v.dtype)
    o = jnp.einsum('brnhqk,brnkhd->brnqhd', p, vv)
    o = o.reshape(B, dil, M, H, Dh).transpose(0, 2, 1, 3, 4).reshape(B, Sp, H, Dh)[:, :S]
    lse = lse.transpose(0, 1, 2, 4, 3).reshape(B, dil, M, H).transpose(0, 2, 1, 3).reshape(B, Sp, H)[:, :S]
    return o, lse


def dilated_attention(q, k, v, bias_tab):
    outs, lses = [], []
    for window, dil in DILATED_PAIRS:
        o, l = dilated_branch(q, k, v, bias_tab, window, dil)
        outs.append(o.astype(jnp.float32))
        lses.append(l)
    alpha = jax.nn.softmax(jnp.stack(lses, axis=0), axis=0)
    out = jnp.einsum('gbsh,gbshd->bshd', alpha, jnp.stack(outs, axis=0))
    return out.astype(v.dtype)


def diff_attention(q, k, v, bias_tab, lam):
    B, S, H = q.shape[:3]
    nblk = S // Q_BLOCK
    pos = jnp.arange(S, dtype=jnp.int32)

    def one_block(args):
        qb, t = args
        logits = jnp.einsum('bqhcd,bkhcd->bchqk', qb, k).astype(jnp.float32) * C_HALF ** -0.5
        dist = t[:, None] - pos[None, :]
        logits = logits + bias_tab[t5_bucket(dist)].transpose(2, 0, 1)
        logits = jnp.where(dist >= 0, logits, -jnp.inf)
        p = jax.nn.softmax(logits, axis=-1)
        a = (p[:, 0] - lam * p[:, 1]).astype(v.dtype)
        return jnp.einsum('bhqk,bkhd->bqhd', a, v)

    qblocks = q.reshape(B, nblk, Q_BLOCK, H, 2, C_HALF).swapaxes(0, 1)
    out = lax.map(one_block, (qblocks, pos.reshape(nblk, Q_BLOCK)))
    return out.swapaxes(0, 1).reshape(B, S, H, HEAD_DIM)


def swiglu(h, wg, wu, wd):
    a = jax.nn.silu(jnp.einsum('bsd,df->bsf', h, wg)) * jnp.einsum('bsd,df->bsf', h, wu)
    return jnp.einsum('bsf,fd->bsd', a, wd)


def moe_swiglu(h, w_router, wg, wu, wd):
    logits = jnp.einsum('bsd,de->bse', h, w_router).astype(jnp.float32)
    top_vals, top_idx = lax.top_k(logits, TOP_K_EXPERTS)
    gates = jax.nn.softmax(top_vals, axis=-1)
    combine = jnp.sum(jax.nn.one_hot(top_idx, N_EXPERTS, dtype=jnp.float32) * gates[..., None], axis=-2)
    out = jnp.zeros_like(h)
    for e in range(N_EXPERTS):
        out = out + combine[..., e:e + 1].astype(h.dtype) * swiglu(h, wg[e], wu[e], wd[e])
    return out


def setup_inputs(seed: int = 0) -> dict:
    key = jax.random.key(seed)
    ks = jax.random.split(key, 24)
    f32 = jnp.float32

    def nrm(k, shape, scale):
        return jax.random.normal(k, shape, f32) * scale

    def gain(k, shape):
        return 1.0 + 0.02 * jax.random.normal(k, shape, f32)

    return {
        'x': nrm(ks[0], (BATCH, SEQ, D_MODEL), 1.0),
        'w_in': nrm(ks[1], (DEPTH, D_MODEL, N_IN), D_MODEL ** -0.5),
        'w_out': nrm(ks[2], (DEPTH, D_MODEL, D_MODEL), D_MODEL ** -0.5),
        'attn_norm_g': gain(ks[3], (DEPTH, D_MODEL)),
        'ffn_norm_g': gain(ks[4], (DEPTH, D_MODEL)),
        'q_norm_a': gain(ks[5], (DEPTH, HEAD_DIM)),
        'k_norm_a': gain(ks[6], (DEPTH, HEAD_DIM)),
        'q_norm_b': gain(ks[7], (DEPTH, HEAD_DIM)),
        'k_norm_b': gain(ks[8], (DEPTH, HEAD_DIM)),
        'q_norm_c': gain(ks[9], (DEPTH, C_HALF)),
        'k_norm_c': gain(ks[10], (DEPTH, C_HALF)),
        'lambda_q1': nrm(ks[11], (DEPTH, C_HALF), 0.1),
        'lambda_k1': nrm(ks[12], (DEPTH, C_HALF), 0.1),
        'lambda_q2': nrm(ks[13], (DEPTH, C_HALF), 0.1),
        'lambda_k2': nrm(ks[14], (DEPTH, C_HALF), 0.1),
        'diff_subln_g': gain(ks[15], (DEPTH, HEAD_DIM)),
        'rel_bias': nrm(ks[16], (N_BUCKETS, A_HEADS + B_HEADS + C_HEADS), 0.3),
        'w_dense_gate': nrm(ks[17], (N_DENSE, D_MODEL, D_FF), D_MODEL ** -0.5),
        'w_dense_up': nrm(ks[18], (N_DENSE, D_MODEL, D_FF), D_MODEL ** -0.5),
        'w_dense_down': nrm(ks[19], (N_DENSE, D_FF, D_MODEL), D_FF ** -0.5),
        'w_router': nrm(ks[20], (N_MOE, D_MODEL, N_EXPERTS), D_MODEL ** -0.5),
        'w_moe_gate': nrm(ks[21], (N_MOE, N_EXPERTS, D_MODEL, D_FF_EXPERT), D_MODEL ** -0.5),
        'w_moe_up': nrm(ks[22], (N_MOE, N_EXPERTS, D_MODEL, D_FF_EXPERT), D_MODEL ** -0.5),
        'w_moe_down': nrm(ks[23], (N_MOE, N_EXPERTS, D_FF_EXPERT, D_MODEL), D_FF_EXPERT ** -0.5),
    }


def reference(x, w_in, w_out, attn_norm_g, ffn_norm_g, q_norm_a, k_norm_a, q_norm_b, k_norm_b,
              q_norm_c, k_norm_c, lambda_q1, lambda_k1, lambda_q2, lambda_k2, diff_subln_g, rel_bias,
              w_dense_gate, w_dense_up, w_dense_down, w_router, w_moe_gate, w_moe_up, w_moe_down):
    B, S, _ = x.shape
    split_points = np.cumsum(SPLIT_SIZES)[:-1].tolist()
    bias_a = rel_bias[:, :A_HEADS]
    bias_b = rel_bias[:, A_HEADS:A_HEADS + B_HEADS]
    bias_c = rel_bias[:, A_HEADS + B_HEADS:]

    def heads(t, n):
        return t.reshape(B, S, n, HEAD_DIM)

    for layer in range(DEPTH):
        h = rms_norm(x, attn_norm_g[layer])
        proj = jnp.einsum('bsd,dn->bsn', h, w_in[layer])
        qa, ka, va, qi, ki, wi, qb, kb, vb, qc, kc, vc = jnp.split(proj, split_points, axis=-1)

        oa = dsa_attention(rms_norm(heads(qa, A_HEADS), q_norm_a[layer]),
                           rms_norm(heads(ka, A_HEADS), k_norm_a[layer]),
                           heads(va, A_HEADS),
                           qi.reshape(B, S, IDX_HEADS, IDX_DIM), ki, wi, bias_a)

        ob = dilated_attention(rms_norm(heads(qb, B_HEADS), q_norm_b[layer]),
                               rms_norm(heads(kb, B_HEADS), k_norm_b[layer]),
                               heads(vb, B_HEADS), bias_b)

        lam_init = 0.8 - 0.6 * math.exp(-0.3 * layer)
        lam = (jnp.exp(jnp.sum(lambda_q1[layer].astype(jnp.float32) * lambda_k1[layer].astype(jnp.float32)))
               - jnp.exp(jnp.sum(lambda_q2[layer].astype(jnp.float32) * lambda_k2[layer].astype(jnp.float32)))
               + lam_init)
        oc = diff_attention(rms_norm(qc.reshape(B, S, C_HEADS, 2, C_HALF), q_norm_c[layer]),
                            rms_norm(kc.reshape(B, S, C_HEADS, 2, C_HALF), k_norm_c[layer]),
                            heads(vc, C_HEADS), bias_c, lam)
        oc = rms_norm(oc, diff_subln_g[layer]) * (1.0 - lam_init)

        mix = jnp.concatenate([oa.reshape(B, S, -1), ob.reshape(B, S, -1), oc.reshape(B, S, -1)], axis=-1)
        x = x + jnp.einsum('bsm,md->bsd', mix, w_out[layer])

        h = rms_norm(x, ffn_norm_g[layer])
        f = layer // 2
        if layer % 2 == 0:
            x = x + swiglu(h, w_dense_gate[f], w_dense_up[f], w_dense_down[f])
        else:
            x = x + moe_swiglu(h, w_router[f], w_moe_gate[f], w_moe_up[f], w_moe_down[f])
    return x
```

```python
import functools
import math

import jax
import jax.numpy as jnp
import numpy as np
from jax import lax
from jax.experimental import pallas as pl
from jax.experimental.pallas import tpu as pltpu

F32 = jnp.float32
BF16 = jnp.bfloat16

HEAD_DIM = 128
A_HEADS, B_HEADS, C_HEADS = 4, 6, 6
C_HALF = HEAD_DIM // 2
IDX_HEADS, IDX_DIM = 16, 64
TOPK_MAX = 256
DILATED_PAIRS = ((128, 1), (512, 4), (2048, 16))
DIL_STEPS = 128
N_BUCKETS = 32
MAX_DISTANCE = 2048
N_EXPERTS = 8
RMS_EPS = 1e-6
NEG = -1e30
INT_MIN = -(2 ** 31)
LANE = 128
VMEM_LIMIT = 56 * 1024 * 1024

BUCKET_EDGES = tuple(range(16)) + (16,) + tuple(
    int(math.ceil(16.0 * (MAX_DISTANCE / 16.0) ** (k / 16.0))) for k in range(1, 16))
FAR_DIST = BUCKET_EDGES[-1]


def _cparams(sem):
    return pltpu.CompilerParams(dimension_semantics=sem, vmem_limit_bytes=VMEM_LIMIT)


def _dot(a, b):
    return jnp.dot(a, b, preferred_element_type=F32)


def _dot_nt(a, b):
    return lax.dot_general(a, b, (((1,), (1,)), ((), ())), preferred_element_type=F32)


def _rmsnorm_kernel(x_ref, g_ref, o_ref):
    x = x_ref[...]
    ms = jnp.mean(x * x, axis=-1, keepdims=True)
    o_ref[...] = (x * lax.rsqrt(ms + RMS_EPS) * g_ref[...]).astype(o_ref.dtype)


def rmsnorm(x, g, tm=512):
    S, D = x.shape
    return pl.pallas_call(
        _rmsnorm_kernel,
        grid=(S // tm,),
        in_specs=[pl.BlockSpec((tm, D), lambda i: (i, 0)), pl.BlockSpec((1, D), lambda i: (0, 0))],
        out_specs=pl.BlockSpec((tm, D), lambda i: (i, 0)),
        out_shape=jax.ShapeDtypeStruct((S, D), BF16),
        compiler_params=_cparams(("parallel",)),
        name="rmsnorm",
    )(x, g.reshape(1, D))


def _proj_kernel(h_ref, w_ref, g_ref, o_ref, *, group):
    acc = _dot(h_ref[...], w_ref[...])
    tn = acc.shape[1]
    if group == 0:
        o_ref[...] = acc.astype(o_ref.dtype)
        return
    for c in range(tn // LANE):
        a = acc[:, c * LANE:(c + 1) * LANE]
        sq = a * a
        if group == LANE:
            ms = jnp.mean(sq, axis=-1, keepdims=True)
        else:
            lane = lax.broadcasted_iota(jnp.int32, sq.shape, 1)
            lo = jnp.sum(jnp.where(lane < group, sq, 0.0), axis=-1, keepdims=True)
            hi = jnp.sum(jnp.where(lane >= group, sq, 0.0), axis=-1, keepdims=True)
            ms = jnp.where(lane < group, lo, hi) * (1.0 / group)
        y = a * lax.rsqrt(ms + RMS_EPS) * g_ref[:, c * LANE:(c + 1) * LANE]
        o_ref[:, c * LANE:(c + 1) * LANE] = y.astype(o_ref.dtype)


def project(h, w, gains, group, out_dtype, tm=1024, tn=256):
    S, K = h.shape
    N = w.shape[1]
    return pl.pallas_call(
        functools.partial(_proj_kernel, group=group),
        grid=(S // tm, N // tn),
        in_specs=[pl.BlockSpec((tm, K), lambda i, j: (i, 0)),
                  pl.BlockSpec((K, tn), lambda i, j: (0, j)),
                  pl.BlockSpec((1, tn), lambda i, j: (0, j))],
        out_specs=pl.BlockSpec((tm, tn), lambda i, j: (i, j)),
        out_shape=jax.ShapeDtypeStruct((S, N), out_dtype),
        compiler_params=_cparams(("parallel", "arbitrary")),
        name=f"proj_g{group}",
    )(h, w, gains.reshape(1, N))


def _band_kernel(tab_ref, o_ref, *, c0, mult, head_off, mask_mode):
    hh = head_off + pl.program_id(0)
    _, rows, width = o_ref.shape
    delta = (lax.broadcasted_iota(jnp.int32, (rows, width), 0)
             - lax.broadcasted_iota(jnp.int32, (rows, width), 1) + c0)
    dist = delta * mult
    val = jnp.full((rows, width), tab_ref[0, hh], F32)
    for b in range(1, N_BUCKETS):
        val = jnp.where(dist >= BUCKET_EDGES[b], tab_ref[b, hh], val)
    if mask_mode == "causal":
        val = jnp.where(delta >= 0, val, NEG)
    elif mask_mode == "window":
        val = jnp.where((delta >= 0) & (delta <= DIL_STEPS), val, NEG)
    o_ref[0] = val


def bias_band(rel_bias, n_heads, head_off, rows, width, c0, mult, mask_mode):
    return pl.pallas_call(
        functools.partial(_band_kernel, c0=c0, mult=mult, head_off=head_off, mask_mode=mask_mode),
        grid=(n_heads,),
        in_specs=[pl.BlockSpec(memory_space=pltpu.SMEM)],
        out_specs=pl.BlockSpec((1, rows, width), lambda h: (h, 0, 0)),
        out_shape=jax.ShapeDtypeStruct((n_heads, rows, width), F32),
        compiler_params=_cparams(("parallel",)),
        name=f"bias_band_{mask_mode}",
    )(rel_bias)


def _band_c0(tk):
    return -(-(FAR_DIST + tk - 1) // LANE) * LANE


def _dsa_kernel(q_ref, qi_ref, kw_ref, k_ref, v_ref, kidx_ref, band_ref, o_ref,
                keys_ref, qih_ref, wb_ref, acc_ref, m_ref, l_ref, *, tq, kc, topk, c0):
    i = pl.program_id(0)
    t0 = i * tq
    nchunk = (t0 + tq + kc - 1) // kc
    nslab = kc // LANE

    qi = qi_ref[...]
    w = kw_ref[:, IDX_DIM:IDX_DIM + IDX_HEADS] * (IDX_HEADS ** -0.5 * IDX_DIM ** -0.5)
    for h in range(IDX_HEADS):
        qih_ref[h] = qi[:, h * IDX_DIM:(h + 1) * IDX_DIM].astype(BF16)
        wb_ref[h] = jnp.broadcast_to(w[:, h:h + 1], (tq, LANE))

    pos_q = t0 + lax.broadcasted_iota(jnp.int32, (tq, kc), 0)
    col = lax.broadcasted_iota(jnp.int32, (tq, kc), 1)

    def score_chunk(c, carry):
        s0 = pl.multiple_of(c * kc, kc)
        kx = kidx_ref[pl.ds(s0, kc), :][:, :IDX_DIM].astype(BF16)
        acc = jnp.zeros((tq, kc), F32)
        for h in range(IDX_HEADS):
            d = jnp.maximum(_dot_nt(qih_ref[h], kx), 0.0)
            acc = acc + jnp.concatenate([wb_ref[h]] * nslab, axis=1) * d
        bits = lax.bitcast_convert_type(acc, jnp.int32)
        key = bits ^ ((bits >> 31) & 0x7FFFFFFF)
        keys_ref[:, pl.ds(s0, kc)] = jnp.where(s0 + col <= pos_q, key, INT_MIN)
        return carry

    lax.fori_loop(0, nchunk, score_chunk, 0)

    def count_ge(cand):
        cb = jnp.broadcast_to(cand, (tq, LANE))

        def body(c, cnt):
            s0 = pl.multiple_of(c * kc, kc)
            kk = keys_ref[:, pl.ds(s0, kc)]
            for j in range(nslab):
                cnt = cnt + jnp.where(kk[:, j * LANE:(j + 1) * LANE] >= cb, 1.0, 0.0)
            return cnt

        cnt = lax.fori_loop(0, nchunk, body, jnp.zeros((tq, LANE), F32))
        return jnp.sum(cnt, axis=1, keepdims=True)

    kf = float(topk)
    thr = jnp.where(count_ge(jnp.zeros((tq, 1), jnp.int32)) >= kf, 0, INT_MIN).astype(jnp.int32)

    def bit_body(b, t):
        cand = t + jnp.left_shift(jnp.int32(1), 30 - b)
        return jnp.where(count_ge(cand) >= kf, cand, t)

    thr = lax.fori_loop(0, 31, bit_body, thr)
    thr = jnp.maximum(thr, INT_MIN + 1)

    m_ref[...] = jnp.full(m_ref.shape, NEG, F32)
    l_ref[...] = jnp.zeros(l_ref.shape, F32)
    acc_ref[...] = jnp.zeros(acc_ref.shape, F32)
    scale = HEAD_DIM ** -0.5

    def att_chunk(c, carry):
        s0 = pl.multiple_of(c * kc, kc)
        sel = keys_ref[:, pl.ds(s0, kc)] >= thr
        off = pl.multiple_of(c0 - jnp.minimum(t0 - s0, c0), LANE)
        for h in range(A_HEADS):
            hs = slice(h * HEAD_DIM, (h + 1) * HEAD_DIM)
            kh = k_ref[pl.ds(s0, kc), hs]
            vh = v_ref[pl.ds(s0, kc), hs]
            s = _dot_nt(q_ref[:, hs], kh) * scale + band_ref[h, :, pl.ds(off, kc)]
            s = jnp.where(sel, s, NEG)
            m_prev = m_ref[h]
            m_new = jnp.maximum(m_prev, jnp.max(s, axis=1, keepdims=True))
            alpha = jnp.exp(m_prev - m_new)
            p = jnp.exp(s - m_new)
            l_ref[h] = alpha * l_ref[h] + jnp.sum(p, axis=1, keepdims=True)
            acc_ref[h] = alpha * acc_ref[h] + _dot(p.astype(BF16), vh)
            m_ref[h] = m_new
        return carry

    lax.fori_loop(0, nchunk, att_chunk, 0)
    for h in range(A_HEADS):
        o_ref[:, h * HEAD_DIM:(h + 1) * HEAD_DIM] = (acc_ref[h] / l_ref[h]).astype(o_ref.dtype)


def dsa_attention(qk, vall, pidx, band, tq=128, kc=512):
    S = qk.shape[0]
    topk = min(TOPK_MAX, S // 4)
    aw = A_HEADS * HEAD_DIM
    iw = IDX_HEADS * IDX_DIM
    once = pl.Buffered(1)
    return pl.pallas_call(
        functools.partial(_dsa_kernel, tq=tq, kc=kc, topk=topk, c0=_band_c0(kc)),
        grid=(S // tq,),
        in_specs=[pl.BlockSpec((tq, aw), lambda i: (i, 0)),
                  pl.BlockSpec((tq, iw), lambda i: (i, 0)),
                  pl.BlockSpec((tq, LANE), lambda i: (i, iw // LANE)),
                  pl.BlockSpec((S, aw), lambda i: (0, 1), pipeline_mode=once),
                  pl.BlockSpec((S, aw), lambda i: (0, 0), pipeline_mode=once),
                  pl.BlockSpec((S, LANE), lambda i: (0, iw // LANE), pipeline_mode=once),
                  pl.BlockSpec(band.shape, lambda i: (0, 0, 0), pipeline_mode=once)],
        out_specs=pl.BlockSpec((tq, aw), lambda i: (i, 0)),
        out_shape=jax.ShapeDtypeStruct((S, aw), BF16),
        scratch_shapes=[pltpu.VMEM((tq, S), jnp.int32),
                        pltpu.VMEM((IDX_HEADS, tq, IDX_DIM), BF16),
                        pltpu.VMEM((IDX_HEADS, tq, LANE), F32),
                        pltpu.VMEM((A_HEADS, tq, HEAD_DIM), F32),
                        pltpu.VMEM((A_HEADS, tq, 1), F32),
                        pltpu.VMEM((A_HEADS, tq, 1), F32)],
        compiler_params=_cparams(("arbitrary",)),
        name="dsa_attention",
    )(qk, pidx, pidx, qk, vall, pidx, band)


def _dilated_kernel(q_ref, kp_ref, kc_ref, vp_ref, vc_ref, band_ref, o_ref, lse_ref, *, rows):
    n = pl.program_id(1)
    kfull = jnp.concatenate([kp_ref[...], kc_ref[...]], axis=0)
    vfull = jnp.concatenate([vp_ref[...], vc_ref[...]], axis=0)
    band = band_ref[0]
    scale = HEAD_DIM ** -0.5
    col = lax.broadcasted_iota(jnp.int32, (DIL_STEPS, 2 * DIL_STEPS), 1)
    for b in range(rows // DIL_STEPS):
        r0 = b * DIL_STEPS
        s = _dot_nt(q_ref[r0:r0 + DIL_STEPS, :], kfull[r0:r0 + 2 * DIL_STEPS]) * scale + band
        if b == 0:
            s = jnp.where((n == 0) & (col < DIL_STEPS), NEG, s)
        m = jnp.max(s, axis=1, keepdims=True)
        p = jnp.exp(s - m)
        l = jnp.sum(p, axis=1, keepdims=True)
        o = _dot(p.astype(BF16), vfull[r0:r0 + 2 * DIL_STEPS]) / l
        o_ref[r0:r0 + DIL_STEPS, :] = o
        lse_ref[r0:r0 + DIL_STEPS, :] = jnp.broadcast_to(m + jnp.log(l), (DIL_STEPS, HEAD_DIM))


def dilated_branch(qk, vall, band, dil, v_col0):
    S, wqk = qk.shape
    wv = vall.shape[1]
    M = S // dil
    rows = min(M, 512)
    qk_v = qk.reshape(M, dil * wqk)
    v_v = vall.reshape(M, dil * wv)
    qb, vb = wqk // LANE, wv // LANE
    sub = rows // DIL_STEPS
    vh0 = v_col0 // LANE

    def prev(n):
        return jnp.maximum(n * sub - 1, 0)

    out_sds = jax.ShapeDtypeStruct((M, dil * B_HEADS * HEAD_DIM), F32)
    o, lse = pl.pallas_call(
        functools.partial(_dilated_kernel, rows=rows),
        grid=(dil, M // rows, B_HEADS),
        in_specs=[pl.BlockSpec((rows, LANE), lambda r, n, h: (n, r * qb + h)),
                  pl.BlockSpec((DIL_STEPS, LANE), lambda r, n, h: (prev(n), r * qb + B_HEADS + h)),
                  pl.BlockSpec((rows, LANE), lambda r, n, h: (n, r * qb + B_HEADS + h)),
                  pl.BlockSpec((DIL_STEPS, LANE), lambda r, n, h: (prev(n), r * vb + vh0 + h)),
                  pl.BlockSpec((rows, LANE), lambda r, n, h: (n, r * vb + vh0 + h)),
                  pl.BlockSpec((1, DIL_STEPS, 2 * DIL_STEPS), lambda r, n, h: (h, 0, 0))],
        out_specs=[pl.BlockSpec((rows, LANE), lambda r, n, h: (n, r * B_HEADS + h)),
                   pl.BlockSpec((rows, LANE), lambda r, n, h: (n, r * B_HEADS + h))],
        out_shape=[out_sds, out_sds],
        compiler_params=_cparams(("parallel", "parallel", "parallel")),
        name=f"dilated_d{dil}",
    )(qk_v, qk_v, qk_v, v_v, v_v, band)
    return o.reshape(S, -1), lse.reshape(S, -1)


def _dilated_combine_kernel(o1, o2, o3, l1, l2, l3, out_ref):
    a, b, c = l1[...], l2[...], l3[...]
    m = jnp.maximum(jnp.maximum(a, b), c)
    wa, wb, wc = jnp.exp(a - m), jnp.exp(b - m), jnp.exp(c - m)
    out = (wa * o1[...] + wb * o2[...] + wc * o3[...]) / (wa + wb + wc)
    out_ref[...] = out.astype(out_ref.dtype)


def dilated_combine(outs, lses, tm=512):
    S, W = outs[0].shape
    spec = pl.BlockSpec((tm, W), lambda i: (i, 0))
    return pl.pallas_call(
        _dilated_combine_kernel,
        grid=(S // tm,),
        in_specs=[spec] * 6,
        out_specs=spec,
        out_shape=jax.ShapeDtypeStruct((S, W), BF16),
        compiler_params=_cparams(("parallel",)),
        name="dilated_combine",
    )(*outs, *lses)


def _diff_kernel(lam_ref, g_ref, q_ref, k_ref, v_ref, band_ref, o_ref, acc_ref, m_ref, l_ref,
                 *, tq, tk, c0, lam_init):
    i = pl.program_id(1)
    t0 = i * tq
    nchunk = (t0 + tq + tk - 1) // tk
    q = q_ref[...]
    lane = lax.broadcasted_iota(jnp.int32, q.shape, 1)
    zero = jnp.zeros_like(q)
    qs = (jnp.where(lane < C_HALF, q, zero), jnp.where(lane >= C_HALF, q, zero))

    m_ref[...] = jnp.full(m_ref.shape, NEG, F32)
    l_ref[...] = jnp.zeros(l_ref.shape, F32)
    acc_ref[...] = jnp.zeros(acc_ref.shape, F32)

    def chunk(c, carry):
        s0 = pl.multiple_of(c * tk, tk)
        kh = k_ref[pl.ds(s0, tk), :]
        vh = v_ref[pl.ds(s0, tk), :]
        off = pl.multiple_of(c0 - jnp.minimum(t0 - s0, c0), LANE)
        bias = band_ref[0, :, pl.ds(off, tk)]
        for j in range(2):
            s = _dot_nt(qs[j], kh) + bias
            m_prev = m_ref[j]
            m_new = jnp.maximum(m_prev, jnp.max(s, axis=1, keepdims=True))
            alpha = jnp.exp(m_prev - m_new)
            p = jnp.exp(s - m_new)
            l_ref[j] = alpha * l_ref[j] + jnp.sum(p, axis=1, keepdims=True)
            acc_ref[j] = alpha * acc_ref[j] + _dot(p.astype(BF16), vh)
            m_ref[j] = m_new
        return carry

    lax.fori_loop(0, nchunk, chunk, 0)

    lp = lam_ref[...]
    lam = (jnp.exp(jnp.sum(lp[0:1] * lp[1:2], axis=1, keepdims=True))
           - jnp.exp(jnp.sum(lp[2:3] * lp[3:4], axis=1, keepdims=True)) + lam_init)
    o = acc_ref[0] / l_ref[0] - lam * (acc_ref[1] / l_ref[1])
    ms = jnp.mean(o * o, axis=-1, keepdims=True)
    o = o * lax.rsqrt(ms + RMS_EPS) * g_ref[...] * (1.0 - lam_init)
    o_ref[...] = o.astype(o_ref.dtype)


def diff_attention(qk, vall, band, lam_params, subln_g, lam_init, v_col0, tq=256, tk=512):
    S = qk.shape[0]
    vh0 = v_col0 // LANE
    return pl.pallas_call(
        functools.partial(_diff_kernel, tq=tq, tk=tk, c0=_band_c0(tk), lam_init=lam_init),
        grid=(C_HEADS, S // tq),
        in_specs=[pl.BlockSpec((4, C_HALF), lambda h, i: (0, 0)),
                  pl.BlockSpec((1, HEAD_DIM), lambda h, i: (0, 0)),
                  pl.BlockSpec((tq, LANE), lambda h, i: (i, h)),
                  pl.BlockSpec((S, LANE), lambda h, i: (0, C_HEADS + h)),
                  pl.BlockSpec((S, LANE), lambda h, i: (0, vh0 + h)),
                  pl.BlockSpec((1,) + band.shape[1:], lambda h, i: (h, 0, 0))],
        out_specs=pl.BlockSpec((tq, LANE), lambda h, i: (i, h)),
        out_shape=jax.ShapeDtypeStruct((S, C_HEADS * HEAD_DIM), BF16),
        scratch_shapes=[pltpu.VMEM((2, tq, HEAD_DIM), F32),
                        pltpu.VMEM((2, tq, 1), F32),
                        pltpu.VMEM((2, tq, 1), F32)],
        compiler_params=_cparams(("parallel", "arbitrary")),
        name="diff_attention",
    )(lam_params, subln_g.reshape(1, HEAD_DIM), qk, qk, vall, band)


def _outproj_kernel(x_ref, a_ref, b_ref, c_ref, wa_ref, wb_ref, wc_ref, o_ref):
    o_ref[...] = (x_ref[...] + _dot(a_ref[...], wa_ref[...]) + _dot(b_ref[...], wb_ref[...])
                  + _dot(c_ref[...], wc_ref[...]))


def out_projection(x, oa, ob, oc, wa, wb, wc, tm=512, tn=1024):
    S, D = x.shape
    tn = min(tn, D)
    return pl.pallas_call(
        _outproj_kernel,
        grid=(S // tm, D // tn),
        in_specs=[pl.BlockSpec((tm, tn), lambda i, j: (i, j)),
                  pl.BlockSpec((tm, oa.shape[1]), lambda i, j: (i, 0)),
                  pl.BlockSpec((tm, ob.shape[1]), lambda i, j: (i, 0)),
                  pl.BlockSpec((tm, oc.shape[1]), lambda i, j: (i, 0)),
                  pl.BlockSpec((wa.shape[0], tn), lambda i, j: (0, j)),
                  pl.BlockSpec((wb.shape[0], tn), lambda i, j: (0, j)),
                  pl.BlockSpec((wc.shape[0], tn), lambda i, j: (0, j))],
        out_specs=pl.BlockSpec((tm, tn), lambda i, j: (i, j)),
        out_shape=jax.ShapeDtypeStruct((S, D), F32),
        compiler_params=_cparams(("parallel", "arbitrary")),
        name="out_projection",
    )(x, oa, ob, oc, wa, wb, wc)


def _silu(x):
    return x / (1.0 + jnp.exp(-x))


def _ffn_kernel(x_ref, h_ref, wg_ref, wu_ref, wd_ref, o_ref):
    f = pl.program_id(1)

    @pl.when(f == 0)
    def _():
        o_ref[...] = x_ref[...]

    h = h_ref[...]
    a = _silu(_dot(h, wg_ref[...])) * _dot(h, wu_ref[...])
    o_ref[...] += _dot(a.astype(BF16), wd_ref[...])


def ffn_dense(x, h, wg, wu, wd, tm=512, tf=512):
    S, D = x.shape
    F = wg.shape[1]
    return pl.pallas_call(
        _ffn_kernel,
        grid=(S // tm, F // tf),
        in_specs=[pl.BlockSpec((tm, D), lambda i, f: (i, 0)),
                  pl.BlockSpec((tm, D), lambda i, f: (i, 0)),
                  pl.BlockSpec((D, tf), lambda i, f: (0, f)),
                  pl.BlockSpec((D, tf), lambda i, f: (0, f)),
                  pl.BlockSpec((tf, D), lambda i, f: (f, 0))],
        out_specs=pl.BlockSpec((tm, D), lambda i, f: (i, 0)),
        out_shape=jax.ShapeDtypeStruct((S, D), F32),
        compiler_params=_cparams(("parallel", "arbitrary")),
        name="ffn_dense",
    )(x, h, wg, wu, wd)


def _router_kernel(x_ref, g_ref, w_ref, c_ref):
    x = x_ref[...]
    ms = jnp.mean(x * x, axis=-1, keepdims=True)
    h = x * lax.rsqrt(ms + RMS_EPS) * g_ref[...]
    logits = jnp.dot(h, w_ref[...], preferred_element_type=F32, precision=lax.Precision.HIGHEST)
    lane = lax.broadcasted_iota(jnp.int32, logits.shape, 1).astype(F32)
    logits = jnp.where(lane < N_EXPERTS, logits, NEG)
    m1 = jnp.max(logits, axis=1, keepdims=True)
    i1 = jnp.min(jnp.where(logits == m1, lane, float(LANE)), axis=1, keepdims=True)
    rest = jnp.where(lane == i1, NEG, logits)
    m2 = jnp.max(rest, axis=1, keepdims=True)
    i2 = jnp.min(jnp.where(rest == m2, lane, float(LANE)), axis=1, keepdims=True)
    e2 = jnp.exp(m2 - m1)
    g1 = 1.0 / (1.0 + e2)
    g2 = e2 / (1.0 + e2)
    c_ref[...] = jnp.where(lane == i1, g1, 0.0) + jnp.where(lane == i2, g2, 0.0)


def moe_router(x, g, w_router, tm=512):
    S, D = x.shape
    wpad = jnp.zeros((D, LANE), F32).at[:, :N_EXPERTS].set(w_router)
    return pl.pallas_call(
        _router_kernel,
        grid=(S // tm,),
        in_specs=[pl.BlockSpec((tm, D), lambda i: (i, 0)),
                  pl.BlockSpec((1, D), lambda i: (0, 0)),
                  pl.BlockSpec((D, LANE), lambda i: (0, 0))],
        out_specs=pl.BlockSpec((tm, LANE), lambda i: (i, 0)),
        out_shape=jax.ShapeDtypeStruct((S, LANE), F32),
        compiler_params=_cparams(("parallel",)),
        name="moe_router",
    )(x, g.reshape(1, D), wpad)


def _moe_kernel(x_ref, h_ref, c_ref, wg_ref, wu_ref, wd_ref, o_ref, cb_ref):
    e = pl.program_id(1)
    f = pl.program_id(2)

    @pl.when((e == 0) & (f == 0))
    def _():
        o_ref[...] = x_ref[...]

    @pl.when(f == 0)
    def _():
        c = c_ref[...]
        lane = lax.broadcasted_iota(jnp.int32, c.shape, 1)
        cb_ref[...] = jnp.sum(jnp.where(lane == e, c, 0.0), axis=1, keepdims=True)

    h = h_ref[...]
    a = _silu(_dot(h, wg_ref[0])) * _dot(h, wu_ref[0]) * cb_ref[...]
    o_ref[...] += _dot(a.astype(BF16), wd_ref[0])


def ffn_moe(x, h, combine, wg, wu, wd, tm=512, tf=512):
    S, D = x.shape
    E, _, F = wg.shape
    return pl.pallas_call(
        _moe_kernel,
        grid=(S // tm, E, F // tf),
        in_specs=[pl.BlockSpec((tm, D), lambda i, e, f: (i, 0)),
                  pl.BlockSpec((tm, D), lambda i, e, f: (i, 0)),
                  pl.BlockSpec((tm, LANE), lambda i, e, f: (i, 0)),
                  pl.BlockSpec((1, D, tf), lambda i, e, f: (e, 0, f)),
                  pl.BlockSpec((1, D, tf), lambda i, e, f: (e, 0, f)),
                  pl.BlockSpec((1, tf, D), lambda i, e, f: (e, f, 0))],
        out_specs=pl.BlockSpec((tm, D), lambda i, e, f: (i, 0)),
        out_shape=jax.ShapeDtypeStruct((S, D), F32),
        scratch_shapes=[pltpu.VMEM((tm, 1), F32)],
        compiler_params=_cparams(("parallel", "arbitrary", "arbitrary")),
        name="ffn_moe",
    )(x, h, combine, wg, wu, wd)


def _tile_gain(g, reps, scale=1.0):
    return jnp.tile(g.astype(F32) * scale, reps)


def kernel(x, w_in, w_out, attn_norm_g, ffn_norm_g, q_norm_a, k_norm_a, q_norm_b, k_norm_b,
           q_norm_c, k_norm_c, lambda_q1, lambda_k1, lambda_q2, lambda_k2, diff_subln_g, rel_bias,
           w_dense_gate, w_dense_up, w_dense_down, w_router, w_moe_gate, w_moe_up, w_moe_down):
    B, S, D = x.shape
    depth = w_in.shape[0]
    assert B == 1 and S % (DIL_STEPS * DILATED_PAIRS[-1][1]) == 0
    aw, bw, cw = A_HEADS * HEAD_DIM, B_HEADS * HEAD_DIM, C_HEADS * HEAD_DIM
    iw = IDX_HEADS * IDX_DIM
    sizes = (aw, aw, aw, iw, IDX_DIM, IDX_HEADS, bw, bw, bw, cw, cw, cw)
    offs = np.concatenate([[0], np.cumsum(sizes)]).tolist()

    def cols(w, *segs):
        return jnp.concatenate([w[:, offs[s]:offs[s + 1]] for s in segs], axis=1)

    band_a = bias_band(rel_bias, A_HEADS, 0, 128, _band_c0(512) + 512, _band_c0(512), 1, "none")
    band_c = bias_band(rel_bias, C_HEADS, A_HEADS + B_HEADS, 256, _band_c0(512) + 512, _band_c0(512), 1,
                       "causal")
    bands_b = [bias_band(rel_bias, B_HEADS, A_HEADS, DIL_STEPS, 2 * DIL_STEPS, DIL_STEPS, dil, "window")
               for _, dil in DILATED_PAIRS]

    xs = x.reshape(S, D)
    for layer in range(depth):
        wl = w_in[layer]
        w_pa = cols(wl, 0, 1).astype(BF16)
        w_pb = cols(wl, 6, 7).astype(BF16)
        w_pc = cols(wl, 9, 10).astype(BF16)
        w_pv = cols(wl, 2, 8, 11).astype(BF16)
        w_pi = jnp.pad(cols(wl, 3, 4, 5), ((0, 0), (0, LANE - IDX_DIM - IDX_HEADS))).astype(BF16)
        g_pa = jnp.concatenate([_tile_gain(q_norm_a[layer], A_HEADS), _tile_gain(k_norm_a[layer], A_HEADS)])
        g_pb = jnp.concatenate([_tile_gain(q_norm_b[layer], B_HEADS), _tile_gain(k_norm_b[layer], B_HEADS)])
        g_pc = jnp.concatenate([_tile_gain(q_norm_c[layer], 2 * C_HEADS, C_HALF ** -0.5),
                                _tile_gain(k_norm_c[layer], 2 * C_HEADS)])

        h = rmsnorm(xs, attn_norm_g[layer])
        p_a = project(h, w_pa, g_pa, HEAD_DIM, BF16)
        p_b = project(h, w_pb, g_pb, HEAD_DIM, BF16)
        p_c = project(h, w_pc, g_pc, C_HALF, BF16)
        p_v = project(h, w_pv, jnp.ones((w_pv.shape[1],), F32), 0, BF16)
        p_i = project(h, w_pi, jnp.ones((w_pi.shape[1],), F32), 0, F32, tn=128)

        oa = dsa_attention(p_a, p_v, p_i, band_a)

        branches = [dilated_branch(p_b, p_v, bands_b[g], dil, aw) for g, (_, dil) in enumerate(DILATED_PAIRS)]
        ob = dilated_combine([o for o, _ in branches], [l for _, l in branches])

        lam_init = 0.8 - 0.6 * math.exp(-0.3 * layer)
        lam_params = jnp.stack([lambda_q1[layer], lambda_k1[layer], lambda_q2[layer], lambda_k2[layer]]).astype(F32)
        oc = diff_attention(p_c, p_v, band_c, lam_params, diff_subln_g[layer].astype(F32), lam_init, aw + bw)

        wo = w_out[layer].astype(BF16)
        xs = out_projection(xs, oa, ob, oc, wo[:aw], wo[aw:aw + bw], wo[aw + bw:])

        f = layer // 2
        if layer % 2 == 0:
            h2 = rmsnorm(xs, ffn_norm_g[layer])
            xs = ffn_dense(xs, h2, w_dense_gate[f].astype(BF16), w_dense_up[f].astype(BF16),
                           w_dense_down[f].astype(BF16))
        else:
            h2 = rmsnorm(xs, ffn_norm_g[layer])
            combine = moe_router(xs, ffn_norm_g[layer], w_router[f])
            xs = ffn_moe(xs, h2, combine, w_moe_gate[f].astype(BF16), w_moe_up[f].astype(BF16),
                         w_moe_down[f].astype(BF16))
    return xs.reshape(B, S, D)
```

```python
import functools
import math

import jax
import jax.numpy as jnp
import numpy as np
from jax import lax
from jax.experimental import pallas as pl
from jax.experimental.pallas import tpu as pltpu

F32 = jnp.float32
BF16 = jnp.bfloat16

HEAD_DIM = 128
A_HEADS, B_HEADS, C_HEADS = 4, 6, 6
C_HALF = HEAD_DIM // 2
IDX_HEADS, IDX_DIM = 16, 64
TOPK_MAX = 256
DILATED_PAIRS = ((128, 1), (512, 4), (2048, 16))
DIL_STEPS = 128
N_BUCKETS = 32
MAX_DISTANCE = 2048
N_EXPERTS = 8
RMS_EPS = 1e-6
NEG = -1e30
INT_MIN = -(2 ** 31)
LANE = 128
VMEM_LIMIT = 56 * 1024 * 1024

BUCKET_EDGES = tuple(range(16)) + (16,) + tuple(
    int(math.ceil(16.0 * (MAX_DISTANCE / 16.0) ** (k / 16.0))) for k in range(1, 16))
FAR_DIST = BUCKET_EDGES[-1]


def _cparams(sem):
    return pltpu.CompilerParams(dimension_semantics=sem, vmem_limit_bytes=VMEM_LIMIT)


def _dot(a, b):
    return jnp.dot(a, b, preferred_element_type=F32)


def _dot_nt(a, b):
    return lax.dot_general(a, b, (((1,), (1,)), ((), ())), preferred_element_type=F32)


def _rmsnorm_kernel(x_ref, g_ref, o_ref):
    x = x_ref[...]
    ms = jnp.mean(x * x, axis=-1, keepdims=True)
    o_ref[...] = (x * lax.rsqrt(ms + RMS_EPS) * g_ref[...]).astype(o_ref.dtype)


def rmsnorm(x, g, tm=512):
    S, D = x.shape
    return pl.pallas_call(
        _rmsnorm_kernel,
        grid=(S // tm,),
        in_specs=[pl.BlockSpec((tm, D), lambda i: (i, 0)), pl.BlockSpec((1, D), lambda i: (0, 0))],
        out_specs=pl.BlockSpec((tm, D), lambda i: (i, 0)),
        out_shape=jax.ShapeDtypeStruct((S, D), BF16),
        compiler_params=_cparams(("parallel",)),
        name="rmsnorm",
    )(x, g.reshape(1, D))


def _proj_kernel(h_ref, w_ref, g_ref, o_ref, *, group):
    acc = _dot(h_ref[...], w_ref[...])
    tn = acc.shape[1]
    if group == 0:
        o_ref[...] = acc.astype(o_ref.dtype)
        return
    for c in range(tn // LANE):
        a = acc[:, c * LANE:(c + 1) * LANE]
        sq = a * a
        if group == LANE:
            ms = jnp.mean(sq, axis=-1, keepdims=True)
        else:
            lane = lax.broadcasted_iota(jnp.int32, sq.shape, 1)
            lo = jnp.sum(jnp.where(lane < group, sq, 0.0), axis=-1, keepdims=True)
            hi = jnp.sum(jnp.where(lane >= group, sq, 0.0), axis=-1, keepdims=True)
            ms = jnp.where(lane < group, lo, hi) * (1.0 / group)
        y = a * lax.rsqrt(ms + RMS_EPS) * g_ref[:, c * LANE:(c + 1) * LANE]
        o_ref[:, c * LANE:(c + 1) * LANE] = y.astype(o_ref.dtype)


def project(h, w, gains, group, out_dtype, tm=1024, tn=256):
    S, K = h.shape
    N = w.shape[1]
    return pl.pallas_call(
        functools.partial(_proj_kernel, group=group),
        grid=(S // tm, N // tn),
        in_specs=[pl.BlockSpec((tm, K), lambda i, j: (i, 0)),
                  pl.BlockSpec((K, tn), lambda i, j: (0, j)),
                  pl.BlockSpec((1, tn), lambda i, j: (0, j))],
        out_specs=pl.BlockSpec((tm, tn), lambda i, j: (i, j)),
        out_shape=jax.ShapeDtypeStruct((S, N), out_dtype),
        compiler_params=_cparams(("parallel", "arbitrary")),
        name=f"proj_g{group}",
    )(h, w, gains.reshape(1, N))


def _proj_t_kernel(wt_ref, h_ref, o_ref):
    o_ref[...] = _dot_nt(wt_ref[...], h_ref[...]).astype(o_ref.dtype)


def project_t(h, wt, out_dtype, tm=1024, tn=256):
    S, K = h.shape
    N = wt.shape[0]
    return pl.pallas_call(
        _proj_t_kernel,
        grid=(S // tm, N // tn),
        in_specs=[pl.BlockSpec((tn, K), lambda i, j: (j, 0)),
                  pl.BlockSpec((tm, K), lambda i, j: (i, 0))],
        out_specs=pl.BlockSpec((tn, tm), lambda i, j: (j, i)),
        out_shape=jax.ShapeDtypeStruct((N, S), out_dtype),
        compiler_params=_cparams(("parallel", "arbitrary")),
        name="proj_t",
    )(wt, h)


def _band_kernel(tab_ref, o_ref, *, c0, mult, head_off, mask_mode, key_axis):
    hh = head_off + pl.program_id(0)
    _, rows, width = o_ref.shape
    delta = (lax.broadcasted_iota(jnp.int32, (rows, width), 1 - key_axis)
             - lax.broadcasted_iota(jnp.int32, (rows, width), key_axis) + c0)
    dist = delta * mult
    val = jnp.full((rows, width), tab_ref[0, hh], F32)
    for b in range(1, N_BUCKETS):
        val = jnp.where(dist >= BUCKET_EDGES[b], tab_ref[b, hh], val)
    if mask_mode == "causal":
        val = jnp.where(delta >= 0, val, NEG)
    elif mask_mode == "window":
        val = jnp.where((delta >= 0) & (delta <= DIL_STEPS), val, NEG)
    o_ref[0] = val


def bias_band(rel_bias, n_heads, head_off, rows, width, c0, mult, mask_mode, key_axis=1):
    return pl.pallas_call(
        functools.partial(_band_kernel, c0=c0, mult=mult, head_off=head_off, mask_mode=mask_mode,
                          key_axis=key_axis),
        grid=(n_heads,),
        in_specs=[pl.BlockSpec(memory_space=pltpu.SMEM)],
        out_specs=pl.BlockSpec((1, rows, width), lambda h: (h, 0, 0)),
        out_shape=jax.ShapeDtypeStruct((n_heads, rows, width), F32),
        compiler_params=_cparams(("parallel",)),
        name=f"bias_band_{mask_mode}",
    )(rel_bias)


def _band_c0(tk):
    return -(-(FAR_DIST + tk - 1) // LANE) * LANE


def _dsa_kernel(q_ref, qi_ref, kw_ref, k_ref, v_ref, kidx_ref, band_ref, o_ref,
                keys_ref, qih_ref, wb_ref, acc_ref, m_ref, l_ref, *, tq, kc, topk, c0):
    i = pl.program_id(0)
    t0 = i * tq
    nchunk = (t0 + tq + kc - 1) // kc
    nslab = kc // LANE

    qi = qi_ref[...]
    w = kw_ref[:, IDX_DIM:IDX_DIM + IDX_HEADS] * (IDX_HEADS ** -0.5 * IDX_DIM ** -0.5)
    for h in range(IDX_HEADS):
        qih_ref[h] = qi[:, h * IDX_DIM:(h + 1) * IDX_DIM].astype(BF16)
        wb_ref[h] = jnp.broadcast_to(w[:, h:h + 1], (tq, LANE))

    pos_q = t0 + lax.broadcasted_iota(jnp.int32, (tq, kc), 0)
    col = lax.broadcasted_iota(jnp.int32, (tq, kc), 1)

    def score_chunk(c, carry):
        s0 = pl.multiple_of(c * kc, kc)
        kx = kidx_ref[pl.ds(s0, kc), :][:, :IDX_DIM].astype(BF16)
        acc = jnp.zeros((tq, kc), F32)
        for h in range(IDX_HEADS):
            d = jnp.maximum(_dot_nt(qih_ref[h], kx), 0.0)
            acc = acc + jnp.concatenate([wb_ref[h]] * nslab, axis=1) * d
        bits = lax.bitcast_convert_type(acc, jnp.int32)
        key = bits ^ ((bits >> 31) & 0x7FFFFFFF)
        keys_ref[:, pl.ds(s0, kc)] = jnp.where(s0 + col <= pos_q, key, INT_MIN)
        return carry

    lax.fori_loop(0, nchunk, score_chunk, 0)

    def count_ge(cand):
        cb = jnp.broadcast_to(cand, (tq, LANE))

        def body(c, cnt):
            s0 = pl.multiple_of(c * kc, kc)
            kk = keys_ref[:, pl.ds(s0, kc)]
            for j in range(nslab):
                cnt = cnt + jnp.where(kk[:, j * LANE:(j + 1) * LANE] >= cb, 1.0, 0.0)
            return cnt

        cnt = lax.fori_loop(0, nchunk, body, jnp.zeros((tq, LANE), F32))
        return jnp.sum(cnt, axis=1, keepdims=True)

    kf = float(topk)
    thr = jnp.where(count_ge(jnp.zeros((tq, 1), jnp.int32)) >= kf, 0, INT_MIN).astype(jnp.int32)

    def bit_body(b, t):
        cand = t + jnp.left_shift(jnp.int32(1), 30 - b)
        return jnp.where(count_ge(cand) >= kf, cand, t)

    thr = lax.fori_loop(0, 31, bit_body, thr)
    thr = jnp.maximum(thr, INT_MIN + 1)

    m_ref[...] = jnp.full(m_ref.shape, NEG, F32)
    l_ref[...] = jnp.zeros(l_ref.shape, F32)
    acc_ref[...] = jnp.zeros(acc_ref.shape, F32)
    scale = HEAD_DIM ** -0.5

    def att_chunk(c, carry):
        s0 = pl.multiple_of(c * kc, kc)
        sel = keys_ref[:, pl.ds(s0, kc)] >= thr
        off = pl.multiple_of(c0 - jnp.minimum(t0 - s0, c0), LANE)
        for h in range(A_HEADS):
            hs = slice(h * HEAD_DIM, (h + 1) * HEAD_DIM)
            kh = k_ref[pl.ds(s0, kc), hs]
            vh = v_ref[pl.ds(s0, kc), hs]
            s = _dot_nt(q_ref[:, hs], kh) * scale + band_ref[h, :, pl.ds(off, kc)]
            s = jnp.where(sel, s, NEG)
            m_prev = m_ref[h]
            m_new = jnp.maximum(m_prev, jnp.max(s, axis=1, keepdims=True))
            alpha = jnp.exp(m_prev - m_new)
            p = jnp.exp(s - m_new)
            l_ref[h] = alpha * l_ref[h] + jnp.sum(p, axis=1, keepdims=True)
            acc_ref[h] = alpha * acc_ref[h] + _dot(p.astype(BF16), vh)
            m_ref[h] = m_new
        return carry

    lax.fori_loop(0, nchunk, att_chunk, 0)
    for h in range(A_HEADS):
        o_ref[:, h * HEAD_DIM:(h + 1) * HEAD_DIM] = (acc_ref[h] / l_ref[h]).astype(o_ref.dtype)


def dsa_attention(qk, vall, pidx, band, tq=128, kc=512):
    S = qk.shape[0]
    topk = min(TOPK_MAX, S // 4)
    aw = A_HEADS * HEAD_DIM
    iw = IDX_HEADS * IDX_DIM
    once = pl.Buffered(1)
    return pl.pallas_call(
        functools.partial(_dsa_kernel, tq=tq, kc=kc, topk=topk, c0=_band_c0(kc)),
        grid=(S // tq,),
        in_specs=[pl.BlockSpec((tq, aw), lambda i: (i, 0)),
                  pl.BlockSpec((tq, iw), lambda i: (i, 0)),
                  pl.BlockSpec((tq, LANE), lambda i: (i, iw // LANE)),
                  pl.BlockSpec((S, aw), lambda i: (0, 1), pipeline_mode=once),
                  pl.BlockSpec((S, aw), lambda i: (0, 0), pipeline_mode=once),
                  pl.BlockSpec((S, LANE), lambda i: (0, iw // LANE), pipeline_mode=once),
                  pl.BlockSpec(band.shape, lambda i: (0, 0, 0), pipeline_mode=once)],
        out_specs=pl.BlockSpec((tq, aw), lambda i: (i, 0)),
        out_shape=jax.ShapeDtypeStruct((S, aw), BF16),
        scratch_shapes=[pltpu.VMEM((tq, S), jnp.int32),
                        pltpu.VMEM((IDX_HEADS, tq, IDX_DIM), BF16),
                        pltpu.VMEM((IDX_HEADS, tq, LANE), F32),
                        pltpu.VMEM((A_HEADS, tq, HEAD_DIM), F32),
                        pltpu.VMEM((A_HEADS, tq, 1), F32),
                        pltpu.VMEM((A_HEADS, tq, 1), F32)],
        compiler_params=_cparams(("arbitrary",)),
        name="dsa_attention",
    )(qk, pidx, pidx, qk, vall, pidx, band)


def _dilated_kernel(q_ref, kp_ref, kc_ref, vp_ref, vc_ref, band_ref, o_ref, lse_ref, *, rows):
    n = pl.program_id(1)
    kfull = jnp.concatenate([kp_ref[...], kc_ref[...]], axis=0)
    vfull = jnp.concatenate([vp_ref[...], vc_ref[...]], axis=0)
    band = band_ref[0]
    scale = HEAD_DIM ** -0.5
    col = lax.broadcasted_iota(jnp.int32, (DIL_STEPS, 2 * DIL_STEPS), 1)
    for b in range(rows // DIL_STEPS):
        r0 = b * DIL_STEPS
        s = _dot_nt(q_ref[r0:r0 + DIL_STEPS, :], kfull[r0:r0 + 2 * DIL_STEPS]) * scale + band
        if b == 0:
            s = jnp.where((n == 0) & (col < DIL_STEPS), NEG, s)
        m = jnp.max(s, axis=1, keepdims=True)
        p = jnp.exp(s - m)
        l = jnp.sum(p, axis=1, keepdims=True)
        o = _dot(p.astype(BF16), vfull[r0:r0 + 2 * DIL_STEPS]) / l
        o_ref[r0:r0 + DIL_STEPS, :] = o
        lse_ref[r0:r0 + DIL_STEPS, :] = jnp.broadcast_to(m + jnp.log(l), (DIL_STEPS, HEAD_DIM))


def dilated_branch(qk, vall, band, dil, v_col0):
    S, wqk = qk.shape
    wv = vall.shape[1]
    M = S // dil
    rows = min(M, 512)
    qk_v = qk.reshape(M, dil * wqk)
    v_v = vall.reshape(M, dil * wv)
    qb, vb = wqk // LANE, wv // LANE
    sub = rows // DIL_STEPS
    vh0 = v_col0 // LANE

    def prev(n):
        return jnp.maximum(n * sub - 1, 0)

    out_sds = jax.ShapeDtypeStruct((M, dil * B_HEADS * HEAD_DIM), F32)
    o, lse = pl.pallas_call(
        functools.partial(_dilated_kernel, rows=rows),
        grid=(dil, M // rows, B_HEADS),
        in_specs=[pl.BlockSpec((rows, LANE), lambda r, n, h: (n, r * qb + h)),
                  pl.BlockSpec((DIL_STEPS, LANE), lambda r, n, h: (prev(n), r * qb + B_HEADS + h)),
                  pl.BlockSpec((rows, LANE), lambda r, n, h: (n, r * qb + B_HEADS + h)),
                  pl.BlockSpec((DIL_STEPS, LANE), lambda r, n, h: (prev(n), r * vb + vh0 + h)),
                  pl.BlockSpec((rows, LANE), lambda r, n, h: (n, r * vb + vh0 + h)),
                  pl.BlockSpec((1, DIL_STEPS, 2 * DIL_STEPS), lambda r, n, h: (h, 0, 0))],
        out_specs=[pl.BlockSpec((rows, LANE), lambda r, n, h: (n, r * B_HEADS + h)),
                   pl.BlockSpec((rows, LANE), lambda r, n, h: (n, r * B_HEADS + h))],
        out_shape=[out_sds, out_sds],
        compiler_params=_cparams(("parallel", "parallel", "parallel")),
        name=f"dilated_d{dil}",
    )(qk_v, qk_v, qk_v, v_v, v_v, band)
    return o.reshape(S, -1), lse.reshape(S, -1)


def _dilated_combine_kernel(o1, o2, o3, l1, l2, l3, out_ref):
    a, b, c = l1[...], l2[...], l3[...]
    m = jnp.maximum(jnp.maximum(a, b), c)
    wa, wb, wc = jnp.exp(a - m), jnp.exp(b - m), jnp.exp(c - m)
    out = (wa * o1[...] + wb * o2[...] + wc * o3[...]) / (wa + wb + wc)
    out_ref[...] = out.astype(out_ref.dtype)


def dilated_combine(outs, lses, tm=512):
    S, W = outs[0].shape
    spec = pl.BlockSpec((tm, W), lambda i: (i, 0))
    return pl.pallas_call(
        _dilated_combine_kernel,
        grid=(S // tm,),
        in_specs=[spec] * 6,
        out_specs=spec,
        out_shape=jax.ShapeDtypeStruct((S, W), BF16),
        compiler_params=_cparams(("parallel",)),
        name="dilated_combine",
    )(*outs, *lses)


def _diff_kernel(lam_ref, g_ref, q_ref, k_ref, vt_ref, band_ref, o_ref, acc_ref, m_ref, l_ref,
                 *, tq, tk, c0, lam_init):
    i = pl.program_id(1)
    t0 = i * tq
    nchunk = (t0 + tq + tk - 1) // tk
    q = q_ref[...]
    lane = lax.broadcasted_iota(jnp.int32, q.shape, 1)
    zero = jnp.zeros_like(q)
    qs = (jnp.where(lane < C_HALF, q, zero), jnp.where(lane >= C_HALF, q, zero))

    m_ref[...] = jnp.full(m_ref.shape, NEG, F32)
    l_ref[...] = jnp.zeros(l_ref.shape, F32)
    acc_ref[...] = jnp.zeros(acc_ref.shape, F32)

    def chunk(c, carry):
        s0 = pl.multiple_of(c * tk, tk)
        kh = k_ref[pl.ds(s0, tk), :]
        vt = vt_ref[:, pl.ds(s0, tk)]
        off = pl.multiple_of(c0 - jnp.minimum(t0 - s0, c0), LANE)
        bias = band_ref[0, pl.ds(off, tk), :]
        for j in range(2):
            s = _dot_nt(kh, qs[j]) + bias
            m_prev = m_ref[j]
            m_new = jnp.maximum(m_prev, jnp.max(s, axis=0, keepdims=True))
            alpha = jnp.exp(m_prev - m_new)
            p = jnp.exp(s - m_new)
            l_ref[j] = alpha * l_ref[j] + jnp.sum(p, axis=0, keepdims=True)
            acc_ref[j] = alpha * acc_ref[j] + _dot(vt, p.astype(BF16))
            m_ref[j] = m_new
        return carry

    lax.fori_loop(0, nchunk, chunk, 0)

    lp = lam_ref[...]
    lam = (jnp.exp(jnp.sum(lp[0:1] * lp[1:2], axis=1, keepdims=True))
           - jnp.exp(jnp.sum(lp[2:3] * lp[3:4], axis=1, keepdims=True)) + lam_init)
    o = acc_ref[0] / l_ref[0] - lam * (acc_ref[1] / l_ref[1])
    ms = jnp.mean(o * o, axis=0, keepdims=True)
    o = o * lax.rsqrt(ms + RMS_EPS) * (g_ref[...] * (1.0 - lam_init))
    o_ref[...] = o.T.astype(o_ref.dtype)


def diff_attention(qk, vt, band, lam_params, subln_g, lam_init, vt_row0, tq=256, tk=512):
    S = qk.shape[0]
    vh0 = vt_row0 // HEAD_DIM
    return pl.pallas_call(
        functools.partial(_diff_kernel, tq=tq, tk=tk, c0=_band_c0(tk), lam_init=lam_init),
        grid=(C_HEADS, S // tq),
        in_specs=[pl.BlockSpec((4, C_HALF), lambda h, i: (0, 0)),
                  pl.BlockSpec((HEAD_DIM, 1), lambda h, i: (0, 0)),
                  pl.BlockSpec((tq, LANE), lambda h, i: (i, h)),
                  pl.BlockSpec((S, LANE), lambda h, i: (0, C_HEADS + h)),
                  pl.BlockSpec((HEAD_DIM, S), lambda h, i: (vh0 + h, 0)),
                  pl.BlockSpec((1,) + band.shape[1:], lambda h, i: (h, 0, 0))],
        out_specs=pl.BlockSpec((tq, LANE), lambda h, i: (i, h)),
        out_shape=jax.ShapeDtypeStruct((S, C_HEADS * HEAD_DIM), BF16),
        scratch_shapes=[pltpu.VMEM((2, HEAD_DIM, tq), F32),
                        pltpu.VMEM((2, 1, tq), F32),
                        pltpu.VMEM((2, 1, tq), F32)],
        compiler_params=_cparams(("parallel", "arbitrary")),
        name="diff_attention",
    )(lam_params, subln_g.reshape(HEAD_DIM, 1), qk, qk, vt, band)


def _outproj_kernel(x_ref, a_ref, b_ref, c_ref, wa_ref, wb_ref, wc_ref, o_ref):
    o_ref[...] = (x_ref[...] + _dot(a_ref[...], wa_ref[...]) + _dot(b_ref[...], wb_ref[...])
                  + _dot(c_ref[...], wc_ref[...]))


def out_projection(x, oa, ob, oc, wa, wb, wc, tm=512, tn=1024):
    S, D = x.shape
    tn = min(tn, D)
    return pl.pallas_call(
        _outproj_kernel,
        grid=(S // tm, D // tn),
        in_specs=[pl.BlockSpec((tm, tn), lambda i, j: (i, j)),
                  pl.BlockSpec((tm, oa.shape[1]), lambda i, j: (i, 0)),
                  pl.BlockSpec((tm, ob.shape[1]), lambda i, j: (i, 0)),
                  pl.BlockSpec((tm, oc.shape[1]), lambda i, j: (i, 0)),
                  pl.BlockSpec((wa.shape[0], tn), lambda i, j: (0, j)),
                  pl.BlockSpec((wb.shape[0], tn), lambda i, j: (0, j)),
                  pl.BlockSpec((wc.shape[0], tn), lambda i, j: (0, j))],
        out_specs=pl.BlockSpec((tm, tn), lambda i, j: (i, j)),
        out_shape=jax.ShapeDtypeStruct((S, D), F32),
        compiler_params=_cparams(("parallel", "arbitrary")),
        name="out_projection",
    )(x, oa, ob, oc, wa, wb, wc)


def _silu(x):
    return x / (1.0 + jnp.exp(-x))


def _ffn_kernel(x_ref, h_ref, wg_ref, wu_ref, wd_ref, o_ref):
    f = pl.program_id(1)

    @pl.when(f == 0)
    def _():
        o_ref[...] = x_ref[...]

    h = h_ref[...]
    a = _silu(_dot(h, wg_ref[...])) * _dot(h, wu_ref[...])
    o_ref[...] += _dot(a.astype(BF16), wd_ref[...])


def ffn_dense(x, h, wg, wu, wd, tm=512, tf=512):
    S, D = x.shape
    F = wg.shape[1]
    return pl.pallas_call(
        _ffn_kernel,
        grid=(S // tm, F // tf),
        in_specs=[pl.BlockSpec((tm, D), lambda i, f: (i, 0)),
                  pl.BlockSpec((tm, D), lambda i, f: (i, 0)),
                  pl.BlockSpec((D, tf), lambda i, f: (0, f)),
                  pl.BlockSpec((D, tf), lambda i, f: (0, f)),
                  pl.BlockSpec((tf, D), lambda i, f: (f, 0))],
        out_specs=pl.BlockSpec((tm, D), lambda i, f: (i, 0)),
        out_shape=jax.ShapeDtypeStruct((S, D), F32),
        compiler_params=_cparams(("parallel", "arbitrary")),
        name="ffn_dense",
    )(x, h, wg, wu, wd)


def _router_kernel(x_ref, g_ref, w_ref, c_ref, sel_ref):
    x = x_ref[...]
    ms = jnp.mean(x * x, axis=-1, keepdims=True)
    h = x * lax.rsqrt(ms + RMS_EPS) * g_ref[...]
    logits = jnp.dot(h, w_ref[...], preferred_element_type=F32, precision=lax.Precision.HIGHEST)
    lane = lax.broadcasted_iota(jnp.int32, logits.shape, 1).astype(F32)
    logits = jnp.where(lane < N_EXPERTS, logits, NEG)
    m1 = jnp.max(logits, axis=1, keepdims=True)
    i1 = jnp.min(jnp.where(logits == m1, lane, float(LANE)), axis=1, keepdims=True)
    rest = jnp.where(lane == i1, NEG, logits)
    m2 = jnp.max(rest, axis=1, keepdims=True)
    i2 = jnp.min(jnp.where(rest == m2, lane, float(LANE)), axis=1, keepdims=True)
    e2 = jnp.exp(m2 - m1)
    g1 = 1.0 / (1.0 + e2)
    g2 = e2 / (1.0 + e2)
    pick1, pick2 = lane == i1, lane == i2
    c_ref[...] = jnp.where(pick1, g1, 0.0) + jnp.where(pick2, g2, 0.0)
    sel_ref[...] = jnp.where(pick1 | pick2, 1.0, 0.0)


def moe_router(x, g, w_router, tm=512):
    S, D = x.shape
    wpad = jnp.zeros((D, LANE), F32).at[:, :N_EXPERTS].set(w_router)
    spec = pl.BlockSpec((tm, LANE), lambda i: (i, 0))
    sds = jax.ShapeDtypeStruct((S, LANE), F32)
    return pl.pallas_call(
        _router_kernel,
        grid=(S // tm,),
        in_specs=[pl.BlockSpec((tm, D), lambda i: (i, 0)),
                  pl.BlockSpec((1, D), lambda i: (0, 0)),
                  pl.BlockSpec((D, LANE), lambda i: (0, 0))],
        out_specs=[spec, spec],
        out_shape=[sds, sds],
        compiler_params=_cparams(("parallel",)),
        name="moe_router",
    )(x, g.reshape(1, D), wpad)


MOE_CH = 256
MOE_TILE = 512


def _moe_rank_kernel(sel_ref, rm_ref, rmt_ref, cum_ref, tot_ref, carry_ref):
    c = pl.program_id(0)

    @pl.when(c == 0)
    def _():
        carry_ref[...] = jnp.zeros(carry_ref.shape, F32)

    sel = sel_ref[...]
    ch = sel.shape[0]
    before = (lax.broadcasted_iota(jnp.int32, (ch, ch), 1) < lax.broadcasted_iota(jnp.int32, (ch, ch), 0))
    rank = _dot(jnp.where(before, 1.0, 0.0).astype(BF16), sel.astype(BF16)) + carry_ref[...]
    rm = jnp.where(sel > 0.0, rank, -1.0)
    rm_ref[...] = rm
    rmt_ref[...] = rm.T[:N_EXPERTS]
    cum_ref[0] = carry_ref[...]
    carry_ref[...] += jnp.sum(sel, axis=0, keepdims=True)
    tot_ref[...] = carry_ref[...]


def moe_rank(sel):
    S = sel.shape[0]
    nch = S // MOE_CH
    return pl.pallas_call(
        _moe_rank_kernel,
        grid=(nch,),
        in_specs=[pl.BlockSpec((MOE_CH, LANE), lambda c: (c, 0))],
        out_specs=[pl.BlockSpec((MOE_CH, LANE), lambda c: (c, 0)),
                   pl.BlockSpec((N_EXPERTS, MOE_CH), lambda c: (0, c)),
                   pl.BlockSpec((1, 1, LANE), lambda c: (c, 0, 0)),
                   pl.BlockSpec((1, LANE), lambda c: (0, 0))],
        out_shape=[jax.ShapeDtypeStruct((S, LANE), F32),
                   jax.ShapeDtypeStruct((N_EXPERTS, S), F32),
                   jax.ShapeDtypeStruct((nch, 1, LANE), F32),
                   jax.ShapeDtypeStruct((1, LANE), F32)],
        scratch_shapes=[pltpu.VMEM((1, LANE), F32)],
        compiler_params=_cparams(("arbitrary",)),
        name="moe_rank",
    )(sel)


def _moe_expert_kernel(te_ref, tr0_ref, tclo_ref, tchi_ref, tval_ref,
                       rk_ref, h_hbm, wg_ref, wu_ref, wd_ref, y_ref,
                       hs_ref, acc_ref, hbuf_ref, sem_ref):
    j = pl.program_id(0)
    f = pl.program_id(1)
    nf = pl.num_programs(1)
    valid = tval_ref[j] > 0
    rows = hs_ref.shape[0]

    def chunk_copy(c, slot):
        return pltpu.make_async_copy(h_hbm.at[pl.ds(c * MOE_CH, MOE_CH)], hbuf_ref.at[slot], sem_ref.at[slot])

    @pl.when(valid & (f == 0))
    def _():
        clo, chi = tclo_ref[j], tchi_ref[j]
        want = (tr0_ref[j] + lax.broadcasted_iota(jnp.int32, (rows, MOE_CH), 0)).astype(F32)
        acc_ref[...] = jnp.zeros(acc_ref.shape, F32)

        @pl.when(chi > clo)
        def _():
            chunk_copy(clo, 0).start()

        def body(c, carry):
            slot = (c - clo) % 2

            @pl.when(c + 1 < chi)
            def _():
                chunk_copy(c + 1, 1 - slot).start()

            chunk_copy(c, slot).wait()
            rk = rk_ref[0, :, pl.ds(pl.multiple_of(c * MOE_CH, MOE_CH), MOE_CH)]
            onehot = jnp.where(rk == want, 1.0, 0.0).astype(BF16)
            acc_ref[...] += _dot(onehot, hbuf_ref[slot])
            return carry

        lax.fori_loop(clo, chi, body, 0)
        hs_ref[...] = acc_ref[...].astype(BF16)
        acc_ref[...] = jnp.zeros(acc_ref.shape, F32)

    @pl.when(valid)
    def _():
        hs = hs_ref[...]
        a = _silu(_dot(hs, wg_ref[0])) * _dot(hs, wu_ref[0])
        acc_ref[...] += _dot(a.astype(BF16), wd_ref[0])

    @pl.when(f == nf - 1)
    def _():
        y_ref[...] = jnp.where(valid, acc_ref[...], 0.0).astype(y_ref.dtype)


def _moe_combine_kernel(ca_ref, cb_ref, roff_ref, x_ref, rm_ref, cw_ref, ya_ref, yb_ref, o_ref):
    c = pl.program_id(0)
    e = pl.program_id(1)

    @pl.when(e == 0)
    def _():
        o_ref[...] = x_ref[...]

    rm = rm_ref[...]
    lane = lax.broadcasted_iota(jnp.int32, rm.shape, 1)
    mine = lane == e
    pos = jnp.sum(jnp.where(mine, rm, 0.0), axis=1, keepdims=True)
    gate = jnp.sum(jnp.where(mine, cw_ref[...], 0.0), axis=1, keepdims=True)
    row = jnp.where(pos >= 0.0, pos + roff_ref[e].astype(F32), -1.0)
    k = c * pl.num_programs(1) + e
    ca, cb = ca_ref[k], cb_ref[k]
    cr = ya_ref.shape[0]
    col = lax.broadcasted_iota(jnp.int32, (rm.shape[0], cr), 1)
    for y_ref, base in ((ya_ref, ca * cr), (yb_ref, jnp.where(cb != ca, cb * cr, -2 * cr))):
        onehot = jnp.where(row == (base + col).astype(F32), 1.0, 0.0).astype(BF16)
        o_ref[...] += gate * _dot(onehot, y_ref[...])


def ffn_moe(x, h, combine, sel, wg, wu, wd, tf=512):
    S, D = x.shape
    E, _, F = wg.shape
    T = MOE_TILE
    nch = S // MOE_CH
    J = 2 * S // T + E
    rm, rmt, cum, tot = moe_rank(sel)

    cnt = tot[0, :E].astype(jnp.int32)
    ntile = (cnt + T - 1) // T
    tend = jnp.cumsum(ntile)
    tstart = tend - ntile
    jj = jnp.arange(J, dtype=jnp.int32)
    tval = (jj < tend[-1]).astype(jnp.int32)
    te = jnp.minimum(jnp.searchsorted(tend, jnp.minimum(jj, tend[-1] - 1), side="right"), E - 1).astype(jnp.int32)
    tr0 = (jnp.minimum(jj, tend[-1] - 1) - tstart[te]) * T
    cumx = cum[:, 0, :E].astype(jnp.int32)
    cumi = jnp.concatenate([cumx[1:], cnt[None]], axis=0)
    tclo = jnp.sum(cumi[:, te] <= tr0[None, :], axis=0).astype(jnp.int32)
    tchi = jnp.sum(cumx[:, te] < (tr0 + T)[None, :], axis=0).astype(jnp.int32)
    roff = (tstart * T).astype(jnp.int32)
    nrow_chunks = J * T // MOE_CH
    ca = ((roff[None, :] + cumx) // MOE_CH).astype(jnp.int32)
    cb = jnp.minimum(ca + 1, nrow_chunks - 1)
    ca, cb = ca.reshape(-1), cb.reshape(-1)

    nf = F // tf

    def wmap(j, f, te_r, tr0_r, clo_r, chi_r, val_r):
        return (te_r[j], 0, jnp.where(val_r[j] > 0, f, nf - 1))

    def wdmap(j, f, te_r, tr0_r, clo_r, chi_r, val_r):
        return (te_r[j], jnp.where(val_r[j] > 0, f, nf - 1), 0)

    y = pl.pallas_call(
        _moe_expert_kernel,
        grid_spec=pltpu.PrefetchScalarGridSpec(
            num_scalar_prefetch=5,
            grid=(J, nf),
            in_specs=[pl.BlockSpec((1, 1, S), lambda j, f, te_r, *_: (te_r[j], 0, 0)),
                      pl.BlockSpec(memory_space=pl.ANY),
                      pl.BlockSpec((1, D, tf), wmap),
                      pl.BlockSpec((1, D, tf), wmap),
                      pl.BlockSpec((1, tf, D), wdmap)],
            out_specs=pl.BlockSpec((T, D), lambda j, f, *_: (j, 0)),
            scratch_shapes=[pltpu.VMEM((T, D), BF16),
                            pltpu.VMEM((T, D), F32),
                            pltpu.VMEM((2, MOE_CH, D), BF16),
                            pltpu.SemaphoreType.DMA((2,))]),
        out_shape=jax.ShapeDtypeStruct((J * T, D), BF16),
        compiler_params=_cparams(("arbitrary", "arbitrary")),
        name="moe_experts",
    )(te, tr0, tclo, tchi, tval, rmt.reshape(E, 1, S), h, wg, wu, wd)

    return pl.pallas_call(
        _moe_combine_kernel,
        grid_spec=pltpu.PrefetchScalarGridSpec(
            num_scalar_prefetch=3,
            grid=(nch, E),
            in_specs=[pl.BlockSpec((MOE_CH, D), lambda c, e, *_: (c, 0)),
                      pl.BlockSpec((MOE_CH, LANE), lambda c, e, *_: (c, 0)),
                      pl.BlockSpec((MOE_CH, LANE), lambda c, e, *_: (c, 0)),
                      pl.BlockSpec((MOE_CH, D), lambda c, e, ca_r, cb_r, ro_r: (ca_r[c * E + e], 0)),
                      pl.BlockSpec((MOE_CH, D), lambda c, e, ca_r, cb_r, ro_r: (cb_r[c * E + e], 0))],
            out_specs=pl.BlockSpec((MOE_CH, D), lambda c, e, *_: (c, 0))),
        out_shape=jax.ShapeDtypeStruct((S, D), F32),
        compiler_params=_cparams(("parallel", "arbitrary")),
        name="moe_combine",
    )(ca, cb, roff, x, rm, combine, y, y)


def _tile_gain(g, reps, scale=1.0):
    return jnp.tile(g.astype(F32) * scale, reps)


def kernel(x, w_in, w_out, attn_norm_g, ffn_norm_g, q_norm_a, k_norm_a, q_norm_b, k_norm_b,
           q_norm_c, k_norm_c, lambda_q1, lambda_k1, lambda_q2, lambda_k2, diff_subln_g, rel_bias,
           w_dense_gate, w_dense_up, w_dense_down, w_router, w_moe_gate, w_moe_up, w_moe_down):
    B, S, D = x.shape
    depth = w_in.shape[0]
    assert B == 1 and S % (DIL_STEPS * DILATED_PAIRS[-1][1]) == 0
    aw, bw, cw = A_HEADS * HEAD_DIM, B_HEADS * HEAD_DIM, C_HEADS * HEAD_DIM
    iw = IDX_HEADS * IDX_DIM
    sizes = (aw, aw, aw, iw, IDX_DIM, IDX_HEADS, bw, bw, bw, cw, cw, cw)
    offs = np.concatenate([[0], np.cumsum(sizes)]).tolist()

    def cols(w, *segs):
        return jnp.concatenate([w[:, offs[s]:offs[s + 1]] for s in segs], axis=1)

    band_a = bias_band(rel_bias, A_HEADS, 0, 128, _band_c0(512) + 512, _band_c0(512), 1, "none")
    band_c = bias_band(rel_bias, C_HEADS, A_HEADS + B_HEADS, _band_c0(512) + 512, 256, _band_c0(512), 1,
                       "causal", key_axis=0)
    bands_b = [bias_band(rel_bias, B_HEADS, A_HEADS, DIL_STEPS, 2 * DIL_STEPS, DIL_STEPS, dil, "window")
               for _, dil in DILATED_PAIRS]

    xs = x.reshape(S, D)
    for layer in range(depth):
        wl = w_in[layer]
        w_pa = cols(wl, 0, 1).astype(BF16)
        w_pb = cols(wl, 6, 7).astype(BF16)
        w_pc = cols(wl, 9, 10).astype(BF16)
        w_pv = cols(wl, 2, 8, 11).astype(BF16)
        w_pi = jnp.pad(cols(wl, 3, 4, 5), ((0, 0), (0, LANE - IDX_DIM - IDX_HEADS))).astype(BF16)
        g_pa = jnp.concatenate([_tile_gain(q_norm_a[layer], A_HEADS), _tile_gain(k_norm_a[layer], A_HEADS)])
        g_pb = jnp.concatenate([_tile_gain(q_norm_b[layer], B_HEADS), _tile_gain(k_norm_b[layer], B_HEADS)])
        g_pc = jnp.concatenate([_tile_gain(q_norm_c[layer], 2 * C_HEADS, C_HALF ** -0.5),
                                _tile_gain(k_norm_c[layer], 2 * C_HEADS)])

        h = rmsnorm(xs, attn_norm_g[layer])
        p_a = project(h, w_pa, g_pa, HEAD_DIM, BF16)
        p_b = project(h, w_pb, g_pb, HEAD_DIM, BF16)
        p_c = project(h, w_pc, g_pc, C_HALF, BF16)
        p_v = project(h, w_pv, jnp.ones((w_pv.shape[1],), F32), 0, BF16)
        p_i = project(h, w_pi, jnp.ones((w_pi.shape[1],), F32), 0, F32, tn=128)
        vt_c = project_t(h, cols(wl, 11).T.astype(BF16), BF16)

        oa = dsa_attention(p_a, p_v, p_i, band_a)

        branches = [dilated_branch(p_b, p_v, bands_b[g], dil, aw) for g, (_, dil) in enumerate(DILATED_PAIRS)]
        ob = dilated_combine([o for o, _ in branches], [l for _, l in branches])

        lam_init = 0.8 - 0.6 * math.exp(-0.3 * layer)
        lam_params = jnp.stack([lambda_q1[layer], lambda_k1[layer], lambda_q2[layer], lambda_k2[layer]]).astype(F32)
        oc = diff_attention(p_c, vt_c, band_c, lam_params, diff_subln_g[layer].astype(F32), lam_init, 0)

        wo = w_out[layer].astype(BF16)
        xs = out_projection(xs, oa, ob, oc, wo[:aw], wo[aw:aw + bw], wo[aw + bw:])

        f = layer // 2
        if layer % 2 == 0:
            h2 = rmsnorm(xs, ffn_norm_g[layer])
            xs = ffn_dense(xs, h2, w_dense_gate[f].astype(BF16), w_dense_up[f].astype(BF16),
                           w_dense_down[f].astype(BF16))
        else:
            h2 = rmsnorm(xs, ffn_norm_g[layer])
            combine, sel = moe_router(xs, ffn_norm_g[layer], w_router[f])
            xs = ffn_moe(xs, h2, combine, sel, w_moe_gate[f].astype(BF16), w_moe_up[f].astype(BF16),
                         w_moe_down[f].astype(BF16))
    return xs.reshape(B, S, D)
```

```python
import functools
import math

import jax
import jax.numpy as jnp
import numpy as np
from jax import lax
from jax.experimental import pallas as pl
from jax.experimental.pallas import tpu as pltpu

F32 = jnp.float32
BF16 = jnp.bfloat16

HEAD_DIM = 128
A_HEADS, B_HEADS, C_HEADS = 4, 6, 6
C_HALF = HEAD_DIM // 2
IDX_HEADS, IDX_DIM = 16, 64
TOPK_MAX = 256
DILATED_PAIRS = ((128, 1), (512, 4), (2048, 16))
DIL_STEPS = 128
N_BUCKETS = 32
MAX_DISTANCE = 2048
N_EXPERTS = 8
RMS_EPS = 1e-6
NEG = -1e30
INT_MIN = -(2 ** 31)
LANE = 128
VMEM_LIMIT = 56 * 1024 * 1024

BUCKET_EDGES = tuple(range(16)) + (16,) + tuple(
    int(math.ceil(16.0 * (MAX_DISTANCE / 16.0) ** (k / 16.0))) for k in range(1, 16))
FAR_DIST = BUCKET_EDGES[-1]


def _cparams(sem):
    return pltpu.CompilerParams(dimension_semantics=sem, vmem_limit_bytes=VMEM_LIMIT)


def _dot(a, b):
    return jnp.dot(a, b, preferred_element_type=F32)


def _dot_nt(a, b):
    return lax.dot_general(a, b, (((1,), (1,)), ((), ())), preferred_element_type=F32)


def _rmsnorm_kernel(x_ref, g_ref, o_ref):
    x = x_ref[...]
    ms = jnp.mean(x * x, axis=-1, keepdims=True)
    o_ref[...] = (x * lax.rsqrt(ms + RMS_EPS) * g_ref[...]).astype(o_ref.dtype)


def rmsnorm(x, g, tm=512):
    S, D = x.shape
    return pl.pallas_call(
        _rmsnorm_kernel,
        grid=(S // tm,),
        in_specs=[pl.BlockSpec((tm, D), lambda i: (i, 0)), pl.BlockSpec((1, D), lambda i: (0, 0))],
        out_specs=pl.BlockSpec((tm, D), lambda i: (i, 0)),
        out_shape=jax.ShapeDtypeStruct((S, D), BF16),
        compiler_params=_cparams(("parallel",)),
        name="rmsnorm",
    )(x, g.reshape(1, D))


def _proj_kernel(h_ref, w_ref, g_ref, o_ref, *, group):
    acc = _dot(h_ref[...], w_ref[...])
    tn = acc.shape[1]
    if group == 0:
        o_ref[...] = acc.astype(o_ref.dtype)
        return
    for c in range(tn // LANE):
        a = acc[:, c * LANE:(c + 1) * LANE]
        sq = a * a
        if group == LANE:
            ms = jnp.mean(sq, axis=-1, keepdims=True)
        else:
            lane = lax.broadcasted_iota(jnp.int32, sq.shape, 1)
            lo = jnp.sum(jnp.where(lane < group, sq, 0.0), axis=-1, keepdims=True)
            hi = jnp.sum(jnp.where(lane >= group, sq, 0.0), axis=-1, keepdims=True)
            ms = jnp.where(lane < group, lo, hi) * (1.0 / group)
        y = a * lax.rsqrt(ms + RMS_EPS) * g_ref[:, c * LANE:(c + 1) * LANE]
        o_ref[:, c * LANE:(c + 1) * LANE] = y.astype(o_ref.dtype)


def project(h, w, gains, group, out_dtype, tm=1024, tn=256):
    S, K = h.shape
    N = w.shape[1]
    return pl.pallas_call(
        functools.partial(_proj_kernel, group=group),
        grid=(S // tm, N // tn),
        in_specs=[pl.BlockSpec((tm, K), lambda i, j: (i, 0)),
                  pl.BlockSpec((K, tn), lambda i, j: (0, j)),
                  pl.BlockSpec((1, tn), lambda i, j: (0, j))],
        out_specs=pl.BlockSpec((tm, tn), lambda i, j: (i, j)),
        out_shape=jax.ShapeDtypeStruct((S, N), out_dtype),
        compiler_params=_cparams(("parallel", "arbitrary")),
        name=f"proj_g{group}",
    )(h, w, gains.reshape(1, N))


def _proj_t_kernel(wt_ref, h_ref, o_ref):
    o_ref[...] = _dot_nt(wt_ref[...], h_ref[...]).astype(o_ref.dtype)


def project_t(h, wt, out_dtype, tm=1024, tn=256):
    S, K = h.shape
    N = wt.shape[0]
    tn = min(tn, N)
    return pl.pallas_call(
        _proj_t_kernel,
        grid=(S // tm, N // tn),
        in_specs=[pl.BlockSpec((tn, K), lambda i, j: (j, 0)),
                  pl.BlockSpec((tm, K), lambda i, j: (i, 0))],
        out_specs=pl.BlockSpec((tn, tm), lambda i, j: (j, i)),
        out_shape=jax.ShapeDtypeStruct((N, S), out_dtype),
        compiler_params=_cparams(("parallel", "arbitrary")),
        name="proj_t",
    )(wt, h)


def _band_kernel(tab_ref, o_ref, *, c0, mult, head_off, mask_mode, key_axis):
    hh = head_off + pl.program_id(0)
    _, rows, width = o_ref.shape
    delta = (lax.broadcasted_iota(jnp.int32, (rows, width), 1 - key_axis)
             - lax.broadcasted_iota(jnp.int32, (rows, width), key_axis) + c0)
    dist = delta * mult
    val = jnp.full((rows, width), tab_ref[0, hh], F32)
    for b in range(1, N_BUCKETS):
        val = jnp.where(dist >= BUCKET_EDGES[b], tab_ref[b, hh], val)
    if mask_mode == "causal":
        val = jnp.where(delta >= 0, val, NEG)
    elif mask_mode == "window":
        val = jnp.where((delta >= 0) & (delta <= DIL_STEPS), val, NEG)
    o_ref[0] = val


def bias_band(rel_bias, n_heads, head_off, rows, width, c0, mult, mask_mode, key_axis=1):
    return pl.pallas_call(
        functools.partial(_band_kernel, c0=c0, mult=mult, head_off=head_off, mask_mode=mask_mode,
                          key_axis=key_axis),
        grid=(n_heads,),
        in_specs=[pl.BlockSpec(memory_space=pltpu.SMEM)],
        out_specs=pl.BlockSpec((1, rows, width), lambda h: (h, 0, 0)),
        out_shape=jax.ShapeDtypeStruct((n_heads, rows, width), F32),
        compiler_params=_cparams(("parallel",)),
        name=f"bias_band_{mask_mode}",
    )(rel_bias)


def _band_c0(tk):
    return -(-(FAR_DIST + tk - 1) // LANE) * LANE


def _dsa_kernel(q_ref, qi_ref, wt_ref, k_ref, vt_ref, kidx_ref, band_ref, o_ref,
                keys_ref, qih_ref, s_ref, acc_ref, m_ref, l_ref, *, tq, kc, topk, c0):
    i = pl.program_id(0)
    t0 = i * tq
    nchunk = (t0 + tq + kc - 1) // kc
    half = lax.broadcasted_iota(jnp.int32, (tq, LANE), 1) < IDX_DIM

    for h in range(IDX_HEADS):
        slab = qi_ref[:, (h // 2) * LANE:(h // 2 + 1) * LANE]
        keep = half if h % 2 == 0 else jnp.logical_not(half)
        qih_ref[h] = jnp.where(keep, slab, jnp.zeros_like(slab))
    w = wt_ref[...] * (IDX_HEADS ** -0.5 * IDX_DIM ** -0.5)

    pos_q = t0 + lax.broadcasted_iota(jnp.int32, (kc, tq), 1)
    row = lax.broadcasted_iota(jnp.int32, (kc, tq), 0)

    def score_chunk(c, carry):
        s0 = pl.multiple_of(c * kc, kc)
        kx = kidx_ref[pl.ds(s0, kc), :]
        acc = jnp.zeros((kc, tq), F32)
        for h in range(IDX_HEADS):
            acc = acc + w[h:h + 1, :] * jnp.maximum(_dot_nt(kx, qih_ref[h]), 0.0)
        bits = lax.bitcast_convert_type(acc, jnp.int32)
        key = bits ^ ((bits >> 31) & 0x7FFFFFFF)
        keys_ref[pl.ds(s0, kc), :] = jnp.where(s0 + row <= pos_q, key, INT_MIN)
        return carry

    lax.fori_loop(0, nchunk, score_chunk, 0)

    lanes_cnt = 32

    def count_ge(cand):
        def body(c, cnt):
            base = pl.multiple_of(c * kc, kc)
            for g in range(kc // lanes_cnt):
                kk = keys_ref[pl.ds(base + g * lanes_cnt, lanes_cnt), :]
                cnt = cnt + jnp.where(kk >= cand, 1.0, 0.0)
            return cnt

        cnt = lax.fori_loop(0, nchunk, body, jnp.zeros((lanes_cnt, tq), F32))
        return jnp.sum(cnt, axis=0, keepdims=True)

    kf = float(topk)
    thr = jnp.where(count_ge(jnp.zeros((1, tq), jnp.int32)) >= kf, 0, INT_MIN).astype(jnp.int32)

    def bit_body(b, t):
        cand = t + jnp.left_shift(jnp.int32(1), 30 - b)
        return jnp.where(count_ge(cand) >= kf, cand, t)

    thr = lax.fori_loop(0, 31, bit_body, thr)
    thr = jnp.maximum(thr, INT_MIN + 1)

    m_ref[...] = jnp.full(m_ref.shape, NEG, F32)
    l_ref[...] = jnp.zeros(l_ref.shape, F32)
    acc_ref[...] = jnp.zeros(acc_ref.shape, F32)
    scale = HEAD_DIM ** -0.5

    def logits(c):
        s0 = pl.multiple_of(jnp.minimum(c, nchunk - 1) * kc, kc)
        off = pl.multiple_of(c0 - jnp.clip(t0 - c * kc, -kc, c0), LANE)
        sel = keys_ref[pl.ds(s0, kc), :] >= thr
        parts = []
        for h in range(A_HEADS):
            hs = slice(h * HEAD_DIM, (h + 1) * HEAD_DIM)
            s = _dot_nt(k_ref[pl.ds(s0, kc), hs], q_ref[:, hs]) * scale + band_ref[h, pl.ds(off, kc), :]
            parts.append(jnp.where(sel, s, NEG))
        return jnp.concatenate(parts, axis=1)

    def consume(c, s):
        s0 = pl.multiple_of(jnp.minimum(c, nchunk - 1) * kc, kc)
        m_prev = m_ref[...]
        m_new = jnp.maximum(m_prev, jnp.max(s, axis=0, keepdims=True))
        alpha = jnp.exp(m_prev - m_new)
        p = jnp.exp(s - m_new)
        l_ref[...] = alpha * l_ref[...] + jnp.sum(p, axis=0, keepdims=True)
        p = p.astype(BF16)
        pv = [_dot(vt_ref[h * HEAD_DIM:(h + 1) * HEAD_DIM, pl.ds(s0, kc)], p[:, h * tq:(h + 1) * tq])
              for h in range(A_HEADS)]
        acc_ref[...] = alpha * acc_ref[...] + jnp.concatenate(pv, axis=1)
        m_ref[...] = m_new

    s_ref[0] = logits(0)

    def chunk_pair(cc, carry):
        c = 2 * cc
        s_ref[1] = logits(c + 1)
        consume(c, s_ref[0])
        s_ref[0] = logits(c + 2)
        consume(c + 1, s_ref[1])
        return carry

    lax.fori_loop(0, (nchunk + 1) // 2, chunk_pair, 0)
    o = acc_ref[...] / l_ref[...]
    for h in range(A_HEADS):
        o_ref[:, h * HEAD_DIM:(h + 1) * HEAD_DIM] = o[:, h * tq:(h + 1) * tq].T.astype(o_ref.dtype)


def dsa_attention(qk, vt, pidx, wt, band, tq=256, kc=512):
    S = qk.shape[0]
    topk = min(TOPK_MAX, S // 4)
    aw = A_HEADS * HEAD_DIM
    iw = IDX_HEADS * IDX_DIM
    once = pl.Buffered(1)
    return pl.pallas_call(
        functools.partial(_dsa_kernel, tq=tq, kc=kc, topk=topk, c0=_band_c0(kc)),
        grid=(S // tq,),
        in_specs=[pl.BlockSpec((tq, aw), lambda i: (i, 0)),
                  pl.BlockSpec((tq, iw), lambda i: (i, 0)),
                  pl.BlockSpec((IDX_HEADS, tq), lambda i: (0, i)),
                  pl.BlockSpec((S, aw), lambda i: (0, 1), pipeline_mode=once),
                  pl.BlockSpec((aw, S), lambda i: (0, 0), pipeline_mode=once),
                  pl.BlockSpec((S, LANE), lambda i: (0, iw // LANE), pipeline_mode=once),
                  pl.BlockSpec(band.shape, lambda i: (0, 0, 0), pipeline_mode=once)],
        out_specs=pl.BlockSpec((tq, aw), lambda i: (i, 0)),
        out_shape=jax.ShapeDtypeStruct((S, aw), BF16),
        scratch_shapes=[pltpu.VMEM((S, tq), jnp.int32),
                        pltpu.VMEM((IDX_HEADS, tq, LANE), BF16),
                        pltpu.VMEM((2, kc, A_HEADS * tq), F32),
                        pltpu.VMEM((HEAD_DIM, A_HEADS * tq), F32),
                        pltpu.VMEM((1, A_HEADS * tq), F32),
                        pltpu.VMEM((1, A_HEADS * tq), F32)],
        compiler_params=_cparams(("arbitrary",)),
        name="dsa_attention",
    )(qk, pidx, wt, qk, vt, pidx, band)


def _dilated_kernel(q_ref, kp_ref, kc_ref, vp_ref, vc_ref, band_ref, o_ref, lse_ref, *, rows):
    n = pl.program_id(1)
    kfull = jnp.concatenate([kp_ref[...], kc_ref[...]], axis=0)
    vfull = jnp.concatenate([vp_ref[...], vc_ref[...]], axis=0)
    band = band_ref[0]
    scale = HEAD_DIM ** -0.5
    col = lax.broadcasted_iota(jnp.int32, (DIL_STEPS, 2 * DIL_STEPS), 1)
    for b in range(rows // DIL_STEPS):
        r0 = b * DIL_STEPS
        s = _dot_nt(q_ref[r0:r0 + DIL_STEPS, :], kfull[r0:r0 + 2 * DIL_STEPS]) * scale + band
        if b == 0:
            s = jnp.where((n == 0) & (col < DIL_STEPS), NEG, s)
        m = jnp.max(s, axis=1, keepdims=True)
        p = jnp.exp(s - m)
        l = jnp.sum(p, axis=1, keepdims=True)
        o = _dot(p.astype(BF16), vfull[r0:r0 + 2 * DIL_STEPS]) / l
        o_ref[r0:r0 + DIL_STEPS, :] = o
        lse_ref[r0:r0 + DIL_STEPS, :] = jnp.broadcast_to(m + jnp.log(l), (DIL_STEPS, HEAD_DIM))


def dilated_branch(qk, vall, band, dil, v_col0):
    S, wqk = qk.shape
    wv = vall.shape[1]
    M = S // dil
    rows = min(M, 512)
    qk_v = qk.reshape(M, dil * wqk)
    v_v = vall.reshape(M, dil * wv)
    qb, vb = wqk // LANE, wv // LANE
    sub = rows // DIL_STEPS
    vh0 = v_col0 // LANE

    def prev(n):
        return jnp.maximum(n * sub - 1, 0)

    out_sds = jax.ShapeDtypeStruct((M, dil * B_HEADS * HEAD_DIM), F32)
    o, lse = pl.pallas_call(
        functools.partial(_dilated_kernel, rows=rows),
        grid=(dil, M // rows, B_HEADS),
        in_specs=[pl.BlockSpec((rows, LANE), lambda r, n, h: (n, r * qb + h)),
                  pl.BlockSpec((DIL_STEPS, LANE), lambda r, n, h: (prev(n), r * qb + B_HEADS + h)),
                  pl.BlockSpec((rows, LANE), lambda r, n, h: (n, r * qb + B_HEADS + h)),
                  pl.BlockSpec((DIL_STEPS, LANE), lambda r, n, h: (prev(n), r * vb + vh0 + h)),
                  pl.BlockSpec((rows, LANE), lambda r, n, h: (n, r * vb + vh0 + h)),
                  pl.BlockSpec((1, DIL_STEPS, 2 * DIL_STEPS), lambda r, n, h: (h, 0, 0))],
        out_specs=[pl.BlockSpec((rows, LANE), lambda r, n, h: (n, r * B_HEADS + h)),
                   pl.BlockSpec((rows, LANE), lambda r, n, h: (n, r * B_HEADS + h))],
        out_shape=[out_sds, out_sds],
        compiler_params=_cparams(("parallel", "parallel", "parallel")),
        name=f"dilated_d{dil}",
    )(qk_v, qk_v, qk_v, v_v, v_v, band)
    return o.reshape(S, -1), lse.reshape(S, -1)


def _dilated_combine_kernel(o1, o2, o3, l1, l2, l3, out_ref):
    a, b, c = l1[...], l2[...], l3[...]
    m = jnp.maximum(jnp.maximum(a, b), c)
    wa, wb, wc = jnp.exp(a - m), jnp.exp(b - m), jnp.exp(c - m)
    out = (wa * o1[...] + wb * o2[...] + wc * o3[...]) / (wa + wb + wc)
    out_ref[...] = out.astype(out_ref.dtype)


def dilated_combine(outs, lses, tm=512):
    S, W = outs[0].shape
    spec = pl.BlockSpec((tm, W), lambda i: (i, 0))
    return pl.pallas_call(
        _dilated_combine_kernel,
        grid=(S // tm,),
        in_specs=[spec] * 6,
        out_specs=spec,
        out_shape=jax.ShapeDtypeStruct((S, W), BF16),
        compiler_params=_cparams(("parallel",)),
        name="dilated_combine",
    )(*outs, *lses)


def _diff_kernel(lam_ref, g_ref, q_ref, k_ref, vt_ref, band_ref, o_ref, acc_ref, m_ref, l_ref, s_ref,
                 *, tq, tk, c0, lam_init):
    i = pl.program_id(1)
    t0 = i * tq
    nchunk = (t0 + tq + tk - 1) // tk
    last_chunk = k_ref.shape[0] // tk - 1
    q = q_ref[...]
    lane = lax.broadcasted_iota(jnp.int32, q.shape, 1)
    zero = jnp.zeros_like(q)
    q2 = jnp.concatenate([jnp.where(lane < C_HALF, q, zero), jnp.where(lane >= C_HALF, q, zero)], axis=0)

    m_ref[...] = jnp.full(m_ref.shape, NEG, F32)
    l_ref[...] = jnp.zeros(l_ref.shape, F32)
    acc_ref[...] = jnp.zeros(acc_ref.shape, F32)

    def logits(c):
        s0 = pl.multiple_of(jnp.minimum(c, last_chunk) * tk, tk)
        off = pl.multiple_of(c0 - jnp.clip(t0 - c * tk, -tk, c0), LANE)
        bias = band_ref[0, pl.ds(off, tk), :]
        return _dot_nt(k_ref[pl.ds(s0, tk), :], q2) + jnp.concatenate([bias, bias], axis=1)

    def consume(c, s):
        s0 = pl.multiple_of(jnp.minimum(c, last_chunk) * tk, tk)
        m_prev = m_ref[...]
        m_new = jnp.maximum(m_prev, jnp.max(s, axis=0, keepdims=True))
        alpha = jnp.exp(m_prev - m_new)
        p = jnp.exp(s - m_new)
        l_ref[...] = alpha * l_ref[...] + jnp.sum(p, axis=0, keepdims=True)
        acc_ref[...] = alpha * acc_ref[...] + _dot(vt_ref[:, pl.ds(s0, tk)], p.astype(BF16))
        m_ref[...] = m_new

    s_ref[0] = logits(0)

    def chunk_pair(cc, carry):
        c = 2 * cc
        s_ref[1] = logits(c + 1)
        consume(c, s_ref[0])
        s_ref[0] = logits(c + 2)
        consume(c + 1, s_ref[1])
        return carry

    lax.fori_loop(0, (nchunk + 1) // 2, chunk_pair, 0)

    lp = lam_ref[...]
    lam = (jnp.exp(jnp.sum(lp[0:1] * lp[1:2], axis=1, keepdims=True))
           - jnp.exp(jnp.sum(lp[2:3] * lp[3:4], axis=1, keepdims=True)) + lam_init)
    o = acc_ref[...] / l_ref[...]
    o = o[:, :tq] - lam * o[:, tq:]
    ms = jnp.mean(o * o, axis=0, keepdims=True)
    o = o * lax.rsqrt(ms + RMS_EPS) * (g_ref[...] * (1.0 - lam_init))
    o_ref[...] = o.T.astype(o_ref.dtype)


def diff_attention(qk, vt, band, lam_params, subln_g, lam_init, vt_row0, tq=256, tk=512):
    S = qk.shape[0]
    vh0 = vt_row0 // HEAD_DIM
    return pl.pallas_call(
        functools.partial(_diff_kernel, tq=tq, tk=tk, c0=_band_c0(tk), lam_init=lam_init),
        grid=(C_HEADS, S // tq),
        in_specs=[pl.BlockSpec((4, C_HALF), lambda h, i: (0, 0)),
                  pl.BlockSpec((HEAD_DIM, 1), lambda h, i: (0, 0)),
                  pl.BlockSpec((tq, LANE), lambda h, i: (i, h)),
                  pl.BlockSpec((S, LANE), lambda h, i: (0, C_HEADS + h)),
                  pl.BlockSpec((HEAD_DIM, S), lambda h, i: (vh0 + h, 0)),
                  pl.BlockSpec((1,) + band.shape[1:], lambda h, i: (h, 0, 0))],
        out_specs=pl.BlockSpec((tq, LANE), lambda h, i: (i, h)),
        out_shape=jax.ShapeDtypeStruct((S, C_HEADS * HEAD_DIM), BF16),
        scratch_shapes=[pltpu.VMEM((HEAD_DIM, 2 * tq), F32),
                        pltpu.VMEM((1, 2 * tq), F32),
                        pltpu.VMEM((1, 2 * tq), F32),
                        pltpu.VMEM((2, tk, 2 * tq), F32)],
        compiler_params=_cparams(("parallel", "arbitrary")),
        name="diff_attention",
    )(lam_params, subln_g.reshape(HEAD_DIM, 1), qk, qk, vt, band)


def _outproj_kernel(x_ref, a_ref, b_ref, c_ref, wa_ref, wb_ref, wc_ref, o_ref):
    o_ref[...] = (x_ref[...] + _dot(a_ref[...], wa_ref[...]) + _dot(b_ref[...], wb_ref[...])
                  + _dot(c_ref[...], wc_ref[...]))


def out_projection(x, oa, ob, oc, wa, wb, wc, tm=512, tn=1024):
    S, D = x.shape
    tn = min(tn, D)
    return pl.pallas_call(
        _outproj_kernel,
        grid=(S // tm, D // tn),
        in_specs=[pl.BlockSpec((tm, tn), lambda i, j: (i, j)),
                  pl.BlockSpec((tm, oa.shape[1]), lambda i, j: (i, 0)),
                  pl.BlockSpec((tm, ob.shape[1]), lambda i, j: (i, 0)),
                  pl.BlockSpec((tm, oc.shape[1]), lambda i, j: (i, 0)),
                  pl.BlockSpec((wa.shape[0], tn), lambda i, j: (0, j)),
                  pl.BlockSpec((wb.shape[0], tn), lambda i, j: (0, j)),
                  pl.BlockSpec((wc.shape[0], tn), lambda i, j: (0, j))],
        out_specs=pl.BlockSpec((tm, tn), lambda i, j: (i, j)),
        out_shape=jax.ShapeDtypeStruct((S, D), F32),
        compiler_params=_cparams(("parallel", "arbitrary")),
        name="out_projection",
    )(x, oa, ob, oc, wa, wb, wc)


def _silu(x):
    return x / (1.0 + jnp.exp(-x))


def _ffn_kernel(x_ref, h_ref, wg_ref, wu_ref, wd_ref, o_ref):
    f = pl.program_id(1)

    @pl.when(f == 0)
    def _():
        o_ref[...] = x_ref[...]

    h = h_ref[...]
    a = _silu(_dot(h, wg_ref[...])) * _dot(h, wu_ref[...])
    o_ref[...] += _dot(a.astype(BF16), wd_ref[...])


def ffn_dense(x, h, wg, wu, wd, tm=512, tf=512):
    S, D = x.shape
    F = wg.shape[1]
    return pl.pallas_call(
        _ffn_kernel,
        grid=(S // tm, F // tf),
        in_specs=[pl.BlockSpec((tm, D), lambda i, f: (i, 0)),
                  pl.BlockSpec((tm, D), lambda i, f: (i, 0)),
                  pl.BlockSpec((D, tf), lambda i, f: (0, f)),
                  pl.BlockSpec((D, tf), lambda i, f: (0, f)),
                  pl.BlockSpec((tf, D), lambda i, f: (f, 0))],
        out_specs=pl.BlockSpec((tm, D), lambda i, f: (i, 0)),
        out_shape=jax.ShapeDtypeStruct((S, D), F32),
        compiler_params=_cparams(("parallel", "arbitrary")),
        name="ffn_dense",
    )(x, h, wg, wu, wd)


def _router_kernel(x_ref, g_ref, w_ref, c_ref, sel_ref):
    x = x_ref[...]
    ms = jnp.mean(x * x, axis=-1, keepdims=True)
    h = x * lax.rsqrt(ms + RMS_EPS) * g_ref[...]
    logits = jnp.dot(h, w_ref[...], preferred_element_type=F32, precision=lax.Precision.HIGHEST)
    lane = lax.broadcasted_iota(jnp.int32, logits.shape, 1).astype(F32)
    logits = jnp.where(lane < N_EXPERTS, logits, NEG)
    m1 = jnp.max(logits, axis=1, keepdims=True)
    i1 = jnp.min(jnp.where(logits == m1, lane, float(LANE)), axis=1, keepdims=True)
    rest = jnp.where(lane == i1, NEG, logits)
    m2 = jnp.max(rest, axis=1, keepdims=True)
    i2 = jnp.min(jnp.where(rest == m2, lane, float(LANE)), axis=1, keepdims=True)
    e2 = jnp.exp(m2 - m1)
    g1 = 1.0 / (1.0 + e2)
    g2 = e2 / (1.0 + e2)
    pick1, pick2 = lane == i1, lane == i2
    c_ref[...] = jnp.where(pick1, g1, 0.0) + jnp.where(pick2, g2, 0.0)
    sel_ref[...] = jnp.where(pick1 | pick2, 1.0, 0.0)


def moe_router(x, g, w_router, tm=512):
    S, D = x.shape
    wpad = jnp.zeros((D, LANE), F32).at[:, :N_EXPERTS].set(w_router)
    spec = pl.BlockSpec((tm, LANE), lambda i: (i, 0))
    sds = jax.ShapeDtypeStruct((S, LANE), F32)
    return pl.pallas_call(
        _router_kernel,
        grid=(S // tm,),
        in_specs=[pl.BlockSpec((tm, D), lambda i: (i, 0)),
                  pl.BlockSpec((1, D), lambda i: (0, 0)),
                  pl.BlockSpec((D, LANE), lambda i: (0, 0))],
        out_specs=[spec, spec],
        out_shape=[sds, sds],
        compiler_params=_cparams(("parallel",)),
        name="moe_router",
    )(x, g.reshape(1, D), wpad)


MOE_CH = 256
MOE_TILE = 512


def _moe_rank_kernel(sel_ref, rm_ref, rmt_ref, cum_ref, tot_ref, carry_ref):
    c = pl.program_id(0)

    @pl.when(c == 0)
    def _():
        carry_ref[...] = jnp.zeros(carry_ref.shape, F32)

    sel = sel_ref[...]
    ch = sel.shape[0]
    before = (lax.broadcasted_iota(jnp.int32, (ch, ch), 1) < lax.broadcasted_iota(jnp.int32, (ch, ch), 0))
    rank = _dot(jnp.where(before, 1.0, 0.0).astype(BF16), sel.astype(BF16)) + carry_ref[...]
    rm = jnp.where(sel > 0.0, rank, -1.0)
    rm_ref[...] = rm
    rmt_ref[...] = rm.T[:N_EXPERTS]
    cum_ref[0] = carry_ref[...]
    carry_ref[...] += jnp.sum(sel, axis=0, keepdims=True)
    tot_ref[...] = carry_ref[...]


def moe_rank(sel):
    S = sel.shape[0]
    nch = S // MOE_CH
    return pl.pallas_call(
        _moe_rank_kernel,
        grid=(nch,),
        in_specs=[pl.BlockSpec((MOE_CH, LANE), lambda c: (c, 0))],
        out_specs=[pl.BlockSpec((MOE_CH, LANE), lambda c: (c, 0)),
                   pl.BlockSpec((N_EXPERTS, MOE_CH), lambda c: (0, c)),
                   pl.BlockSpec((1, 1, LANE), lambda c: (c, 0, 0)),
                   pl.BlockSpec((1, LANE), lambda c: (0, 0))],
        out_shape=[jax.ShapeDtypeStruct((S, LANE), F32),
                   jax.ShapeDtypeStruct((N_EXPERTS, S), F32),
                   jax.ShapeDtypeStruct((nch, 1, LANE), F32),
                   jax.ShapeDtypeStruct((1, LANE), F32)],
        scratch_shapes=[pltpu.VMEM((1, LANE), F32)],
        compiler_params=_cparams(("arbitrary",)),
        name="moe_rank",
    )(sel)


def _moe_expert_kernel(te_ref, tr0_ref, tclo_ref, tchi_ref, tval_ref,
                       rk_ref, h_hbm, wg_ref, wu_ref, wd_ref, y_ref,
                       hs_ref, acc_ref, hbuf_ref, sem_ref):
    j = pl.program_id(0)
    f = pl.program_id(1)
    nf = pl.num_programs(1)
    valid = tval_ref[j] > 0
    rows = hs_ref.shape[0]

    def chunk_copy(c, slot):
        return pltpu.make_async_copy(h_hbm.at[pl.ds(c * MOE_CH, MOE_CH)], hbuf_ref.at[slot], sem_ref.at[slot])

    @pl.when(valid & (f == 0))
    def _():
        clo, chi = tclo_ref[j], tchi_ref[j]
        want = (tr0_ref[j] + lax.broadcasted_iota(jnp.int32, (rows, MOE_CH), 0)).astype(F32)
        acc_ref[...] = jnp.zeros(acc_ref.shape, F32)

        @pl.when(chi > clo)
        def _():
            chunk_copy(clo, 0).start()

        def body(c, carry):
            slot = (c - clo) % 2

            @pl.when(c + 1 < chi)
            def _():
                chunk_copy(c + 1, 1 - slot).start()

            chunk_copy(c, slot).wait()
            rk = rk_ref[0, :, pl.ds(pl.multiple_of(c * MOE_CH, MOE_CH), MOE_CH)]
            onehot = jnp.where(rk == want, 1.0, 0.0).astype(BF16)
            acc_ref[...] += _dot(onehot, hbuf_ref[slot])
            return carry

        lax.fori_loop(clo, chi, body, 0)
        hs_ref[...] = acc_ref[...].astype(BF16)
        acc_ref[...] = jnp.zeros(acc_ref.shape, F32)

    @pl.when(valid)
    def _():
        hs = hs_ref[...]
        a = _silu(_dot(hs, wg_ref[0])) * _dot(hs, wu_ref[0])
        acc_ref[...] += _dot(a.astype(BF16), wd_ref[0])

    @pl.when(f == nf - 1)
    def _():
        y_ref[...] = jnp.where(valid, acc_ref[...], 0.0).astype(y_ref.dtype)


def _moe_combine_kernel(ca_ref, cb_ref, roff_ref, x_ref, rm_ref, cw_ref, ya_ref, yb_ref, o_ref):
    c = pl.program_id(0)
    e = pl.program_id(1)

    @pl.when(e == 0)
    def _():
        o_ref[...] = x_ref[...]

    rm = rm_ref[...]
    lane = lax.broadcasted_iota(jnp.int32, rm.shape, 1)
    mine = lane == e
    pos = jnp.sum(jnp.where(mine, rm, 0.0), axis=1, keepdims=True)
    gate = jnp.sum(jnp.where(mine, cw_ref[...], 0.0), axis=1, keepdims=True)
    row = jnp.where(pos >= 0.0, pos + roff_ref[e].astype(F32), -1.0)
    k = c * pl.num_programs(1) + e
    ca, cb = ca_ref[k], cb_ref[k]
    cr = ya_ref.shape[0]
    col = lax.broadcasted_iota(jnp.int32, (rm.shape[0], cr), 1)
    for y_ref, base in ((ya_ref, ca * cr), (yb_ref, jnp.where(cb != ca, cb * cr, -2 * cr))):
        onehot = jnp.where(row == (base + col).astype(F32), 1.0, 0.0).astype(BF16)
        o_ref[...] += gate * _dot(onehot, y_ref[...])


def ffn_moe(x, h, combine, sel, wg, wu, wd, tf=512):
    S, D = x.shape
    E, _, F = wg.shape
    T = MOE_TILE
    nch = S // MOE_CH
    J = 2 * S // T + E
    rm, rmt, cum, tot = moe_rank(sel)

    cnt = tot[0, :E].astype(jnp.int32)
    ntile = (cnt + T - 1) // T
    tend = jnp.cumsum(ntile)
    tstart = tend - ntile
    jj = jnp.arange(J, dtype=jnp.int32)
    tval = (jj < tend[-1]).astype(jnp.int32)
    te = jnp.minimum(jnp.searchsorted(tend, jnp.minimum(jj, tend[-1] - 1), side="right"), E - 1).astype(jnp.int32)
    tr0 = (jnp.minimum(jj, tend[-1] - 1) - tstart[te]) * T
    cumx = cum[:, 0, :E].astype(jnp.int32)
    cumi = jnp.concatenate([cumx[1:], cnt[None]], axis=0)
    tclo = jnp.sum(cumi[:, te] <= tr0[None, :], axis=0).astype(jnp.int32)
    tchi = jnp.sum(cumx[:, te] < (tr0 + T)[None, :], axis=0).astype(jnp.int32)
    roff = (tstart * T).astype(jnp.int32)
    nrow_chunks = J * T // MOE_CH
    ca = ((roff[None, :] + cumx) // MOE_CH).astype(jnp.int32)
    cb = jnp.minimum(ca + 1, nrow_chunks - 1)
    ca, cb = ca.reshape(-1), cb.reshape(-1)

    nf = F // tf

    def wmap(j, f, te_r, tr0_r, clo_r, chi_r, val_r):
        return (te_r[j], 0, jnp.where(val_r[j] > 0, f, nf - 1))

    def wdmap(j, f, te_r, tr0_r, clo_r, chi_r, val_r):
        return (te_r[j], jnp.where(val_r[j] > 0, f, nf - 1), 0)

    y = pl.pallas_call(
        _moe_expert_kernel,
        grid_spec=pltpu.PrefetchScalarGridSpec(
            num_scalar_prefetch=5,
            grid=(J, nf),
            in_specs=[pl.BlockSpec((1, 1, S), lambda j, f, te_r, *_: (te_r[j], 0, 0)),
                      pl.BlockSpec(memory_space=pl.ANY),
                      pl.BlockSpec((1, D, tf), wmap),
                      pl.BlockSpec((1, D, tf), wmap),
                      pl.BlockSpec((1, tf, D), wdmap)],
            out_specs=pl.BlockSpec((T, D), lambda j, f, *_: (j, 0)),
            scratch_shapes=[pltpu.VMEM((T, D), BF16),
                            pltpu.VMEM((T, D), F32),
                            pltpu.VMEM((2, MOE_CH, D), BF16),
                            pltpu.SemaphoreType.DMA((2,))]),
        out_shape=jax.ShapeDtypeStruct((J * T, D), BF16),
        compiler_params=_cparams(("arbitrary", "arbitrary")),
        name="moe_experts",
    )(te, tr0, tclo, tchi, tval, rmt.reshape(E, 1, S), h, wg, wu, wd)

    return pl.pallas_call(
        _moe_combine_kernel,
        grid_spec=pltpu.PrefetchScalarGridSpec(
            num_scalar_prefetch=3,
            grid=(nch, E),
            in_specs=[pl.BlockSpec((MOE_CH, D), lambda c, e, *_: (c, 0)),
                      pl.BlockSpec((MOE_CH, LANE), lambda c, e, *_: (c, 0)),
                      pl.BlockSpec((MOE_CH, LANE), lambda c, e, *_: (c, 0)),
                      pl.BlockSpec((MOE_CH, D), lambda c, e, ca_r, cb_r, ro_r: (ca_r[c * E + e], 0)),
                      pl.BlockSpec((MOE_CH, D), lambda c, e, ca_r, cb_r, ro_r: (cb_r[c * E + e], 0))],
            out_specs=pl.BlockSpec((MOE_CH, D), lambda c, e, *_: (c, 0))),
        out_shape=jax.ShapeDtypeStruct((S, D), F32),
        compiler_params=_cparams(("parallel", "arbitrary")),
        name="moe_combine",
    )(ca, cb, roff, x, rm, combine, y, y)


def _tile_gain(g, reps, scale=1.0):
    return jnp.tile(g.astype(F32) * scale, reps)


def kernel(x, w_in, w_out, attn_norm_g, ffn_norm_g, q_norm_a, k_norm_a, q_norm_b, k_norm_b,
           q_norm_c, k_norm_c, lambda_q1, lambda_k1, lambda_q2, lambda_k2, diff_subln_g, rel_bias,
           w_dense_gate, w_dense_up, w_dense_down, w_router, w_moe_gate, w_moe_up, w_moe_down):
    B, S, D = x.shape
    depth = w_in.shape[0]
    assert B == 1 and S % (DIL_STEPS * DILATED_PAIRS[-1][1]) == 0
    aw, bw, cw = A_HEADS * HEAD_DIM, B_HEADS * HEAD_DIM, C_HEADS * HEAD_DIM
    iw = IDX_HEADS * IDX_DIM
    sizes = (aw, aw, aw, iw, IDX_DIM, IDX_HEADS, bw, bw, bw, cw, cw, cw)
    offs = np.concatenate([[0], np.cumsum(sizes)]).tolist()

    def cols(w, *segs):
        return jnp.concatenate([w[:, offs[s]:offs[s + 1]] for s in segs], axis=1)

    band_a = bias_band(rel_bias, A_HEADS, 0, _band_c0(512) + 2 * 512, 256, _band_c0(512), 1,
                       "causal", key_axis=0)
    band_c = bias_band(rel_bias, C_HEADS, A_HEADS + B_HEADS, _band_c0(512) + 2 * 512, 256, _band_c0(512), 1,
                       "causal", key_axis=0)
    bands_b = [bias_band(rel_bias, B_HEADS, A_HEADS, DIL_STEPS, 2 * DIL_STEPS, DIL_STEPS, dil, "window")
               for _, dil in DILATED_PAIRS]

    xs = x.reshape(S, D)
    for layer in range(depth):
        wl = w_in[layer]
        w_pa = cols(wl, 0, 1).astype(BF16)
        w_pb = cols(wl, 6, 7).astype(BF16)
        w_pc = cols(wl, 9, 10).astype(BF16)
        w_pv = cols(wl, 8).astype(BF16)
        w_pi = cols(wl, 3, 4, 4).astype(BF16)
        w_vt = cols(wl, 2, 11).T.astype(BF16)
        w_wt = cols(wl, 5).T.astype(BF16)
        g_pa = jnp.concatenate([_tile_gain(q_norm_a[layer], A_HEADS), _tile_gain(k_norm_a[layer], A_HEADS)])
        g_pb = jnp.concatenate([_tile_gain(q_norm_b[layer], B_HEADS), _tile_gain(k_norm_b[layer], B_HEADS)])
        g_pc = jnp.concatenate([_tile_gain(q_norm_c[layer], 2 * C_HEADS, C_HALF ** -0.5),
                                _tile_gain(k_norm_c[layer], 2 * C_HEADS)])

        h = rmsnorm(xs, attn_norm_g[layer])
        p_a = project(h, w_pa, g_pa, HEAD_DIM, BF16)
        p_b = project(h, w_pb, g_pb, HEAD_DIM, BF16)
        p_c = project(h, w_pc, g_pc, C_HALF, BF16)
        p_v = project(h, w_pv, jnp.ones((w_pv.shape[1],), F32), 0, BF16)
        p_i = project(h, w_pi, jnp.ones((w_pi.shape[1],), F32), 0, BF16, tn=128)
        vt = project_t(h, w_vt, BF16)
        wt = project_t(h, w_wt, F32)

        oa = dsa_attention(p_a, vt, p_i, wt, band_a)

        branches = [dilated_branch(p_b, p_v, bands_b[g], dil, 0) for g, (_, dil) in enumerate(DILATED_PAIRS)]
        ob = dilated_combine([o for o, _ in branches], [l for _, l in branches])

        lam_init = 0.8 - 0.6 * math.exp(-0.3 * layer)
        lam_params = jnp.stack([lambda_q1[layer], lambda_k1[layer], lambda_q2[layer], lambda_k2[layer]]).astype(F32)
        oc = diff_attention(p_c, vt, band_c, lam_params, diff_subln_g[layer].astype(F32), lam_init, aw)

        wo = w_out[layer].astype(BF16)
        xs = out_projection(xs, oa, ob, oc, wo[:aw], wo[aw:aw + bw], wo[aw + bw:])

        f = layer // 2
        if layer % 2 == 0:
            h2 = rmsnorm(xs, ffn_norm_g[layer])
            xs = ffn_dense(xs, h2, w_dense_gate[f].astype(BF16), w_dense_up[f].astype(BF16),
                           w_dense_down[f].astype(BF16))
        else:
            h2 = rmsnorm(xs, ffn_norm_g[layer])
            combine, sel = moe_router(xs, ffn_norm_g[layer], w_router[f])
            xs = ffn_moe(xs, h2, combine, sel, w_moe_gate[f].astype(BF16), w_moe_up[f].astype(BF16),
                         w_moe_down[f].astype(BF16))
    return xs.reshape(B, S, D)
```

```python
import functools
import math

import jax
import jax.numpy as jnp
import numpy as np
from jax import lax
from jax.experimental import pallas as pl
from jax.experimental.pallas import tpu as pltpu

F32 = jnp.float32
BF16 = jnp.bfloat16

HEAD_DIM = 128
A_HEADS, B_HEADS, C_HEADS = 4, 6, 6
C_HALF = HEAD_DIM // 2
IDX_HEADS, IDX_DIM = 16, 64
TOPK_MAX = 256
DILATED_PAIRS = ((128, 1), (512, 4), (2048, 16))
DIL_STEPS = 128
N_BUCKETS = 32
MAX_DISTANCE = 2048
N_EXPERTS = 8
RMS_EPS = 1e-6
NEG = -1e30
INT_MIN = -(2 ** 31)
LANE = 128
VMEM_LIMIT = 56 * 1024 * 1024

BUCKET_EDGES = tuple(range(16)) + (16,) + tuple(
    int(math.ceil(16.0 * (MAX_DISTANCE / 16.0) ** (k / 16.0))) for k in range(1, 16))
FAR_DIST = BUCKET_EDGES[-1]


def _cparams(sem):
    return pltpu.CompilerParams(dimension_semantics=sem, vmem_limit_bytes=VMEM_LIMIT)


def _dot(a, b):
    return jnp.dot(a, b, preferred_element_type=F32)


def _dot_nt(a, b):
    return lax.dot_general(a, b, (((1,), (1,)), ((), ())), preferred_element_type=F32)


def _rmsnorm_kernel(x_ref, g_ref, o_ref):
    x = x_ref[...]
    ms = jnp.mean(x * x, axis=-1, keepdims=True)
    o_ref[...] = (x * lax.rsqrt(ms + RMS_EPS) * g_ref[...]).astype(o_ref.dtype)


def rmsnorm(x, g, tm=512):
    S, D = x.shape
    return pl.pallas_call(
        _rmsnorm_kernel,
        grid=(S // tm,),
        in_specs=[pl.BlockSpec((tm, D), lambda i: (i, 0)), pl.BlockSpec((1, D), lambda i: (0, 0))],
        out_specs=pl.BlockSpec((tm, D), lambda i: (i, 0)),
        out_shape=jax.ShapeDtypeStruct((S, D), BF16),
        compiler_params=_cparams(("parallel",)),
        name="rmsnorm",
    )(x, g.reshape(1, D))


def _proj_kernel(h_ref, w_ref, g_ref, o_ref, *, group):
    acc = _dot(h_ref[...], w_ref[...])
    tn = acc.shape[1]
    if group == 0:
        o_ref[...] = acc.astype(o_ref.dtype)
        return
    for c in range(tn // LANE):
        a = acc[:, c * LANE:(c + 1) * LANE]
        sq = a * a
        if group == LANE:
            ms = jnp.mean(sq, axis=-1, keepdims=True)
        else:
            lane = lax.broadcasted_iota(jnp.int32, sq.shape, 1)
            lo = jnp.sum(jnp.where(lane < group, sq, 0.0), axis=-1, keepdims=True)
            hi = jnp.sum(jnp.where(lane >= group, sq, 0.0), axis=-1, keepdims=True)
            ms = jnp.where(lane < group, lo, hi) * (1.0 / group)
        y = a * lax.rsqrt(ms + RMS_EPS) * g_ref[:, c * LANE:(c + 1) * LANE]
        o_ref[:, c * LANE:(c + 1) * LANE] = y.astype(o_ref.dtype)


def project(h, w, gains, group, out_dtype, tm=1024, tn=256):
    S, K = h.shape
    N = w.shape[1]
    return pl.pallas_call(
        functools.partial(_proj_kernel, group=group),
        grid=(S // tm, N // tn),
        in_specs=[pl.BlockSpec((tm, K), lambda i, j: (i, 0)),
                  pl.BlockSpec((K, tn), lambda i, j: (0, j)),
                  pl.BlockSpec((1, tn), lambda i, j: (0, j))],
        out_specs=pl.BlockSpec((tm, tn), lambda i, j: (i, j)),
        out_shape=jax.ShapeDtypeStruct((S, N), out_dtype),
        compiler_params=_cparams(("parallel", "arbitrary")),
        name=f"proj_g{group}",
    )(h, w, gains.reshape(1, N))


def _proj_t_kernel(wt_ref, h_ref, o_ref):
    o_ref[...] = _dot_nt(wt_ref[...], h_ref[...]).astype(o_ref.dtype)


def project_t(h, wt, out_dtype, tm=1024, tn=256):
    S, K = h.shape
    N = wt.shape[0]
    tn = min(tn, N)
    return pl.pallas_call(
        _proj_t_kernel,
        grid=(S // tm, N // tn),
        in_specs=[pl.BlockSpec((tn, K), lambda i, j: (j, 0)),
                  pl.BlockSpec((tm, K), lambda i, j: (i, 0))],
        out_specs=pl.BlockSpec((tn, tm), lambda i, j: (j, i)),
        out_shape=jax.ShapeDtypeStruct((N, S), out_dtype),
        compiler_params=_cparams(("parallel", "arbitrary")),
        name="proj_t",
    )(wt, h)


BAND_ROWS = 256


def _band_kernel(tab_ref, o_ref, *, c0, head_off, dil):
    hh = head_off + pl.program_id(0)
    _, rows, width = o_ref.shape
    dist = (lax.broadcasted_iota(jnp.int32, (rows, width), 1)
            - lax.broadcasted_iota(jnp.int32, (rows, width), 0) + (c0 - pl.program_id(1) * rows))
    val = jnp.full((rows, width), tab_ref[0, hh], F32)
    for b in range(1, N_BUCKETS):
        val = jnp.where(dist >= BUCKET_EDGES[b], tab_ref[b, hh], val)
    ok = dist >= 0
    if dil:
        ok = ok & (dist <= DIL_STEPS * dil) & ((dist & (dil - 1)) == 0)
    o_ref[0] = jnp.where(ok, val, NEG)


def bias_band(rel_bias, n_heads, head_off, rows, width, c0, dil=0):
    assert rows % BAND_ROWS == 0 and dil & (dil - 1) == 0
    return pl.pallas_call(
        functools.partial(_band_kernel, c0=c0, head_off=head_off, dil=dil),
        grid=(n_heads, rows // BAND_ROWS),
        in_specs=[pl.BlockSpec(memory_space=pltpu.SMEM)],
        out_specs=pl.BlockSpec((1, BAND_ROWS, width), lambda h, r: (h, r, 0)),
        out_shape=jax.ShapeDtypeStruct((n_heads, rows, width), F32),
        compiler_params=_cparams(("parallel", "parallel")),
        name=f"bias_band_d{dil}",
    )(rel_bias)


def _band_c0(tk):
    return -(-(FAR_DIST + tk - 1) // LANE) * LANE


def _dsa_kernel(q_ref, qi_ref, wt_ref, k_ref, vt_ref, kidx_ref, band_ref, o_ref,
                keys_ref, qih_ref, s_ref, acc_ref, m_ref, l_ref, *, tq, kc, topk, c0):
    i = pl.program_id(0)
    t0 = i * tq
    nchunk = (t0 + tq + kc - 1) // kc
    half = lax.broadcasted_iota(jnp.int32, (tq, LANE), 1) < IDX_DIM

    for h in range(IDX_HEADS):
        slab = qi_ref[:, (h // 2) * LANE:(h // 2 + 1) * LANE]
        keep = half if h % 2 == 0 else jnp.logical_not(half)
        qih_ref[h] = jnp.where(keep, slab, jnp.zeros_like(slab))
    w = wt_ref[...] * (IDX_HEADS ** -0.5 * IDX_DIM ** -0.5)

    pos_q = t0 + lax.broadcasted_iota(jnp.int32, (kc, tq), 1)
    row = lax.broadcasted_iota(jnp.int32, (kc, tq), 0)

    def score_chunk(c, carry):
        s0 = pl.multiple_of(c * kc, kc)
        kx = kidx_ref[pl.ds(s0, kc), :]
        acc = jnp.zeros((kc, tq), F32)
        for h in range(IDX_HEADS):
            acc = acc + w[h:h + 1, :] * jnp.maximum(_dot_nt(kx, qih_ref[h]), 0.0)
        bits = lax.bitcast_convert_type(acc, jnp.int32)
        key = bits ^ ((bits >> 31) & 0x7FFFFFFF)
        keys_ref[pl.ds(s0, kc), :] = jnp.where(s0 + row <= pos_q, key, INT_MIN)
        return carry

    lax.fori_loop(0, nchunk, score_chunk, 0)

    lanes_cnt = 32

    def count_ge(cand):
        def body(c, cnt):
            base = pl.multiple_of(c * kc, kc)
            for g in range(kc // lanes_cnt):
                kk = keys_ref[pl.ds(base + g * lanes_cnt, lanes_cnt), :]
                cnt = cnt + jnp.where(kk >= cand, 1.0, 0.0)
            return cnt

        cnt = lax.fori_loop(0, nchunk, body, jnp.zeros((lanes_cnt, tq), F32))
        return jnp.sum(cnt, axis=0, keepdims=True)

    kf = float(topk)
    thr = jnp.where(count_ge(jnp.zeros((1, tq), jnp.int32)) >= kf, 0, INT_MIN).astype(jnp.int32)

    def bit_body(b, t):
        cand = t + jnp.left_shift(jnp.int32(1), 30 - b)
        return jnp.where(count_ge(cand) >= kf, cand, t)

    thr = lax.fori_loop(0, 31, bit_body, thr)
    thr = jnp.maximum(thr, INT_MIN + 1)

    m_ref[...] = jnp.full(m_ref.shape, NEG, F32)
    l_ref[...] = jnp.zeros(l_ref.shape, F32)
    acc_ref[...] = jnp.zeros(acc_ref.shape, F32)
    scale = HEAD_DIM ** -0.5

    def logits(c):
        s0 = pl.multiple_of(jnp.minimum(c, nchunk - 1) * kc, kc)
        off = pl.multiple_of(c0 - jnp.clip(t0 - c * kc, -kc, c0), LANE)
        sel = keys_ref[pl.ds(s0, kc), :] >= thr
        parts = []
        for h in range(A_HEADS):
            hs = slice(h * HEAD_DIM, (h + 1) * HEAD_DIM)
            s = _dot_nt(k_ref[pl.ds(s0, kc), hs], q_ref[:, hs]) * scale + band_ref[h, pl.ds(off, kc), :]
            parts.append(jnp.where(sel, s, NEG))
        return jnp.concatenate(parts, axis=1)

    def consume(c, s):
        s0 = pl.multiple_of(jnp.minimum(c, nchunk - 1) * kc, kc)
        m_prev = m_ref[...]
        m_new = jnp.maximum(m_prev, jnp.max(s, axis=0, keepdims=True))
        alpha = jnp.exp(m_prev - m_new)
        p = jnp.exp(s - m_new)
        l_ref[...] = alpha * l_ref[...] + jnp.sum(p, axis=0, keepdims=True)
        p = p.astype(BF16)
        pv = [_dot(vt_ref[h * HEAD_DIM:(h + 1) * HEAD_DIM, pl.ds(s0, kc)], p[:, h * tq:(h + 1) * tq])
              for h in range(A_HEADS)]
        acc_ref[...] = alpha * acc_ref[...] + jnp.concatenate(pv, axis=1)
        m_ref[...] = m_new

    s_ref[0] = logits(0)

    def chunk_pair(cc, carry):
        c = 2 * cc
        s_ref[1] = logits(c + 1)
        consume(c, s_ref[0])
        s_ref[0] = logits(c + 2)
        consume(c + 1, s_ref[1])
        return carry

    lax.fori_loop(0, (nchunk + 1) // 2, chunk_pair, 0)
    o = acc_ref[...] / l_ref[...]
    for h in range(A_HEADS):
        o_ref[:, h * HEAD_DIM:(h + 1) * HEAD_DIM] = o[:, h * tq:(h + 1) * tq].T.astype(o_ref.dtype)


def dsa_attention(qk, vt, pidx, wt, band, tq=256, kc=512):
    S = qk.shape[0]
    topk = min(TOPK_MAX, S // 4)
    aw = A_HEADS * HEAD_DIM
    iw = IDX_HEADS * IDX_DIM
    once = pl.Buffered(1)
    return pl.pallas_call(
        functools.partial(_dsa_kernel, tq=tq, kc=kc, topk=topk, c0=_band_c0(kc)),
        grid=(S // tq,),
        in_specs=[pl.BlockSpec((tq, aw), lambda i: (i, 0)),
                  pl.BlockSpec((tq, iw), lambda i: (i, 0)),
                  pl.BlockSpec((IDX_HEADS, tq), lambda i: (0, i)),
                  pl.BlockSpec((S, aw), lambda i: (0, 1), pipeline_mode=once),
                  pl.BlockSpec((aw, S), lambda i: (0, 0), pipeline_mode=once),
                  pl.BlockSpec((S, LANE), lambda i: (0, iw // LANE), pipeline_mode=once),
                  pl.BlockSpec(band.shape, lambda i: (0, 0, 0), pipeline_mode=once)],
        out_specs=pl.BlockSpec((tq, aw), lambda i: (i, 0)),
        out_shape=jax.ShapeDtypeStruct((S, aw), BF16),
        scratch_shapes=[pltpu.VMEM((S, tq), jnp.int32),
                        pltpu.VMEM((IDX_HEADS, tq, LANE), BF16),
                        pltpu.VMEM((2, kc, A_HEADS * tq), F32),
                        pltpu.VMEM((HEAD_DIM, A_HEADS * tq), F32),
                        pltpu.VMEM((1, A_HEADS * tq), F32),
                        pltpu.VMEM((1, A_HEADS * tq), F32)],
        compiler_params=_cparams(("arbitrary",)),
        name="dsa_attention",
    )(qk, pidx, wt, qk, vt, pidx, band)


DIL_ROWS = 384


def _dilated_kernel(q_ref, k_ref, vt_ref, b0_ref, b1_ref, b2_ref, o_ref, s_ref, *, tq):
    t0 = pl.program_id(1) * tq
    q = q_ref[...]
    scale = HEAD_DIM ** -0.5
    tiles = []
    row0 = 0
    for band_ref, (_, dil) in zip((b0_ref, b1_ref, b2_ref), DILATED_PAIRS):
        span = DIL_STEPS * dil
        start = jnp.maximum(t0 - span, 0)
        off = span - (t0 - start)
        for j in range((span + tq) // DIL_ROWS):
            ks = pl.multiple_of(start + j * DIL_ROWS, LANE)
            bs = pl.multiple_of(off + j * DIL_ROWS, LANE)
            s_ref[row0:row0 + DIL_ROWS, :] = (_dot_nt(k_ref[pl.ds(ks, DIL_ROWS), :], q) * scale
                                              + band_ref[0, pl.ds(bs, DIL_ROWS), :])
            tiles.append((ks, row0))
            row0 += DIL_ROWS
    m = jnp.full((1, tq), NEG, F32)
    l = jnp.zeros((1, tq), F32)
    acc = jnp.zeros((HEAD_DIM, tq), F32)
    for ks, r0 in tiles:
        s = s_ref[r0:r0 + DIL_ROWS, :]
        m_new = jnp.maximum(m, jnp.max(s, axis=0, keepdims=True))
        alpha = jnp.exp(m - m_new)
        p = jnp.exp(s - m_new)
        l = alpha * l + jnp.sum(p, axis=0, keepdims=True)
        acc = alpha * acc + _dot(vt_ref[:, pl.ds(ks, DIL_ROWS)], p.astype(BF16))
        m = m_new
    o_ref[...] = (acc / l).T.astype(o_ref.dtype)


def dilated_attention(qk, vt, bands, vt_row0, tq=256):
    S = qk.shape[0]
    vh0 = vt_row0 // HEAD_DIM
    lengths = [DIL_STEPS * dil + tq for _, dil in DILATED_PAIRS]
    assert all(n % DIL_ROWS == 0 and n <= S for n in lengths)
    return pl.pallas_call(
        functools.partial(_dilated_kernel, tq=tq),
        grid=(B_HEADS, S // tq),
        in_specs=[pl.BlockSpec((tq, LANE), lambda h, i: (i, h)),
                  pl.BlockSpec((S, LANE), lambda h, i: (0, B_HEADS + h)),
                  pl.BlockSpec((HEAD_DIM, S), lambda h, i: (vh0 + h, 0))]
                 + [pl.BlockSpec((1,) + b.shape[1:], lambda h, i: (h, 0, 0)) for b in bands],
        out_specs=pl.BlockSpec((tq, LANE), lambda h, i: (i, h)),
        out_shape=jax.ShapeDtypeStruct((S, B_HEADS * HEAD_DIM), BF16),
        scratch_shapes=[pltpu.VMEM((sum(lengths), tq), F32)],
        compiler_params=_cparams(("parallel", "arbitrary")),
        name="dilated_attention",
    )(qk, qk, vt, *bands)


def _diff_kernel(lam_ref, g_ref, q_ref, k_ref, vt_ref, band_ref, o_ref, acc_ref, m_ref, l_ref, s_ref,
                 *, tq, tk, c0, lam_init):
    i = pl.program_id(1)
    t0 = i * tq
    nchunk = (t0 + tq + tk - 1) // tk
    last_chunk = k_ref.shape[0] // tk - 1
    q = q_ref[...]
    lane = lax.broadcasted_iota(jnp.int32, q.shape, 1)
    zero = jnp.zeros_like(q)
    q2 = jnp.concatenate([jnp.where(lane < C_HALF, q, zero), jnp.where(lane >= C_HALF, q, zero)], axis=0)

    m_ref[...] = jnp.full(m_ref.shape, NEG, F32)
    l_ref[...] = jnp.zeros(l_ref.shape, F32)
    acc_ref[...] = jnp.zeros(acc_ref.shape, F32)

    def logits(c):
        s0 = pl.multiple_of(jnp.minimum(c, last_chunk) * tk, tk)
        off = pl.multiple_of(c0 - jnp.clip(t0 - c * tk, -tk, c0), LANE)
        bias = band_ref[0, pl.ds(off, tk), :]
        return _dot_nt(k_ref[pl.ds(s0, tk), :], q2) + jnp.concatenate([bias, bias], axis=1)

    def consume(c, s):
        s0 = pl.multiple_of(jnp.minimum(c, last_chunk) * tk, tk)
        m_prev = m_ref[...]
        m_new = jnp.maximum(m_prev, jnp.max(s, axis=0, keepdims=True))
        alpha = jnp.exp(m_prev - m_new)
        p = jnp.exp(s - m_new)
        l_ref[...] = alpha * l_ref[...] + jnp.sum(p, axis=0, keepdims=True)
        acc_ref[...] = alpha * acc_ref[...] + _dot(vt_ref[:, pl.ds(s0, tk)], p.astype(BF16))
        m_ref[...] = m_new

    s_ref[0] = logits(0)

    def chunk_pair(cc, carry):
        c = 2 * cc
        s_ref[1] = logits(c + 1)
        consume(c, s_ref[0])
        s_ref[0] = logits(c + 2)
        consume(c + 1, s_ref[1])
        return carry

    lax.fori_loop(0, (nchunk + 1) // 2, chunk_pair, 0)

    lp = lam_ref[...]
    lam = (jnp.exp(jnp.sum(lp[0:1] * lp[1:2], axis=1, keepdims=True))
           - jnp.exp(jnp.sum(lp[2:3] * lp[3:4], axis=1, keepdims=True)) + lam_init)
    o = acc_ref[...] / l_ref[...]
    o = o[:, :tq] - lam * o[:, tq:]
    ms = jnp.mean(o * o, axis=0, keepdims=True)
    o = o * lax.rsqrt(ms + RMS_EPS) * (g_ref[...] * (1.0 - lam_init))
    o_ref[...] = o.T.astype(o_ref.dtype)


def diff_attention(qk, vt, band, lam_params, subln_g, lam_init, vt_row0, tq=256, tk=512):
    S = qk.shape[0]
    vh0 = vt_row0 // HEAD_DIM
    return pl.pallas_call(
        functools.partial(_diff_kernel, tq=tq, tk=tk, c0=_band_c0(tk), lam_init=lam_init),
        grid=(C_HEADS, S // tq),
        in_specs=[pl.BlockSpec((4, C_HALF), lambda h, i: (0, 0)),
                  pl.BlockSpec((HEAD_DIM, 1), lambda h, i: (0, 0)),
                  pl.BlockSpec((tq, LANE), lambda h, i: (i, h)),
                  pl.BlockSpec((S, LANE), lambda h, i: (0, C_HEADS + h)),
                  pl.BlockSpec((HEAD_DIM, S), lambda h, i: (vh0 + h, 0)),
                  pl.BlockSpec((1,) + band.shape[1:], lambda h, i: (h, 0, 0))],
        out_specs=pl.BlockSpec((tq, LANE), lambda h, i: (i, h)),
        out_shape=jax.ShapeDtypeStruct((S, C_HEADS * HEAD_DIM), BF16),
        scratch_shapes=[pltpu.VMEM((HEAD_DIM, 2 * tq), F32),
                        pltpu.VMEM((1, 2 * tq), F32),
                        pltpu.VMEM((1, 2 * tq), F32),
                        pltpu.VMEM((2, tk, 2 * tq), F32)],
        compiler_params=_cparams(("parallel", "arbitrary")),
        name="diff_attention",
    )(lam_params, subln_g.reshape(HEAD_DIM, 1), qk, qk, vt, band)


def _outproj_kernel(x_ref, a_ref, b_ref, c_ref, wa_ref, wb_ref, wc_ref, o_ref):
    o_ref[...] = (x_ref[...] + _dot(a_ref[...], wa_ref[...]) + _dot(b_ref[...], wb_ref[...])
                  + _dot(c_ref[...], wc_ref[...]))


def out_projection(x, oa, ob, oc, wa, wb, wc, tm=512, tn=1024):
    S, D = x.shape
    tn = min(tn, D)
    return pl.pallas_call(
        _outproj_kernel,
        grid=(S // tm, D // tn),
        in_specs=[pl.BlockSpec((tm, tn), lambda i, j: (i, j)),
                  pl.BlockSpec((tm, oa.shape[1]), lambda i, j: (i, 0)),
                  pl.BlockSpec((tm, ob.shape[1]), lambda i, j: (i, 0)),
                  pl.BlockSpec((tm, oc.shape[1]), lambda i, j: (i, 0)),
                  pl.BlockSpec((wa.shape[0], tn), lambda i, j: (0, j)),
                  pl.BlockSpec((wb.shape[0], tn), lambda i, j: (0, j)),
                  pl.BlockSpec((wc.shape[0], tn), lambda i, j: (0, j))],
        out_specs=pl.BlockSpec((tm, tn), lambda i, j: (i, j)),
        out_shape=jax.ShapeDtypeStruct((S, D), F32),
        compiler_params=_cparams(("parallel", "arbitrary")),
        name="out_projection",
    )(x, oa, ob, oc, wa, wb, wc)


def _silu(x):
    return x / (1.0 + jnp.exp(-x))


def _ffn_kernel(x_ref, h_ref, wg_ref, wu_ref, wd_ref, o_ref):
    f = pl.program_id(1)

    @pl.when(f == 0)
    def _():
        o_ref[...] = x_ref[...]

    h = h_ref[...]
    a = _silu(_dot(h, wg_ref[...])) * _dot(h, wu_ref[...])
    o_ref[...] += _dot(a.astype(BF16), wd_ref[...])


def ffn_dense(x, h, wg, wu, wd, tm=512, tf=512):
    S, D = x.shape
    F = wg.shape[1]
    return pl.pallas_call(
        _ffn_kernel,
        grid=(S // tm, F // tf),
        in_specs=[pl.BlockSpec((tm, D), lambda i, f: (i, 0)),
                  pl.BlockSpec((tm, D), lambda i, f: (i, 0)),
                  pl.BlockSpec((D, tf), lambda i, f: (0, f)),
                  pl.BlockSpec((D, tf), lambda i, f: (0, f)),
                  pl.BlockSpec((tf, D), lambda i, f: (f, 0))],
        out_specs=pl.BlockSpec((tm, D), lambda i, f: (i, 0)),
        out_shape=jax.ShapeDtypeStruct((S, D), F32),
        compiler_params=_cparams(("parallel", "arbitrary")),
        name="ffn_dense",
    )(x, h, wg, wu, wd)


def _router_kernel(x_ref, g_ref, w_ref, c_ref, sel_ref):
    x = x_ref[...]
    ms = jnp.mean(x * x, axis=-1, keepdims=True)
    h = x * lax.rsqrt(ms + RMS_EPS) * g_ref[...]
    logits = jnp.dot(h, w_ref[...], preferred_element_type=F32, precision=lax.Precision.HIGHEST)
    lane = lax.broadcasted_iota(jnp.int32, logits.shape, 1).astype(F32)
    logits = jnp.where(lane < N_EXPERTS, logits, NEG)
    m1 = jnp.max(logits, axis=1, keepdims=True)
    i1 = jnp.min(jnp.where(logits == m1, lane, float(LANE)), axis=1, keepdims=True)
    rest = jnp.where(lane == i1, NEG, logits)
    m2 = jnp.max(rest, axis=1, keepdims=True)
    i2 = jnp.min(jnp.where(rest == m2, lane, float(LANE)), axis=1, keepdims=True)
    e2 = jnp.exp(m2 - m1)
    g1 = 1.0 / (1.0 + e2)
    g2 = e2 / (1.0 + e2)
    pick1, pick2 = lane == i1, lane == i2
    c_ref[...] = jnp.where(pick1, g1, 0.0) + jnp.where(pick2, g2, 0.0)
    sel_ref[...] = jnp.where(pick1 | pick2, 1.0, 0.0)


def moe_router(x, g, w_router, tm=512):
    S, D = x.shape
    wpad = jnp.zeros((D, LANE), F32).at[:, :N_EXPERTS].set(w_router)
    spec = pl.BlockSpec((tm, LANE), lambda i: (i, 0))
    sds = jax.ShapeDtypeStruct((S, LANE), F32)
    return pl.pallas_call(
        _router_kernel,
        grid=(S // tm,),
        in_specs=[pl.BlockSpec((tm, D), lambda i: (i, 0)),
                  pl.BlockSpec((1, D), lambda i: (0, 0)),
                  pl.BlockSpec((D, LANE), lambda i: (0, 0))],
        out_specs=[spec, spec],
        out_shape=[sds, sds],
        compiler_params=_cparams(("parallel",)),
        name="moe_router",
    )(x, g.reshape(1, D), wpad)


MOE_CH = 256
MOE_TILE = 512


def _moe_rank_kernel(sel_ref, rm_ref, rmt_ref, cum_ref, tot_ref, carry_ref):
    c = pl.program_id(0)

    @pl.when(c == 0)
    def _():
        carry_ref[...] = jnp.zeros(carry_ref.shape, F32)

    sel = sel_ref[...]
    ch = sel.shape[0]
    before = (lax.broadcasted_iota(jnp.int32, (ch, ch), 1) < lax.broadcasted_iota(jnp.int32, (ch, ch), 0))
    rank = _dot(jnp.where(before, 1.0, 0.0).astype(BF16), sel.astype(BF16)) + carry_ref[...]
    rm = jnp.where(sel > 0.0, rank, -1.0)
    rm_ref[...] = rm
    rmt_ref[...] = rm.T[:N_EXPERTS]
    cum_ref[0] = carry_ref[...]
    carry_ref[...] += jnp.sum(sel, axis=0, keepdims=True)
    tot_ref[...] = carry_ref[...]


def moe_rank(sel):
    S = sel.shape[0]
    nch = S // MOE_CH
    return pl.pallas_call(
        _moe_rank_kernel,
        grid=(nch,),
        in_specs=[pl.BlockSpec((MOE_CH, LANE), lambda c: (c, 0))],
        out_specs=[pl.BlockSpec((MOE_CH, LANE), lambda c: (c, 0)),
                   pl.BlockSpec((N_EXPERTS, MOE_CH), lambda c: (0, c)),
                   pl.BlockSpec((1, 1, LANE), lambda c: (c, 0, 0)),
                   pl.BlockSpec((1, LANE), lambda c: (0, 0))],
        out_shape=[jax.ShapeDtypeStruct((S, LANE), F32),
                   jax.ShapeDtypeStruct((N_EXPERTS, S), F32),
                   jax.ShapeDtypeStruct((nch, 1, LANE), F32),
                   jax.ShapeDtypeStruct((1, LANE), F32)],
        scratch_shapes=[pltpu.VMEM((1, LANE), F32)],
        compiler_params=_cparams(("arbitrary",)),
        name="moe_rank",
    )(sel)


def _moe_expert_kernel(te_ref, tr0_ref, tclo_ref, tchi_ref, tval_ref,
                       rk_ref, h_hbm, wg_ref, wu_ref, wd_ref, y_ref,
                       hs_ref, acc_ref, hbuf_ref, sem_ref):
    j = pl.program_id(0)
    f = pl.program_id(1)
    nf = pl.num_programs(1)
    valid = tval_ref[j] > 0
    rows = hs_ref.shape[0]

    def chunk_copy(c, slot):
        return pltpu.make_async_copy(h_hbm.at[pl.ds(c * MOE_CH, MOE_CH)], hbuf_ref.at[slot], sem_ref.at[slot])

    @pl.when(valid & (f == 0))
    def _():
        clo, chi = tclo_ref[j], tchi_ref[j]
        want = (tr0_ref[j] + lax.broadcasted_iota(jnp.int32, (rows, MOE_CH), 0)).astype(F32)
        acc_ref[...] = jnp.zeros(acc_ref.shape, F32)

        @pl.when(chi > clo)
        def _():
            chunk_copy(clo, 0).start()

        def body(c, carry):
            slot = (c - clo) % 2

            @pl.when(c + 1 < chi)
            def _():
                chunk_copy(c + 1, 1 - slot).start()

            chunk_copy(c, slot).wait()
            rk = rk_ref[0, :, pl.ds(pl.multiple_of(c * MOE_CH, MOE_CH), MOE_CH)]
            onehot = jnp.where(rk == want, 1.0, 0.0).astype(BF16)
            acc_ref[...] += _dot(onehot, hbuf_ref[slot])
            return carry

        lax.fori_loop(clo, chi, body, 0)
        hs_ref[...] = acc_ref[...].astype(BF16)
        acc_ref[...] = jnp.zeros(acc_ref.shape, F32)

    @pl.when(valid)
    def _():
        hs = hs_ref[...]
        a = _silu(_dot(hs, wg_ref[0])) * _dot(hs, wu_ref[0])
        acc_ref[...] += _dot(a.astype(BF16), wd_ref[0])

    @pl.when(f == nf - 1)
    def _():
        y_ref[...] = jnp.where(valid, acc_ref[...], 0.0).astype(y_ref.dtype)


def _moe_combine_kernel(ca_ref, cb_ref, roff_ref, x_ref, rm_ref, cw_ref, ya_ref, yb_ref, o_ref):
    c = pl.program_id(0)
    e = pl.program_id(1)

    @pl.when(e == 0)
    def _():
        o_ref[...] = x_ref[...]

    rm = rm_ref[...]
    lane = lax.broadcasted_iota(jnp.int32, rm.shape, 1)
    mine = lane == e
    pos = jnp.sum(jnp.where(mine, rm, 0.0), axis=1, keepdims=True)
    gate = jnp.sum(jnp.where(mine, cw_ref[...], 0.0), axis=1, keepdims=True)
    row = jnp.where(pos >= 0.0, pos + roff_ref[e].astype(F32), -1.0)
    k = c * pl.num_programs(1) + e
    ca, cb = ca_ref[k], cb_ref[k]
    cr = ya_ref.shape[0]
    col = lax.broadcasted_iota(jnp.int32, (rm.shape[0], cr), 1)
    for y_ref, base in ((ya_ref, ca * cr), (yb_ref, jnp.where(cb != ca, cb * cr, -2 * cr))):
        onehot = jnp.where(row == (base + col).astype(F32), 1.0, 0.0).astype(BF16)
        o_ref[...] += gate * _dot(onehot, y_ref[...])


def ffn_moe(x, h, combine, sel, wg, wu, wd, tf=512):
    S, D = x.shape
    E, _, F = wg.shape
    T = MOE_TILE
    nch = S // MOE_CH
    J = 2 * S // T + E
    rm, rmt, cum, tot = moe_rank(sel)

    cnt = tot[0, :E].astype(jnp.int32)
    ntile = (cnt + T - 1) // T
    tend = jnp.cumsum(ntile)
    tstart = tend - ntile
    jj = jnp.arange(J, dtype=jnp.int32)
    tval = (jj < tend[-1]).astype(jnp.int32)
    te = jnp.minimum(jnp.searchsorted(tend, jnp.minimum(jj, tend[-1] - 1), side="right"), E - 1).astype(jnp.int32)
    tr0 = (jnp.minimum(jj, tend[-1] - 1) - tstart[te]) * T
    cumx = cum[:, 0, :E].astype(jnp.int32)
    cumi = jnp.concatenate([cumx[1:], cnt[None]], axis=0)
    tclo = jnp.sum(cumi[:, te] <= tr0[None, :], axis=0).astype(jnp.int32)
    tchi = jnp.sum(cumx[:, te] < (tr0 + T)[None, :], axis=0).astype(jnp.int32)
    roff = (tstart * T).astype(jnp.int32)
    nrow_chunks = J * T // MOE_CH
    ca = ((roff[None, :] + cumx) // MOE_CH).astype(jnp.int32)
    cb = jnp.minimum(ca + 1, nrow_chunks - 1)
    ca, cb = ca.reshape(-1), cb.reshape(-1)

    nf = F // tf

    def wmap(j, f, te_r, tr0_r, clo_r, chi_r, val_r):
        return (te_r[j], 0, jnp.where(val_r[j] > 0, f, nf - 1))

    def wdmap(j, f, te_r, tr0_r, clo_r, chi_r, val_r):
        return (te_r[j], jnp.where(val_r[j] > 0, f, nf - 1), 0)

    y = pl.pallas_call(
        _moe_expert_kernel,
        grid_spec=pltpu.PrefetchScalarGridSpec(
            num_scalar_prefetch=5,
            grid=(J, nf),
            in_specs=[pl.BlockSpec((1, 1, S), lambda j, f, te_r, *_: (te_r[j], 0, 0)),
                      pl.BlockSpec(memory_space=pl.ANY),
                      pl.BlockSpec((1, D, tf), wmap),
                      pl.BlockSpec((1, D, tf), wmap),
                      pl.BlockSpec((1, tf, D), wdmap)],
            out_specs=pl.BlockSpec((T, D), lambda j, f, *_: (j, 0)),
            scratch_shapes=[pltpu.VMEM((T, D), BF16),
                            pltpu.VMEM((T, D), F32),
                            pltpu.VMEM((2, MOE_CH, D), BF16),
                            pltpu.SemaphoreType.DMA((2,))]),
        out_shape=jax.ShapeDtypeStruct((J * T, D), BF16),
        compiler_params=_cparams(("arbitrary", "arbitrary")),
        name="moe_experts",
    )(te, tr0, tclo, tchi, tval, rmt.reshape(E, 1, S), h, wg, wu, wd)

    return pl.pallas_call(
        _moe_combine_kernel,
        grid_spec=pltpu.PrefetchScalarGridSpec(
            num_scalar_prefetch=3,
            grid=(nch, E),
            in_specs=[pl.BlockSpec((MOE_CH, D), lambda c, e, *_: (c, 0)),
                      pl.BlockSpec((MOE_CH, LANE), lambda c, e, *_: (c, 0)),
                      pl.BlockSpec((MOE_CH, LANE), lambda c, e, *_: (c, 0)),
                      pl.BlockSpec((MOE_CH, D), lambda c, e, ca_r, cb_r, ro_r: (ca_r[c * E + e], 0)),
                      pl.BlockSpec((MOE_CH, D), lambda c, e, ca_r, cb_r, ro_r: (cb_r[c * E + e], 0))],
            out_specs=pl.BlockSpec((MOE_CH, D), lambda c, e, *_: (c, 0))),
        out_shape=jax.ShapeDtypeStruct((S, D), F32),
        compiler_params=_cparams(("parallel", "arbitrary")),
        name="moe_combine",
    )(ca, cb, roff, x, rm, combine, y, y)


def _tile_gain(g, reps, scale=1.0):
    return jnp.tile(g.astype(F32) * scale, reps)


def kernel(x, w_in, w_out, attn_norm_g, ffn_norm_g, q_norm_a, k_norm_a, q_norm_b, k_norm_b,
           q_norm_c, k_norm_c, lambda_q1, lambda_k1, lambda_q2, lambda_k2, diff_subln_g, rel_bias,
           w_dense_gate, w_dense_up, w_dense_down, w_router, w_moe_gate, w_moe_up, w_moe_down):
    B, S, D = x.shape
    depth = w_in.shape[0]
    assert B == 1 and S % (DIL_STEPS * DILATED_PAIRS[-1][1]) == 0
    aw, bw, cw = A_HEADS * HEAD_DIM, B_HEADS * HEAD_DIM, C_HEADS * HEAD_DIM
    iw = IDX_HEADS * IDX_DIM
    sizes = (aw, aw, aw, iw, IDX_DIM, IDX_HEADS, bw, bw, bw, cw, cw, cw)
    offs = np.concatenate([[0], np.cumsum(sizes)]).tolist()

    def cols(w, *segs):
        return jnp.concatenate([w[:, offs[s]:offs[s + 1]] for s in segs], axis=1)

    tq, tk = 256, 512
    band_a = bias_band(rel_bias, A_HEADS, 0, _band_c0(tk) + 2 * tk, tq, _band_c0(tk))
    band_c = bias_band(rel_bias, C_HEADS, A_HEADS + B_HEADS, _band_c0(tk) + 2 * tk, tq, _band_c0(tk))
    bands_b = [bias_band(rel_bias, B_HEADS, A_HEADS, 2 * DIL_STEPS * dil + tq, tq, DIL_STEPS * dil, dil)
               for _, dil in DILATED_PAIRS]

    xs = x.reshape(S, D)
    for layer in range(depth):
        wl = w_in[layer]
        w_pa = cols(wl, 0, 1).astype(BF16)
        w_pb = cols(wl, 6, 7).astype(BF16)
        w_pc = cols(wl, 9, 10).astype(BF16)
        w_pi = cols(wl, 3, 4, 4).astype(BF16)
        w_vt = cols(wl, 2, 8, 11).T.astype(BF16)
        w_wt = cols(wl, 5).T.astype(BF16)
        g_pa = jnp.concatenate([_tile_gain(q_norm_a[layer], A_HEADS), _tile_gain(k_norm_a[layer], A_HEADS)])
        g_pb = jnp.concatenate([_tile_gain(q_norm_b[layer], B_HEADS), _tile_gain(k_norm_b[layer], B_HEADS)])
        g_pc = jnp.concatenate([_tile_gain(q_norm_c[layer], 2 * C_HEADS, C_HALF ** -0.5),
                                _tile_gain(k_norm_c[layer], 2 * C_HEADS)])

        h = rmsnorm(xs, attn_norm_g[layer])
        p_a = project(h, w_pa, g_pa, HEAD_DIM, BF16)
        p_b = project(h, w_pb, g_pb, HEAD_DIM, BF16)
        p_c = project(h, w_pc, g_pc, C_HALF, BF16)
        p_i = project(h, w_pi, jnp.ones((w_pi.shape[1],), F32), 0, BF16, tn=128)
        vt = project_t(h, w_vt, BF16)
        wt = project_t(h, w_wt, F32)

        oa = dsa_attention(p_a, vt, p_i, wt, band_a)

        ob = dilated_attention(p_b, vt, bands_b, aw, tq=tq)

        lam_init = 0.8 - 0.6 * math.exp(-0.3 * layer)
        lam_params = jnp.stack([lambda_q1[layer], lambda_k1[layer], lambda_q2[layer], lambda_k2[layer]]).astype(F32)
        oc = diff_attention(p_c, vt, band_c, lam_params, diff_subln_g[layer].astype(F32), lam_init, aw + bw,
                            tq=tq, tk=tk)

        wo = w_out[layer].astype(BF16)
        xs = out_projection(xs, oa, ob, oc, wo[:aw], wo[aw:aw + bw], wo[aw + bw:])

        f = layer // 2
        if layer % 2 == 0:
            h2 = rmsnorm(xs, ffn_norm_g[layer])
            xs = ffn_dense(xs, h2, w_dense_gate[f].astype(BF16), w_dense_up[f].astype(BF16),
                           w_dense_down[f].astype(BF16))
        else:
            h2 = rmsnorm(xs, ffn_norm_g[layer])
            combine, sel = moe_router(xs, ffn_norm_g[layer], w_router[f])
            xs = ffn_moe(xs, h2, combine, sel, w_moe_gate[f].astype(BF16), w_moe_up[f].astype(BF16),
                         w_moe_down[f].astype(BF16))
    return xs.reshape(B, S, D)
```

```python
import functools
import math

import jax
import jax.numpy as jnp
import numpy as np
from jax import lax
from jax.experimental import pallas as pl
from jax.experimental.pallas import tpu as pltpu

F32 = jnp.float32
BF16 = jnp.bfloat16

HEAD_DIM = 128
A_HEADS, B_HEADS, C_HEADS = 4, 6, 6
C_HALF = HEAD_DIM // 2
IDX_HEADS, IDX_DIM = 16, 64
TOPK_MAX = 256
DILATED_PAIRS = ((128, 1), (512, 4), (2048, 16))
DIL_STEPS = 128
N_BUCKETS = 32
MAX_DISTANCE = 2048
N_EXPERTS = 8
RMS_EPS = 1e-6
NEG = -1e30
LOG2E = math.log2(math.e)
INT_MIN = -(2 ** 31)
LANE = 128
VMEM_LIMIT = 56 * 1024 * 1024

BUCKET_EDGES = tuple(range(16)) + (16,) + tuple(
    int(math.ceil(16.0 * (MAX_DISTANCE / 16.0) ** (k / 16.0))) for k in range(1, 16))
FAR_DIST = BUCKET_EDGES[-1]


def _cparams(sem):
    return pltpu.CompilerParams(dimension_semantics=sem, vmem_limit_bytes=VMEM_LIMIT)


def _dot(a, b):
    return jnp.dot(a, b, preferred_element_type=F32)


def _dot_nt(a, b):
    return lax.dot_general(a, b, (((1,), (1,)), ((), ())), preferred_element_type=F32)


def _rmsnorm_kernel(x_ref, g_ref, o_ref):
    x = x_ref[...]
    ms = jnp.mean(x * x, axis=-1, keepdims=True)
    o_ref[...] = (x * lax.rsqrt(ms + RMS_EPS) * g_ref[...]).astype(o_ref.dtype)


def rmsnorm(x, g, tm=512):
    S, D = x.shape
    return pl.pallas_call(
        _rmsnorm_kernel,
        grid=(S // tm,),
        in_specs=[pl.BlockSpec((tm, D), lambda i: (i, 0)), pl.BlockSpec((1, D), lambda i: (0, 0))],
        out_specs=pl.BlockSpec((tm, D), lambda i: (i, 0)),
        out_shape=jax.ShapeDtypeStruct((S, D), BF16),
        compiler_params=_cparams(("parallel",)),
        name="rmsnorm",
    )(x, g.reshape(1, D))


def _proj_kernel(h_ref, w_ref, g_ref, o_ref, *, group):
    acc = _dot(h_ref[...], w_ref[...])
    tn = acc.shape[1]
    if group == 0:
        o_ref[...] = acc.astype(o_ref.dtype)
        return
    for c in range(tn // LANE):
        a = acc[:, c * LANE:(c + 1) * LANE]
        sq = a * a
        if group == LANE:
            ms = jnp.mean(sq, axis=-1, keepdims=True)
        else:
            lane = lax.broadcasted_iota(jnp.int32, sq.shape, 1)
            lo = jnp.sum(jnp.where(lane < group, sq, 0.0), axis=-1, keepdims=True)
            hi = jnp.sum(jnp.where(lane >= group, sq, 0.0), axis=-1, keepdims=True)
            ms = jnp.where(lane < group, lo, hi) * (1.0 / group)
        y = a * lax.rsqrt(ms + RMS_EPS) * g_ref[:, c * LANE:(c + 1) * LANE]
        o_ref[:, c * LANE:(c + 1) * LANE] = y.astype(o_ref.dtype)


def project(h, w, gains, group, out_dtype, tm=1024, tn=256):
    S, K = h.shape
    N = w.shape[1]
    return pl.pallas_call(
        functools.partial(_proj_kernel, group=group),
        grid=(S // tm, N // tn),
        in_specs=[pl.BlockSpec((tm, K), lambda i, j: (i, 0)),
                  pl.BlockSpec((K, tn), lambda i, j: (0, j)),
                  pl.BlockSpec((1, tn), lambda i, j: (0, j))],
        out_specs=pl.BlockSpec((tm, tn), lambda i, j: (i, j)),
        out_shape=jax.ShapeDtypeStruct((S, N), out_dtype),
        compiler_params=_cparams(("parallel", "arbitrary")),
        name=f"proj_g{group}",
    )(h, w, gains.reshape(1, N))


def _proj_t_kernel(wt_ref, h_ref, o_ref):
    o_ref[...] = _dot_nt(wt_ref[...], h_ref[...]).astype(o_ref.dtype)


def project_t(h, wt, out_dtype, tm=1024, tn=256):
    S, K = h.shape
    N = wt.shape[0]
    tn = min(tn, N)
    return pl.pallas_call(
        _proj_t_kernel,
        grid=(S // tm, N // tn),
        in_specs=[pl.BlockSpec((tn, K), lambda i, j: (j, 0)),
                  pl.BlockSpec((tm, K), lambda i, j: (i, 0))],
        out_specs=pl.BlockSpec((tn, tm), lambda i, j: (j, i)),
        out_shape=jax.ShapeDtypeStruct((N, S), out_dtype),
        compiler_params=_cparams(("parallel", "arbitrary")),
        name="proj_t",
    )(wt, h)


BAND_ROWS = 256


def _band_kernel(tab_ref, o_ref, *, c0, head_off, dil):
    hh = head_off + pl.program_id(0)
    _, rows, width = o_ref.shape
    dist = (lax.broadcasted_iota(jnp.int32, (rows, width), 1)
            - lax.broadcasted_iota(jnp.int32, (rows, width), 0) + (c0 - pl.program_id(1) * rows))
    val = jnp.full((rows, width), tab_ref[0, hh] * LOG2E, F32)
    for b in range(1, N_BUCKETS):
        val = jnp.where(dist >= BUCKET_EDGES[b], tab_ref[b, hh] * LOG2E, val)
    ok = dist >= 0
    if dil:
        ok = ok & (dist <= DIL_STEPS * dil) & ((dist & (dil - 1)) == 0)
    o_ref[0] = jnp.where(ok, val, NEG)


def bias_band(rel_bias, n_heads, head_off, rows, width, c0, dil=0):
    assert rows % BAND_ROWS == 0 and dil & (dil - 1) == 0
    return pl.pallas_call(
        functools.partial(_band_kernel, c0=c0, head_off=head_off, dil=dil),
        grid=(n_heads, rows // BAND_ROWS),
        in_specs=[pl.BlockSpec(memory_space=pltpu.SMEM)],
        out_specs=pl.BlockSpec((1, BAND_ROWS, width), lambda h, r: (h, r, 0)),
        out_shape=jax.ShapeDtypeStruct((n_heads, rows, width), F32),
        compiler_params=_cparams(("parallel", "parallel")),
        name=f"bias_band_d{dil}",
    )(rel_bias)


def _band_c0(tk):
    return -(-(FAR_DIST + tk - 1) // LANE) * LANE


def _dsa_kernel(q_ref, qi_ref, wt_ref, k_ref, vt_ref, kidx_ref, band_ref, o_ref,
                keys_ref, qih_ref, s_ref, acc_ref, m_ref, l_ref, *, tq, kc, topk, c0):
    i = pl.program_id(0)
    t0 = i * tq
    nchunk = (t0 + tq + kc - 1) // kc
    half = lax.broadcasted_iota(jnp.int32, (tq, LANE), 1) < IDX_DIM

    for h in range(IDX_HEADS):
        slab = qi_ref[:, (h // 2) * LANE:(h // 2 + 1) * LANE]
        keep = half if h % 2 == 0 else jnp.logical_not(half)
        qih_ref[h] = jnp.where(keep, slab, jnp.zeros_like(slab))
    w = wt_ref[...] * (IDX_HEADS ** -0.5 * IDX_DIM ** -0.5)

    pos_q = t0 + lax.broadcasted_iota(jnp.int32, (kc, tq), 1)
    row = lax.broadcasted_iota(jnp.int32, (kc, tq), 0)

    def score_chunk(c, carry):
        s0 = pl.multiple_of(c * kc, kc)
        kx = kidx_ref[pl.ds(s0, kc), :]
        acc = jnp.zeros((kc, tq), F32)
        for h in range(IDX_HEADS):
            acc = acc + w[h:h + 1, :] * jnp.maximum(_dot_nt(kx, qih_ref[h]), 0.0)
        bits = lax.bitcast_convert_type(acc, jnp.int32)
        key = bits ^ ((bits >> 31) & 0x7FFFFFFF)
        keys_ref[pl.ds(s0, kc), :] = jnp.where(s0 + row <= pos_q, key, INT_MIN)
        return carry

    lax.fori_loop(0, nchunk, score_chunk, 0)

    lanes_cnt = 32

    def count_ge(cand):
        def body(c, cnt):
            base = pl.multiple_of(c * kc, kc)
            for g in range(kc // lanes_cnt):
                kk = keys_ref[pl.ds(base + g * lanes_cnt, lanes_cnt), :]
                cnt = cnt + jnp.where(kk >= cand, 1.0, 0.0)
            return cnt

        cnt = lax.fori_loop(0, nchunk, body, jnp.zeros((lanes_cnt, tq), F32))
        return jnp.sum(cnt, axis=0, keepdims=True)

    kf = float(topk)
    thr = jnp.where(count_ge(jnp.zeros((1, tq), jnp.int32)) >= kf, 0, INT_MIN).astype(jnp.int32)

    def bit_body(b, t):
        cand = t + jnp.left_shift(jnp.int32(1), 30 - b)
        return jnp.where(count_ge(cand) >= kf, cand, t)

    thr = lax.fori_loop(0, 31, bit_body, thr)
    thr = jnp.maximum(thr, INT_MIN + 1)

    m_ref[...] = jnp.full(m_ref.shape, NEG, F32)
    l_ref[...] = jnp.zeros(l_ref.shape, F32)
    acc_ref[...] = jnp.zeros(acc_ref.shape, F32)

    def logits(c):
        s0 = pl.multiple_of(jnp.minimum(c, nchunk - 1) * kc, kc)
        off = pl.multiple_of(c0 - jnp.clip(t0 - c * kc, -kc, c0), LANE)
        sel = keys_ref[pl.ds(s0, kc), :] >= thr
        parts = []
        for h in range(A_HEADS):
            hs = slice(h * HEAD_DIM, (h + 1) * HEAD_DIM)
            s = _dot_nt(k_ref[pl.ds(s0, kc), hs], q_ref[:, hs]) + band_ref[h, pl.ds(off, kc), :]
            parts.append(jnp.where(sel, s, NEG))
        return jnp.concatenate(parts, axis=1)

    def consume(c, s):
        s0 = pl.multiple_of(jnp.minimum(c, nchunk - 1) * kc, kc)
        m_prev = m_ref[...]
        m_new = jnp.maximum(m_prev, jnp.max(s, axis=0, keepdims=True))
        alpha = jnp.exp2(m_prev - m_new)
        p = jnp.exp2(s - m_new)
        l_ref[...] = alpha * l_ref[...] + jnp.sum(p, axis=0, keepdims=True)
        p = p.astype(BF16)
        pv = [_dot(vt_ref[h * HEAD_DIM:(h + 1) * HEAD_DIM, pl.ds(s0, kc)], p[:, h * tq:(h + 1) * tq])
              for h in range(A_HEADS)]
        acc_ref[...] = alpha * acc_ref[...] + jnp.concatenate(pv, axis=1)
        m_ref[...] = m_new

    s_ref[0] = logits(0)

    def chunk_pair(cc, carry):
        c = 2 * cc
        s_ref[1] = logits(c + 1)
        consume(c, s_ref[0])
        s_ref[0] = logits(c + 2)
        consume(c + 1, s_ref[1])
        return carry

    lax.fori_loop(0, (nchunk + 1) // 2, chunk_pair, 0)
    o = acc_ref[...] / l_ref[...]
    for h in range(A_HEADS):
        o_ref[:, h * HEAD_DIM:(h + 1) * HEAD_DIM] = o[:, h * tq:(h + 1) * tq].T.astype(o_ref.dtype)


def dsa_attention(qk, vt, pidx, wt, band, tq=256, kc=512):
    S = qk.shape[0]
    topk = min(TOPK_MAX, S // 4)
    aw = A_HEADS * HEAD_DIM
    iw = IDX_HEADS * IDX_DIM
    once = pl.Buffered(1)
    return pl.pallas_call(
        functools.partial(_dsa_kernel, tq=tq, kc=kc, topk=topk, c0=_band_c0(kc)),
        grid=(S // tq,),
        in_specs=[pl.BlockSpec((tq, aw), lambda i: (i, 0)),
                  pl.BlockSpec((tq, iw), lambda i: (i, 0)),
                  pl.BlockSpec((IDX_HEADS, tq), lambda i: (0, i)),
                  pl.BlockSpec((S, aw), lambda i: (0, 1), pipeline_mode=once),
                  pl.BlockSpec((aw, S), lambda i: (0, 0), pipeline_mode=once),
                  pl.BlockSpec((S, LANE), lambda i: (0, iw // LANE), pipeline_mode=once),
                  pl.BlockSpec(band.shape, lambda i: (0, 0, 0), pipeline_mode=once)],
        out_specs=pl.BlockSpec((tq, aw), lambda i: (i, 0)),
        out_shape=jax.ShapeDtypeStruct((S, aw), BF16),
        scratch_shapes=[pltpu.VMEM((S, tq), jnp.int32),
                        pltpu.VMEM((IDX_HEADS, tq, LANE), BF16),
                        pltpu.VMEM((2, kc, A_HEADS * tq), F32),
                        pltpu.VMEM((HEAD_DIM, A_HEADS * tq), F32),
                        pltpu.VMEM((1, A_HEADS * tq), F32),
                        pltpu.VMEM((1, A_HEADS * tq), F32)],
        compiler_params=_cparams(("arbitrary",)),
        name="dsa_attention",
    )(qk, pidx, wt, qk, vt, pidx, band)


DIL_ROWS = 384


def _dilated_kernel(q_ref, k_ref, vt_ref, b0_ref, b1_ref, b2_ref, o_ref, s_ref, *, tq):
    t0 = pl.program_id(1) * tq
    q = q_ref[...]
    tiles = []
    row0 = 0
    for band_ref, (_, dil) in zip((b0_ref, b1_ref, b2_ref), DILATED_PAIRS):
        span = DIL_STEPS * dil
        start = jnp.maximum(t0 - span, 0)
        off = span - (t0 - start)
        for j in range((span + tq) // DIL_ROWS):
            ks = pl.multiple_of(start + j * DIL_ROWS, LANE)
            bs = pl.multiple_of(off + j * DIL_ROWS, LANE)
            s_ref[row0:row0 + DIL_ROWS, :] = (_dot_nt(k_ref[pl.ds(ks, DIL_ROWS), :], q)
                                              + band_ref[0, pl.ds(bs, DIL_ROWS), :])
            tiles.append((ks, row0))
            row0 += DIL_ROWS
    m = jnp.full((1, tq), NEG, F32)
    l = jnp.zeros((1, tq), F32)
    acc = jnp.zeros((HEAD_DIM, tq), F32)
    for ks, r0 in tiles:
        s = s_ref[r0:r0 + DIL_ROWS, :]
        m_new = jnp.maximum(m, jnp.max(s, axis=0, keepdims=True))
        alpha = jnp.exp2(m - m_new)
        p = jnp.exp2(s - m_new)
        l = alpha * l + jnp.sum(p, axis=0, keepdims=True)
        acc = alpha * acc + _dot(vt_ref[:, pl.ds(ks, DIL_ROWS)], p.astype(BF16))
        m = m_new
    o_ref[...] = (acc / l).T.astype(o_ref.dtype)


def dilated_attention(qk, vt, bands, vt_row0, tq=256):
    S = qk.shape[0]
    vh0 = vt_row0 // HEAD_DIM
    lengths = [DIL_STEPS * dil + tq for _, dil in DILATED_PAIRS]
    assert all(n % DIL_ROWS == 0 and n <= S for n in lengths)
    return pl.pallas_call(
        functools.partial(_dilated_kernel, tq=tq),
        grid=(B_HEADS, S // tq),
        in_specs=[pl.BlockSpec((tq, LANE), lambda h, i: (i, h)),
                  pl.BlockSpec((S, LANE), lambda h, i: (0, B_HEADS + h)),
                  pl.BlockSpec((HEAD_DIM, S), lambda h, i: (vh0 + h, 0))]
                 + [pl.BlockSpec((1,) + b.shape[1:], lambda h, i: (h, 0, 0)) for b in bands],
        out_specs=pl.BlockSpec((tq, LANE), lambda h, i: (i, h)),
        out_shape=jax.ShapeDtypeStruct((S, B_HEADS * HEAD_DIM), BF16),
        scratch_shapes=[pltpu.VMEM((sum(lengths), tq), F32)],
        compiler_params=_cparams(("parallel", "arbitrary")),
        name="dilated_attention",
    )(qk, qk, vt, *bands)


def _diff_kernel(lam_ref, g_ref, q_ref, k_ref, vt_ref, band_ref, o_ref, acc_ref, m_ref, l_ref, s_ref,
                 *, tq, tk, c0, lam_init):
    i = pl.program_id(1)
    t0 = i * tq
    nchunk = (t0 + tq + tk - 1) // tk
    last_chunk = k_ref.shape[0] // tk - 1
    q = q_ref[...]
    lane = lax.broadcasted_iota(jnp.int32, q.shape, 1)
    zero = jnp.zeros_like(q)
    q2 = jnp.concatenate([jnp.where(lane < C_HALF, q, zero), jnp.where(lane >= C_HALF, q, zero)], axis=0)

    m_ref[...] = jnp.full(m_ref.shape, NEG, F32)
    l_ref[...] = jnp.zeros(l_ref.shape, F32)
    acc_ref[...] = jnp.zeros(acc_ref.shape, F32)

    def logits(c):
        s0 = pl.multiple_of(jnp.minimum(c, last_chunk) * tk, tk)
        off = pl.multiple_of(c0 - jnp.clip(t0 - c * tk, -tk, c0), LANE)
        bias = band_ref[0, pl.ds(off, tk), :]
        return _dot_nt(k_ref[pl.ds(s0, tk), :], q2) + jnp.concatenate([bias, bias], axis=1)

    def consume(c, s):
        s0 = pl.multiple_of(jnp.minimum(c, last_chunk) * tk, tk)
        m_prev = m_ref[...]
        m_new = jnp.maximum(m_prev, jnp.max(s, axis=0, keepdims=True))
        alpha = jnp.exp2(m_prev - m_new)
        p = jnp.exp2(s - m_new)
        l_ref[...] = alpha * l_ref[...] + jnp.sum(p, axis=0, keepdims=True)
        acc_ref[...] = alpha * acc_ref[...] + _dot(vt_ref[:, pl.ds(s0, tk)], p.astype(BF16))
        m_ref[...] = m_new

    s_ref[0] = logits(0)

    def chunk_pair(cc, carry):
        c = 2 * cc
        s_ref[1] = logits(c + 1)
        consume(c, s_ref[0])
        s_ref[0] = logits(c + 2)
        consume(c + 1, s_ref[1])
        return carry

    lax.fori_loop(0, (nchunk + 1) // 2, chunk_pair, 0)

    lp = lam_ref[...]
    lam = (jnp.exp(jnp.sum(lp[0:1] * lp[1:2], axis=1, keepdims=True))
           - jnp.exp(jnp.sum(lp[2:3] * lp[3:4], axis=1, keepdims=True)) + lam_init)
    o = acc_ref[...] / l_ref[...]
    o = o[:, :tq] - lam * o[:, tq:]
    ms = jnp.mean(o * o, axis=0, keepdims=True)
    o = o * lax.rsqrt(ms + RMS_EPS) * (g_ref[...] * (1.0 - lam_init))
    o_ref[...] = o.T.astype(o_ref.dtype)


def diff_attention(qk, vt, band, lam_params, subln_g, lam_init, vt_row0, tq=256, tk=512):
    S = qk.shape[0]
    vh0 = vt_row0 // HEAD_DIM
    return pl.pallas_call(
        functools.partial(_diff_kernel, tq=tq, tk=tk, c0=_band_c0(tk), lam_init=lam_init),
        grid=(C_HEADS, S // tq),
        in_specs=[pl.BlockSpec((4, C_HALF), lambda h, i: (0, 0)),
                  pl.BlockSpec((HEAD_DIM, 1), lambda h, i: (0, 0)),
                  pl.BlockSpec((tq, LANE), lambda h, i: (i, h)),
                  pl.BlockSpec((S, LANE), lambda h, i: (0, C_HEADS + h)),
                  pl.BlockSpec((HEAD_DIM, S), lambda h, i: (vh0 + h, 0)),
                  pl.BlockSpec((1,) + band.shape[1:], lambda h, i: (h, 0, 0))],
        out_specs=pl.BlockSpec((tq, LANE), lambda h, i: (i, h)),
        out_shape=jax.ShapeDtypeStruct((S, C_HEADS * HEAD_DIM), BF16),
        scratch_shapes=[pltpu.VMEM((HEAD_DIM, 2 * tq), F32),
                        pltpu.VMEM((1, 2 * tq), F32),
                        pltpu.VMEM((1, 2 * tq), F32),
                        pltpu.VMEM((2, tk, 2 * tq), F32)],
        compiler_params=_cparams(("parallel", "arbitrary")),
        name="diff_attention",
    )(lam_params, subln_g.reshape(HEAD_DIM, 1), qk, qk, vt, band)


def _outproj_kernel(x_ref, a_ref, b_ref, c_ref, wa_ref, wb_ref, wc_ref, o_ref):
    o_ref[...] = (x_ref[...] + _dot(a_ref[...], wa_ref[...]) + _dot(b_ref[...], wb_ref[...])
                  + _dot(c_ref[...], wc_ref[...]))


def out_projection(x, oa, ob, oc, wa, wb, wc, tm=512, tn=1024):
    S, D = x.shape
    tn = min(tn, D)
    return pl.pallas_call(
        _outproj_kernel,
        grid=(S // tm, D // tn),
        in_specs=[pl.BlockSpec((tm, tn), lambda i, j: (i, j)),
                  pl.BlockSpec((tm, oa.shape[1]), lambda i, j: (i, 0)),
                  pl.BlockSpec((tm, ob.shape[1]), lambda i, j: (i, 0)),
                  pl.BlockSpec((tm, oc.shape[1]), lambda i, j: (i, 0)),
                  pl.BlockSpec((wa.shape[0], tn), lambda i, j: (0, j)),
                  pl.BlockSpec((wb.shape[0], tn), lambda i, j: (0, j)),
                  pl.BlockSpec((wc.shape[0], tn), lambda i, j: (0, j))],
        out_specs=pl.BlockSpec((tm, tn), lambda i, j: (i, j)),
        out_shape=jax.ShapeDtypeStruct((S, D), F32),
        compiler_params=_cparams(("parallel", "arbitrary")),
        name="out_projection",
    )(x, oa, ob, oc, wa, wb, wc)


def _silu(x):
    return x / (1.0 + jnp.exp(-x))


def _ffn_kernel(x_ref, h_ref, wg_ref, wu_ref, wd_ref, o_ref):
    f = pl.program_id(1)

    @pl.when(f == 0)
    def _():
        o_ref[...] = x_ref[...]

    h = h_ref[...]
    a = _silu(_dot(h, wg_ref[...])) * _dot(h, wu_ref[...])
    o_ref[...] += _dot(a.astype(BF16), wd_ref[...])


def ffn_dense(x, h, wg, wu, wd, tm=512, tf=512):
    S, D = x.shape
    F = wg.shape[1]
    return pl.pallas_call(
        _ffn_kernel,
        grid=(S // tm, F // tf),
        in_specs=[pl.BlockSpec((tm, D), lambda i, f: (i, 0)),
                  pl.BlockSpec((tm, D), lambda i, f: (i, 0)),
                  pl.BlockSpec((D, tf), lambda i, f: (0, f)),
                  pl.BlockSpec((D, tf), lambda i, f: (0, f)),
                  pl.BlockSpec((tf, D), lambda i, f: (f, 0))],
        out_specs=pl.BlockSpec((tm, D), lambda i, f: (i, 0)),
        out_shape=jax.ShapeDtypeStruct((S, D), F32),
        compiler_params=_cparams(("parallel", "arbitrary")),
        name="ffn_dense",
    )(x, h, wg, wu, wd)


def _router_kernel(x_ref, g_ref, w_ref, c_ref, sel_ref):
    x = x_ref[...]
    ms = jnp.mean(x * x, axis=-1, keepdims=True)
    h = x * lax.rsqrt(ms + RMS_EPS) * g_ref[...]
    logits = jnp.dot(h, w_ref[...], preferred_element_type=F32, precision=lax.Precision.HIGHEST)
    lane = lax.broadcasted_iota(jnp.int32, logits.shape, 1).astype(F32)
    logits = jnp.where(lane < N_EXPERTS, logits, NEG)
    m1 = jnp.max(logits, axis=1, keepdims=True)
    i1 = jnp.min(jnp.where(logits == m1, lane, float(LANE)), axis=1, keepdims=True)
    rest = jnp.where(lane == i1, NEG, logits)
    m2 = jnp.max(rest, axis=1, keepdims=True)
    i2 = jnp.min(jnp.where(rest == m2, lane, float(LANE)), axis=1, keepdims=True)
    e2 = jnp.exp(m2 - m1)
    g1 = 1.0 / (1.0 + e2)
    g2 = e2 / (1.0 + e2)
    pick1, pick2 = lane == i1, lane == i2
    c_ref[...] = jnp.where(pick1, g1, 0.0) + jnp.where(pick2, g2, 0.0)
    sel_ref[...] = jnp.where(pick1 | pick2, 1.0, 0.0)


def moe_router(x, g, w_router, tm=512):
    S, D = x.shape
    wpad = jnp.zeros((D, LANE), F32).at[:, :N_EXPERTS].set(w_router)
    spec = pl.BlockSpec((tm, LANE), lambda i: (i, 0))
    sds = jax.ShapeDtypeStruct((S, LANE), F32)
    return pl.pallas_call(
        _router_kernel,
        grid=(S // tm,),
        in_specs=[pl.BlockSpec((tm, D), lambda i: (i, 0)),
                  pl.BlockSpec((1, D), lambda i: (0, 0)),
                  pl.BlockSpec((D, LANE), lambda i: (0, 0))],
        out_specs=[spec, spec],
        out_shape=[sds, sds],
        compiler_params=_cparams(("parallel",)),
        name="moe_router",
    )(x, g.reshape(1, D), wpad)


MOE_CH = 256
MOE_TILE = 512


def _moe_rank_kernel(sel_ref, rm_ref, rmt_ref, cum_ref, tot_ref, carry_ref):
    c = pl.program_id(0)

    @pl.when(c == 0)
    def _():
        carry_ref[...] = jnp.zeros(carry_ref.shape, F32)

    sel = sel_ref[...]
    ch = sel.shape[0]
    before = (lax.broadcasted_iota(jnp.int32, (ch, ch), 1) < lax.broadcasted_iota(jnp.int32, (ch, ch), 0))
    rank = _dot(jnp.where(before, 1.0, 0.0).astype(BF16), sel.astype(BF16)) + carry_ref[...]
    rm = jnp.where(sel > 0.0, rank, -1.0)
    rm_ref[...] = rm
    rmt_ref[...] = rm.T[:N_EXPERTS]
    cum_ref[0] = carry_ref[...]
    carry_ref[...] += jnp.sum(sel, axis=0, keepdims=True)
    tot_ref[...] = carry_ref[...]


def moe_rank(sel):
    S = sel.shape[0]
    nch = S // MOE_CH
    return pl.pallas_call(
        _moe_rank_kernel,
        grid=(nch,),
        in_specs=[pl.BlockSpec((MOE_CH, LANE), lambda c: (c, 0))],
        out_specs=[pl.BlockSpec((MOE_CH, LANE), lambda c: (c, 0)),
                   pl.BlockSpec((N_EXPERTS, MOE_CH), lambda c: (0, c)),
                   pl.BlockSpec((1, 1, LANE), lambda c: (c, 0, 0)),
                   pl.BlockSpec((1, LANE), lambda c: (0, 0))],
        out_shape=[jax.ShapeDtypeStruct((S, LANE), F32),
                   jax.ShapeDtypeStruct((N_EXPERTS, S), F32),
                   jax.ShapeDtypeStruct((nch, 1, LANE), F32),
                   jax.ShapeDtypeStruct((1, LANE), F32)],
        scratch_shapes=[pltpu.VMEM((1, LANE), F32)],
        compiler_params=_cparams(("arbitrary",)),
        name="moe_rank",
    )(sel)


def _moe_expert_kernel(te_ref, tr0_ref, tclo_ref, tchi_ref, tval_ref,
                       rk_ref, h_hbm, wg_ref, wu_ref, wd_ref, y_ref,
                       hs_ref, acc_ref, hbuf_ref, sem_ref):
    j = pl.program_id(0)
    f = pl.program_id(1)
    nf = pl.num_programs(1)
    valid = tval_ref[j] > 0
    rows = hs_ref.shape[0]

    def chunk_copy(c, slot):
        return pltpu.make_async_copy(h_hbm.at[pl.ds(c * MOE_CH, MOE_CH)], hbuf_ref.at[slot], sem_ref.at[slot])

    @pl.when(valid & (f == 0))
    def _():
        clo, chi = tclo_ref[j], tchi_ref[j]
        want = (tr0_ref[j] + lax.broadcasted_iota(jnp.int32, (rows, MOE_CH), 0)).astype(F32)
        acc_ref[...] = jnp.zeros(acc_ref.shape, F32)

        @pl.when(chi > clo)
        def _():
            chunk_copy(clo, 0).start()

        def body(c, carry):
            slot = (c - clo) % 2

            @pl.when(c + 1 < chi)
            def _():
                chunk_copy(c + 1, 1 - slot).start()

            chunk_copy(c, slot).wait()
            rk = rk_ref[0, :, pl.ds(pl.multiple_of(c * MOE_CH, MOE_CH), MOE_CH)]
            onehot = jnp.where(rk == want, 1.0, 0.0).astype(BF16)
            acc_ref[...] += _dot(onehot, hbuf_ref[slot])
            return carry

        lax.fori_loop(clo, chi, body, 0)
        hs_ref[...] = acc_ref[...].astype(BF16)
        acc_ref[...] = jnp.zeros(acc_ref.shape, F32)

    @pl.when(valid)
    def _():
        hs = hs_ref[...]
        a = _silu(_dot(hs, wg_ref[0])) * _dot(hs, wu_ref[0])
        acc_ref[...] += _dot(a.astype(BF16), wd_ref[0])

    @pl.when(f == nf - 1)
    def _():
        y_ref[...] = jnp.where(valid, acc_ref[...], 0.0).astype(y_ref.dtype)


def _moe_combine_kernel(ca_ref, cb_ref, roff_ref, x_ref, rm_ref, cw_ref, ya_ref, yb_ref, o_ref):
    c = pl.program_id(0)
    e = pl.program_id(1)

    @pl.when(e == 0)
    def _():
        o_ref[...] = x_ref[...]

    rm = rm_ref[...]
    lane = lax.broadcasted_iota(jnp.int32, rm.shape, 1)
    mine = lane == e
    pos = jnp.sum(jnp.where(mine, rm, 0.0), axis=1, keepdims=True)
    gate = jnp.sum(jnp.where(mine, cw_ref[...], 0.0), axis=1, keepdims=True)
    row = jnp.where(pos >= 0.0, pos + roff_ref[e].astype(F32), -1.0)
    k = c * pl.num_programs(1) + e
    ca, cb = ca_ref[k], cb_ref[k]
    cr = ya_ref.shape[0]
    col = lax.broadcasted_iota(jnp.int32, (rm.shape[0], cr), 1)
    for y_ref, base in ((ya_ref, ca * cr), (yb_ref, jnp.where(cb != ca, cb * cr, -2 * cr))):
        onehot = jnp.where(row == (base + col).astype(F32), 1.0, 0.0).astype(BF16)
        o_ref[...] += gate * _dot(onehot, y_ref[...])


def ffn_moe(x, h, combine, sel, wg, wu, wd, tf=1024):
    S, D = x.shape
    E, _, F = wg.shape
    tf = min(tf, F)
    T = MOE_TILE
    nch = S // MOE_CH
    J = 2 * S // T + E
    rm, rmt, cum, tot = moe_rank(sel)

    cnt = tot[0, :E].astype(jnp.int32)
    ntile = (cnt + T - 1) // T
    tend = jnp.cumsum(ntile)
    tstart = tend - ntile
    jj = jnp.arange(J, dtype=jnp.int32)
    tval = (jj < tend[-1]).astype(jnp.int32)
    te = jnp.minimum(jnp.searchsorted(tend, jnp.minimum(jj, tend[-1] - 1), side="right"), E - 1).astype(jnp.int32)
    tr0 = (jnp.minimum(jj, tend[-1] - 1) - tstart[te]) * T
    cumx = cum[:, 0, :E].astype(jnp.int32)
    cumi = jnp.concatenate([cumx[1:], cnt[None]], axis=0)
    tclo = jnp.sum(cumi[:, te] <= tr0[None, :], axis=0).astype(jnp.int32)
    tchi = jnp.sum(cumx[:, te] < (tr0 + T)[None, :], axis=0).astype(jnp.int32)
    roff = (tstart * T).astype(jnp.int32)
    nrow_chunks = J * T // MOE_CH
    ca = ((roff[None, :] + cumx) // MOE_CH).astype(jnp.int32)
    cb = jnp.minimum(ca + 1, nrow_chunks - 1)
    ca, cb = ca.reshape(-1), cb.reshape(-1)

    nf = F // tf

    def wmap(j, f, te_r, tr0_r, clo_r, chi_r, val_r):
        return (te_r[j], 0, jnp.where(val_r[j] > 0, f, nf - 1))

    def wdmap(j, f, te_r, tr0_r, clo_r, chi_r, val_r):
        return (te_r[j], jnp.where(val_r[j] > 0, f, nf - 1), 0)

    y = pl.pallas_call(
        _moe_expert_kernel,
        grid_spec=pltpu.PrefetchScalarGridSpec(
            num_scalar_prefetch=5,
            grid=(J, nf),
            in_specs=[pl.BlockSpec((1, 1, S), lambda j, f, te_r, *_: (te_r[j], 0, 0)),
                      pl.BlockSpec(memory_space=pl.ANY),
                      pl.BlockSpec((1, D, tf), wmap),
                      pl.BlockSpec((1, D, tf), wmap),
                      pl.BlockSpec((1, tf, D), wdmap)],
            out_specs=pl.BlockSpec((T, D), lambda j, f, *_: (j, 0)),
            scratch_shapes=[pltpu.VMEM((T, D), BF16),
                            pltpu.VMEM((T, D), F32),
                            pltpu.VMEM((2, MOE_CH, D), BF16),
                            pltpu.SemaphoreType.DMA((2,))]),
        out_shape=jax.ShapeDtypeStruct((J * T, D), BF16),
        compiler_params=_cparams(("arbitrary", "arbitrary")),
        name="moe_experts",
    )(te, tr0, tclo, tchi, tval, rmt.reshape(E, 1, S), h, wg, wu, wd)

    return pl.pallas_call(
        _moe_combine_kernel,
        grid_spec=pltpu.PrefetchScalarGridSpec(
            num_scalar_prefetch=3,
            grid=(nch, E),
            in_specs=[pl.BlockSpec((MOE_CH, D), lambda c, e, *_: (c, 0)),
                      pl.BlockSpec((MOE_CH, LANE), lambda c, e, *_: (c, 0)),
                      pl.BlockSpec((MOE_CH, LANE), lambda c, e, *_: (c, 0)),
                      pl.BlockSpec((MOE_CH, D), lambda c, e, ca_r, cb_r, ro_r: (ca_r[c * E + e], 0)),
                      pl.BlockSpec((MOE_CH, D), lambda c, e, ca_r, cb_r, ro_r: (cb_r[c * E + e], 0))],
            out_specs=pl.BlockSpec((MOE_CH, D), lambda c, e, *_: (c, 0))),
        out_shape=jax.ShapeDtypeStruct((S, D), F32),
        compiler_params=_cparams(("parallel", "arbitrary")),
        name="moe_combine",
    )(ca, cb, roff, x, rm, combine, y, y)


def _tile_gain(g, reps, scale=1.0):
    return jnp.tile(g.astype(F32) * scale, reps)


def kernel(x, w_in, w_out, attn_norm_g, ffn_norm_g, q_norm_a, k_norm_a, q_norm_b, k_norm_b,
           q_norm_c, k_norm_c, lambda_q1, lambda_k1, lambda_q2, lambda_k2, diff_subln_g, rel_bias,
           w_dense_gate, w_dense_up, w_dense_down, w_router, w_moe_gate, w_moe_up, w_moe_down):
    B, S, D = x.shape
    depth = w_in.shape[0]
    assert B == 1 and S % (DIL_STEPS * DILATED_PAIRS[-1][1]) == 0
    aw, bw, cw = A_HEADS * HEAD_DIM, B_HEADS * HEAD_DIM, C_HEADS * HEAD_DIM
    iw = IDX_HEADS * IDX_DIM
    sizes = (aw, aw, aw, iw, IDX_DIM, IDX_HEADS, bw, bw, bw, cw, cw, cw)
    offs = np.concatenate([[0], np.cumsum(sizes)]).tolist()

    def cols(w, *segs):
        return jnp.concatenate([w[:, offs[s]:offs[s + 1]] for s in segs], axis=1)

    tq, tk = 256, 512
    band_a = bias_band(rel_bias, A_HEADS, 0, _band_c0(tk) + 2 * tk, tq, _band_c0(tk))
    band_c = bias_band(rel_bias, C_HEADS, A_HEADS + B_HEADS, _band_c0(tk) + 2 * tk, tq, _band_c0(tk))
    bands_b = [bias_band(rel_bias, B_HEADS, A_HEADS, 2 * DIL_STEPS * dil + tq, tq, DIL_STEPS * dil, dil)
               for _, dil in DILATED_PAIRS]

    xs = x.reshape(S, D)
    for layer in range(depth):
        wl = w_in[layer]
        w_pa = cols(wl, 0, 1).astype(BF16)
        w_pb = cols(wl, 6, 7).astype(BF16)
        w_pc = cols(wl, 9, 10).astype(BF16)
        w_pi = cols(wl, 3, 4, 4).astype(BF16)
        w_vt = cols(wl, 2, 8, 11).T.astype(BF16)
        w_wt = cols(wl, 5).T.astype(BF16)
        qs_ab, qs_c = HEAD_DIM ** -0.5 * LOG2E, C_HALF ** -0.5 * LOG2E
        g_pa = jnp.concatenate([_tile_gain(q_norm_a[layer], A_HEADS, qs_ab), _tile_gain(k_norm_a[layer], A_HEADS)])
        g_pb = jnp.concatenate([_tile_gain(q_norm_b[layer], B_HEADS, qs_ab), _tile_gain(k_norm_b[layer], B_HEADS)])
        g_pc = jnp.concatenate([_tile_gain(q_norm_c[layer], 2 * C_HEADS, qs_c),
                                _tile_gain(k_norm_c[layer], 2 * C_HEADS)])

        h = rmsnorm(xs, attn_norm_g[layer])
        p_a = project(h, w_pa, g_pa, HEAD_DIM, BF16, tn=512)
        p_b = project(h, w_pb, g_pb, HEAD_DIM, BF16, tn=512)
        p_c = project(h, w_pc, g_pc, C_HALF, BF16, tn=512)
        p_i = project(h, w_pi, jnp.ones((w_pi.shape[1],), F32), 0, BF16, tn=384)
        vt = project_t(h, w_vt, BF16)
        wt = project_t(h, w_wt, F32)

        oa = dsa_attention(p_a, vt, p_i, wt, band_a)

        ob = dilated_attention(p_b, vt, bands_b, aw, tq=tq)

        lam_init = 0.8 - 0.6 * math.exp(-0.3 * layer)
        lam_params = jnp.stack([lambda_q1[layer], lambda_k1[layer], lambda_q2[layer], lambda_k2[layer]]).astype(F32)
        oc = diff_attention(p_c, vt, band_c, lam_params, diff_subln_g[layer].astype(F32), lam_init, aw + bw,
                            tq=tq, tk=tk)

        wo = w_out[layer].astype(BF16)
        xs = out_projection(xs, oa, ob, oc, wo[:aw], wo[aw:aw + bw], wo[aw + bw:])

        f = layer // 2
        if layer % 2 == 0:
            h2 = rmsnorm(xs, ffn_norm_g[layer])
            xs = ffn_dense(xs, h2, w_dense_gate[f].astype(BF16), w_dense_up[f].astype(BF16),
                           w_dense_down[f].astype(BF16))
        else:
            h2 = rmsnorm(xs, ffn_norm_g[layer])
            combine, sel = moe_router(xs, ffn_norm_g[layer], w_router[f])
            xs = ffn_moe(xs, h2, combine, sel, w_moe_gate[f].astype(BF16), w_moe_up[f].astype(BF16),
                         w_moe_down[f].astype(BF16))
    return xs.reshape(B, S, D)
```

```python
import functools
import math

import jax
import jax.numpy as jnp
import numpy as np
from jax import lax
from jax.experimental import pallas as pl
from jax.experimental.pallas import tpu as pltpu

F32 = jnp.float32
BF16 = jnp.bfloat16

HEAD_DIM = 128
A_HEADS, B_HEADS, C_HEADS = 4, 6, 6
C_HALF = HEAD_DIM // 2
IDX_HEADS, IDX_DIM = 16, 64
TOPK_MAX = 256
DILATED_PAIRS = ((128, 1), (512, 4), (2048, 16))
DIL_STEPS = 128
N_BUCKETS = 32
MAX_DISTANCE = 2048
N_EXPERTS = 8
RMS_EPS = 1e-6
NEG = -1e30
LOG2E = math.log2(math.e)
INT_MIN = -(2 ** 31)
I16_BIAS = 2 ** 15
CNT_ROWS = 64
LANE = 128
VMEM_LIMIT = 56 * 1024 * 1024

BUCKET_EDGES = tuple(range(16)) + (16,) + tuple(
    int(math.ceil(16.0 * (MAX_DISTANCE / 16.0) ** (k / 16.0))) for k in range(1, 16))
FAR_DIST = BUCKET_EDGES[-1]


def _cparams(sem):
    return pltpu.CompilerParams(dimension_semantics=sem, vmem_limit_bytes=VMEM_LIMIT)


def _dot(a, b):
    return jnp.dot(a, b, preferred_element_type=F32)


def _dot_nt(a, b):
    return lax.dot_general(a, b, (((1,), (1,)), ((), ())), preferred_element_type=F32)


def _rmsnorm_kernel(x_ref, g_ref, o_ref):
    x = x_ref[...]
    ms = jnp.mean(x * x, axis=-1, keepdims=True)
    o_ref[...] = (x * lax.rsqrt(ms + RMS_EPS) * g_ref[...]).astype(o_ref.dtype)


def rmsnorm(x, g, tm=512):
    S, D = x.shape
    return pl.pallas_call(
        _rmsnorm_kernel,
        grid=(S // tm,),
        in_specs=[pl.BlockSpec((tm, D), lambda i: (i, 0)), pl.BlockSpec((1, D), lambda i: (0, 0))],
        out_specs=pl.BlockSpec((tm, D), lambda i: (i, 0)),
        out_shape=jax.ShapeDtypeStruct((S, D), BF16),
        compiler_params=_cparams(("parallel",)),
        name="rmsnorm",
    )(x, g.reshape(1, D))


def _proj_kernel(h_ref, w_ref, g_ref, o_ref, *, group):
    acc = _dot(h_ref[...], w_ref[...])
    tn = acc.shape[1]
    if group == 0:
        o_ref[...] = acc.astype(o_ref.dtype)
        return
    for c in range(tn // LANE):
        a = acc[:, c * LANE:(c + 1) * LANE]
        sq = a * a
        if group == LANE:
            ms = jnp.mean(sq, axis=-1, keepdims=True)
        else:
            lane = lax.broadcasted_iota(jnp.int32, sq.shape, 1)
            lo = jnp.sum(jnp.where(lane < group, sq, 0.0), axis=-1, keepdims=True)
            hi = jnp.sum(jnp.where(lane >= group, sq, 0.0), axis=-1, keepdims=True)
            ms = jnp.where(lane < group, lo, hi) * (1.0 / group)
        y = a * lax.rsqrt(ms + RMS_EPS) * g_ref[:, c * LANE:(c + 1) * LANE]
        o_ref[:, c * LANE:(c + 1) * LANE] = y.astype(o_ref.dtype)


def project(h, w, gains, group, out_dtype, tm=1024, tn=256):
    S, K = h.shape
    N = w.shape[1]
    return pl.pallas_call(
        functools.partial(_proj_kernel, group=group),
        grid=(S // tm, N // tn),
        in_specs=[pl.BlockSpec((tm, K), lambda i, j: (i, 0)),
                  pl.BlockSpec((K, tn), lambda i, j: (0, j)),
                  pl.BlockSpec((1, tn), lambda i, j: (0, j))],
        out_specs=pl.BlockSpec((tm, tn), lambda i, j: (i, j)),
        out_shape=jax.ShapeDtypeStruct((S, N), out_dtype),
        compiler_params=_cparams(("parallel", "arbitrary")),
        name=f"proj_g{group}",
    )(h, w, gains.reshape(1, N))


def _proj_t_kernel(wt_ref, h_ref, o_ref):
    o_ref[...] = _dot_nt(wt_ref[...], h_ref[...]).astype(o_ref.dtype)


def project_t(h, wt, out_dtype, tm=1024, tn=256):
    S, K = h.shape
    N = wt.shape[0]
    tn = min(tn, N)
    return pl.pallas_call(
        _proj_t_kernel,
        grid=(S // tm, N // tn),
        in_specs=[pl.BlockSpec((tn, K), lambda i, j: (j, 0)),
                  pl.BlockSpec((tm, K), lambda i, j: (i, 0))],
        out_specs=pl.BlockSpec((tn, tm), lambda i, j: (j, i)),
        out_shape=jax.ShapeDtypeStruct((N, S), out_dtype),
        compiler_params=_cparams(("parallel", "arbitrary")),
        name="proj_t",
    )(wt, h)


BAND_ROWS = 256


def _band_kernel(tab_ref, o_ref, *, c0, head_off, dil):
    hh = head_off + pl.program_id(0)
    _, rows, width = o_ref.shape
    dist = (lax.broadcasted_iota(jnp.int32, (rows, width), 1)
            - lax.broadcasted_iota(jnp.int32, (rows, width), 0) + (c0 - pl.program_id(1) * rows))
    val = jnp.full((rows, width), tab_ref[0, hh] * LOG2E, F32)
    for b in range(1, N_BUCKETS):
        val = jnp.where(dist >= BUCKET_EDGES[b], tab_ref[b, hh] * LOG2E, val)
    ok = dist >= 0
    if dil:
        ok = ok & (dist <= DIL_STEPS * dil) & ((dist & (dil - 1)) == 0)
    o_ref[0] = jnp.where(ok, val, NEG)


def bias_band(rel_bias, n_heads, head_off, rows, width, c0, dil=0):
    assert rows % BAND_ROWS == 0 and dil & (dil - 1) == 0
    return pl.pallas_call(
        functools.partial(_band_kernel, c0=c0, head_off=head_off, dil=dil),
        grid=(n_heads, rows // BAND_ROWS),
        in_specs=[pl.BlockSpec(memory_space=pltpu.SMEM)],
        out_specs=pl.BlockSpec((1, BAND_ROWS, width), lambda h, r: (h, r, 0)),
        out_shape=jax.ShapeDtypeStruct((n_heads, rows, width), F32),
        compiler_params=_cparams(("parallel", "parallel")),
        name=f"bias_band_d{dil}",
    )(rel_bias)


def _band_c0(tk):
    return -(-(FAR_DIST + tk - 1) // LANE) * LANE


def _dsa_kernel(q_ref, qi_ref, wt_ref, k_ref, vt_ref, kidx_ref, band_ref, o_ref,
                hi_ref, lo_ref, cut_ref, qih_ref, s_ref, acc_ref, m_ref, l_ref, *, tq, kc, topk, c0):
    i = pl.program_id(0)
    t0 = i * tq
    nchunk = (t0 + tq + kc - 1) // kc
    half = lax.broadcasted_iota(jnp.int32, (tq, LANE), 1) < IDX_DIM

    for h in range(IDX_HEADS):
        slab = qi_ref[:, (h // 2) * LANE:(h // 2 + 1) * LANE]
        keep = half if h % 2 == 0 else jnp.logical_not(half)
        qih_ref[h] = jnp.where(keep, slab, jnp.zeros_like(slab))
    w = wt_ref[...] * (IDX_HEADS ** -0.5 * IDX_DIM ** -0.5)

    pos_q = t0 + lax.broadcasted_iota(jnp.int32, (kc, tq), 1)
    row = lax.broadcasted_iota(jnp.int32, (kc, tq), 0)

    def score_chunk(c, carry):
        s0 = pl.multiple_of(c * kc, kc)
        kx = kidx_ref[pl.ds(s0, kc), :]
        acc = jnp.zeros((kc, tq), F32)
        for h in range(IDX_HEADS):
            acc = acc + w[h:h + 1, :] * jnp.maximum(_dot_nt(kx, qih_ref[h]), 0.0)
        bits = lax.bitcast_convert_type(acc, jnp.int32)
        key = bits ^ ((bits >> 31) & 0x7FFFFFFF)
        key = jnp.where(s0 + row <= pos_q, key, INT_MIN)
        hi_ref[pl.ds(s0, kc), :] = (key >> 16).astype(jnp.int16)
        lo_ref[pl.ds(s0, kc), :] = ((key & 0xFFFF) - I16_BIAS).astype(jnp.int16)
        return carry

    lax.fori_loop(0, nchunk, score_chunk, 0)

    one16, zero16 = jnp.ones((), jnp.int16), jnp.zeros((), jnp.int16)
    row16 = lax.broadcasted_iota(jnp.int32, (kc, tq), 0).astype(jnp.int16)
    rowc16 = lax.broadcasted_iota(jnp.int32, (CNT_ROWS, tq), 0).astype(jnp.int16)
    n_keys = k_ref.shape[0]

    def to16(v):
        return jnp.clip(v, -I16_BIAS, I16_BIAS - 1).astype(jnp.int16)

    def count(hit):
        def body(c, cnt):
            base = pl.multiple_of(c * kc, kc)
            for g in range(kc // CNT_ROWS):
                cnt = cnt + jnp.where(hit(base + g * CNT_ROWS), one16, zero16)
            return cnt

        cnt = lax.fori_loop(0, nchunk, body, jnp.zeros((CNT_ROWS, tq), jnp.int16))
        return jnp.sum(cnt.astype(jnp.int32).astype(F32), axis=0, keepdims=True)

    def count_ge(ref, cand):
        c16 = cand.astype(jnp.int16)
        return count(lambda r: ref[pl.ds(r, CNT_ROWS), :] >= c16)

    def count_gt(ref, cand):
        c16 = cand.astype(jnp.int16)
        return count(lambda r: ref[pl.ds(r, CNT_ROWS), :] > c16)

    def kth_largest(ref, need):
        t = jnp.where(count_ge(ref, jnp.zeros((1, tq), jnp.int32)) >= need, 0, -I16_BIAS).astype(jnp.int32)

        def bit_body(b, t):
            cand = t + jnp.left_shift(jnp.int32(1), 14 - b)
            return jnp.where(count_ge(ref, cand) >= need, cand, t)

        return lax.fori_loop(0, 15, bit_body, t)

    kf = jnp.full((1, tq), float(topk), F32)
    thi = kth_largest(hi_ref, kf)
    need_lo = kf - count_gt(hi_ref, thi)
    thi16 = thi.astype(jnp.int16)

    def mask_low(c, carry):
        rows = pl.ds(pl.multiple_of(c * kc, kc), kc)
        lo_ref[rows, :] = jnp.where(hi_ref[rows, :] == thi16, lo_ref[rows, :], jnp.int16(-I16_BIAS))
        return carry

    lax.fori_loop(0, nchunk, mask_low, 0)
    tlo = kth_largest(lo_ref, need_lo)
    tlo16 = tlo.astype(jnp.int16)

    def is_eq(r):
        return (hi_ref[pl.ds(r, CNT_ROWS), :] == thi16) & (lo_ref[pl.ds(r, CNT_ROWS), :] == tlo16)

    keep_eq = need_lo - count_gt(lo_ref, tlo)
    excess = (count(is_eq) > keep_eq) & (thi > -I16_BIAS)
    cut_ref[...] = jnp.full(cut_ref.shape, n_keys, jnp.int32)

    @pl.when(jnp.max(jnp.broadcast_to(jnp.where(excess, 1.0, 0.0), (8, tq))) > 0.0)
    def _():
        nbits = n_keys.bit_length()

        def bit_body(b, j):
            cand = j + jnp.left_shift(jnp.int32(1), nbits - 1 - b)
            before = count(lambda r: is_eq(r) & (rowc16 < to16(cand - r)))
            return jnp.where(before < keep_eq, cand, j)

        j = lax.fori_loop(0, nbits, bit_body, jnp.zeros((1, tq), jnp.int32))
        cut_ref[...] = jnp.where(excess, j, n_keys)

    cut = cut_ref[...]
    thi_sel16 = jnp.maximum(thi, 1 - I16_BIAS).astype(jnp.int16)

    m_ref[...] = jnp.full(m_ref.shape, NEG, F32)
    l_ref[...] = jnp.zeros(l_ref.shape, F32)
    acc_ref[...] = jnp.zeros(acc_ref.shape, F32)

    def logits(c):
        s0 = pl.multiple_of(jnp.minimum(c, nchunk - 1) * kc, kc)
        off = pl.multiple_of(c0 - jnp.clip(t0 - c * kc, -kc, c0), LANE)
        hi = hi_ref[pl.ds(s0, kc), :]
        lo = lo_ref[pl.ds(s0, kc), :]
        sel = (hi > thi_sel16) | ((hi == thi_sel16)
                                  & ((lo > tlo16) | ((lo == tlo16) & (row16 <= to16(cut - s0)))))
        drop = jnp.where(sel, jnp.zeros((), jnp.int16), jnp.ones((), jnp.int16))
        mask_bias = drop.astype(jnp.int32).astype(F32) * NEG
        parts = []
        for h in range(A_HEADS):
            hs = slice(h * HEAD_DIM, (h + 1) * HEAD_DIM)
            parts.append(_dot_nt(k_ref[pl.ds(s0, kc), hs], q_ref[:, hs]) + band_ref[h, pl.ds(off, kc), :]
                         + mask_bias)
        return jnp.concatenate(parts, axis=1)

    def consume(c, s):
        s0 = pl.multiple_of(jnp.minimum(c, nchunk - 1) * kc, kc)
        m_prev = m_ref[...]
        m_new = jnp.maximum(m_prev, jnp.max(s, axis=0, keepdims=True))
        alpha = jnp.exp2(m_prev - m_new)
        p = jnp.exp2(s - m_new)
        l_ref[...] = alpha * l_ref[...] + jnp.sum(p, axis=0, keepdims=True)
        p = p.astype(BF16)
        pv = [_dot(vt_ref[h * HEAD_DIM:(h + 1) * HEAD_DIM, pl.ds(s0, kc)], p[:, h * tq:(h + 1) * tq])
              for h in range(A_HEADS)]
        acc_ref[...] = alpha * acc_ref[...] + jnp.concatenate(pv, axis=1)
        m_ref[...] = m_new

    s_ref[0] = logits(0)

    def chunk_pair(cc, carry):
        c = 2 * cc
        s_ref[1] = logits(c + 1)
        consume(c, s_ref[0])
        s_ref[0] = logits(c + 2)
        consume(c + 1, s_ref[1])
        return carry

    lax.fori_loop(0, (nchunk + 1) // 2, chunk_pair, 0)
    o = acc_ref[...] / l_ref[...]
    for h in range(A_HEADS):
        o_ref[:, h * HEAD_DIM:(h + 1) * HEAD_DIM] = o[:, h * tq:(h + 1) * tq].T.astype(o_ref.dtype)


def dsa_attention(qk, vt, pidx, wt, band, tq=256, kc=512):
    S = qk.shape[0]
    topk = min(TOPK_MAX, S // 4)
    aw = A_HEADS * HEAD_DIM
    iw = IDX_HEADS * IDX_DIM
    once = pl.Buffered(1)
    return pl.pallas_call(
        functools.partial(_dsa_kernel, tq=tq, kc=kc, topk=topk, c0=_band_c0(kc)),
        grid=(S // tq,),
        in_specs=[pl.BlockSpec((tq, aw), lambda i: (i, 0)),
                  pl.BlockSpec((tq, iw), lambda i: (i, 0)),
                  pl.BlockSpec((IDX_HEADS, tq), lambda i: (0, i)),
                  pl.BlockSpec((S, aw), lambda i: (0, 1), pipeline_mode=once),
                  pl.BlockSpec((aw, S), lambda i: (0, 0), pipeline_mode=once),
                  pl.BlockSpec((S, LANE), lambda i: (0, iw // LANE), pipeline_mode=once),
                  pl.BlockSpec(band.shape, lambda i: (0, 0, 0), pipeline_mode=once)],
        out_specs=pl.BlockSpec((tq, aw), lambda i: (i, 0)),
        out_shape=jax.ShapeDtypeStruct((S, aw), BF16),
        scratch_shapes=[pltpu.VMEM((S, tq), jnp.int16),
                        pltpu.VMEM((S, tq), jnp.int16),
                        pltpu.VMEM((1, tq), jnp.int32),
                        pltpu.VMEM((IDX_HEADS, tq, LANE), BF16),
                        pltpu.VMEM((2, kc, A_HEADS * tq), F32),
                        pltpu.VMEM((HEAD_DIM, A_HEADS * tq), F32),
                        pltpu.VMEM((1, A_HEADS * tq), F32),
                        pltpu.VMEM((1, A_HEADS * tq), F32)],
        compiler_params=_cparams(("arbitrary",)),
        name="dsa_attention",
    )(qk, pidx, wt, qk, vt, pidx, band)


DIL_ROWS = 384


def _dilated_kernel(q_ref, k_ref, vt_ref, b0_ref, b1_ref, b2_ref, o_ref, s_ref, *, tq):
    t0 = pl.program_id(1) * tq
    q = q_ref[...]
    tiles = []
    row0 = 0
    for band_ref, (_, dil) in zip((b0_ref, b1_ref, b2_ref), DILATED_PAIRS):
        span = DIL_STEPS * dil
        start = jnp.maximum(t0 - span, 0)
        off = span - (t0 - start)
        for j in range((span + tq) // DIL_ROWS):
            ks = pl.multiple_of(start + j * DIL_ROWS, LANE)
            bs = pl.multiple_of(off + j * DIL_ROWS, LANE)
            s_ref[row0:row0 + DIL_ROWS, :] = (_dot_nt(k_ref[pl.ds(ks, DIL_ROWS), :], q)
                                              + band_ref[0, pl.ds(bs, DIL_ROWS), :])
            tiles.append((ks, row0))
            row0 += DIL_ROWS
    m = jnp.full((1, tq), NEG, F32)
    l = jnp.zeros((1, tq), F32)
    acc = jnp.zeros((HEAD_DIM, tq), F32)
    for ks, r0 in tiles:
        s = s_ref[r0:r0 + DIL_ROWS, :]
        m_new = jnp.maximum(m, jnp.max(s, axis=0, keepdims=True))
        alpha = jnp.exp2(m - m_new)
        p = jnp.exp2(s - m_new)
        l = alpha * l + jnp.sum(p, axis=0, keepdims=True)
        acc = alpha * acc + _dot(vt_ref[:, pl.ds(ks, DIL_ROWS)], p.astype(BF16))
        m = m_new
    o_ref[...] = (acc / l).T.astype(o_ref.dtype)


def dilated_attention(qk, vt, bands, vt_row0, tq=256):
    S = qk.shape[0]
    vh0 = vt_row0 // HEAD_DIM
    lengths = [DIL_STEPS * dil + tq for _, dil in DILATED_PAIRS]
    assert all(n % DIL_ROWS == 0 and n <= S for n in lengths)
    return pl.pallas_call(
        functools.partial(_dilated_kernel, tq=tq),
        grid=(B_HEADS, S // tq),
        in_specs=[pl.BlockSpec((tq, LANE), lambda h, i: (i, h)),
                  pl.BlockSpec((S, LANE), lambda h, i: (0, B_HEADS + h)),
                  pl.BlockSpec((HEAD_DIM, S), lambda h, i: (vh0 + h, 0))]
                 + [pl.BlockSpec((1,) + b.shape[1:], lambda h, i: (h, 0, 0)) for b in bands],
        out_specs=pl.BlockSpec((tq, LANE), lambda h, i: (i, h)),
        out_shape=jax.ShapeDtypeStruct((S, B_HEADS * HEAD_DIM), BF16),
        scratch_shapes=[pltpu.VMEM((sum(lengths), tq), F32)],
        compiler_params=_cparams(("parallel", "arbitrary")),
        name="dilated_attention",
    )(qk, qk, vt, *bands)


def _diff_kernel(lam_ref, g_ref, q_ref, k_ref, vt_ref, band_ref, o_ref, acc_ref, m_ref, l_ref, s_ref,
                 *, tq, tk, c0, lam_init):
    i = pl.program_id(1)
    t0 = i * tq
    nchunk = (t0 + tq + tk - 1) // tk
    last_chunk = k_ref.shape[0] // tk - 1
    q = q_ref[...]
    lane = lax.broadcasted_iota(jnp.int32, q.shape, 1)
    zero = jnp.zeros_like(q)
    q2 = jnp.concatenate([jnp.where(lane < C_HALF, q, zero), jnp.where(lane >= C_HALF, q, zero)], axis=0)

    m_ref[...] = jnp.full(m_ref.shape, NEG, F32)
    l_ref[...] = jnp.zeros(l_ref.shape, F32)
    acc_ref[...] = jnp.zeros(acc_ref.shape, F32)

    def logits(c):
        s0 = pl.multiple_of(jnp.minimum(c, last_chunk) * tk, tk)
        off = pl.multiple_of(c0 - jnp.clip(t0 - c * tk, -tk, c0), LANE)
        bias = band_ref[0, pl.ds(off, tk), :]
        return _dot_nt(k_ref[pl.ds(s0, tk), :], q2) + jnp.concatenate([bias, bias], axis=1)

    def consume(c, s):
        s0 = pl.multiple_of(jnp.minimum(c, last_chunk) * tk, tk)
        m_prev = m_ref[...]
        m_new = jnp.maximum(m_prev, jnp.max(s, axis=0, keepdims=True))
        alpha = jnp.exp2(m_prev - m_new)
        p = jnp.exp2(s - m_new)
        l_ref[...] = alpha * l_ref[...] + jnp.sum(p, axis=0, keepdims=True)
        acc_ref[...] = alpha * acc_ref[...] + _dot(vt_ref[:, pl.ds(s0, tk)], p.astype(BF16))
        m_ref[...] = m_new

    s_ref[0] = logits(0)

    def chunk_pair(cc, carry):
        c = 2 * cc
        s_ref[1] = logits(c + 1)
        consume(c, s_ref[0])
        s_ref[0] = logits(c + 2)
        consume(c + 1, s_ref[1])
        return carry

    lax.fori_loop(0, (nchunk + 1) // 2, chunk_pair, 0)

    lp = lam_ref[...]
    lam = (jnp.exp(jnp.sum(lp[0:1] * lp[1:2], axis=1, keepdims=True))
           - jnp.exp(jnp.sum(lp[2:3] * lp[3:4], axis=1, keepdims=True)) + lam_init)
    o = acc_ref[...] / l_ref[...]
    o = o[:, :tq] - lam * o[:, tq:]
    ms = jnp.mean(o * o, axis=0, keepdims=True)
    o = o * lax.rsqrt(ms + RMS_EPS) * (g_ref[...] * (1.0 - lam_init))
    o_ref[...] = o.T.astype(o_ref.dtype)


def diff_attention(qk, vt, band, lam_params, subln_g, lam_init, vt_row0, tq=256, tk=512):
    S = qk.shape[0]
    vh0 = vt_row0 // HEAD_DIM
    return pl.pallas_call(
        functools.partial(_diff_kernel, tq=tq, tk=tk, c0=_band_c0(tk), lam_init=lam_init),
        grid=(C_HEADS, S // tq),
        in_specs=[pl.BlockSpec((4, C_HALF), lambda h, i: (0, 0)),
                  pl.BlockSpec((HEAD_DIM, 1), lambda h, i: (0, 0)),
                  pl.BlockSpec((tq, LANE), lambda h, i: (i, h)),
                  pl.BlockSpec((S, LANE), lambda h, i: (0, C_HEADS + h)),
                  pl.BlockSpec((HEAD_DIM, S), lambda h, i: (vh0 + h, 0)),
                  pl.BlockSpec((1,) + band.shape[1:], lambda h, i: (h, 0, 0))],
        out_specs=pl.BlockSpec((tq, LANE), lambda h, i: (i, h)),
        out_shape=jax.ShapeDtypeStruct((S, C_HEADS * HEAD_DIM), BF16),
        scratch_shapes=[pltpu.VMEM((HEAD_DIM, 2 * tq), F32),
                        pltpu.VMEM((1, 2 * tq), F32),
                        pltpu.VMEM((1, 2 * tq), F32),
                        pltpu.VMEM((2, tk, 2 * tq), F32)],
        compiler_params=_cparams(("parallel", "arbitrary")),
        name="diff_attention",
    )(lam_params, subln_g.reshape(HEAD_DIM, 1), qk, qk, vt, band)


def _outproj_kernel(x_ref, a_ref, b_ref, c_ref, wa_ref, wb_ref, wc_ref, o_ref):
    o_ref[...] = (x_ref[...] + _dot(a_ref[...], wa_ref[...]) + _dot(b_ref[...], wb_ref[...])
                  + _dot(c_ref[...], wc_ref[...]))


def out_projection(x, oa, ob, oc, wa, wb, wc, tm=512, tn=1024):
    S, D = x.shape
    tn = min(tn, D)
    return pl.pallas_call(
        _outproj_kernel,
        grid=(S // tm, D // tn),
        in_specs=[pl.BlockSpec((tm, tn), lambda i, j: (i, j)),
                  pl.BlockSpec((tm, oa.shape[1]), lambda i, j: (i, 0)),
                  pl.BlockSpec((tm, ob.shape[1]), lambda i, j: (i, 0)),
                  pl.BlockSpec((tm, oc.shape[1]), lambda i, j: (i, 0)),
                  pl.BlockSpec((wa.shape[0], tn), lambda i, j: (0, j)),
                  pl.BlockSpec((wb.shape[0], tn), lambda i, j: (0, j)),
                  pl.BlockSpec((wc.shape[0], tn), lambda i, j: (0, j))],
        out_specs=pl.BlockSpec((tm, tn), lambda i, j: (i, j)),
        out_shape=jax.ShapeDtypeStruct((S, D), F32),
        compiler_params=_cparams(("parallel", "arbitrary")),
        name="out_projection",
    )(x, oa, ob, oc, wa, wb, wc)


def _silu(x):
    return x / (1.0 + jnp.exp(-x))


def _ffn_kernel(x_ref, h_ref, wg_ref, wu_ref, wd_ref, o_ref):
    f = pl.program_id(1)

    @pl.when(f == 0)
    def _():
        o_ref[...] = x_ref[...]

    h = h_ref[...]
    a = _silu(_dot(h, wg_ref[...])) * _dot(h, wu_ref[...])
    o_ref[...] += _dot(a.astype(BF16), wd_ref[...])


def ffn_dense(x, h, wg, wu, wd, tm=512, tf=512):
    S, D = x.shape
    F = wg.shape[1]
    return pl.pallas_call(
        _ffn_kernel,
        grid=(S // tm, F // tf),
        in_specs=[pl.BlockSpec((tm, D), lambda i, f: (i, 0)),
                  pl.BlockSpec((tm, D), lambda i, f: (i, 0)),
                  pl.BlockSpec((D, tf), lambda i, f: (0, f)),
                  pl.BlockSpec((D, tf), lambda i, f: (0, f)),
                  pl.BlockSpec((tf, D), lambda i, f: (f, 0))],
        out_specs=pl.BlockSpec((tm, D), lambda i, f: (i, 0)),
        out_shape=jax.ShapeDtypeStruct((S, D), F32),
        compiler_params=_cparams(("parallel", "arbitrary")),
        name="ffn_dense",
    )(x, h, wg, wu, wd)


def _router_kernel(x_ref, g_ref, w_ref, c_ref, sel_ref):
    x = x_ref[...]
    ms = jnp.mean(x * x, axis=-1, keepdims=True)
    h = x * lax.rsqrt(ms + RMS_EPS) * g_ref[...]
    logits = jnp.dot(h, w_ref[...], preferred_element_type=F32, precision=lax.Precision.HIGHEST)
    lane = lax.broadcasted_iota(jnp.int32, logits.shape, 1).astype(F32)
    logits = jnp.where(lane < N_EXPERTS, logits, NEG)
    m1 = jnp.max(logits, axis=1, keepdims=True)
    i1 = jnp.min(jnp.where(logits == m1, lane, float(LANE)), axis=1, keepdims=True)
    rest = jnp.where(lane == i1, NEG, logits)
    m2 = jnp.max(rest, axis=1, keepdims=True)
    i2 = jnp.min(jnp.where(rest == m2, lane, float(LANE)), axis=1, keepdims=True)
    e2 = jnp.exp(m2 - m1)
    g1 = 1.0 / (1.0 + e2)
    g2 = e2 / (1.0 + e2)
    pick1, pick2 = lane == i1, lane == i2
    c_ref[...] = jnp.where(pick1, g1, 0.0) + jnp.where(pick2, g2, 0.0)
    sel_ref[...] = jnp.where(pick1 | pick2, 1.0, 0.0)


def moe_router(x, g, w_router, tm=512):
    S, D = x.shape
    wpad = jnp.zeros((D, LANE), F32).at[:, :N_EXPERTS].set(w_router)
    spec = pl.BlockSpec((tm, LANE), lambda i: (i, 0))
    sds = jax.ShapeDtypeStruct((S, LANE), F32)
    return pl.pallas_call(
        _router_kernel,
        grid=(S // tm,),
        in_specs=[pl.BlockSpec((tm, D), lambda i: (i, 0)),
                  pl.BlockSpec((1, D), lambda i: (0, 0)),
                  pl.BlockSpec((D, LANE), lambda i: (0, 0))],
        out_specs=[spec, spec],
        out_shape=[sds, sds],
        compiler_params=_cparams(("parallel",)),
        name="moe_router",
    )(x, g.reshape(1, D), wpad)


MOE_CH = 256
MOE_TILE = 512


def _moe_rank_kernel(sel_ref, rm_ref, rmt_ref, cum_ref, tot_ref, carry_ref):
    c = pl.program_id(0)

    @pl.when(c == 0)
    def _():
        carry_ref[...] = jnp.zeros(carry_ref.shape, F32)

    sel = sel_ref[...]
    ch = sel.shape[0]
    before = (lax.broadcasted_iota(jnp.int32, (ch, ch), 1) < lax.broadcasted_iota(jnp.int32, (ch, ch), 0))
    rank = _dot(jnp.where(before, 1.0, 0.0).astype(BF16), sel.astype(BF16)) + carry_ref[...]
    rm = jnp.where(sel > 0.0, rank, -1.0)
    rm_ref[...] = rm
    rmt_ref[...] = rm.T[:N_EXPERTS]
    cum_ref[0] = carry_ref[...]
    carry_ref[...] += jnp.sum(sel, axis=0, keepdims=True)
    tot_ref[...] = carry_ref[...]


def moe_rank(sel):
    S = sel.shape[0]
    nch = S // MOE_CH
    return pl.pallas_call(
        _moe_rank_kernel,
        grid=(nch,),
        in_specs=[pl.BlockSpec((MOE_CH, LANE), lambda c: (c, 0))],
        out_specs=[pl.BlockSpec((MOE_CH, LANE), lambda c: (c, 0)),
                   pl.BlockSpec((N_EXPERTS, MOE_CH), lambda c: (0, c)),
                   pl.BlockSpec((1, 1, LANE), lambda c: (c, 0, 0)),
                   pl.BlockSpec((1, LANE), lambda c: (0, 0))],
        out_shape=[jax.ShapeDtypeStruct((S, LANE), F32),
                   jax.ShapeDtypeStruct((N_EXPERTS, S), F32),
                   jax.ShapeDtypeStruct((nch, 1, LANE), F32),
                   jax.ShapeDtypeStruct((1, LANE), F32)],
        scratch_shapes=[pltpu.VMEM((1, LANE), F32)],
        compiler_params=_cparams(("arbitrary",)),
        name="moe_rank",
    )(sel)


def _moe_expert_kernel(te_ref, tr0_ref, tclo_ref, tchi_ref, tval_ref,
                       rk_ref, h_hbm, wg_ref, wu_ref, wd_ref, y_ref,
                       hs_ref, acc_ref, hbuf_ref, sem_ref):
    j = pl.program_id(0)
    f = pl.program_id(1)
    nf = pl.num_programs(1)
    valid = tval_ref[j] > 0
    rows = hs_ref.shape[0]

    def chunk_copy(c, slot):
        return pltpu.make_async_copy(h_hbm.at[pl.ds(c * MOE_CH, MOE_CH)], hbuf_ref.at[slot], sem_ref.at[slot])

    @pl.when(valid & (f == 0))
    def _():
        clo, chi = tclo_ref[j], tchi_ref[j]
        want = (tr0_ref[j] + lax.broadcasted_iota(jnp.int32, (rows, MOE_CH), 0)).astype(F32)
        acc_ref[...] = jnp.zeros(acc_ref.shape, F32)

        @pl.when(chi > clo)
        def _():
            chunk_copy(clo, 0).start()

        def body(c, carry):
            slot = (c - clo) % 2

            @pl.when(c + 1 < chi)
            def _():
                chunk_copy(c + 1, 1 - slot).start()

            chunk_copy(c, slot).wait()
            rk = rk_ref[0, :, pl.ds(pl.multiple_of(c * MOE_CH, MOE_CH), MOE_CH)]
            onehot = jnp.where(rk == want, 1.0, 0.0).astype(BF16)
            acc_ref[...] += _dot(onehot, hbuf_ref[slot])
            return carry

        lax.fori_loop(clo, chi, body, 0)
        hs_ref[...] = acc_ref[...].astype(BF16)
        acc_ref[...] = jnp.zeros(acc_ref.shape, F32)

    @pl.when(valid)
    def _():
        hs = hs_ref[...]
        a = _silu(_dot(hs, wg_ref[0])) * _dot(hs, wu_ref[0])
        acc_ref[...] += _dot(a.astype(BF16), wd_ref[0])

    @pl.when(f == nf - 1)
    def _():
        y_ref[...] = jnp.where(valid, acc_ref[...], 0.0).astype(y_ref.dtype)


def _moe_combine_kernel(ca_ref, cb_ref, roff_ref, x_ref, rm_ref, cw_ref, ya_ref, yb_ref, o_ref):
    c = pl.program_id(0)
    e = pl.program_id(1)

    @pl.when(e == 0)
    def _():
        o_ref[...] = x_ref[...]

    rm = rm_ref[...]
    lane = lax.broadcasted_iota(jnp.int32, rm.shape, 1)
    mine = lane == e
    pos = jnp.sum(jnp.where(mine, rm, 0.0), axis=1, keepdims=True)
    gate = jnp.sum(jnp.where(mine, cw_ref[...], 0.0), axis=1, keepdims=True)
    row = jnp.where(pos >= 0.0, pos + roff_ref[e].astype(F32), -1.0)
    k = c * pl.num_programs(1) + e
    ca, cb = ca_ref[k], cb_ref[k]
    cr = ya_ref.shape[0]
    col = lax.broadcasted_iota(jnp.int32, (rm.shape[0], cr), 1)
    for y_ref, base in ((ya_ref, ca * cr), (yb_ref, jnp.where(cb != ca, cb * cr, -2 * cr))):
        onehot = jnp.where(row == (base + col).astype(F32), 1.0, 0.0).astype(BF16)
        o_ref[...] += gate * _dot(onehot, y_ref[...])


def ffn_moe(x, h, combine, sel, wg, wu, wd, tf=1024):
    S, D = x.shape
    E, _, F = wg.shape
    tf = min(tf, F)
    T = MOE_TILE
    nch = S // MOE_CH
    J = 2 * S // T + E
    rm, rmt, cum, tot = moe_rank(sel)

    cnt = tot[0, :E].astype(jnp.int32)
    ntile = (cnt + T - 1) // T
    tend = jnp.cumsum(ntile)
    tstart = tend - ntile
    jj = jnp.arange(J, dtype=jnp.int32)
    tval = (jj < tend[-1]).astype(jnp.int32)
    te = jnp.minimum(jnp.searchsorted(tend, jnp.minimum(jj, tend[-1] - 1), side="right"), E - 1).astype(jnp.int32)
    tr0 = (jnp.minimum(jj, tend[-1] - 1) - tstart[te]) * T
    cumx = cum[:, 0, :E].astype(jnp.int32)
    cumi = jnp.concatenate([cumx[1:], cnt[None]], axis=0)
    tclo = jnp.sum(cumi[:, te] <= tr0[None, :], axis=0).astype(jnp.int32)
    tchi = jnp.sum(cumx[:, te] < (tr0 + T)[None, :], axis=0).astype(jnp.int32)
    roff = (tstart * T).astype(jnp.int32)
    nrow_chunks = J * T // MOE_CH
    ca = ((roff[None, :] + cumx) // MOE_CH).astype(jnp.int32)
    cb = jnp.minimum(ca + 1, nrow_chunks - 1)
    ca, cb = ca.reshape(-1), cb.reshape(-1)

    nf = F // tf

    def wmap(j, f, te_r, tr0_r, clo_r, chi_r, val_r):
        return (te_r[j], 0, jnp.where(val_r[j] > 0, f, nf - 1))

    def wdmap(j, f, te_r, tr0_r, clo_r, chi_r, val_r):
        return (te_r[j], jnp.where(val_r[j] > 0, f, nf - 1), 0)

    y = pl.pallas_call(
        _moe_expert_kernel,
        grid_spec=pltpu.PrefetchScalarGridSpec(
            num_scalar_prefetch=5,
            grid=(J, nf),
            in_specs=[pl.BlockSpec((1, 1, S), lambda j, f, te_r, *_: (te_r[j], 0, 0)),
                      pl.BlockSpec(memory_space=pl.ANY),
                      pl.BlockSpec((1, D, tf), wmap),
                      pl.BlockSpec((1, D, tf), wmap),
                      pl.BlockSpec((1, tf, D), wdmap)],
            out_specs=pl.BlockSpec((T, D), lambda j, f, *_: (j, 0)),
            scratch_shapes=[pltpu.VMEM((T, D), BF16),
                            pltpu.VMEM((T, D), F32),
                            pltpu.VMEM((2, MOE_CH, D), BF16),
                            pltpu.SemaphoreType.DMA((2,))]),
        out_shape=jax.ShapeDtypeStruct((J * T, D), BF16),
        compiler_params=_cparams(("arbitrary", "arbitrary")),
        name="moe_experts",
    )(te, tr0, tclo, tchi, tval, rmt.reshape(E, 1, S), h, wg, wu, wd)

    return pl.pallas_call(
        _moe_combine_kernel,
        grid_spec=pltpu.PrefetchScalarGridSpec(
            num_scalar_prefetch=3,
            grid=(nch, E),
            in_specs=[pl.BlockSpec((MOE_CH, D), lambda c, e, *_: (c, 0)),
                      pl.BlockSpec((MOE_CH, LANE), lambda c, e, *_: (c, 0)),
                      pl.BlockSpec((MOE_CH, LANE), lambda c, e, *_: (c, 0)),
                      pl.BlockSpec((MOE_CH, D), lambda c, e, ca_r, cb_r, ro_r: (ca_r[c * E + e], 0)),
                      pl.BlockSpec((MOE_CH, D), lambda c, e, ca_r, cb_r, ro_r: (cb_r[c * E + e], 0))],
            out_specs=pl.BlockSpec((MOE_CH, D), lambda c, e, *_: (c, 0))),
        out_shape=jax.ShapeDtypeStruct((S, D), F32),
        compiler_params=_cparams(("parallel", "arbitrary")),
        name="moe_combine",
    )(ca, cb, roff, x, rm, combine, y, y)


def _tile_gain(g, reps, scale=1.0):
    return jnp.tile(g.astype(F32) * scale, reps)


def kernel(x, w_in, w_out, attn_norm_g, ffn_norm_g, q_norm_a, k_norm_a, q_norm_b, k_norm_b,
           q_norm_c, k_norm_c, lambda_q1, lambda_k1, lambda_q2, lambda_k2, diff_subln_g, rel_bias,
           w_dense_gate, w_dense_up, w_dense_down, w_router, w_moe_gate, w_moe_up, w_moe_down):
    B, S, D = x.shape
    depth = w_in.shape[0]
    assert B == 1 and S % (DIL_STEPS * DILATED_PAIRS[-1][1]) == 0
    aw, bw, cw = A_HEADS * HEAD_DIM, B_HEADS * HEAD_DIM, C_HEADS * HEAD_DIM
    iw = IDX_HEADS * IDX_DIM
    sizes = (aw, aw, aw, iw, IDX_DIM, IDX_HEADS, bw, bw, bw, cw, cw, cw)
    offs = np.concatenate([[0], np.cumsum(sizes)]).tolist()

    def cols(w, *segs):
        return jnp.concatenate([w[:, offs[s]:offs[s + 1]] for s in segs], axis=1)

    tq, tk = 256, 512
    band_a = bias_band(rel_bias, A_HEADS, 0, _band_c0(tk) + 2 * tk, tq, _band_c0(tk))
    band_c = bias_band(rel_bias, C_HEADS, A_HEADS + B_HEADS, _band_c0(tk) + 2 * tk, tq, _band_c0(tk))
    bands_b = [bias_band(rel_bias, B_HEADS, A_HEADS, 2 * DIL_STEPS * dil + tq, tq, DIL_STEPS * dil, dil)
               for _, dil in DILATED_PAIRS]

    xs = x.reshape(S, D)
    for layer in range(depth):
        wl = w_in[layer]
        w_pa = cols(wl, 0, 1).astype(BF16)
        w_pb = cols(wl, 6, 7).astype(BF16)
        w_pc = cols(wl, 9, 10).astype(BF16)
        w_pi = cols(wl, 3, 4, 4).astype(BF16)
        w_vt = cols(wl, 2, 8, 11).T.astype(BF16)
        w_wt = cols(wl, 5).T.astype(BF16)
        qs_ab, qs_c = HEAD_DIM ** -0.5 * LOG2E, C_HALF ** -0.5 * LOG2E
        g_pa = jnp.concatenate([_tile_gain(q_norm_a[layer], A_HEADS, qs_ab), _tile_gain(k_norm_a[layer], A_HEADS)])
        g_pb = jnp.concatenate([_tile_gain(q_norm_b[layer], B_HEADS, qs_ab), _tile_gain(k_norm_b[layer], B_HEADS)])
        g_pc = jnp.concatenate([_tile_gain(q_norm_c[layer], 2 * C_HEADS, qs_c),
                                _tile_gain(k_norm_c[layer], 2 * C_HEADS)])

        h = rmsnorm(xs, attn_norm_g[layer])
        p_a = project(h, w_pa, g_pa, HEAD_DIM, BF16, tn=512)
        p_b = project(h, w_pb, g_pb, HEAD_DIM, BF16, tn=512)
        p_c = project(h, w_pc, g_pc, C_HALF, BF16, tn=512)
        p_i = project(h, w_pi, jnp.ones((w_pi.shape[1],), F32), 0, BF16, tn=384)
        vt = project_t(h, w_vt, BF16)
        wt = project_t(h, w_wt, F32)

        oa = dsa_attention(p_a, vt, p_i, wt, band_a)

        ob = dilated_attention(p_b, vt, bands_b, aw, tq=tq)

        lam_init = 0.8 - 0.6 * math.exp(-0.3 * layer)
        lam_params = jnp.stack([lambda_q1[layer], lambda_k1[layer], lambda_q2[layer], lambda_k2[layer]]).astype(F32)
        oc = diff_attention(p_c, vt, band_c, lam_params, diff_subln_g[layer].astype(F32), lam_init, aw + bw,
                            tq=tq, tk=tk)

        wo = w_out[layer].astype(BF16)
        xs = out_projection(xs, oa, ob, oc, wo[:aw], wo[aw:aw + bw], wo[aw + bw:])

        f = layer // 2
        if layer % 2 == 0:
            h2 = rmsnorm(xs, ffn_norm_g[layer])
            xs = ffn_dense(xs, h2, w_dense_gate[f].astype(BF16), w_dense_up[f].astype(BF16),
                           w_dense_down[f].astype(BF16))
        else:
            h2 = rmsnorm(xs, ffn_norm_g[layer])
            combine, sel = moe_router(xs, ffn_norm_g[layer], w_router[f])
            xs = ffn_moe(xs, h2, combine, sel, w_moe_gate[f].astype(BF16), w_moe_up[f].astype(BF16),
                         w_moe_down[f].astype(BF16))
    return xs.reshape(B, S, D)
```

```python
import functools
import math

import jax
import jax.numpy as jnp
import numpy as np
from jax import lax
from jax.experimental import pallas as pl
from jax.experimental.pallas import tpu as pltpu

F32 = jnp.float32
BF16 = jnp.bfloat16

HEAD_DIM = 128
A_HEADS, B_HEADS, C_HEADS = 4, 6, 6
C_HALF = HEAD_DIM // 2
IDX_HEADS, IDX_DIM = 16, 64
TOPK_MAX = 256
DILATED_PAIRS = ((128, 1), (512, 4), (2048, 16))
DIL_STEPS = 128
N_BUCKETS = 32
MAX_DISTANCE = 2048
N_EXPERTS = 8
RMS_EPS = 1e-6
NEG = -1e30
LOG2E = math.log2(math.e)
INT_MIN = -(2 ** 31)
I16_BIAS = 2 ** 15
CNT_ROWS = 64
LANE = 128
VMEM_LIMIT = 56 * 1024 * 1024

BUCKET_EDGES = tuple(range(16)) + (16,) + tuple(
    int(math.ceil(16.0 * (MAX_DISTANCE / 16.0) ** (k / 16.0))) for k in range(1, 16))
FAR_DIST = BUCKET_EDGES[-1]


def _cparams(sem):
    return pltpu.CompilerParams(dimension_semantics=sem, vmem_limit_bytes=VMEM_LIMIT)


def _dot(a, b):
    return jnp.dot(a, b, preferred_element_type=F32)


def _dot_nt(a, b):
    return lax.dot_general(a, b, (((1,), (1,)), ((), ())), preferred_element_type=F32)


def _rmsnorm_kernel(x_ref, g_ref, o_ref):
    x = x_ref[...]
    ms = jnp.mean(x * x, axis=-1, keepdims=True)
    o_ref[...] = (x * lax.rsqrt(ms + RMS_EPS) * g_ref[...]).astype(o_ref.dtype)


def rmsnorm(x, g, tm=512):
    S, D = x.shape
    return pl.pallas_call(
        _rmsnorm_kernel,
        grid=(S // tm,),
        in_specs=[pl.BlockSpec((tm, D), lambda i: (i, 0)), pl.BlockSpec((1, D), lambda i: (0, 0))],
        out_specs=pl.BlockSpec((tm, D), lambda i: (i, 0)),
        out_shape=jax.ShapeDtypeStruct((S, D), BF16),
        compiler_params=_cparams(("parallel",)),
        name="rmsnorm",
    )(x, g.reshape(1, D))


def _proj_kernel(h_ref, w_ref, g_ref, o_ref, *, group):
    acc = _dot(h_ref[...], w_ref[...])
    tn = acc.shape[1]
    if group == 0:
        o_ref[...] = acc.astype(o_ref.dtype)
        return
    for c in range(tn // LANE):
        a = acc[:, c * LANE:(c + 1) * LANE]
        sq = a * a
        if group == LANE:
            ms = jnp.mean(sq, axis=-1, keepdims=True)
        else:
            lane = lax.broadcasted_iota(jnp.int32, sq.shape, 1)
            lo = jnp.sum(jnp.where(lane < group, sq, 0.0), axis=-1, keepdims=True)
            hi = jnp.sum(jnp.where(lane >= group, sq, 0.0), axis=-1, keepdims=True)
            ms = jnp.where(lane < group, lo, hi) * (1.0 / group)
        y = a * lax.rsqrt(ms + RMS_EPS) * g_ref[:, c * LANE:(c + 1) * LANE]
        o_ref[:, c * LANE:(c + 1) * LANE] = y.astype(o_ref.dtype)


def project(h, w, gains, group, out_dtype, tm=1024, tn=256):
    S, K = h.shape
    N = w.shape[1]
    return pl.pallas_call(
        functools.partial(_proj_kernel, group=group),
        grid=(S // tm, N // tn),
        in_specs=[pl.BlockSpec((tm, K), lambda i, j: (i, 0)),
                  pl.BlockSpec((K, tn), lambda i, j: (0, j)),
                  pl.BlockSpec((1, tn), lambda i, j: (0, j))],
        out_specs=pl.BlockSpec((tm, tn), lambda i, j: (i, j)),
        out_shape=jax.ShapeDtypeStruct((S, N), out_dtype),
        compiler_params=_cparams(("parallel", "arbitrary")),
        name=f"proj_g{group}",
    )(h, w, gains.reshape(1, N))


def _proj_t_kernel(wt_ref, h_ref, o_ref):
    o_ref[...] = _dot_nt(wt_ref[...], h_ref[...]).astype(o_ref.dtype)


def project_t(h, wt, out_dtype, tm=1024, tn=256):
    S, K = h.shape
    N = wt.shape[0]
    tn = min(tn, N)
    return pl.pallas_call(
        _proj_t_kernel,
        grid=(S // tm, N // tn),
        in_specs=[pl.BlockSpec((tn, K), lambda i, j: (j, 0)),
                  pl.BlockSpec((tm, K), lambda i, j: (i, 0))],
        out_specs=pl.BlockSpec((tn, tm), lambda i, j: (j, i)),
        out_shape=jax.ShapeDtypeStruct((N, S), out_dtype),
        compiler_params=_cparams(("parallel", "arbitrary")),
        name="proj_t",
    )(wt, h)


BAND_ROWS = 256


def _band_kernel(tab_ref, o_ref, *, c0, head_off, dil):
    hh = head_off + pl.program_id(0)
    _, rows, width = o_ref.shape
    dist = (lax.broadcasted_iota(jnp.int32, (rows, width), 1)
            - lax.broadcasted_iota(jnp.int32, (rows, width), 0) + (c0 - pl.program_id(1) * rows))
    val = jnp.full((rows, width), tab_ref[0, hh] * LOG2E, F32)
    for b in range(1, N_BUCKETS):
        val = jnp.where(dist >= BUCKET_EDGES[b], tab_ref[b, hh] * LOG2E, val)
    ok = dist >= 0
    if dil:
        ok = ok & (dist <= DIL_STEPS * dil) & ((dist & (dil - 1)) == 0)
    o_ref[0] = jnp.where(ok, val, NEG)


def bias_band(rel_bias, n_heads, head_off, rows, width, c0, dil=0):
    assert rows % BAND_ROWS == 0 and dil & (dil - 1) == 0
    return pl.pallas_call(
        functools.partial(_band_kernel, c0=c0, head_off=head_off, dil=dil),
        grid=(n_heads, rows // BAND_ROWS),
        in_specs=[pl.BlockSpec(memory_space=pltpu.SMEM)],
        out_specs=pl.BlockSpec((1, BAND_ROWS, width), lambda h, r: (h, r, 0)),
        out_shape=jax.ShapeDtypeStruct((n_heads, rows, width), F32),
        compiler_params=_cparams(("parallel", "parallel")),
        name=f"bias_band_d{dil}",
    )(rel_bias)


def _band_c0(tk):
    return -(-(FAR_DIST + tk - 1) // LANE) * LANE


def _dsa_kernel(q_ref, qi_ref, wt_ref, k_ref, vt_ref, kidx_ref, band_ref, o_ref,
                hi_ref, lo_ref, cut_ref, qih_ref, s_ref, acc_ref, m_ref, l_ref, *, tq, kc, topk, c0):
    i = pl.program_id(0)
    t0 = i * tq
    nchunk = (t0 + tq + kc - 1) // kc
    half = lax.broadcasted_iota(jnp.int32, (tq, LANE), 1) < IDX_DIM

    for h in range(IDX_HEADS):
        slab = qi_ref[:, (h // 2) * LANE:(h // 2 + 1) * LANE]
        keep = half if h % 2 == 0 else jnp.logical_not(half)
        qih_ref[h] = jnp.where(keep, slab, jnp.zeros_like(slab))
    w = wt_ref[...] * (IDX_HEADS ** -0.5 * IDX_DIM ** -0.5)

    pos_q = t0 + lax.broadcasted_iota(jnp.int32, (kc, tq), 1)
    row = lax.broadcasted_iota(jnp.int32, (kc, tq), 0)

    def score_chunk(c, carry):
        s0 = pl.multiple_of(c * kc, kc)
        kx = kidx_ref[pl.ds(s0, kc), :]
        acc = jnp.zeros((kc, tq), F32)
        for h in range(IDX_HEADS):
            acc = acc + w[h:h + 1, :] * jnp.maximum(_dot_nt(kx, qih_ref[h]), 0.0)
        bits = lax.bitcast_convert_type(acc, jnp.int32)
        key = bits ^ ((bits >> 31) & 0x7FFFFFFF)
        key = jnp.where(s0 + row <= pos_q, key, INT_MIN)
        hi_ref[pl.ds(s0, kc), :] = (key >> 16).astype(jnp.int16)
        lo_ref[pl.ds(s0, kc), :] = ((key & 0xFFFF) - I16_BIAS).astype(jnp.int16)
        return carry

    lax.fori_loop(0, nchunk, score_chunk, 0)

    one16, zero16 = jnp.ones((), jnp.int16), jnp.zeros((), jnp.int16)
    row16 = lax.broadcasted_iota(jnp.int32, (kc, tq), 0).astype(jnp.int16)
    rowc16 = lax.broadcasted_iota(jnp.int32, (CNT_ROWS, tq), 0).astype(jnp.int16)
    n_keys = k_ref.shape[0]

    def to16(v):
        return jnp.clip(v, -I16_BIAS, I16_BIAS - 1).astype(jnp.int16)

    def count(hit):
        def body(c, cnt):
            base = pl.multiple_of(c * kc, kc)
            for g in range(kc // CNT_ROWS):
                cnt = cnt + jnp.where(hit(base + g * CNT_ROWS), one16, zero16)
            return cnt

        cnt = lax.fori_loop(0, nchunk, body, jnp.zeros((CNT_ROWS, tq), jnp.int16))
        return jnp.sum(cnt.astype(jnp.int32).astype(F32), axis=0, keepdims=True)

    def count_ge(ref, cand):
        c16 = cand.astype(jnp.int16)
        return count(lambda r: ref[pl.ds(r, CNT_ROWS), :] >= c16)

    def count_gt(ref, cand):
        c16 = cand.astype(jnp.int16)
        return count(lambda r: ref[pl.ds(r, CNT_ROWS), :] > c16)

    def kth_largest(ref, need):
        t = jnp.where(count_ge(ref, jnp.zeros((1, tq), jnp.int32)) >= need, 0, -I16_BIAS).astype(jnp.int32)

        def bit_body(b, t):
            cand = t + jnp.left_shift(jnp.int32(1), 14 - b)
            return jnp.where(count_ge(ref, cand) >= need, cand, t)

        return lax.fori_loop(0, 15, bit_body, t)

    kf = jnp.full((1, tq), float(topk), F32)
    thi = kth_largest(hi_ref, kf)
    need_lo = kf - count_gt(hi_ref, thi)
    thi16 = thi.astype(jnp.int16)

    def mask_low(c, carry):
        rows = pl.ds(pl.multiple_of(c * kc, kc), kc)
        lo_ref[rows, :] = jnp.where(hi_ref[rows, :] == thi16, lo_ref[rows, :], jnp.int16(-I16_BIAS))
        return carry

    lax.fori_loop(0, nchunk, mask_low, 0)
    tlo = kth_largest(lo_ref, need_lo)
    tlo16 = tlo.astype(jnp.int16)

    def is_eq(r):
        return (hi_ref[pl.ds(r, CNT_ROWS), :] == thi16) & (lo_ref[pl.ds(r, CNT_ROWS), :] == tlo16)

    keep_eq = need_lo - count_gt(lo_ref, tlo)
    excess = (count(is_eq) > keep_eq) & (thi > -I16_BIAS)
    cut_ref[...] = jnp.full(cut_ref.shape, n_keys, jnp.int32)

    @pl.when(jnp.max(jnp.broadcast_to(jnp.where(excess, 1.0, 0.0), (8, tq))) > 0.0)
    def _():
        nbits = n_keys.bit_length()

        def bit_body(b, j):
            cand = j + jnp.left_shift(jnp.int32(1), nbits - 1 - b)
            before = count(lambda r: is_eq(r) & (rowc16 < to16(cand - r)))
            return jnp.where(before < keep_eq, cand, j)

        j = lax.fori_loop(0, nbits, bit_body, jnp.zeros((1, tq), jnp.int32))
        cut_ref[...] = jnp.where(excess, j, n_keys)

    cut = cut_ref[...]
    thi_sel16 = jnp.maximum(thi, 1 - I16_BIAS).astype(jnp.int16)

    m_ref[...] = jnp.full(m_ref.shape, NEG, F32)
    l_ref[...] = jnp.zeros(l_ref.shape, F32)
    acc_ref[...] = jnp.zeros(acc_ref.shape, F32)

    def logits(c):
        s0 = pl.multiple_of(jnp.minimum(c, nchunk - 1) * kc, kc)
        off = pl.multiple_of(c0 - jnp.clip(t0 - c * kc, -kc, c0), LANE)
        hi = hi_ref[pl.ds(s0, kc), :]
        lo = lo_ref[pl.ds(s0, kc), :]
        sel = (hi > thi_sel16) | ((hi == thi_sel16)
                                  & ((lo > tlo16) | ((lo == tlo16) & (row16 <= to16(cut - s0)))))
        drop = jnp.where(sel, jnp.zeros((), jnp.int16), jnp.ones((), jnp.int16))
        mask_bias = drop.astype(jnp.int32).astype(F32) * NEG
        parts = []
        for h in range(A_HEADS):
            hs = slice(h * HEAD_DIM, (h + 1) * HEAD_DIM)
            parts.append(_dot_nt(k_ref[pl.ds(s0, kc), hs], q_ref[:, hs]) + band_ref[h, pl.ds(off, kc), :]
                         + mask_bias)
        return jnp.concatenate(parts, axis=1)

    def consume(c, s):
        s0 = pl.multiple_of(jnp.minimum(c, nchunk - 1) * kc, kc)
        m_prev = m_ref[...]
        m_new = jnp.maximum(m_prev, jnp.max(s, axis=0, keepdims=True))
        alpha = jnp.exp2(m_prev - m_new)
        p = jnp.exp2(s - m_new)
        l_ref[...] = alpha * l_ref[...] + jnp.sum(p, axis=0, keepdims=True)
        p = p.astype(BF16)
        pv = [_dot(vt_ref[h * HEAD_DIM:(h + 1) * HEAD_DIM, pl.ds(s0, kc)], p[:, h * tq:(h + 1) * tq])
              for h in range(A_HEADS)]
        acc_ref[...] = alpha * acc_ref[...] + jnp.concatenate(pv, axis=1)
        m_ref[...] = m_new

    s_ref[0] = logits(0)

    def chunk_pair(cc, carry):
        c = 2 * cc
        s_ref[1] = logits(c + 1)
        consume(c, s_ref[0])
        s_ref[0] = logits(c + 2)
        consume(c + 1, s_ref[1])
        return carry

    lax.fori_loop(0, (nchunk + 1) // 2, chunk_pair, 0)
    o = acc_ref[...] / l_ref[...]
    for h in range(A_HEADS):
        o_ref[:, h * HEAD_DIM:(h + 1) * HEAD_DIM] = o[:, h * tq:(h + 1) * tq].T.astype(o_ref.dtype)


def dsa_attention(qk, vt, pidx, wt, band, tq=256, kc=512):
    S = qk.shape[0]
    topk = min(TOPK_MAX, S // 4)
    aw = A_HEADS * HEAD_DIM
    iw = IDX_HEADS * IDX_DIM
    once = pl.Buffered(1)
    return pl.pallas_call(
        functools.partial(_dsa_kernel, tq=tq, kc=kc, topk=topk, c0=_band_c0(kc)),
        grid=(S // tq,),
        in_specs=[pl.BlockSpec((tq, aw), lambda i: (i, 0)),
                  pl.BlockSpec((tq, iw), lambda i: (i, 0)),
                  pl.BlockSpec((IDX_HEADS, tq), lambda i: (0, i)),
                  pl.BlockSpec((S, aw), lambda i: (0, 1), pipeline_mode=once),
                  pl.BlockSpec((aw, S), lambda i: (0, 0), pipeline_mode=once),
                  pl.BlockSpec((S, LANE), lambda i: (0, iw // LANE), pipeline_mode=once),
                  pl.BlockSpec(band.shape, lambda i: (0, 0, 0), pipeline_mode=once)],
        out_specs=pl.BlockSpec((tq, aw), lambda i: (i, 0)),
        out_shape=jax.ShapeDtypeStruct((S, aw), BF16),
        scratch_shapes=[pltpu.VMEM((S, tq), jnp.int16),
                        pltpu.VMEM((S, tq), jnp.int16),
                        pltpu.VMEM((1, tq), jnp.int32),
                        pltpu.VMEM((IDX_HEADS, tq, LANE), BF16),
                        pltpu.VMEM((2, kc, A_HEADS * tq), F32),
                        pltpu.VMEM((HEAD_DIM, A_HEADS * tq), F32),
                        pltpu.VMEM((1, A_HEADS * tq), F32),
                        pltpu.VMEM((1, A_HEADS * tq), F32)],
        compiler_params=_cparams(("arbitrary",)),
        name="dsa_attention",
    )(qk, pidx, wt, qk, vt, pidx, band)


DIL_ROWS = 384


def _dilated_kernel(q_ref, k_ref, vt_ref, b0_ref, b1_ref, b2_ref, o_ref, s_ref, *, tq):
    t0 = pl.program_id(1) * tq
    q = q_ref[...]
    tiles = []
    row0 = 0
    for band_ref, (_, dil) in zip((b0_ref, b1_ref, b2_ref), DILATED_PAIRS):
        span = DIL_STEPS * dil
        start = jnp.maximum(t0 - span, 0)
        off = span - (t0 - start)
        for j in range((span + tq) // DIL_ROWS):
            ks = pl.multiple_of(start + j * DIL_ROWS, LANE)
            bs = pl.multiple_of(off + j * DIL_ROWS, LANE)
            s_ref[row0:row0 + DIL_ROWS, :] = (_dot_nt(k_ref[pl.ds(ks, DIL_ROWS), :], q)
                                              + band_ref[0, pl.ds(bs, DIL_ROWS), :])
            tiles.append((ks, row0))
            row0 += DIL_ROWS
    m = jnp.full((1, tq), NEG, F32)
    l = jnp.zeros((1, tq), F32)
    acc = jnp.zeros((HEAD_DIM, tq), F32)
    for ks, r0 in tiles:
        s = s_ref[r0:r0 + DIL_ROWS, :]
        m_new = jnp.maximum(m, jnp.max(s, axis=0, keepdims=True))
        alpha = jnp.exp2(m - m_new)
        p = jnp.exp2(s - m_new)
        l = alpha * l + jnp.sum(p, axis=0, keepdims=True)
        acc = alpha * acc + _dot(vt_ref[:, pl.ds(ks, DIL_ROWS)], p.astype(BF16))
        m = m_new
    o_ref[...] = (acc / l).T.astype(o_ref.dtype)


def dilated_attention(qk, vt, bands, vt_row0, tq=256):
    S = qk.shape[0]
    vh0 = vt_row0 // HEAD_DIM
    lengths = [DIL_STEPS * dil + tq for _, dil in DILATED_PAIRS]
    assert all(n % DIL_ROWS == 0 and n <= S for n in lengths)
    return pl.pallas_call(
        functools.partial(_dilated_kernel, tq=tq),
        grid=(B_HEADS, S // tq),
        in_specs=[pl.BlockSpec((tq, LANE), lambda h, i: (i, h)),
                  pl.BlockSpec((S, LANE), lambda h, i: (0, B_HEADS + h)),
                  pl.BlockSpec((HEAD_DIM, S), lambda h, i: (vh0 + h, 0))]
                 + [pl.BlockSpec((1,) + b.shape[1:], lambda h, i: (h, 0, 0)) for b in bands],
        out_specs=pl.BlockSpec((tq, LANE), lambda h, i: (i, h)),
        out_shape=jax.ShapeDtypeStruct((S, B_HEADS * HEAD_DIM), BF16),
        scratch_shapes=[pltpu.VMEM((sum(lengths), tq), F32)],
        compiler_params=_cparams(("parallel", "arbitrary")),
        name="dilated_attention",
    )(qk, qk, vt, *bands)


def _diff_kernel(lam_ref, g_ref, q_ref, k_ref, vt_ref, band_ref, o_ref, acc_ref, m_ref, l_ref, s_ref,
                 *, tq, tk, c0, lam_init):
    i = pl.program_id(1)
    t0 = i * tq
    nchunk = (t0 + tq + tk - 1) // tk
    last_chunk = k_ref.shape[0] // tk - 1
    q = q_ref[...]
    lane = lax.broadcasted_iota(jnp.int32, q.shape, 1)
    zero = jnp.zeros_like(q)
    q2 = jnp.concatenate([jnp.where(lane < C_HALF, q, zero), jnp.where(lane >= C_HALF, q, zero)], axis=0)

    m_ref[...] = jnp.full(m_ref.shape, NEG, F32)
    l_ref[...] = jnp.zeros(l_ref.shape, F32)
    acc_ref[...] = jnp.zeros(acc_ref.shape, F32)

    def logits(c):
        s0 = pl.multiple_of(jnp.minimum(c, last_chunk) * tk, tk)
        off = pl.multiple_of(c0 - jnp.clip(t0 - c * tk, -tk, c0), LANE)
        bias = band_ref[0, pl.ds(off, tk), :]
        return _dot_nt(k_ref[pl.ds(s0, tk), :], q2) + jnp.concatenate([bias, bias], axis=1)

    def consume(c, s):
        s0 = pl.multiple_of(jnp.minimum(c, last_chunk) * tk, tk)
        m_prev = m_ref[...]
        m_new = jnp.maximum(m_prev, jnp.max(s, axis=0, keepdims=True))
        alpha = jnp.exp2(m_prev - m_new)
        p = jnp.exp2(s - m_new)
        l_ref[...] = alpha * l_ref[...] + jnp.sum(p, axis=0, keepdims=True)
        acc_ref[...] = alpha * acc_ref[...] + _dot(vt_ref[:, pl.ds(s0, tk)], p.astype(BF16))
        m_ref[...] = m_new

    s_ref[0] = logits(0)

    def chunk_pair(cc, carry):
        c = 2 * cc
        s_ref[1] = logits(c + 1)
        consume(c, s_ref[0])
        s_ref[0] = logits(c + 2)
        consume(c + 1, s_ref[1])
        return carry

    lax.fori_loop(0, (nchunk + 1) // 2, chunk_pair, 0)

    lp = lam_ref[...]
    lam = (jnp.exp(jnp.sum(lp[0:1] * lp[1:2], axis=1, keepdims=True))
           - jnp.exp(jnp.sum(lp[2:3] * lp[3:4], axis=1, keepdims=True)) + lam_init)
    o = acc_ref[...] / l_ref[...]
    o = o[:, :tq] - lam * o[:, tq:]
    ms = jnp.mean(o * o, axis=0, keepdims=True)
    o = o * lax.rsqrt(ms + RMS_EPS) * (g_ref[...] * (1.0 - lam_init))
    o_ref[...] = o.T.astype(o_ref.dtype)


def diff_attention(qk, vt, band, lam_params, subln_g, lam_init, vt_row0, tq=256, tk=512):
    S = qk.shape[0]
    vh0 = vt_row0 // HEAD_DIM
    return pl.pallas_call(
        functools.partial(_diff_kernel, tq=tq, tk=tk, c0=_band_c0(tk), lam_init=lam_init),
        grid=(C_HEADS, S // tq),
        in_specs=[pl.BlockSpec((4, C_HALF), lambda h, i: (0, 0)),
                  pl.BlockSpec((HEAD_DIM, 1), lambda h, i: (0, 0)),
                  pl.BlockSpec((tq, LANE), lambda h, i: (i, h)),
                  pl.BlockSpec((S, LANE), lambda h, i: (0, C_HEADS + h)),
                  pl.BlockSpec((HEAD_DIM, S), lambda h, i: (vh0 + h, 0)),
                  pl.BlockSpec((1,) + band.shape[1:], lambda h, i: (h, 0, 0))],
        out_specs=pl.BlockSpec((tq, LANE), lambda h, i: (i, h)),
        out_shape=jax.ShapeDtypeStruct((S, C_HEADS * HEAD_DIM), BF16),
        scratch_shapes=[pltpu.VMEM((HEAD_DIM, 2 * tq), F32),
                        pltpu.VMEM((1, 2 * tq), F32),
                        pltpu.VMEM((1, 2 * tq), F32),
                        pltpu.VMEM((2, tk, 2 * tq), F32)],
        compiler_params=_cparams(("parallel", "arbitrary")),
        name="diff_attention",
    )(lam_params, subln_g.reshape(HEAD_DIM, 1), qk, qk, vt, band)


def _outproj_kernel(x_ref, a_ref, b_ref, c_ref, wa_ref, wb_ref, wc_ref, o_ref):
    o_ref[...] = (x_ref[...] + _dot(a_ref[...], wa_ref[...]) + _dot(b_ref[...], wb_ref[...])
                  + _dot(c_ref[...], wc_ref[...]))


def out_projection(x, oa, ob, oc, wa, wb, wc, tm=512, tn=1024):
    S, D = x.shape
    tn = min(tn, D)
    return pl.pallas_call(
        _outproj_kernel,
        grid=(S // tm, D // tn),
        in_specs=[pl.BlockSpec((tm, tn), lambda i, j: (i, j)),
                  pl.BlockSpec((tm, oa.shape[1]), lambda i, j: (i, 0)),
                  pl.BlockSpec((tm, ob.shape[1]), lambda i, j: (i, 0)),
                  pl.BlockSpec((tm, oc.shape[1]), lambda i, j: (i, 0)),
                  pl.BlockSpec((wa.shape[0], tn), lambda i, j: (0, j)),
                  pl.BlockSpec((wb.shape[0], tn), lambda i, j: (0, j)),
                  pl.BlockSpec((wc.shape[0], tn), lambda i, j: (0, j))],
        out_specs=pl.BlockSpec((tm, tn), lambda i, j: (i, j)),
        out_shape=jax.ShapeDtypeStruct((S, D), F32),
        compiler_params=_cparams(("parallel", "arbitrary")),
        name="out_projection",
    )(x, oa, ob, oc, wa, wb, wc)


def _silu(x):
    return x / (1.0 + jnp.exp(-x))


def _ffn_kernel(x_ref, h_ref, wg_ref, wu_ref, wd_ref, o_ref):
    f = pl.program_id(1)

    @pl.when(f == 0)
    def _():
        o_ref[...] = x_ref[...]

    h = h_ref[...]
    a = _silu(_dot(h, wg_ref[...].astype(BF16))) * _dot(h, wu_ref[...].astype(BF16))
    o_ref[...] += _dot(a.astype(BF16), wd_ref[...].astype(BF16))


def ffn_dense(x, h, wg, wu, wd, tm=512, tf=512):
    S, D = x.shape
    F = wg.shape[1]
    return pl.pallas_call(
        _ffn_kernel,
        grid=(S // tm, F // tf),
        in_specs=[pl.BlockSpec((tm, D), lambda i, f: (i, 0)),
                  pl.BlockSpec((tm, D), lambda i, f: (i, 0)),
                  pl.BlockSpec((D, tf), lambda i, f: (0, f)),
                  pl.BlockSpec((D, tf), lambda i, f: (0, f)),
                  pl.BlockSpec((tf, D), lambda i, f: (f, 0))],
        out_specs=pl.BlockSpec((tm, D), lambda i, f: (i, 0)),
        out_shape=jax.ShapeDtypeStruct((S, D), F32),
        compiler_params=_cparams(("parallel", "arbitrary")),
        name="ffn_dense",
    )(x, h, wg, wu, wd)


def _router_kernel(x_ref, g_ref, w_ref, c_ref, sel_ref):
    x = x_ref[...]
    ms = jnp.mean(x * x, axis=-1, keepdims=True)
    h = x * lax.rsqrt(ms + RMS_EPS) * g_ref[...]
    logits = jnp.dot(h, w_ref[...], preferred_element_type=F32, precision=lax.Precision.HIGHEST)
    lane = lax.broadcasted_iota(jnp.int32, logits.shape, 1).astype(F32)
    logits = jnp.where(lane < N_EXPERTS, logits, NEG)
    m1 = jnp.max(logits, axis=1, keepdims=True)
    i1 = jnp.min(jnp.where(logits == m1, lane, float(LANE)), axis=1, keepdims=True)
    rest = jnp.where(lane == i1, NEG, logits)
    m2 = jnp.max(rest, axis=1, keepdims=True)
    i2 = jnp.min(jnp.where(rest == m2, lane, float(LANE)), axis=1, keepdims=True)
    e2 = jnp.exp(m2 - m1)
    g1 = 1.0 / (1.0 + e2)
    g2 = e2 / (1.0 + e2)
    pick1, pick2 = lane == i1, lane == i2
    c_ref[...] = jnp.where(pick1, g1, 0.0) + jnp.where(pick2, g2, 0.0)
    sel_ref[...] = jnp.where(pick1 | pick2, 1.0, 0.0)


def moe_router(x, g, w_router, tm=512):
    S, D = x.shape
    wpad = jnp.zeros((D, LANE), F32).at[:, :N_EXPERTS].set(w_router)
    spec = pl.BlockSpec((tm, LANE), lambda i: (i, 0))
    sds = jax.ShapeDtypeStruct((S, LANE), F32)
    return pl.pallas_call(
        _router_kernel,
        grid=(S // tm,),
        in_specs=[pl.BlockSpec((tm, D), lambda i: (i, 0)),
                  pl.BlockSpec((1, D), lambda i: (0, 0)),
                  pl.BlockSpec((D, LANE), lambda i: (0, 0))],
        out_specs=[spec, spec],
        out_shape=[sds, sds],
        compiler_params=_cparams(("parallel",)),
        name="moe_router",
    )(x, g.reshape(1, D), wpad)


MOE_CH = 256
MOE_TILE = 512


def _moe_rank_kernel(sel_ref, rm_ref, rmt_ref, cum_ref, tot_ref, carry_ref):
    c = pl.program_id(0)

    @pl.when(c == 0)
    def _():
        carry_ref[...] = jnp.zeros(carry_ref.shape, F32)

    sel = sel_ref[...]
    ch = sel.shape[0]
    before = (lax.broadcasted_iota(jnp.int32, (ch, ch), 1) < lax.broadcasted_iota(jnp.int32, (ch, ch), 0))
    rank = _dot(jnp.where(before, 1.0, 0.0).astype(BF16), sel.astype(BF16)) + carry_ref[...]
    rm = jnp.where(sel > 0.0, rank, -1.0)
    rm_ref[...] = rm
    rmt_ref[...] = rm.T[:N_EXPERTS]
    cum_ref[0] = carry_ref[...]
    carry_ref[...] += jnp.sum(sel, axis=0, keepdims=True)
    tot_ref[...] = carry_ref[...]


def moe_rank(sel):
    S = sel.shape[0]
    nch = S // MOE_CH
    return pl.pallas_call(
        _moe_rank_kernel,
        grid=(nch,),
        in_specs=[pl.BlockSpec((MOE_CH, LANE), lambda c: (c, 0))],
        out_specs=[pl.BlockSpec((MOE_CH, LANE), lambda c: (c, 0)),
                   pl.BlockSpec((N_EXPERTS, MOE_CH), lambda c: (0, c)),
                   pl.BlockSpec((1, 1, LANE), lambda c: (c, 0, 0)),
                   pl.BlockSpec((1, LANE), lambda c: (0, 0))],
        out_shape=[jax.ShapeDtypeStruct((S, LANE), F32),
                   jax.ShapeDtypeStruct((N_EXPERTS, S), F32),
                   jax.ShapeDtypeStruct((nch, 1, LANE), F32),
                   jax.ShapeDtypeStruct((1, LANE), F32)],
        scratch_shapes=[pltpu.VMEM((1, LANE), F32)],
        compiler_params=_cparams(("arbitrary",)),
        name="moe_rank",
    )(sel)


def _moe_expert_kernel(te_ref, tr0_ref, tclo_ref, tchi_ref, tval_ref,
                       rk_ref, h_hbm, wg_ref, wu_ref, wd_ref, y_ref,
                       hs_ref, acc_ref, hbuf_ref, sem_ref):
    j = pl.program_id(0)
    f = pl.program_id(1)
    nf = pl.num_programs(1)
    valid = tval_ref[j] > 0
    rows = hs_ref.shape[0]

    def chunk_copy(c, slot):
        return pltpu.make_async_copy(h_hbm.at[pl.ds(c * MOE_CH, MOE_CH)], hbuf_ref.at[slot], sem_ref.at[slot])

    @pl.when(valid & (f == 0))
    def _():
        clo, chi = tclo_ref[j], tchi_ref[j]
        want = (tr0_ref[j] + lax.broadcasted_iota(jnp.int32, (rows, MOE_CH), 0)).astype(F32)
        acc_ref[...] = jnp.zeros(acc_ref.shape, F32)

        @pl.when(chi > clo)
        def _():
            chunk_copy(clo, 0).start()

        def body(c, carry):
            slot = (c - clo) % 2

            @pl.when(c + 1 < chi)
            def _():
                chunk_copy(c + 1, 1 - slot).start()

            chunk_copy(c, slot).wait()
            rk = rk_ref[0, :, pl.ds(pl.multiple_of(c * MOE_CH, MOE_CH), MOE_CH)]
            onehot = jnp.where(rk == want, 1.0, 0.0).astype(BF16)
            acc_ref[...] += _dot(onehot, hbuf_ref[slot])
            return carry

        lax.fori_loop(clo, chi, body, 0)
        hs_ref[...] = acc_ref[...].astype(BF16)
        acc_ref[...] = jnp.zeros(acc_ref.shape, F32)

    @pl.when(valid)
    def _():
        hs = hs_ref[...]
        a = _silu(_dot(hs, wg_ref[0].astype(BF16))) * _dot(hs, wu_ref[0].astype(BF16))
        acc_ref[...] += _dot(a.astype(BF16), wd_ref[0].astype(BF16))

    @pl.when(f == nf - 1)
    def _():
        y_ref[...] = jnp.where(valid, acc_ref[...], 0.0).astype(y_ref.dtype)


def _moe_combine_kernel(ca_ref, cb_ref, roff_ref, x_ref, rm_ref, cw_ref, ya_ref, yb_ref, o_ref):
    c = pl.program_id(0)
    e = pl.program_id(1)

    @pl.when(e == 0)
    def _():
        o_ref[...] = x_ref[...]

    rm = rm_ref[...]
    lane = lax.broadcasted_iota(jnp.int32, rm.shape, 1)
    mine = lane == e
    pos = jnp.sum(jnp.where(mine, rm, 0.0), axis=1, keepdims=True)
    gate = jnp.sum(jnp.where(mine, cw_ref[...], 0.0), axis=1, keepdims=True)
    row = jnp.where(pos >= 0.0, pos + roff_ref[e].astype(F32), -1.0)
    k = c * pl.num_programs(1) + e
    ca, cb = ca_ref[k], cb_ref[k]
    cr = ya_ref.shape[0]
    col = lax.broadcasted_iota(jnp.int32, (rm.shape[0], cr), 1)
    for y_ref, base in ((ya_ref, ca * cr), (yb_ref, jnp.where(cb != ca, cb * cr, -2 * cr))):
        onehot = jnp.where(row == (base + col).astype(F32), 1.0, 0.0).astype(BF16)
        o_ref[...] += gate * _dot(onehot, y_ref[...])


def ffn_moe(x, h, combine, sel, wg, wu, wd, tf=512):
    S, D = x.shape
    E, _, F = wg.shape
    tf = min(tf, F)
    T = MOE_TILE
    nch = S // MOE_CH
    J = 2 * S // T + E
    rm, rmt, cum, tot = moe_rank(sel)

    cnt = tot[0, :E].astype(jnp.int32)
    ntile = (cnt + T - 1) // T
    tend = jnp.cumsum(ntile)
    tstart = tend - ntile
    jj = jnp.arange(J, dtype=jnp.int32)
    tval = (jj < tend[-1]).astype(jnp.int32)
    te = jnp.minimum(jnp.searchsorted(tend, jnp.minimum(jj, tend[-1] - 1), side="right"), E - 1).astype(jnp.int32)
    tr0 = (jnp.minimum(jj, tend[-1] - 1) - tstart[te]) * T
    cumx = cum[:, 0, :E].astype(jnp.int32)
    cumi = jnp.concatenate([cumx[1:], cnt[None]], axis=0)
    tclo = jnp.sum(cumi[:, te] <= tr0[None, :], axis=0).astype(jnp.int32)
    tchi = jnp.sum(cumx[:, te] < (tr0 + T)[None, :], axis=0).astype(jnp.int32)
    roff = (tstart * T).astype(jnp.int32)
    nrow_chunks = J * T // MOE_CH
    ca = ((roff[None, :] + cumx) // MOE_CH).astype(jnp.int32)
    cb = jnp.minimum(ca + 1, nrow_chunks - 1)
    ca, cb = ca.reshape(-1), cb.reshape(-1)

    nf = F // tf

    def wmap(j, f, te_r, tr0_r, clo_r, chi_r, val_r):
        return (te_r[j], 0, jnp.where(val_r[j] > 0, f, nf - 1))

    def wdmap(j, f, te_r, tr0_r, clo_r, chi_r, val_r):
        return (te_r[j], jnp.where(val_r[j] > 0, f, nf - 1), 0)

    y = pl.pallas_call(
        _moe_expert_kernel,
        grid_spec=pltpu.PrefetchScalarGridSpec(
            num_scalar_prefetch=5,
            grid=(J, nf),
            in_specs=[pl.BlockSpec((1, 1, S), lambda j, f, te_r, *_: (te_r[j], 0, 0)),
                      pl.BlockSpec(memory_space=pl.ANY),
                      pl.BlockSpec((1, D, tf), wmap),
                      pl.BlockSpec((1, D, tf), wmap),
                      pl.BlockSpec((1, tf, D), wdmap)],
            out_specs=pl.BlockSpec((T, D), lambda j, f, *_: (j, 0)),
            scratch_shapes=[pltpu.VMEM((T, D), BF16),
                            pltpu.VMEM((T, D), F32),
                            pltpu.VMEM((2, MOE_CH, D), BF16),
                            pltpu.SemaphoreType.DMA((2,))]),
        out_shape=jax.ShapeDtypeStruct((J * T, D), BF16),
        compiler_params=_cparams(("arbitrary", "arbitrary")),
        name="moe_experts",
    )(te, tr0, tclo, tchi, tval, rmt.reshape(E, 1, S), h, wg, wu, wd)

    return pl.pallas_call(
        _moe_combine_kernel,
        grid_spec=pltpu.PrefetchScalarGridSpec(
            num_scalar_prefetch=3,
            grid=(nch, E),
            in_specs=[pl.BlockSpec((MOE_CH, D), lambda c, e, *_: (c, 0)),
                      pl.BlockSpec((MOE_CH, LANE), lambda c, e, *_: (c, 0)),
                      pl.BlockSpec((MOE_CH, LANE), lambda c, e, *_: (c, 0)),
                      pl.BlockSpec((MOE_CH, D), lambda c, e, ca_r, cb_r, ro_r: (ca_r[c * E + e], 0)),
                      pl.BlockSpec((MOE_CH, D), lambda c, e, ca_r, cb_r, ro_r: (cb_r[c * E + e], 0))],
            out_specs=pl.BlockSpec((MOE_CH, D), lambda c, e, *_: (c, 0))),
        out_shape=jax.ShapeDtypeStruct((S, D), F32),
        compiler_params=_cparams(("parallel", "arbitrary")),
        name="moe_combine",
    )(ca, cb, roff, x, rm, combine, y, y)


def _tile_gain(g, reps, scale=1.0):
    return jnp.tile(g.astype(F32) * scale, reps)


def kernel(x, w_in, w_out, attn_norm_g, ffn_norm_g, q_norm_a, k_norm_a, q_norm_b, k_norm_b,
           q_norm_c, k_norm_c, lambda_q1, lambda_k1, lambda_q2, lambda_k2, diff_subln_g, rel_bias,
           w_dense_gate, w_dense_up, w_dense_down, w_router, w_moe_gate, w_moe_up, w_moe_down):
    B, S, D = x.shape
    depth = w_in.shape[0]
    assert B == 1 and S % (DIL_STEPS * DILATED_PAIRS[-1][1]) == 0
    aw, bw, cw = A_HEADS * HEAD_DIM, B_HEADS * HEAD_DIM, C_HEADS * HEAD_DIM
    iw = IDX_HEADS * IDX_DIM
    sizes = (aw, aw, aw, iw, IDX_DIM, IDX_HEADS, bw, bw, bw, cw, cw, cw)
    offs = np.concatenate([[0], np.cumsum(sizes)]).tolist()

    def cols(w, *segs):
        return jnp.concatenate([w[:, offs[s]:offs[s + 1]] for s in segs], axis=1)

    tq, tk = 256, 512
    band_a = bias_band(rel_bias, A_HEADS, 0, _band_c0(tk) + 2 * tk, tq, _band_c0(tk))
    band_c = bias_band(rel_bias, C_HEADS, A_HEADS + B_HEADS, _band_c0(tk) + 2 * tk, tq, _band_c0(tk))
    bands_b = [bias_band(rel_bias, B_HEADS, A_HEADS, 2 * DIL_STEPS * dil + tq, tq, DIL_STEPS * dil, dil)
               for _, dil in DILATED_PAIRS]

    xs = x.reshape(S, D)
    for layer in range(depth):
        wl = w_in[layer]
        w_pa = cols(wl, 0, 1).astype(BF16)
        w_pb = cols(wl, 6, 7).astype(BF16)
        w_pc = cols(wl, 9, 10).astype(BF16)
        w_pi = cols(wl, 3, 4, 4).astype(BF16)
        w_vt = cols(wl, 2, 8, 11).T.astype(BF16)
        w_wt = cols(wl, 5).T.astype(BF16)
        qs_ab, qs_c = HEAD_DIM ** -0.5 * LOG2E, C_HALF ** -0.5 * LOG2E
        g_pa = jnp.concatenate([_tile_gain(q_norm_a[layer], A_HEADS, qs_ab), _tile_gain(k_norm_a[layer], A_HEADS)])
        g_pb = jnp.concatenate([_tile_gain(q_norm_b[layer], B_HEADS, qs_ab), _tile_gain(k_norm_b[layer], B_HEADS)])
        g_pc = jnp.concatenate([_tile_gain(q_norm_c[layer], 2 * C_HEADS, qs_c),
                                _tile_gain(k_norm_c[layer], 2 * C_HEADS)])

        h = rmsnorm(xs, attn_norm_g[layer])
        p_a = project(h, w_pa, g_pa, HEAD_DIM, BF16, tn=512)
        p_b = project(h, w_pb, g_pb, HEAD_DIM, BF16, tn=512)
        p_c = project(h, w_pc, g_pc, C_HALF, BF16, tn=512)
        p_i = project(h, w_pi, jnp.ones((w_pi.shape[1],), F32), 0, BF16, tn=384)
        vt = project_t(h, w_vt, BF16)
        wt = project_t(h, w_wt, F32)

        oa = dsa_attention(p_a, vt, p_i, wt, band_a)

        ob = dilated_attention(p_b, vt, bands_b, aw, tq=tq)

        lam_init = 0.8 - 0.6 * math.exp(-0.3 * layer)
        lam_params = jnp.stack([lambda_q1[layer], lambda_k1[layer], lambda_q2[layer], lambda_k2[layer]]).astype(F32)
        oc = diff_attention(p_c, vt, band_c, lam_params, diff_subln_g[layer].astype(F32), lam_init, aw + bw,
                            tq=tq, tk=tk)

        wo = w_out[layer].astype(BF16)
        xs = out_projection(xs, oa, ob, oc, wo[:aw], wo[aw:aw + bw], wo[aw + bw:])

        f = layer // 2
        if layer % 2 == 0:
            h2 = rmsnorm(xs, ffn_norm_g[layer])
            xs = ffn_dense(xs, h2, w_dense_gate[f], w_dense_up[f], w_dense_down[f])
        else:
            h2 = rmsnorm(xs, ffn_norm_g[layer])
            combine, sel = moe_router(xs, ffn_norm_g[layer], w_router[f])
            xs = ffn_moe(xs, h2, combine, sel, w_moe_gate[f], w_moe_up[f], w_moe_down[f])
    return xs.reshape(B, S, D)
```

```python
import functools
import math

import jax
import jax.numpy as jnp
import numpy as np
from jax import lax
from jax.experimental import pallas as pl
from jax.experimental.pallas import tpu as pltpu

F32 = jnp.float32
BF16 = jnp.bfloat16

HEAD_DIM = 128
A_HEADS, B_HEADS, C_HEADS = 4, 6, 6
C_HALF = HEAD_DIM // 2
IDX_HEADS, IDX_DIM = 16, 64
TOPK_MAX = 256
DILATED_PAIRS = ((128, 1), (512, 4), (2048, 16))
DIL_STEPS = 128
N_BUCKETS = 32
MAX_DISTANCE = 2048
N_EXPERTS = 8
RMS_EPS = 1e-6
NEG = -1e30
LOG2E = math.log2(math.e)
INT_MIN = -(2 ** 31)
I16_BIAS = 2 ** 15
CNT_ROWS = 64
LANE = 128
VMEM_LIMIT = 56 * 1024 * 1024

BUCKET_EDGES = tuple(range(16)) + (16,) + tuple(
    int(math.ceil(16.0 * (MAX_DISTANCE / 16.0) ** (k / 16.0))) for k in range(1, 16))
FAR_DIST = BUCKET_EDGES[-1]


def _cparams(sem):
    return pltpu.CompilerParams(dimension_semantics=sem, vmem_limit_bytes=VMEM_LIMIT)


def _dot(a, b):
    return jnp.dot(a, b, preferred_element_type=F32)


def _dot_nt(a, b):
    return lax.dot_general(a, b, (((1,), (1,)), ((), ())), preferred_element_type=F32)


def _rmsnorm_kernel(x_ref, g_ref, o_ref):
    x = x_ref[...]
    ms = jnp.mean(x * x, axis=-1, keepdims=True)
    o_ref[...] = (x * lax.rsqrt(ms + RMS_EPS) * g_ref[...]).astype(o_ref.dtype)


def rmsnorm(x, g, tm=512):
    S, D = x.shape
    return pl.pallas_call(
        _rmsnorm_kernel,
        grid=(S // tm,),
        in_specs=[pl.BlockSpec((tm, D), lambda i: (i, 0)), pl.BlockSpec((1, D), lambda i: (0, 0))],
        out_specs=pl.BlockSpec((tm, D), lambda i: (i, 0)),
        out_shape=jax.ShapeDtypeStruct((S, D), BF16),
        compiler_params=_cparams(("parallel",)),
        name="rmsnorm",
    )(x, g.reshape(1, D))


def _proj_kernel(h_ref, w_ref, g_ref, o_ref, *, group):
    acc = _dot(h_ref[...], w_ref[...])
    tn = acc.shape[1]
    if group == 0:
        o_ref[...] = acc.astype(o_ref.dtype)
        return
    for c in range(tn // LANE):
        a = acc[:, c * LANE:(c + 1) * LANE]
        sq = a * a
        if group == LANE:
            ms = jnp.mean(sq, axis=-1, keepdims=True)
        else:
            lane = lax.broadcasted_iota(jnp.int32, sq.shape, 1)
            lo = jnp.sum(jnp.where(lane < group, sq, 0.0), axis=-1, keepdims=True)
            hi = jnp.sum(jnp.where(lane >= group, sq, 0.0), axis=-1, keepdims=True)
            ms = jnp.where(lane < group, lo, hi) * (1.0 / group)
        y = a * lax.rsqrt(ms + RMS_EPS) * g_ref[:, c * LANE:(c + 1) * LANE]
        o_ref[:, c * LANE:(c + 1) * LANE] = y.astype(o_ref.dtype)


def project(h, w, gains, group, out_dtype, tm=1024, tn=256):
    S, K = h.shape
    N = w.shape[1]
    return pl.pallas_call(
        functools.partial(_proj_kernel, group=group),
        grid=(S // tm, N // tn),
        in_specs=[pl.BlockSpec((tm, K), lambda i, j: (i, 0)),
                  pl.BlockSpec((K, tn), lambda i, j: (0, j)),
                  pl.BlockSpec((1, tn), lambda i, j: (0, j))],
        out_specs=pl.BlockSpec((tm, tn), lambda i, j: (i, j)),
        out_shape=jax.ShapeDtypeStruct((S, N), out_dtype),
        compiler_params=_cparams(("parallel", "arbitrary")),
        name=f"proj_g{group}",
    )(h, w, gains.reshape(1, N))


def _proj_t_kernel(wt_ref, h_ref, o_ref):
    o_ref[...] = _dot_nt(wt_ref[...], h_ref[...]).astype(o_ref.dtype)


def project_t(h, wt, out_dtype, tm=1024, tn=256):
    S, K = h.shape
    N = wt.shape[0]
    tn = min(tn, N)
    return pl.pallas_call(
        _proj_t_kernel,
        grid=(S // tm, N // tn),
        in_specs=[pl.BlockSpec((tn, K), lambda i, j: (j, 0)),
                  pl.BlockSpec((tm, K), lambda i, j: (i, 0))],
        out_specs=pl.BlockSpec((tn, tm), lambda i, j: (j, i)),
        out_shape=jax.ShapeDtypeStruct((N, S), out_dtype),
        compiler_params=_cparams(("parallel", "arbitrary")),
        name="proj_t",
    )(wt, h)


BAND_ROWS = 256


def _band_kernel(tab_ref, o_ref, *, c0, head_off, dil):
    hh = head_off + pl.program_id(0)
    _, rows, width = o_ref.shape
    dist = (lax.broadcasted_iota(jnp.int32, (rows, width), 1)
            - lax.broadcasted_iota(jnp.int32, (rows, width), 0) + (c0 - pl.program_id(1) * rows))
    val = jnp.full((rows, width), tab_ref[0, hh] * LOG2E, F32)
    for b in range(1, N_BUCKETS):
        val = jnp.where(dist >= BUCKET_EDGES[b], tab_ref[b, hh] * LOG2E, val)
    ok = dist >= 0
    if dil:
        ok = ok & (dist <= DIL_STEPS * dil) & ((dist & (dil - 1)) == 0)
    o_ref[0] = jnp.where(ok, val, NEG)


def bias_band(rel_bias, n_heads, head_off, rows, width, c0, dil=0):
    assert rows % BAND_ROWS == 0 and dil & (dil - 1) == 0
    return pl.pallas_call(
        functools.partial(_band_kernel, c0=c0, head_off=head_off, dil=dil),
        grid=(n_heads, rows // BAND_ROWS),
        in_specs=[pl.BlockSpec(memory_space=pltpu.SMEM)],
        out_specs=pl.BlockSpec((1, BAND_ROWS, width), lambda h, r: (h, r, 0)),
        out_shape=jax.ShapeDtypeStruct((n_heads, rows, width), F32),
        compiler_params=_cparams(("parallel", "parallel")),
        name=f"bias_band_d{dil}",
    )(rel_bias)


def _band_c0(tk):
    return -(-(FAR_DIST + tk - 1) // LANE) * LANE


def _dsa_kernel(q_ref, qi_ref, wt_ref, k_ref, vt_ref, kidx_ref, band_ref, o_ref,
                hi_ref, lo_ref, cut_ref, qih_ref, s_ref, acc_ref, m_ref, l_ref, *, tq, kc, topk, c0):
    i = pl.program_id(0)
    t0 = i * tq
    nchunk = (t0 + tq + kc - 1) // kc
    half = lax.broadcasted_iota(jnp.int32, (tq, LANE), 1) < IDX_DIM

    for h in range(IDX_HEADS):
        slab = qi_ref[:, (h // 2) * LANE:(h // 2 + 1) * LANE]
        keep = half if h % 2 == 0 else jnp.logical_not(half)
        qih_ref[h] = jnp.where(keep, slab, jnp.zeros_like(slab))
    w = wt_ref[...] * (IDX_HEADS ** -0.5 * IDX_DIM ** -0.5)

    pos_q = t0 + lax.broadcasted_iota(jnp.int32, (kc, tq), 1)
    row = lax.broadcasted_iota(jnp.int32, (kc, tq), 0)

    def score_chunk(c, carry):
        s0 = pl.multiple_of(c * kc, kc)
        kx = kidx_ref[pl.ds(s0, kc), :]
        acc = jnp.zeros((kc, tq), F32)
        for h in range(IDX_HEADS):
            acc = acc + w[h:h + 1, :] * jnp.maximum(_dot_nt(kx, qih_ref[h]), 0.0)
        bits = lax.bitcast_convert_type(acc, jnp.int32)
        key = bits ^ ((bits >> 31) & 0x7FFFFFFF)
        key = jnp.where(s0 + row <= pos_q, key, INT_MIN)
        hi_ref[pl.ds(s0, kc), :] = (key >> 16).astype(jnp.int16)
        lo_ref[pl.ds(s0, kc), :] = ((key & 0xFFFF) - I16_BIAS).astype(jnp.int16)
        return carry

    lax.fori_loop(0, nchunk, score_chunk, 0)

    one16, zero16 = jnp.ones((), jnp.int16), jnp.zeros((), jnp.int16)
    row16 = lax.broadcasted_iota(jnp.int32, (kc, tq), 0).astype(jnp.int16)
    rowc16 = lax.broadcasted_iota(jnp.int32, (CNT_ROWS, tq), 0).astype(jnp.int16)
    n_keys = k_ref.shape[0]

    def to16(v):
        return jnp.clip(v, -I16_BIAS, I16_BIAS - 1).astype(jnp.int16)

    def count(hit):
        def body(c, cnt):
            base = pl.multiple_of(c * kc, kc)
            for g in range(kc // CNT_ROWS):
                cnt = cnt + jnp.where(hit(base + g * CNT_ROWS), one16, zero16)
            return cnt

        cnt = lax.fori_loop(0, nchunk, body, jnp.zeros((CNT_ROWS, tq), jnp.int16))
        return jnp.sum(cnt.astype(jnp.int32).astype(F32), axis=0, keepdims=True)

    def count_ge(ref, cand):
        c16 = cand.astype(jnp.int16)
        return count(lambda r: ref[pl.ds(r, CNT_ROWS), :] >= c16)

    def count_gt(ref, cand):
        c16 = cand.astype(jnp.int16)
        return count(lambda r: ref[pl.ds(r, CNT_ROWS), :] > c16)

    def kth_largest(ref, need):
        t = jnp.where(count_ge(ref, jnp.zeros((1, tq), jnp.int32)) >= need, 0, -I16_BIAS).astype(jnp.int32)

        def bit_body(b, t):
            cand = t + jnp.left_shift(jnp.int32(1), 14 - b)
            return jnp.where(count_ge(ref, cand) >= need, cand, t)

        return lax.fori_loop(0, 15, bit_body, t)

    kf = jnp.full((1, tq), float(topk), F32)
    thi = kth_largest(hi_ref, kf)
    need_lo = kf - count_gt(hi_ref, thi)
    thi16 = thi.astype(jnp.int16)

    def mask_low(c, carry):
        rows = pl.ds(pl.multiple_of(c * kc, kc), kc)
        lo_ref[rows, :] = jnp.where(hi_ref[rows, :] == thi16, lo_ref[rows, :], jnp.int16(-I16_BIAS))
        return carry

    lax.fori_loop(0, nchunk, mask_low, 0)
    tlo = kth_largest(lo_ref, need_lo)
    tlo16 = tlo.astype(jnp.int16)

    def is_eq(r):
        return (hi_ref[pl.ds(r, CNT_ROWS), :] == thi16) & (lo_ref[pl.ds(r, CNT_ROWS), :] == tlo16)

    keep_eq = need_lo - count_gt(lo_ref, tlo)
    excess = (count(is_eq) > keep_eq) & (thi > -I16_BIAS)
    cut_ref[...] = jnp.full(cut_ref.shape, n_keys, jnp.int32)

    @pl.when(jnp.max(jnp.broadcast_to(jnp.where(excess, 1.0, 0.0), (8, tq))) > 0.0)
    def _():
        nbits = n_keys.bit_length()

        def bit_body(b, j):
            cand = j + jnp.left_shift(jnp.int32(1), nbits - 1 - b)
            before = count(lambda r: is_eq(r) & (rowc16 < to16(cand - r)))
            return jnp.where(before < keep_eq, cand, j)

        j = lax.fori_loop(0, nbits, bit_body, jnp.zeros((1, tq), jnp.int32))
        cut_ref[...] = jnp.where(excess, j, n_keys)

    cut = cut_ref[...]
    thi_sel16 = jnp.maximum(thi, 1 - I16_BIAS).astype(jnp.int16)

    m_ref[...] = jnp.full(m_ref.shape, NEG, F32)
    l_ref[...] = jnp.zeros(l_ref.shape, F32)
    acc_ref[...] = jnp.zeros(acc_ref.shape, F32)

    def logits(c):
        s0 = pl.multiple_of(jnp.minimum(c, nchunk - 1) * kc, kc)
        off = pl.multiple_of(c0 - jnp.clip(t0 - c * kc, -kc, c0), LANE)
        hi = hi_ref[pl.ds(s0, kc), :]
        lo = lo_ref[pl.ds(s0, kc), :]
        sel = (hi > thi_sel16) | ((hi == thi_sel16)
                                  & ((lo > tlo16) | ((lo == tlo16) & (row16 <= to16(cut - s0)))))
        drop = jnp.where(sel, jnp.zeros((), jnp.int16), jnp.ones((), jnp.int16))
        mask_bias = drop.astype(jnp.int32).astype(F32) * NEG
        parts = []
        for h in range(A_HEADS):
            hs = slice(h * HEAD_DIM, (h + 1) * HEAD_DIM)
            parts.append(_dot_nt(k_ref[pl.ds(s0, kc), hs], q_ref[:, hs]) + band_ref[h, pl.ds(off, kc), :]
                         + mask_bias)
        return jnp.concatenate(parts, axis=1)

    def consume(c, s):
        s0 = pl.multiple_of(jnp.minimum(c, nchunk - 1) * kc, kc)
        m_prev = m_ref[...]
        m_new = jnp.maximum(m_prev, jnp.max(s, axis=0, keepdims=True))
        alpha = jnp.exp2(m_prev - m_new)
        p = jnp.exp2(s - m_new)
        l_ref[...] = alpha * l_ref[...] + jnp.sum(p, axis=0, keepdims=True)
        p = p.astype(BF16)
        pv = [_dot(vt_ref[h * HEAD_DIM:(h + 1) * HEAD_DIM, pl.ds(s0, kc)], p[:, h * tq:(h + 1) * tq])
              for h in range(A_HEADS)]
        acc_ref[...] = alpha * acc_ref[...] + jnp.concatenate(pv, axis=1)
        m_ref[...] = m_new

    s_ref[0] = logits(0)

    def chunk_pair(cc, carry):
        c = 2 * cc
        s_ref[1] = logits(c + 1)
        consume(c, s_ref[0])
        s_ref[0] = logits(c + 2)
        consume(c + 1, s_ref[1])
        return carry

    lax.fori_loop(0, (nchunk + 1) // 2, chunk_pair, 0)
    o = acc_ref[...] / l_ref[...]
    for h in range(A_HEADS):
        o_ref[:, h * HEAD_DIM:(h + 1) * HEAD_DIM] = o[:, h * tq:(h + 1) * tq].T.astype(o_ref.dtype)


def dsa_attention(qk, vt, pidx, wt, band, tq=256, kc=512):
    S = qk.shape[0]
    topk = min(TOPK_MAX, S // 4)
    aw = A_HEADS * HEAD_DIM
    iw = IDX_HEADS * IDX_DIM
    once = pl.Buffered(1)
    return pl.pallas_call(
        functools.partial(_dsa_kernel, tq=tq, kc=kc, topk=topk, c0=_band_c0(kc)),
        grid=(S // tq,),
        in_specs=[pl.BlockSpec((tq, aw), lambda i: (i, 0)),
                  pl.BlockSpec((tq, iw), lambda i: (i, 0)),
                  pl.BlockSpec((IDX_HEADS, tq), lambda i: (0, i)),
                  pl.BlockSpec((S, aw), lambda i: (0, 1), pipeline_mode=once),
                  pl.BlockSpec((aw, S), lambda i: (0, 0), pipeline_mode=once),
                  pl.BlockSpec((S, LANE), lambda i: (0, iw // LANE), pipeline_mode=once),
                  pl.BlockSpec(band.shape, lambda i: (0, 0, 0), pipeline_mode=once)],
        out_specs=pl.BlockSpec((tq, aw), lambda i: (i, 0)),
        out_shape=jax.ShapeDtypeStruct((S, aw), BF16),
        scratch_shapes=[pltpu.VMEM((S, tq), jnp.int16),
                        pltpu.VMEM((S, tq), jnp.int16),
                        pltpu.VMEM((1, tq), jnp.int32),
                        pltpu.VMEM((IDX_HEADS, tq, LANE), BF16),
                        pltpu.VMEM((2, kc, A_HEADS * tq), F32),
                        pltpu.VMEM((HEAD_DIM, A_HEADS * tq), F32),
                        pltpu.VMEM((1, A_HEADS * tq), F32),
                        pltpu.VMEM((1, A_HEADS * tq), F32)],
        compiler_params=_cparams(("arbitrary",)),
        name="dsa_attention",
    )(qk, pidx, wt, qk, vt, pidx, band)


DIL_ROWS = 384


def _dilated_kernel(q_ref, k_ref, vt_ref, b0_ref, b1_ref, b2_ref, o_ref, s_ref, *, tq):
    t0 = pl.program_id(1) * tq
    q = q_ref[...]
    tiles = []
    row0 = 0
    for band_ref, (_, dil) in zip((b0_ref, b1_ref, b2_ref), DILATED_PAIRS):
        span = DIL_STEPS * dil
        start = jnp.maximum(t0 - span, 0)
        off = span - (t0 - start)
        for j in range((span + tq) // DIL_ROWS):
            ks = pl.multiple_of(start + j * DIL_ROWS, LANE)
            bs = pl.multiple_of(off + j * DIL_ROWS, LANE)
            s_ref[row0:row0 + DIL_ROWS, :] = (_dot_nt(k_ref[pl.ds(ks, DIL_ROWS), :], q)
                                              + band_ref[0, pl.ds(bs, DIL_ROWS), :])
            tiles.append((ks, row0))
            row0 += DIL_ROWS
    m = jnp.full((1, tq), NEG, F32)
    l = jnp.zeros((1, tq), F32)
    acc = jnp.zeros((HEAD_DIM, tq), F32)
    for ks, r0 in tiles:
        s = s_ref[r0:r0 + DIL_ROWS, :]
        m_new = jnp.maximum(m, jnp.max(s, axis=0, keepdims=True))
        alpha = jnp.exp2(m - m_new)
        p = jnp.exp2(s - m_new)
        l = alpha * l + jnp.sum(p, axis=0, keepdims=True)
        acc = alpha * acc + _dot(vt_ref[:, pl.ds(ks, DIL_ROWS)], p.astype(BF16))
        m = m_new
    o_ref[...] = (acc / l).T.astype(o_ref.dtype)


def dilated_attention(qk, vt, bands, vt_row0, tq=256):
    S = qk.shape[0]
    vh0 = vt_row0 // HEAD_DIM
    lengths = [DIL_STEPS * dil + tq for _, dil in DILATED_PAIRS]
    assert all(n % DIL_ROWS == 0 and n <= S for n in lengths)
    return pl.pallas_call(
        functools.partial(_dilated_kernel, tq=tq),
        grid=(B_HEADS, S // tq),
        in_specs=[pl.BlockSpec((tq, LANE), lambda h, i: (i, h)),
                  pl.BlockSpec((S, LANE), lambda h, i: (0, B_HEADS + h)),
                  pl.BlockSpec((HEAD_DIM, S), lambda h, i: (vh0 + h, 0))]
                 + [pl.BlockSpec((1,) + b.shape[1:], lambda h, i: (h, 0, 0)) for b in bands],
        out_specs=pl.BlockSpec((tq, LANE), lambda h, i: (i, h)),
        out_shape=jax.ShapeDtypeStruct((S, B_HEADS * HEAD_DIM), BF16),
        scratch_shapes=[pltpu.VMEM((sum(lengths), tq), F32)],
        compiler_params=_cparams(("parallel", "arbitrary")),
        name="dilated_attention",
    )(qk, qk, vt, *bands)


def _diff_kernel(lam_ref, g_ref, q_ref, k_ref, vt_ref, band_ref, o_ref, acc_ref, m_ref, l_ref, s_ref,
                 *, tq, tk, c0, lam_init):
    i = pl.program_id(1)
    t0 = i * tq
    nchunk = (t0 + tq + tk - 1) // tk
    last_chunk = k_ref.shape[0] // tk - 1
    q = q_ref[...]
    lane = lax.broadcasted_iota(jnp.int32, q.shape, 1)
    zero = jnp.zeros_like(q)
    q2 = jnp.concatenate([jnp.where(lane < C_HALF, q, zero), jnp.where(lane >= C_HALF, q, zero)], axis=0)

    m_ref[...] = jnp.full(m_ref.shape, NEG, F32)
    l_ref[...] = jnp.zeros(l_ref.shape, F32)
    acc_ref[...] = jnp.zeros(acc_ref.shape, F32)

    def logits(c):
        s0 = pl.multiple_of(jnp.minimum(c, last_chunk) * tk, tk)
        off = pl.multiple_of(c0 - jnp.clip(t0 - c * tk, -tk, c0), LANE)
        bias = band_ref[0, pl.ds(off, tk), :]
        return _dot_nt(k_ref[pl.ds(s0, tk), :], q2) + jnp.concatenate([bias, bias], axis=1)

    def consume(c, s):
        s0 = pl.multiple_of(jnp.minimum(c, last_chunk) * tk, tk)
        m_prev = m_ref[...]
        m_new = jnp.maximum(m_prev, jnp.max(s, axis=0, keepdims=True))
        alpha = jnp.exp2(m_prev - m_new)
        p = jnp.exp2(s - m_new)
        l_ref[...] = alpha * l_ref[...] + jnp.sum(p, axis=0, keepdims=True)
        acc_ref[...] = alpha * acc_ref[...] + _dot(vt_ref[:, pl.ds(s0, tk)], p.astype(BF16))
        m_ref[...] = m_new

    s_ref[0] = logits(0)

    def chunk_pair(cc, carry):
        c = 2 * cc
        s_ref[1] = logits(c + 1)
        consume(c, s_ref[0])
        s_ref[0] = logits(c + 2)
        consume(c + 1, s_ref[1])
        return carry

    lax.fori_loop(0, (nchunk + 1) // 2, chunk_pair, 0)

    lp = lam_ref[...]
    lam = (jnp.exp(jnp.sum(lp[0:1] * lp[1:2], axis=1, keepdims=True))
           - jnp.exp(jnp.sum(lp[2:3] * lp[3:4], axis=1, keepdims=True)) + lam_init)
    o = acc_ref[...] / l_ref[...]
    o = o[:, :tq] - lam * o[:, tq:]
    ms = jnp.mean(o * o, axis=0, keepdims=True)
    o = o * lax.rsqrt(ms + RMS_EPS) * (g_ref[...] * (1.0 - lam_init))
    o_ref[...] = o.T.astype(o_ref.dtype)


def diff_attention(qk, vt, band, lam_params, subln_g, lam_init, vt_row0, tq=256, tk=512):
    S = qk.shape[0]
    vh0 = vt_row0 // HEAD_DIM
    return pl.pallas_call(
        functools.partial(_diff_kernel, tq=tq, tk=tk, c0=_band_c0(tk), lam_init=lam_init),
        grid=(C_HEADS, S // tq),
        in_specs=[pl.BlockSpec((4, C_HALF), lambda h, i: (0, 0)),
                  pl.BlockSpec((HEAD_DIM, 1), lambda h, i: (0, 0)),
                  pl.BlockSpec((tq, LANE), lambda h, i: (i, h)),
                  pl.BlockSpec((S, LANE), lambda h, i: (0, C_HEADS + h)),
                  pl.BlockSpec((HEAD_DIM, S), lambda h, i: (vh0 + h, 0)),
                  pl.BlockSpec((1,) + band.shape[1:], lambda h, i: (h, 0, 0))],
        out_specs=pl.BlockSpec((tq, LANE), lambda h, i: (i, h)),
        out_shape=jax.ShapeDtypeStruct((S, C_HEADS * HEAD_DIM), BF16),
        scratch_shapes=[pltpu.VMEM((HEAD_DIM, 2 * tq), F32),
                        pltpu.VMEM((1, 2 * tq), F32),
                        pltpu.VMEM((1, 2 * tq), F32),
                        pltpu.VMEM((2, tk, 2 * tq), F32)],
        compiler_params=_cparams(("parallel", "arbitrary")),
        name="diff_attention",
    )(lam_params, subln_g.reshape(HEAD_DIM, 1), qk, qk, vt, band)


def _outproj_kernel(x_ref, a_ref, b_ref, c_ref, wa_ref, wb_ref, wc_ref, o_ref):
    o_ref[...] = (x_ref[...] + _dot(a_ref[...], wa_ref[...]) + _dot(b_ref[...], wb_ref[...])
                  + _dot(c_ref[...], wc_ref[...]))


def out_projection(x, oa, ob, oc, wa, wb, wc, tm=512, tn=1024):
    S, D = x.shape
    tn = min(tn, D)
    return pl.pallas_call(
        _outproj_kernel,
        grid=(S // tm, D // tn),
        in_specs=[pl.BlockSpec((tm, tn), lambda i, j: (i, j)),
                  pl.BlockSpec((tm, oa.shape[1]), lambda i, j: (i, 0)),
                  pl.BlockSpec((tm, ob.shape[1]), lambda i, j: (i, 0)),
                  pl.BlockSpec((tm, oc.shape[1]), lambda i, j: (i, 0)),
                  pl.BlockSpec((wa.shape[0], tn), lambda i, j: (0, j)),
                  pl.BlockSpec((wb.shape[0], tn), lambda i, j: (0, j)),
                  pl.BlockSpec((wc.shape[0], tn), lambda i, j: (0, j))],
        out_specs=pl.BlockSpec((tm, tn), lambda i, j: (i, j)),
        out_shape=jax.ShapeDtypeStruct((S, D), F32),
        compiler_params=_cparams(("parallel", "arbitrary")),
        name="out_projection",
    )(x, oa, ob, oc, wa, wb, wc)


def _silu(x):
    return x / (1.0 + jnp.exp(-x))


def _ffn_kernel(x_ref, h_ref, wg_ref, wu_ref, wd_ref, o_ref):
    f = pl.program_id(1)

    @pl.when(f == 0)
    def _():
        o_ref[...] = x_ref[...]

    h = h_ref[...]
    a = _silu(_dot(h, wg_ref[...].astype(BF16))) * _dot(h, wu_ref[...].astype(BF16))
    o_ref[...] += _dot(a.astype(BF16), wd_ref[...].astype(BF16))


def ffn_dense(x, h, wg, wu, wd, tm=512, tf=512):
    S, D = x.shape
    F = wg.shape[1]
    return pl.pallas_call(
        _ffn_kernel,
        grid=(S // tm, F // tf),
        in_specs=[pl.BlockSpec((tm, D), lambda i, f: (i, 0)),
                  pl.BlockSpec((tm, D), lambda i, f: (i, 0)),
                  pl.BlockSpec((D, tf), lambda i, f: (0, f)),
                  pl.BlockSpec((D, tf), lambda i, f: (0, f)),
                  pl.BlockSpec((tf, D), lambda i, f: (f, 0))],
        out_specs=pl.BlockSpec((tm, D), lambda i, f: (i, 0)),
        out_shape=jax.ShapeDtypeStruct((S, D), F32),
        compiler_params=_cparams(("parallel", "arbitrary")),
        name="ffn_dense",
    )(x, h, wg, wu, wd)


def _router_kernel(x_ref, g_ref, w_ref, c_ref, sel_ref):
    x = x_ref[...]
    ms = jnp.mean(x * x, axis=-1, keepdims=True)
    h = x * lax.rsqrt(ms + RMS_EPS) * g_ref[...]
    logits = jnp.dot(h, w_ref[...], preferred_element_type=F32, precision=lax.Precision.HIGHEST)
    lane = lax.broadcasted_iota(jnp.int32, logits.shape, 1).astype(F32)
    logits = jnp.where(lane < N_EXPERTS, logits, NEG)
    m1 = jnp.max(logits, axis=1, keepdims=True)
    i1 = jnp.min(jnp.where(logits == m1, lane, float(LANE)), axis=1, keepdims=True)
    rest = jnp.where(lane == i1, NEG, logits)
    m2 = jnp.max(rest, axis=1, keepdims=True)
    i2 = jnp.min(jnp.where(rest == m2, lane, float(LANE)), axis=1, keepdims=True)
    e2 = jnp.exp(m2 - m1)
    g1 = 1.0 / (1.0 + e2)
    g2 = e2 / (1.0 + e2)
    pick1, pick2 = lane == i1, lane == i2
    c_ref[...] = jnp.where(pick1, g1, 0.0) + jnp.where(pick2, g2, 0.0)
    sel_ref[...] = jnp.where(pick1 | pick2, 1.0, 0.0)


def moe_router(x, g, w_router, tm=512):
    S, D = x.shape
    wpad = jnp.zeros((D, LANE), F32).at[:, :N_EXPERTS].set(w_router)
    spec = pl.BlockSpec((tm, LANE), lambda i: (i, 0))
    sds = jax.ShapeDtypeStruct((S, LANE), F32)
    return pl.pallas_call(
        _router_kernel,
        grid=(S // tm,),
        in_specs=[pl.BlockSpec((tm, D), lambda i: (i, 0)),
                  pl.BlockSpec((1, D), lambda i: (0, 0)),
                  pl.BlockSpec((D, LANE), lambda i: (0, 0))],
        out_specs=[spec, spec],
        out_shape=[sds, sds],
        compiler_params=_cparams(("parallel",)),
        name="moe_router",
    )(x, g.reshape(1, D), wpad)


MOE_CH = 256
MOE_TILE = 1024


def _moe_rank_kernel(sel_ref, rm_ref, rmt_ref, cum_ref, tot_ref, carry_ref):
    c = pl.program_id(0)

    @pl.when(c == 0)
    def _():
        carry_ref[...] = jnp.zeros(carry_ref.shape, F32)

    sel = sel_ref[...]
    ch = sel.shape[0]
    before = (lax.broadcasted_iota(jnp.int32, (ch, ch), 1) < lax.broadcasted_iota(jnp.int32, (ch, ch), 0))
    rank = _dot(jnp.where(before, 1.0, 0.0).astype(BF16), sel.astype(BF16)) + carry_ref[...]
    rm = jnp.where(sel > 0.0, rank, -1.0)
    rm_ref[...] = rm
    rmt_ref[...] = rm.T[:N_EXPERTS]
    cum_ref[0] = carry_ref[...]
    carry_ref[...] += jnp.sum(sel, axis=0, keepdims=True)
    tot_ref[...] = carry_ref[...]


def moe_rank(sel):
    S = sel.shape[0]
    nch = S // MOE_CH
    return pl.pallas_call(
        _moe_rank_kernel,
        grid=(nch,),
        in_specs=[pl.BlockSpec((MOE_CH, LANE), lambda c: (c, 0))],
        out_specs=[pl.BlockSpec((MOE_CH, LANE), lambda c: (c, 0)),
                   pl.BlockSpec((N_EXPERTS, MOE_CH), lambda c: (0, c)),
                   pl.BlockSpec((1, 1, LANE), lambda c: (c, 0, 0)),
                   pl.BlockSpec((1, LANE), lambda c: (0, 0))],
        out_shape=[jax.ShapeDtypeStruct((S, LANE), F32),
                   jax.ShapeDtypeStruct((N_EXPERTS, S), F32),
                   jax.ShapeDtypeStruct((nch, 1, LANE), F32),
                   jax.ShapeDtypeStruct((1, LANE), F32)],
        scratch_shapes=[pltpu.VMEM((1, LANE), F32)],
        compiler_params=_cparams(("arbitrary",)),
        name="moe_rank",
    )(sel)


def _moe_expert_kernel(te_ref, tr0_ref, tclo_ref, tchi_ref, tval_ref,
                       rk_ref, h_hbm, wg_ref, wu_ref, wd_ref, y_ref,
                       hs_ref, acc_ref, hbuf_ref, sem_ref):
    j = pl.program_id(0)
    f = pl.program_id(1)
    nf = pl.num_programs(1)
    nrows = tval_ref[j]
    tile = hs_ref.shape[0]

    def chunk_copy(c, slot):
        return pltpu.make_async_copy(h_hbm.at[pl.ds(c * MOE_CH, MOE_CH)], hbuf_ref.at[slot], sem_ref.at[slot])

    def gather(rows):
        clo, chi = tclo_ref[j], tchi_ref[j]
        want = (tr0_ref[j] + lax.broadcasted_iota(jnp.int32, (rows, MOE_CH), 0)).astype(F32)
        acc_ref[...] = jnp.zeros(acc_ref.shape, F32)

        @pl.when(chi > clo)
        def _():
            chunk_copy(clo, 0).start()

        def body(c, carry):
            slot = (c - clo) % 2

            @pl.when(c + 1 < chi)
            def _():
                chunk_copy(c + 1, 1 - slot).start()

            chunk_copy(c, slot).wait()
            rk = rk_ref[0, :, pl.ds(pl.multiple_of(c * MOE_CH, MOE_CH), MOE_CH)]
            onehot = jnp.where(rk == want, 1.0, 0.0).astype(BF16)
            acc_ref[:rows] += _dot(onehot, hbuf_ref[slot])
            return carry

        lax.fori_loop(clo, chi, body, 0)
        hs_ref[:rows] = acc_ref[:rows].astype(BF16)
        acc_ref[:rows] = jnp.zeros((rows, acc_ref.shape[1]), F32)

    def swiglu(rows):
        hs = hs_ref[:rows]
        a = _silu(_dot(hs, wg_ref[0].astype(BF16))) * _dot(hs, wu_ref[0].astype(BF16))
        acc_ref[:rows] += _dot(a.astype(BF16), wd_ref[0].astype(BF16))

    for rows, cond in ((tile, nrows > tile // 2), (tile // 2, (nrows > 0) & (nrows <= tile // 2))):
        @pl.when(cond & (f == 0))
        def _():
            gather(rows)

        @pl.when(cond)
        def _():
            swiglu(rows)

    @pl.when(f == nf - 1)
    def _():
        y_ref[...] = jnp.where(nrows > 0, acc_ref[...], 0.0).astype(y_ref.dtype)


def _moe_combine_kernel(ca_ref, cb_ref, roff_ref, x_ref, rm_ref, cw_ref, ya_ref, yb_ref, o_ref):
    c = pl.program_id(0)
    e = pl.program_id(1)

    @pl.when(e == 0)
    def _():
        o_ref[...] = x_ref[...]

    rm = rm_ref[...]
    lane = lax.broadcasted_iota(jnp.int32, rm.shape, 1)
    mine = lane == e
    pos = jnp.sum(jnp.where(mine, rm, 0.0), axis=1, keepdims=True)
    gate = jnp.sum(jnp.where(mine, cw_ref[...], 0.0), axis=1, keepdims=True)
    row = jnp.where(pos >= 0.0, pos + roff_ref[e].astype(F32), -1.0)
    k = c * pl.num_programs(1) + e
    ca, cb = ca_ref[k], cb_ref[k]
    cr = ya_ref.shape[0]
    col = lax.broadcasted_iota(jnp.int32, (rm.shape[0], cr), 1)
    for y_ref, base in ((ya_ref, ca * cr), (yb_ref, jnp.where(cb != ca, cb * cr, -2 * cr))):
        onehot = jnp.where(row == (base + col).astype(F32), 1.0, 0.0).astype(BF16)
        o_ref[...] += gate * _dot(onehot, y_ref[...])


def ffn_moe(x, h, combine, sel, wg, wu, wd, tf=512):
    S, D = x.shape
    E, _, F = wg.shape
    tf = min(tf, F)
    T = MOE_TILE
    nch = S // MOE_CH
    J = 2 * S // T + E
    rm, rmt, cum, tot = moe_rank(sel)

    cnt = tot[0, :E].astype(jnp.int32)
    ntile = (cnt + T - 1) // T
    tend = jnp.cumsum(ntile)
    tstart = tend - ntile
    jj = jnp.arange(J, dtype=jnp.int32)
    te = jnp.minimum(jnp.searchsorted(tend, jnp.minimum(jj, tend[-1] - 1), side="right"), E - 1).astype(jnp.int32)
    tr0 = (jnp.minimum(jj, tend[-1] - 1) - tstart[te]) * T
    tval = jnp.where(jj < tend[-1], jnp.clip(cnt[te] - tr0, 0, T), 0).astype(jnp.int32)
    cumx = cum[:, 0, :E].astype(jnp.int32)
    cumi = jnp.concatenate([cumx[1:], cnt[None]], axis=0)
    tclo = jnp.sum(cumi[:, te] <= tr0[None, :], axis=0).astype(jnp.int32)
    tchi = jnp.sum(cumx[:, te] < (tr0 + tval)[None, :], axis=0).astype(jnp.int32)
    roff = (tstart * T).astype(jnp.int32)
    nrow_chunks = J * T // MOE_CH
    ca = ((roff[None, :] + cumx) // MOE_CH).astype(jnp.int32)
    cb = jnp.minimum(ca + 1, nrow_chunks - 1)
    ca, cb = ca.reshape(-1), cb.reshape(-1)

    nf = F // tf

    def wmap(j, f, te_r, tr0_r, clo_r, chi_r, val_r):
        return (te_r[j], 0, jnp.where(val_r[j] > 0, f, nf - 1))

    def wdmap(j, f, te_r, tr0_r, clo_r, chi_r, val_r):
        return (te_r[j], jnp.where(val_r[j] > 0, f, nf - 1), 0)

    y = pl.pallas_call(
        _moe_expert_kernel,
        grid_spec=pltpu.PrefetchScalarGridSpec(
            num_scalar_prefetch=5,
            grid=(J, nf),
            in_specs=[pl.BlockSpec((1, 1, S), lambda j, f, te_r, *_: (te_r[j], 0, 0)),
                      pl.BlockSpec(memory_space=pl.ANY),
                      pl.BlockSpec((1, D, tf), wmap),
                      pl.BlockSpec((1, D, tf), wmap),
                      pl.BlockSpec((1, tf, D), wdmap)],
            out_specs=pl.BlockSpec((T, D), lambda j, f, *_: (j, 0)),
            scratch_shapes=[pltpu.VMEM((T, D), BF16),
                            pltpu.VMEM((T, D), F32),
                            pltpu.VMEM((2, MOE_CH, D), BF16),
                            pltpu.SemaphoreType.DMA((2,))]),
        out_shape=jax.ShapeDtypeStruct((J * T, D), BF16),
        compiler_params=_cparams(("arbitrary", "arbitrary")),
        name="moe_experts",
    )(te, tr0, tclo, tchi, tval, rmt.reshape(E, 1, S), h, wg, wu, wd)

    return pl.pallas_call(
        _moe_combine_kernel,
        grid_spec=pltpu.PrefetchScalarGridSpec(
            num_scalar_prefetch=3,
            grid=(nch, E),
            in_specs=[pl.BlockSpec((MOE_CH, D), lambda c, e, *_: (c, 0)),
                      pl.BlockSpec((MOE_CH, LANE), lambda c, e, *_: (c, 0)),
                      pl.BlockSpec((MOE_CH, LANE), lambda c, e, *_: (c, 0)),
                      pl.BlockSpec((MOE_CH, D), lambda c, e, ca_r, cb_r, ro_r: (ca_r[c * E + e], 0)),
                      pl.BlockSpec((MOE_CH, D), lambda c, e, ca_r, cb_r, ro_r: (cb_r[c * E + e], 0))],
            out_specs=pl.BlockSpec((MOE_CH, D), lambda c, e, *_: (c, 0))),
        out_shape=jax.ShapeDtypeStruct((S, D), F32),
        compiler_params=_cparams(("parallel", "arbitrary")),
        name="moe_combine",
    )(ca, cb, roff, x, rm, combine, y, y)


def _tile_gain(g, reps, scale=1.0):
    return jnp.tile(g.astype(F32) * scale, reps)


def kernel(x, w_in, w_out, attn_norm_g, ffn_norm_g, q_norm_a, k_norm_a, q_norm_b, k_norm_b,
           q_norm_c, k_norm_c, lambda_q1, lambda_k1, lambda_q2, lambda_k2, diff_subln_g, rel_bias,
           w_dense_gate, w_dense_up, w_dense_down, w_router, w_moe_gate, w_moe_up, w_moe_down):
    B, S, D = x.shape
    depth = w_in.shape[0]
    assert B == 1 and S % (DIL_STEPS * DILATED_PAIRS[-1][1]) == 0
    aw, bw, cw = A_HEADS * HEAD_DIM, B_HEADS * HEAD_DIM, C_HEADS * HEAD_DIM
    iw = IDX_HEADS * IDX_DIM
    sizes = (aw, aw, aw, iw, IDX_DIM, IDX_HEADS, bw, bw, bw, cw, cw, cw)
    offs = np.concatenate([[0], np.cumsum(sizes)]).tolist()

    def cols(w, *segs):
        return jnp.concatenate([w[:, offs[s]:offs[s + 1]] for s in segs], axis=1)

    tq, tk = 256, 512
    band_a = bias_band(rel_bias, A_HEADS, 0, _band_c0(tk) + 2 * tk, tq, _band_c0(tk))
    band_c = bias_band(rel_bias, C_HEADS, A_HEADS + B_HEADS, _band_c0(tk) + 2 * tk, tq, _band_c0(tk))
    bands_b = [bias_band(rel_bias, B_HEADS, A_HEADS, 2 * DIL_STEPS * dil + tq, tq, DIL_STEPS * dil, dil)
               for _, dil in DILATED_PAIRS]

    xs = x.reshape(S, D)
    for layer in range(depth):
        wl = w_in[layer]
        w_pa = cols(wl, 0, 1).astype(BF16)
        w_pb = cols(wl, 6, 7).astype(BF16)
        w_pc = cols(wl, 9, 10).astype(BF16)
        w_pi = cols(wl, 3, 4, 4).astype(BF16)
        w_vt = cols(wl, 2, 8, 11).T.astype(BF16)
        w_wt = cols(wl, 5).T.astype(BF16)
        qs_ab, qs_c = HEAD_DIM ** -0.5 * LOG2E, C_HALF ** -0.5 * LOG2E
        g_pa = jnp.concatenate([_tile_gain(q_norm_a[layer], A_HEADS, qs_ab), _tile_gain(k_norm_a[layer], A_HEADS)])
        g_pb = jnp.concatenate([_tile_gain(q_norm_b[layer], B_HEADS, qs_ab), _tile_gain(k_norm_b[layer], B_HEADS)])
        g_pc = jnp.concatenate([_tile_gain(q_norm_c[layer], 2 * C_HEADS, qs_c),
                                _tile_gain(k_norm_c[layer], 2 * C_HEADS)])

        h = rmsnorm(xs, attn_norm_g[layer])
        p_a = project(h, w_pa, g_pa, HEAD_DIM, BF16, tn=512)
        p_b = project(h, w_pb, g_pb, HEAD_DIM, BF16, tn=512)
        p_c = project(h, w_pc, g_pc, C_HALF, BF16, tn=512)
        p_i = project(h, w_pi, jnp.ones((w_pi.shape[1],), F32), 0, BF16, tn=384)
        vt = project_t(h, w_vt, BF16)
        wt = project_t(h, w_wt, F32)

        oa = dsa_attention(p_a, vt, p_i, wt, band_a)

        ob = dilated_attention(p_b, vt, bands_b, aw, tq=tq)

        lam_init = 0.8 - 0.6 * math.exp(-0.3 * layer)
        lam_params = jnp.stack([lambda_q1[layer], lambda_k1[layer], lambda_q2[layer], lambda_k2[layer]]).astype(F32)
        oc = diff_attention(p_c, vt, band_c, lam_params, diff_subln_g[layer].astype(F32), lam_init, aw + bw,
                            tq=tq, tk=tk)

        wo = w_out[layer].astype(BF16)
        xs = out_projection(xs, oa, ob, oc, wo[:aw], wo[aw:aw + bw], wo[aw + bw:])

        f = layer // 2
        if layer % 2 == 0:
            h2 = rmsnorm(xs, ffn_norm_g[layer])
            xs = ffn_dense(xs, h2, w_dense_gate[f], w_dense_up[f], w_dense_down[f])
        else:
            h2 = rmsnorm(xs, ffn_norm_g[layer])
            combine, sel = moe_router(xs, ffn_norm_g[layer], w_router[f])
            xs = ffn_moe(xs, h2, combine, sel, w_moe_gate[f], w_moe_up[f], w_moe_down[f])
    return xs.reshape(B, S, D)
```

```python
import functools
import math

import jax
import jax.numpy as jnp
import numpy as np
from jax import lax
from jax.experimental import pallas as pl
from jax.experimental.pallas import tpu as pltpu

F32 = jnp.float32
BF16 = jnp.bfloat16

HEAD_DIM = 128
A_HEADS, B_HEADS, C_HEADS = 4, 6, 6
C_HALF = HEAD_DIM // 2
IDX_HEADS, IDX_DIM = 16, 64
TOPK_MAX = 256
DILATED_PAIRS = ((128, 1), (512, 4), (2048, 16))
DIL_STEPS = 128
N_BUCKETS = 32
MAX_DISTANCE = 2048
N_EXPERTS = 8
RMS_EPS = 1e-6
NEG = -1e30
LOG2E = math.log2(math.e)
INT_MIN = -(2 ** 31)
I16_BIAS = 2 ** 15
CNT_ROWS = 64
LANE = 128
VMEM_LIMIT = 56 * 1024 * 1024

BUCKET_EDGES = tuple(range(16)) + (16,) + tuple(
    int(math.ceil(16.0 * (MAX_DISTANCE / 16.0) ** (k / 16.0))) for k in range(1, 16))
FAR_DIST = BUCKET_EDGES[-1]


def _cparams(sem):
    return pltpu.CompilerParams(dimension_semantics=sem, vmem_limit_bytes=VMEM_LIMIT)


def _dot(a, b):
    return jnp.dot(a, b, preferred_element_type=F32)


def _dot_nt(a, b):
    return lax.dot_general(a, b, (((1,), (1,)), ((), ())), preferred_element_type=F32)


def _rmsnorm_kernel(x_ref, g_ref, o_ref):
    x = x_ref[...]
    ms = jnp.mean(x * x, axis=-1, keepdims=True)
    o_ref[...] = (x * lax.rsqrt(ms + RMS_EPS) * g_ref[...]).astype(o_ref.dtype)


def rmsnorm(x, g, tm=512):
    S, D = x.shape
    return pl.pallas_call(
        _rmsnorm_kernel,
        grid=(S // tm,),
        in_specs=[pl.BlockSpec((tm, D), lambda i: (i, 0)), pl.BlockSpec((1, D), lambda i: (0, 0))],
        out_specs=pl.BlockSpec((tm, D), lambda i: (i, 0)),
        out_shape=jax.ShapeDtypeStruct((S, D), BF16),
        compiler_params=_cparams(("parallel",)),
        name="rmsnorm",
    )(x, g.reshape(1, D))


def _proj_kernel(h_ref, w_ref, g_ref, o_ref, *, group):
    acc = _dot(h_ref[...], w_ref[...])
    tn = acc.shape[1]
    if group == 0:
        o_ref[...] = acc.astype(o_ref.dtype)
        return
    for c in range(tn // LANE):
        a = acc[:, c * LANE:(c + 1) * LANE]
        sq = a * a
        if group == LANE:
            ms = jnp.mean(sq, axis=-1, keepdims=True)
        else:
            lane = lax.broadcasted_iota(jnp.int32, sq.shape, 1)
            lo = jnp.sum(jnp.where(lane < group, sq, 0.0), axis=-1, keepdims=True)
            hi = jnp.sum(jnp.where(lane >= group, sq, 0.0), axis=-1, keepdims=True)
            ms = jnp.where(lane < group, lo, hi) * (1.0 / group)
        y = a * lax.rsqrt(ms + RMS_EPS) * g_ref[:, c * LANE:(c + 1) * LANE]
        o_ref[:, c * LANE:(c + 1) * LANE] = y.astype(o_ref.dtype)


def project(h, w, gains, group, out_dtype, tm=1024, tn=256):
    S, K = h.shape
    N = w.shape[1]
    return pl.pallas_call(
        functools.partial(_proj_kernel, group=group),
        grid=(S // tm, N // tn),
        in_specs=[pl.BlockSpec((tm, K), lambda i, j: (i, 0)),
                  pl.BlockSpec((K, tn), lambda i, j: (0, j)),
                  pl.BlockSpec((1, tn), lambda i, j: (0, j))],
        out_specs=pl.BlockSpec((tm, tn), lambda i, j: (i, j)),
        out_shape=jax.ShapeDtypeStruct((S, N), out_dtype),
        compiler_params=_cparams(("parallel", "arbitrary")),
        name=f"proj_g{group}",
    )(h, w, gains.reshape(1, N))


def _proj_t_kernel(wt_ref, h_ref, o_ref):
    o_ref[...] = _dot_nt(wt_ref[...], h_ref[...]).astype(o_ref.dtype)


def project_t(h, wt, out_dtype, tm=1024, tn=256):
    S, K = h.shape
    N = wt.shape[0]
    tn = min(tn, N)
    return pl.pallas_call(
        _proj_t_kernel,
        grid=(S // tm, N // tn),
        in_specs=[pl.BlockSpec((tn, K), lambda i, j: (j, 0)),
                  pl.BlockSpec((tm, K), lambda i, j: (i, 0))],
        out_specs=pl.BlockSpec((tn, tm), lambda i, j: (j, i)),
        out_shape=jax.ShapeDtypeStruct((N, S), out_dtype),
        compiler_params=_cparams(("parallel", "arbitrary")),
        name="proj_t",
    )(wt, h)


BAND_ROWS = 256


def _band_kernel(tab_ref, o_ref, *, c0, head_off, dil):
    hh = head_off + pl.program_id(0)
    _, rows, width = o_ref.shape
    dist = (lax.broadcasted_iota(jnp.int32, (rows, width), 1)
            - lax.broadcasted_iota(jnp.int32, (rows, width), 0) + (c0 - pl.program_id(1) * rows))
    val = jnp.full((rows, width), tab_ref[0, hh] * LOG2E, F32)
    for b in range(1, N_BUCKETS):
        val = jnp.where(dist >= BUCKET_EDGES[b], tab_ref[b, hh] * LOG2E, val)
    ok = dist >= 0
    if dil:
        ok = ok & (dist <= DIL_STEPS * dil) & ((dist & (dil - 1)) == 0)
    o_ref[0] = jnp.where(ok, val, NEG)


def bias_band(rel_bias, n_heads, head_off, rows, width, c0, dil=0):
    assert rows % BAND_ROWS == 0 and dil & (dil - 1) == 0
    return pl.pallas_call(
        functools.partial(_band_kernel, c0=c0, head_off=head_off, dil=dil),
        grid=(n_heads, rows // BAND_ROWS),
        in_specs=[pl.BlockSpec(memory_space=pltpu.SMEM)],
        out_specs=pl.BlockSpec((1, BAND_ROWS, width), lambda h, r: (h, r, 0)),
        out_shape=jax.ShapeDtypeStruct((n_heads, rows, width), F32),
        compiler_params=_cparams(("parallel", "parallel")),
        name=f"bias_band_d{dil}",
    )(rel_bias)


def _band_c0(tk):
    return -(-(FAR_DIST + tk - 1) // LANE) * LANE


def _dsa_kernel(q_ref, qi_ref, wt_ref, k_ref, vt_ref, kidx_ref, band_ref, o_ref,
                hi_ref, lo_ref, cut_ref, qih_ref, s_ref, acc_ref, m_ref, l_ref, *, tq, kc, topk, c0):
    i = pl.program_id(0)
    t0 = i * tq
    nchunk = (t0 + tq + kc - 1) // kc
    half = lax.broadcasted_iota(jnp.int32, (tq, LANE), 1) < IDX_DIM

    for h in range(IDX_HEADS):
        slab = qi_ref[:, (h // 2) * LANE:(h // 2 + 1) * LANE]
        keep = half if h % 2 == 0 else jnp.logical_not(half)
        qih_ref[h] = jnp.where(keep, slab, jnp.zeros_like(slab))
    w = wt_ref[...] * (IDX_HEADS ** -0.5 * IDX_DIM ** -0.5)

    pos_q = t0 + lax.broadcasted_iota(jnp.int32, (kc, tq), 1)
    row = lax.broadcasted_iota(jnp.int32, (kc, tq), 0)

    def score_chunk(c, carry):
        s0 = pl.multiple_of(c * kc, kc)
        kx = kidx_ref[pl.ds(s0, kc), :]
        acc = jnp.zeros((kc, tq), F32)
        for h in range(IDX_HEADS):
            acc = acc + w[h:h + 1, :] * jnp.maximum(_dot_nt(kx, qih_ref[h]), 0.0)
        bits = lax.bitcast_convert_type(acc, jnp.int32)
        key = bits ^ ((bits >> 31) & 0x7FFFFFFF)
        key = jnp.where(s0 + row <= pos_q, key, INT_MIN)
        hi_ref[pl.ds(s0, kc), :] = (key >> 16).astype(jnp.int16)
        lo_ref[pl.ds(s0, kc), :] = ((key & 0xFFFF) - I16_BIAS).astype(jnp.int16)
        return carry

    lax.fori_loop(0, nchunk, score_chunk, 0)

    one16, zero16 = jnp.ones((), jnp.int16), jnp.zeros((), jnp.int16)
    row16 = lax.broadcasted_iota(jnp.int32, (kc, tq), 0).astype(jnp.int16)
    rowc16 = lax.broadcasted_iota(jnp.int32, (CNT_ROWS, tq), 0).astype(jnp.int16)
    n_keys = k_ref.shape[0]

    def to16(v):
        return jnp.clip(v, -I16_BIAS, I16_BIAS - 1).astype(jnp.int16)

    def count(hit):
        def body(c, cnt):
            base = pl.multiple_of(c * kc, kc)
            for g in range(kc // CNT_ROWS):
                cnt = cnt + jnp.where(hit(base + g * CNT_ROWS), one16, zero16)
            return cnt

        cnt = lax.fori_loop(0, nchunk, body, jnp.zeros((CNT_ROWS, tq), jnp.int16))
        return jnp.sum(cnt.astype(jnp.int32).astype(F32), axis=0, keepdims=True)

    def count_ge(ref, cand):
        c16 = cand.astype(jnp.int16)
        return count(lambda r: ref[pl.ds(r, CNT_ROWS), :] >= c16)

    def count_gt(ref, cand):
        c16 = cand.astype(jnp.int16)
        return count(lambda r: ref[pl.ds(r, CNT_ROWS), :] > c16)

    def kth_largest(ref, need):
        t = jnp.where(count_ge(ref, jnp.zeros((1, tq), jnp.int32)) >= need, 0, -I16_BIAS).astype(jnp.int32)

        def bit_body(b, t):
            cand = t + jnp.left_shift(jnp.int32(1), 14 - b)
            return jnp.where(count_ge(ref, cand) >= need, cand, t)

        return lax.fori_loop(0, 15, bit_body, t)

    kf = jnp.full((1, tq), float(topk), F32)
    thi = kth_largest(hi_ref, kf)
    need_lo = kf - count_gt(hi_ref, thi)
    thi16 = thi.astype(jnp.int16)

    def mask_low(c, carry):
        rows = pl.ds(pl.multiple_of(c * kc, kc), kc)
        lo_ref[rows, :] = jnp.where(hi_ref[rows, :] == thi16, lo_ref[rows, :], jnp.int16(-I16_BIAS))
        return carry

    lax.fori_loop(0, nchunk, mask_low, 0)
    tlo = kth_largest(lo_ref, need_lo)
    tlo16 = tlo.astype(jnp.int16)

    def is_eq(r):
        return (hi_ref[pl.ds(r, CNT_ROWS), :] == thi16) & (lo_ref[pl.ds(r, CNT_ROWS), :] == tlo16)

    keep_eq = need_lo - count_gt(lo_ref, tlo)
    excess = (count(is_eq) > keep_eq) & (thi > -I16_BIAS)
    cut_ref[...] = jnp.full(cut_ref.shape, n_keys, jnp.int32)

    @pl.when(jnp.max(jnp.broadcast_to(jnp.where(excess, 1.0, 0.0), (8, tq))) > 0.0)
    def _():
        nbits = n_keys.bit_length()

        def bit_body(b, j):
            cand = j + jnp.left_shift(jnp.int32(1), nbits - 1 - b)
            before = count(lambda r: is_eq(r) & (rowc16 < to16(cand - r)))
            return jnp.where(before < keep_eq, cand, j)

        j = lax.fori_loop(0, nbits, bit_body, jnp.zeros((1, tq), jnp.int32))
        cut_ref[...] = jnp.where(excess, j, n_keys)

    cut = cut_ref[...]
    thi_sel16 = jnp.maximum(thi, 1 - I16_BIAS).astype(jnp.int16)

    m_ref[...] = jnp.full(m_ref.shape, NEG, F32)
    l_ref[...] = jnp.zeros(l_ref.shape, F32)
    acc_ref[...] = jnp.zeros(acc_ref.shape, F32)

    def logits(c):
        s0 = pl.multiple_of(jnp.minimum(c, nchunk - 1) * kc, kc)
        off = pl.multiple_of(c0 - jnp.clip(t0 - c * kc, -kc, c0), LANE)
        hi = hi_ref[pl.ds(s0, kc), :]
        lo = lo_ref[pl.ds(s0, kc), :]
        sel = (hi > thi_sel16) | ((hi == thi_sel16)
                                  & ((lo > tlo16) | ((lo == tlo16) & (row16 <= to16(cut - s0)))))
        drop = jnp.where(sel, jnp.zeros((), jnp.int16), jnp.ones((), jnp.int16))
        mask_bias = drop.astype(jnp.int32).astype(F32) * NEG
        parts = []
        for h in range(A_HEADS):
            hs = slice(h * HEAD_DIM, (h + 1) * HEAD_DIM)
            parts.append(_dot_nt(k_ref[pl.ds(s0, kc), hs], q_ref[:, hs]) + band_ref[h, pl.ds(off, kc), :]
                         + mask_bias)
        return jnp.concatenate(parts, axis=1)

    def consume(c, s):
        s0 = pl.multiple_of(jnp.minimum(c, nchunk - 1) * kc, kc)
        m_prev = m_ref[...]
        m_new = jnp.maximum(m_prev, jnp.max(s, axis=0, keepdims=True))
        alpha = jnp.exp2(m_prev - m_new)
        p = jnp.exp2(s - m_new)
        l_ref[...] = alpha * l_ref[...] + jnp.sum(p, axis=0, keepdims=True)
        p = p.astype(BF16)
        pv = [_dot(vt_ref[h * HEAD_DIM:(h + 1) * HEAD_DIM, pl.ds(s0, kc)], p[:, h * tq:(h + 1) * tq])
              for h in range(A_HEADS)]
        acc_ref[...] = alpha * acc_ref[...] + jnp.concatenate(pv, axis=1)
        m_ref[...] = m_new

    s_ref[0] = logits(0)

    def chunk_pair(cc, carry):
        c = 2 * cc
        s_ref[1] = logits(c + 1)
        consume(c, s_ref[0])
        s_ref[0] = logits(c + 2)
        consume(c + 1, s_ref[1])
        return carry

    lax.fori_loop(0, (nchunk + 1) // 2, chunk_pair, 0)
    o = acc_ref[...] / l_ref[...]
    for h in range(A_HEADS):
        o_ref[:, h * HEAD_DIM:(h + 1) * HEAD_DIM] = o[:, h * tq:(h + 1) * tq].T.astype(o_ref.dtype)


def dsa_attention(qk, vt, pidx, wt, band, tq=256, kc=512):
    S = qk.shape[0]
    topk = min(TOPK_MAX, S // 4)
    aw = A_HEADS * HEAD_DIM
    iw = IDX_HEADS * IDX_DIM
    once = pl.Buffered(1)
    return pl.pallas_call(
        functools.partial(_dsa_kernel, tq=tq, kc=kc, topk=topk, c0=_band_c0(kc)),
        grid=(S // tq,),
        in_specs=[pl.BlockSpec((tq, aw), lambda i: (i, 0)),
                  pl.BlockSpec((tq, iw), lambda i: (i, 0)),
                  pl.BlockSpec((IDX_HEADS, tq), lambda i: (0, i)),
                  pl.BlockSpec((S, aw), lambda i: (0, 1), pipeline_mode=once),
                  pl.BlockSpec((aw, S), lambda i: (0, 0), pipeline_mode=once),
                  pl.BlockSpec((S, LANE), lambda i: (0, iw // LANE), pipeline_mode=once),
                  pl.BlockSpec(band.shape, lambda i: (0, 0, 0), pipeline_mode=once)],
        out_specs=pl.BlockSpec((tq, aw), lambda i: (i, 0)),
        out_shape=jax.ShapeDtypeStruct((S, aw), BF16),
        scratch_shapes=[pltpu.VMEM((S, tq), jnp.int16),
                        pltpu.VMEM((S, tq), jnp.int16),
                        pltpu.VMEM((1, tq), jnp.int32),
                        pltpu.VMEM((IDX_HEADS, tq, LANE), BF16),
                        pltpu.VMEM((2, kc, A_HEADS * tq), F32),
                        pltpu.VMEM((HEAD_DIM, A_HEADS * tq), F32),
                        pltpu.VMEM((1, A_HEADS * tq), F32),
                        pltpu.VMEM((1, A_HEADS * tq), F32)],
        compiler_params=_cparams(("arbitrary",)),
        name="dsa_attention",
    )(qk, pidx, wt, qk, vt, pidx, band)


DIL_ROWS = 384


def _dilated_kernel(q_ref, k_ref, vt_ref, b0_ref, b1_ref, b2_ref, o_ref, s_ref, *, tq):
    t0 = pl.program_id(1) * tq
    q = q_ref[...]
    tiles = []
    row0 = 0
    for band_ref, (_, dil) in zip((b0_ref, b1_ref, b2_ref), DILATED_PAIRS):
        span = DIL_STEPS * dil
        start = jnp.maximum(t0 - span, 0)
        off = span - (t0 - start)
        for j in range((span + tq) // DIL_ROWS):
            ks = pl.multiple_of(start + j * DIL_ROWS, LANE)
            bs = pl.multiple_of(off + j * DIL_ROWS, LANE)
            s_ref[row0:row0 + DIL_ROWS, :] = (_dot_nt(k_ref[pl.ds(ks, DIL_ROWS), :], q)
                                              + band_ref[0, pl.ds(bs, DIL_ROWS), :])
            tiles.append((ks, row0))
            row0 += DIL_ROWS
    m = jnp.full((1, tq), NEG, F32)
    l = jnp.zeros((1, tq), F32)
    acc = jnp.zeros((HEAD_DIM, tq), F32)
    for ks, r0 in tiles:
        s = s_ref[r0:r0 + DIL_ROWS, :]
        m_new = jnp.maximum(m, jnp.max(s, axis=0, keepdims=True))
        alpha = jnp.exp2(m - m_new)
        p = jnp.exp2(s - m_new)
        l = alpha * l + jnp.sum(p, axis=0, keepdims=True)
        acc = alpha * acc + _dot(vt_ref[:, pl.ds(ks, DIL_ROWS)], p.astype(BF16))
        m = m_new
    o_ref[...] = (acc / l).T.astype(o_ref.dtype)


def dilated_attention(qk, vt, bands, vt_row0, tq=256):
    S = qk.shape[0]
    vh0 = vt_row0 // HEAD_DIM
    lengths = [DIL_STEPS * dil + tq for _, dil in DILATED_PAIRS]
    assert all(n % DIL_ROWS == 0 and n <= S for n in lengths)
    return pl.pallas_call(
        functools.partial(_dilated_kernel, tq=tq),
        grid=(B_HEADS, S // tq),
        in_specs=[pl.BlockSpec((tq, LANE), lambda h, i: (i, h)),
                  pl.BlockSpec((S, LANE), lambda h, i: (0, B_HEADS + h)),
                  pl.BlockSpec((HEAD_DIM, S), lambda h, i: (vh0 + h, 0))]
                 + [pl.BlockSpec((1,) + b.shape[1:], lambda h, i: (h, 0, 0)) for b in bands],
        out_specs=pl.BlockSpec((tq, LANE), lambda h, i: (i, h)),
        out_shape=jax.ShapeDtypeStruct((S, B_HEADS * HEAD_DIM), BF16),
        scratch_shapes=[pltpu.VMEM((sum(lengths), tq), F32)],
        compiler_params=_cparams(("parallel", "arbitrary")),
        name="dilated_attention",
    )(qk, qk, vt, *bands)


def _diff_kernel(lam_ref, g_ref, q_ref, k_ref, vt_ref, band_ref, o_ref, acc_ref, m_ref, l_ref, s_ref,
                 *, tq, tk, c0, lam_init):
    i = pl.program_id(1)
    t0 = i * tq
    nchunk = (t0 + tq + tk - 1) // tk
    last_chunk = k_ref.shape[0] // tk - 1
    q = q_ref[...]
    lane = lax.broadcasted_iota(jnp.int32, q.shape, 1)
    zero = jnp.zeros_like(q)
    q2 = jnp.concatenate([jnp.where(lane < C_HALF, q, zero), jnp.where(lane >= C_HALF, q, zero)], axis=0)

    m_ref[...] = jnp.full(m_ref.shape, NEG, F32)
    l_ref[...] = jnp.zeros(l_ref.shape, F32)
    acc_ref[...] = jnp.zeros(acc_ref.shape, F32)

    def logits(c):
        s0 = pl.multiple_of(jnp.minimum(c, last_chunk) * tk, tk)
        off = pl.multiple_of(c0 - jnp.clip(t0 - c * tk, -tk, c0), LANE)
        bias = band_ref[0, pl.ds(off, tk), :]
        return _dot_nt(k_ref[pl.ds(s0, tk), :], q2) + jnp.concatenate([bias, bias], axis=1)

    def consume(c, s):
        s0 = pl.multiple_of(jnp.minimum(c, last_chunk) * tk, tk)
        m_prev = m_ref[...]
        m_new = jnp.maximum(m_prev, jnp.max(s, axis=0, keepdims=True))
        alpha = jnp.exp2(m_prev - m_new)
        p = jnp.exp2(s - m_new)
        l_ref[...] = alpha * l_ref[...] + jnp.sum(p, axis=0, keepdims=True)
        acc_ref[...] = alpha * acc_ref[...] + _dot(vt_ref[:, pl.ds(s0, tk)], p.astype(BF16))
        m_ref[...] = m_new

    s_ref[0] = logits(0)

    def chunk_pair(cc, carry):
        c = 2 * cc
        s_ref[1] = logits(c + 1)
        consume(c, s_ref[0])
        s_ref[0] = logits(c + 2)
        consume(c + 1, s_ref[1])
        return carry

    lax.fori_loop(0, (nchunk + 1) // 2, chunk_pair, 0)

    lp = lam_ref[...]
    lam = (jnp.exp(jnp.sum(lp[0:1] * lp[1:2], axis=1, keepdims=True))
           - jnp.exp(jnp.sum(lp[2:3] * lp[3:4], axis=1, keepdims=True)) + lam_init)
    o = acc_ref[...] / l_ref[...]
    o = o[:, :tq] - lam * o[:, tq:]
    ms = jnp.mean(o * o, axis=0, keepdims=True)
    o = o * lax.rsqrt(ms + RMS_EPS) * (g_ref[...] * (1.0 - lam_init))
    o_ref[...] = o.T.astype(o_ref.dtype)


def diff_attention(qk, vt, band, lam_params, subln_g, lam_init, vt_row0, tq=256, tk=512):
    S = qk.shape[0]
    vh0 = vt_row0 // HEAD_DIM
    return pl.pallas_call(
        functools.partial(_diff_kernel, tq=tq, tk=tk, c0=_band_c0(tk), lam_init=lam_init),
        grid=(C_HEADS, S // tq),
        in_specs=[pl.BlockSpec((4, C_HALF), lambda h, i: (0, 0)),
                  pl.BlockSpec((HEAD_DIM, 1), lambda h, i: (0, 0)),
                  pl.BlockSpec((tq, LANE), lambda h, i: (i, h)),
                  pl.BlockSpec((S, LANE), lambda h, i: (0, C_HEADS + h)),
                  pl.BlockSpec((HEAD_DIM, S), lambda h, i: (vh0 + h, 0)),
                  pl.BlockSpec((1,) + band.shape[1:], lambda h, i: (h, 0, 0))],
        out_specs=pl.BlockSpec((tq, LANE), lambda h, i: (i, h)),
        out_shape=jax.ShapeDtypeStruct((S, C_HEADS * HEAD_DIM), BF16),
        scratch_shapes=[pltpu.VMEM((HEAD_DIM, 2 * tq), F32),
                        pltpu.VMEM((1, 2 * tq), F32),
                        pltpu.VMEM((1, 2 * tq), F32),
                        pltpu.VMEM((2, tk, 2 * tq), F32)],
        compiler_params=_cparams(("parallel", "arbitrary")),
        name="diff_attention",
    )(lam_params, subln_g.reshape(HEAD_DIM, 1), qk, qk, vt, band)


def _outproj_kernel(x_ref, a_ref, b_ref, c_ref, wa_ref, wb_ref, wc_ref, o_ref):
    o_ref[...] = (x_ref[...] + _dot(a_ref[...], wa_ref[...]) + _dot(b_ref[...], wb_ref[...])
                  + _dot(c_ref[...], wc_ref[...]))


def out_projection(x, oa, ob, oc, wa, wb, wc, tm=512, tn=1024):
    S, D = x.shape
    tn = min(tn, D)
    return pl.pallas_call(
        _outproj_kernel,
        grid=(S // tm, D // tn),
        in_specs=[pl.BlockSpec((tm, tn), lambda i, j: (i, j)),
                  pl.BlockSpec((tm, oa.shape[1]), lambda i, j: (i, 0)),
                  pl.BlockSpec((tm, ob.shape[1]), lambda i, j: (i, 0)),
                  pl.BlockSpec((tm, oc.shape[1]), lambda i, j: (i, 0)),
                  pl.BlockSpec((wa.shape[0], tn), lambda i, j: (0, j)),
                  pl.BlockSpec((wb.shape[0], tn), lambda i, j: (0, j)),
                  pl.BlockSpec((wc.shape[0], tn), lambda i, j: (0, j))],
        out_specs=pl.BlockSpec((tm, tn), lambda i, j: (i, j)),
        out_shape=jax.ShapeDtypeStruct((S, D), F32),
        compiler_params=_cparams(("parallel", "arbitrary")),
        name="out_projection",
    )(x, oa, ob, oc, wa, wb, wc)


def _silu(x):
    return x / (1.0 + jnp.exp(-x))


def _ffn_kernel(x_ref, h_ref, wg_ref, wu_ref, wd_ref, o_ref):
    f = pl.program_id(1)

    @pl.when(f == 0)
    def _():
        o_ref[...] = x_ref[...]

    h = h_ref[...]
    a = _silu(_dot(h, wg_ref[...].astype(BF16))) * _dot(h, wu_ref[...].astype(BF16))
    o_ref[...] += _dot(a.astype(BF16), wd_ref[...].astype(BF16))


def ffn_dense(x, h, wg, wu, wd, tm=1024, tf=256):
    S, D = x.shape
    F = wg.shape[1]
    once = pl.Buffered(1)
    return pl.pallas_call(
        _ffn_kernel,
        grid=(S // tm, F // tf),
        in_specs=[pl.BlockSpec((tm, D), lambda i, f: (i, 0), pipeline_mode=once),
                  pl.BlockSpec((tm, D), lambda i, f: (i, 0), pipeline_mode=once),
                  pl.BlockSpec((D, tf), lambda i, f: (0, f)),
                  pl.BlockSpec((D, tf), lambda i, f: (0, f)),
                  pl.BlockSpec((tf, D), lambda i, f: (f, 0))],
        out_specs=pl.BlockSpec((tm, D), lambda i, f: (i, 0)),
        out_shape=jax.ShapeDtypeStruct((S, D), F32),
        compiler_params=_cparams(("parallel", "arbitrary")),
        name="ffn_dense",
    )(x, h, wg, wu, wd)


def _router_kernel(x_ref, g_ref, w_ref, c_ref, sel_ref):
    x = x_ref[...]
    ms = jnp.mean(x * x, axis=-1, keepdims=True)
    h = x * lax.rsqrt(ms + RMS_EPS) * g_ref[...]
    logits = jnp.dot(h, w_ref[...], preferred_element_type=F32, precision=lax.Precision.HIGHEST)
    lane = lax.broadcasted_iota(jnp.int32, logits.shape, 1).astype(F32)
    logits = jnp.where(lane < N_EXPERTS, logits, NEG)
    m1 = jnp.max(logits, axis=1, keepdims=True)
    i1 = jnp.min(jnp.where(logits == m1, lane, float(LANE)), axis=1, keepdims=True)
    rest = jnp.where(lane == i1, NEG, logits)
    m2 = jnp.max(rest, axis=1, keepdims=True)
    i2 = jnp.min(jnp.where(rest == m2, lane, float(LANE)), axis=1, keepdims=True)
    e2 = jnp.exp(m2 - m1)
    g1 = 1.0 / (1.0 + e2)
    g2 = e2 / (1.0 + e2)
    pick1, pick2 = lane == i1, lane == i2
    c_ref[...] = jnp.where(pick1, g1, 0.0) + jnp.where(pick2, g2, 0.0)
    sel_ref[...] = jnp.where(pick1 | pick2, 1.0, 0.0)


def moe_router(x, g, w_router, tm=512):
    S, D = x.shape
    wpad = jnp.zeros((D, LANE), F32).at[:, :N_EXPERTS].set(w_router)
    spec = pl.BlockSpec((tm, LANE), lambda i: (i, 0))
    sds = jax.ShapeDtypeStruct((S, LANE), F32)
    return pl.pallas_call(
        _router_kernel,
        grid=(S // tm,),
        in_specs=[pl.BlockSpec((tm, D), lambda i: (i, 0)),
                  pl.BlockSpec((1, D), lambda i: (0, 0)),
                  pl.BlockSpec((D, LANE), lambda i: (0, 0))],
        out_specs=[spec, spec],
        out_shape=[sds, sds],
        compiler_params=_cparams(("parallel",)),
        name="moe_router",
    )(x, g.reshape(1, D), wpad)


MOE_CH = 256
MOE_TILE = 1024


def _moe_rank_kernel(sel_ref, rm_ref, rmt_ref, cum_ref, tot_ref, carry_ref):
    c = pl.program_id(0)

    @pl.when(c == 0)
    def _():
        carry_ref[...] = jnp.zeros(carry_ref.shape, F32)

    sel = sel_ref[...]
    ch = sel.shape[0]
    before = (lax.broadcasted_iota(jnp.int32, (ch, ch), 1) < lax.broadcasted_iota(jnp.int32, (ch, ch), 0))
    rank = _dot(jnp.where(before, 1.0, 0.0).astype(BF16), sel.astype(BF16)) + carry_ref[...]
    rm = jnp.where(sel > 0.0, rank, -1.0)
    rm_ref[...] = rm
    rmt_ref[...] = rm.T[:N_EXPERTS]
    cum_ref[0] = carry_ref[...]
    carry_ref[...] += jnp.sum(sel, axis=0, keepdims=True)
    tot_ref[...] = carry_ref[...]


def moe_rank(sel):
    S = sel.shape[0]
    nch = S // MOE_CH
    return pl.pallas_call(
        _moe_rank_kernel,
        grid=(nch,),
        in_specs=[pl.BlockSpec((MOE_CH, LANE), lambda c: (c, 0))],
        out_specs=[pl.BlockSpec((MOE_CH, LANE), lambda c: (c, 0)),
                   pl.BlockSpec((N_EXPERTS, MOE_CH), lambda c: (0, c)),
                   pl.BlockSpec((1, 1, LANE), lambda c: (c, 0, 0)),
                   pl.BlockSpec((1, LANE), lambda c: (0, 0))],
        out_shape=[jax.ShapeDtypeStruct((S, LANE), F32),
                   jax.ShapeDtypeStruct((N_EXPERTS, S), F32),
                   jax.ShapeDtypeStruct((nch, 1, LANE), F32),
                   jax.ShapeDtypeStruct((1, LANE), F32)],
        scratch_shapes=[pltpu.VMEM((1, LANE), F32)],
        compiler_params=_cparams(("arbitrary",)),
        name="moe_rank",
    )(sel)


def _moe_expert_kernel(te_ref, tr0_ref, tclo_ref, tchi_ref, tval_ref,
                       rk_ref, h_hbm, wg_ref, wu_ref, wd_ref, y_ref,
                       hs_ref, acc_ref, hbuf_ref, sem_ref):
    j = pl.program_id(0)
    f = pl.program_id(1)
    nf = pl.num_programs(1)
    nrows = tval_ref[j]
    tile = hs_ref.shape[0]

    def chunk_copy(c, slot):
        return pltpu.make_async_copy(h_hbm.at[pl.ds(c * MOE_CH, MOE_CH)], hbuf_ref.at[slot], sem_ref.at[slot])

    def gather(rows):
        clo, chi = tclo_ref[j], tchi_ref[j]
        want = (tr0_ref[j] + lax.broadcasted_iota(jnp.int32, (rows, MOE_CH), 0)).astype(F32)
        acc_ref[...] = jnp.zeros(acc_ref.shape, F32)

        @pl.when(chi > clo)
        def _():
            chunk_copy(clo, 0).start()

        def body(c, carry):
            slot = (c - clo) % 2

            @pl.when(c + 1 < chi)
            def _():
                chunk_copy(c + 1, 1 - slot).start()

            chunk_copy(c, slot).wait()
            rk = rk_ref[0, :, pl.ds(pl.multiple_of(c * MOE_CH, MOE_CH), MOE_CH)]
            onehot = jnp.where(rk == want, 1.0, 0.0).astype(BF16)
            acc_ref[:rows] += _dot(onehot, hbuf_ref[slot])
            return carry

        lax.fori_loop(clo, chi, body, 0)
        hs_ref[:rows] = acc_ref[:rows].astype(BF16)
        acc_ref[:rows] = jnp.zeros((rows, acc_ref.shape[1]), F32)

    def swiglu(rows):
        hs = hs_ref[:rows]
        a = _silu(_dot(hs, wg_ref[0].astype(BF16))) * _dot(hs, wu_ref[0].astype(BF16))
        acc_ref[:rows] += _dot(a.astype(BF16), wd_ref[0].astype(BF16))

    sizes = (tile, tile // 2, tile // 4, tile // 8)
    for rows, below in zip(sizes, sizes[1:] + (0,)):
        cond = (nrows > below) & (nrows <= rows)
        @pl.when(cond & (f == 0))
        def _():
            gather(rows)

        @pl.when(cond)
        def _():
            swiglu(rows)

    @pl.when(f == nf - 1)
    def _():
        y_ref[...] = jnp.where(nrows > 0, acc_ref[...], 0.0).astype(y_ref.dtype)


def _moe_combine_kernel(ca_ref, cb_ref, roff_ref, x_ref, rm_ref, cw_ref, ya_ref, yb_ref, o_ref):
    c = pl.program_id(0)
    e = pl.program_id(1)

    @pl.when(e == 0)
    def _():
        o_ref[...] = x_ref[...]

    rm = rm_ref[...]
    lane = lax.broadcasted_iota(jnp.int32, rm.shape, 1)
    mine = lane == e
    pos = jnp.sum(jnp.where(mine, rm, 0.0), axis=1, keepdims=True)
    gate = jnp.sum(jnp.where(mine, cw_ref[...], 0.0), axis=1, keepdims=True)
    row = jnp.where(pos >= 0.0, pos + roff_ref[e].astype(F32), -1.0)
    k = c * pl.num_programs(1) + e
    ca, cb = ca_ref[k], cb_ref[k]
    cr = ya_ref.shape[0]
    col = lax.broadcasted_iota(jnp.int32, (rm.shape[0], cr), 1)
    for y_ref, base in ((ya_ref, ca * cr), (yb_ref, jnp.where(cb != ca, cb * cr, -2 * cr))):
        onehot = jnp.where(row == (base + col).astype(F32), 1.0, 0.0).astype(BF16)
        o_ref[...] += gate * _dot(onehot, y_ref[...])


def ffn_moe(x, h, combine, sel, wg, wu, wd, tf=512):
    S, D = x.shape
    E, _, F = wg.shape
    tf = min(tf, F)
    T = MOE_TILE
    nch = S // MOE_CH
    J = 2 * S // T + E
    rm, rmt, cum, tot = moe_rank(sel)

    cnt = tot[0, :E].astype(jnp.int32)
    ntile = (cnt + T - 1) // T
    tend = jnp.cumsum(ntile)
    tstart = tend - ntile
    jj = jnp.arange(J, dtype=jnp.int32)
    te = jnp.minimum(jnp.searchsorted(tend, jnp.minimum(jj, tend[-1] - 1), side="right"), E - 1).astype(jnp.int32)
    tr0 = (jnp.minimum(jj, tend[-1] - 1) - tstart[te]) * T
    tval = jnp.where(jj < tend[-1], jnp.clip(cnt[te] - tr0, 0, T), 0).astype(jnp.int32)
    cumx = cum[:, 0, :E].astype(jnp.int32)
    cumi = jnp.concatenate([cumx[1:], cnt[None]], axis=0)
    tclo = jnp.sum(cumi[:, te] <= tr0[None, :], axis=0).astype(jnp.int32)
    tchi = jnp.sum(cumx[:, te] < (tr0 + tval)[None, :], axis=0).astype(jnp.int32)
    roff = (tstart * T).astype(jnp.int32)
    nrow_chunks = J * T // MOE_CH
    ca = ((roff[None, :] + cumx) // MOE_CH).astype(jnp.int32)
    cb = jnp.minimum(ca + 1, nrow_chunks - 1)
    ca, cb = ca.reshape(-1), cb.reshape(-1)

    nf = F // tf

    def wmap(j, f, te_r, tr0_r, clo_r, chi_r, val_r):
        return (te_r[j], 0, jnp.where(val_r[j] > 0, f, nf - 1))

    def wdmap(j, f, te_r, tr0_r, clo_r, chi_r, val_r):
        return (te_r[j], jnp.where(val_r[j] > 0, f, nf - 1), 0)

    y = pl.pallas_call(
        _moe_expert_kernel,
        grid_spec=pltpu.PrefetchScalarGridSpec(
            num_scalar_prefetch=5,
            grid=(J, nf),
            in_specs=[pl.BlockSpec((1, 1, S), lambda j, f, te_r, *_: (te_r[j], 0, 0)),
                      pl.BlockSpec(memory_space=pl.ANY),
                      pl.BlockSpec((1, D, tf), wmap),
                      pl.BlockSpec((1, D, tf), wmap),
                      pl.BlockSpec((1, tf, D), wdmap)],
            out_specs=pl.BlockSpec((T, D), lambda j, f, *_: (j, 0)),
            scratch_shapes=[pltpu.VMEM((T, D), BF16),
                            pltpu.VMEM((T, D), F32),
                            pltpu.VMEM((2, MOE_CH, D), BF16),
                            pltpu.SemaphoreType.DMA((2,))]),
        out_shape=jax.ShapeDtypeStruct((J * T, D), BF16),
        compiler_params=_cparams(("arbitrary", "arbitrary")),
        name="moe_experts",
    )(te, tr0, tclo, tchi, tval, rmt.reshape(E, 1, S), h, wg, wu, wd)

    return pl.pallas_call(
        _moe_combine_kernel,
        grid_spec=pltpu.PrefetchScalarGridSpec(
            num_scalar_prefetch=3,
            grid=(nch, E),
            in_specs=[pl.BlockSpec((MOE_CH, D), lambda c, e, *_: (c, 0)),
                      pl.BlockSpec((MOE_CH, LANE), lambda c, e, *_: (c, 0)),
                      pl.BlockSpec((MOE_CH, LANE), lambda c, e, *_: (c, 0)),
                      pl.BlockSpec((MOE_CH, D), lambda c, e, ca_r, cb_r, ro_r: (ca_r[c * E + e], 0)),
                      pl.BlockSpec((MOE_CH, D), lambda c, e, ca_r, cb_r, ro_r: (cb_r[c * E + e], 0))],
            out_specs=pl.BlockSpec((MOE_CH, D), lambda c, e, *_: (c, 0))),
        out_shape=jax.ShapeDtypeStruct((S, D), F32),
        compiler_params=_cparams(("parallel", "arbitrary")),
        name="moe_combine",
    )(ca, cb, roff, x, rm, combine, y, y)


def _tile_gain(g, reps, scale=1.0):
    return jnp.tile(g.astype(F32) * scale, reps)


def kernel(x, w_in, w_out, attn_norm_g, ffn_norm_g, q_norm_a, k_norm_a, q_norm_b, k_norm_b,
           q_norm_c, k_norm_c, lambda_q1, lambda_k1, lambda_q2, lambda_k2, diff_subln_g, rel_bias,
           w_dense_gate, w_dense_up, w_dense_down, w_router, w_moe_gate, w_moe_up, w_moe_down):
    B, S, D = x.shape
    depth = w_in.shape[0]
    assert B == 1 and S % (DIL_STEPS * DILATED_PAIRS[-1][1]) == 0
    aw, bw, cw = A_HEADS * HEAD_DIM, B_HEADS * HEAD_DIM, C_HEADS * HEAD_DIM
    iw = IDX_HEADS * IDX_DIM
    sizes = (aw, aw, aw, iw, IDX_DIM, IDX_HEADS, bw, bw, bw, cw, cw, cw)
    offs = np.concatenate([[0], np.cumsum(sizes)]).tolist()

    def cols(w, *segs):
        return jnp.concatenate([w[:, offs[s]:offs[s + 1]] for s in segs], axis=1)

    tq, tk = 256, 512
    band_a = bias_band(rel_bias, A_HEADS, 0, _band_c0(tk) + 2 * tk, tq, _band_c0(tk))
    band_c = bias_band(rel_bias, C_HEADS, A_HEADS + B_HEADS, _band_c0(tk) + 2 * tk, tq, _band_c0(tk))
    bands_b = [bias_band(rel_bias, B_HEADS, A_HEADS, 2 * DIL_STEPS * dil + tq, tq, DIL_STEPS * dil, dil)
               for _, dil in DILATED_PAIRS]

    xs = x.reshape(S, D)
    for layer in range(depth):
        wl = w_in[layer]
        w_pa = cols(wl, 0, 1).astype(BF16)
        w_pb = cols(wl, 6, 7).astype(BF16)
        w_pc = cols(wl, 9, 10).astype(BF16)
        w_pi = cols(wl, 3, 4, 4).astype(BF16)
        w_vt = cols(wl, 2, 8, 11).T.astype(BF16)
        w_wt = cols(wl, 5).T.astype(BF16)
        qs_ab, qs_c = HEAD_DIM ** -0.5 * LOG2E, C_HALF ** -0.5 * LOG2E
        g_pa = jnp.concatenate([_tile_gain(q_norm_a[layer], A_HEADS, qs_ab), _tile_gain(k_norm_a[layer], A_HEADS)])
        g_pb = jnp.concatenate([_tile_gain(q_norm_b[layer], B_HEADS, qs_ab), _tile_gain(k_norm_b[layer], B_HEADS)])
        g_pc = jnp.concatenate([_tile_gain(q_norm_c[layer], 2 * C_HEADS, qs_c),
                                _tile_gain(k_norm_c[layer], 2 * C_HEADS)])

        h = rmsnorm(xs, attn_norm_g[layer])
        p_a = project(h, w_pa, g_pa, HEAD_DIM, BF16, tn=512)
        p_b = project(h, w_pb, g_pb, HEAD_DIM, BF16, tn=512)
        p_c = project(h, w_pc, g_pc, C_HALF, BF16, tn=512)
        p_i = project(h, w_pi, jnp.ones((w_pi.shape[1],), F32), 0, BF16, tn=384)
        vt = project_t(h, w_vt, BF16)
        wt = project_t(h, w_wt, F32)

        oa = dsa_attention(p_a, vt, p_i, wt, band_a)

        ob = dilated_attention(p_b, vt, bands_b, aw, tq=tq)

        lam_init = 0.8 - 0.6 * math.exp(-0.3 * layer)
        lam_params = jnp.stack([lambda_q1[layer], lambda_k1[layer], lambda_q2[layer], lambda_k2[layer]]).astype(F32)
        oc = diff_attention(p_c, vt, band_c, lam_params, diff_subln_g[layer].astype(F32), lam_init, aw + bw,
                            tq=tq, tk=tk)

        wo = w_out[layer].astype(BF16)
        xs = out_projection(xs, oa, ob, oc, wo[:aw], wo[aw:aw + bw], wo[aw + bw:])

        f = layer // 2
        if layer % 2 == 0:
            h2 = rmsnorm(xs, ffn_norm_g[layer])
            xs = ffn_dense(xs, h2, w_dense_gate[f], w_dense_up[f], w_dense_down[f])
        else:
            h2 = rmsnorm(xs, ffn_norm_g[layer])
            combine, sel = moe_router(xs, ffn_norm_g[layer], w_router[f])
            xs = ffn_moe(xs, h2, combine, sel, w_moe_gate[f], w_moe_up[f], w_moe_down[f])
    return xs.reshape(B, S, D)
```

```python
import functools
import math

import jax
import jax.numpy as jnp
import numpy as np
from jax import lax
from jax.experimental import pallas as pl
from jax.experimental.pallas import tpu as pltpu

F32 = jnp.float32
BF16 = jnp.bfloat16

HEAD_DIM = 128
A_HEADS, B_HEADS, C_HEADS = 4, 6, 6
C_HALF = HEAD_DIM // 2
IDX_HEADS, IDX_DIM = 16, 64
TOPK_MAX = 256
DILATED_PAIRS = ((128, 1), (512, 4), (2048, 16))
DIL_STEPS = 128
N_BUCKETS = 32
MAX_DISTANCE = 2048
N_EXPERTS = 8
RMS_EPS = 1e-6
NEG = -1e30
LOG2E = math.log2(math.e)
INT_MIN = -(2 ** 31)
I16_BIAS = 2 ** 15
CNT_ROWS = 64
LANE = 128
VMEM_LIMIT = 56 * 1024 * 1024

BUCKET_EDGES = tuple(range(16)) + (16,) + tuple(
    int(math.ceil(16.0 * (MAX_DISTANCE / 16.0) ** (k / 16.0))) for k in range(1, 16))
FAR_DIST = BUCKET_EDGES[-1]


def _cparams(sem):
    return pltpu.CompilerParams(dimension_semantics=sem, vmem_limit_bytes=VMEM_LIMIT)


def _dot(a, b):
    return jnp.dot(a, b, preferred_element_type=F32)


def _dot_nt(a, b):
    return lax.dot_general(a, b, (((1,), (1,)), ((), ())), preferred_element_type=F32)


def _rmsnorm_kernel(x_ref, g_ref, o_ref):
    x = x_ref[...]
    ms = jnp.mean(x * x, axis=-1, keepdims=True)
    o_ref[...] = (x * lax.rsqrt(ms + RMS_EPS) * g_ref[...]).astype(o_ref.dtype)


def rmsnorm(x, g, tm=512):
    S, D = x.shape
    return pl.pallas_call(
        _rmsnorm_kernel,
        grid=(S // tm,),
        in_specs=[pl.BlockSpec((tm, D), lambda i: (i, 0)), pl.BlockSpec((1, D), lambda i: (0, 0))],
        out_specs=pl.BlockSpec((tm, D), lambda i: (i, 0)),
        out_shape=jax.ShapeDtypeStruct((S, D), BF16),
        compiler_params=_cparams(("parallel",)),
        name="rmsnorm",
    )(x, g.reshape(1, D))


def _proj_kernel(h_ref, w_ref, g_ref, o_ref, *, group):
    acc = _dot(h_ref[...], w_ref[...])
    tn = acc.shape[1]
    if group == 0:
        o_ref[...] = acc.astype(o_ref.dtype)
        return
    for c in range(tn // LANE):
        a = acc[:, c * LANE:(c + 1) * LANE]
        sq = a * a
        if group == LANE:
            ms = jnp.mean(sq, axis=-1, keepdims=True)
        else:
            lane = lax.broadcasted_iota(jnp.int32, sq.shape, 1)
            lo = jnp.sum(jnp.where(lane < group, sq, 0.0), axis=-1, keepdims=True)
            hi = jnp.sum(jnp.where(lane >= group, sq, 0.0), axis=-1, keepdims=True)
            ms = jnp.where(lane < group, lo, hi) * (1.0 / group)
        y = a * lax.rsqrt(ms + RMS_EPS) * g_ref[:, c * LANE:(c + 1) * LANE]
        o_ref[:, c * LANE:(c + 1) * LANE] = y.astype(o_ref.dtype)


def project(h, w, gains, group, out_dtype, tm=1024, tn=256):
    S, K = h.shape
    N = w.shape[1]
    return pl.pallas_call(
        functools.partial(_proj_kernel, group=group),
        grid=(S // tm, N // tn),
        in_specs=[pl.BlockSpec((tm, K), lambda i, j: (i, 0)),
                  pl.BlockSpec((K, tn), lambda i, j: (0, j)),
                  pl.BlockSpec((1, tn), lambda i, j: (0, j))],
        out_specs=pl.BlockSpec((tm, tn), lambda i, j: (i, j)),
        out_shape=jax.ShapeDtypeStruct((S, N), out_dtype),
        compiler_params=_cparams(("parallel", "arbitrary")),
        name=f"proj_g{group}",
    )(h, w, gains.reshape(1, N))


def _proj_t_kernel(wt_ref, h_ref, o_ref):
    o_ref[...] = _dot_nt(wt_ref[...], h_ref[...]).astype(o_ref.dtype)


def project_t(h, wt, out_dtype, tm=1024, tn=256):
    S, K = h.shape
    N = wt.shape[0]
    tn = min(tn, N)
    return pl.pallas_call(
        _proj_t_kernel,
        grid=(S // tm, N // tn),
        in_specs=[pl.BlockSpec((tn, K), lambda i, j: (j, 0)),
                  pl.BlockSpec((tm, K), lambda i, j: (i, 0))],
        out_specs=pl.BlockSpec((tn, tm), lambda i, j: (j, i)),
        out_shape=jax.ShapeDtypeStruct((N, S), out_dtype),
        compiler_params=_cparams(("parallel", "arbitrary")),
        name="proj_t",
    )(wt, h)


BAND_ROWS = 256


def _band_kernel(tab_ref, o_ref, *, c0, head_off, dil):
    hh = head_off + pl.program_id(0)
    _, rows, width = o_ref.shape
    base = (lax.broadcasted_iota(jnp.int32, (BAND_ROWS, width), 1)
            - lax.broadcasted_iota(jnp.int32, (BAND_ROWS, width), 0))
    hi_valid = DIL_STEPS * dil if dil else None
    for r in range(rows // BAND_ROWS):
        d0 = c0 - r * BAND_ROWS
        lo, hi = d0 - (BAND_ROWS - 1), d0 + width - 1
        if hi_valid is not None:
            hi = min(hi, hi_valid)
        block = pl.ds(r * BAND_ROWS, BAND_ROWS)
        if hi < 0 or hi < lo:
            o_ref[0, block, :] = jnp.full((BAND_ROWS, width), NEG, F32)
            continue
        dist = base + d0
        first = max(b for b in range(N_BUCKETS) if BUCKET_EDGES[b] <= max(lo, 0))
        val = jnp.full((BAND_ROWS, width), tab_ref[first, hh] * LOG2E, F32)
        for b in range(first + 1, N_BUCKETS):
            if BUCKET_EDGES[b] <= hi:
                val = jnp.where(dist >= BUCKET_EDGES[b], tab_ref[b, hh] * LOG2E, val)
        ok = None
        if lo < 0:
            ok = dist >= 0
        if dil:
            in_win = (dist <= hi_valid) & ((dist & (dil - 1)) == 0)
            ok = in_win if ok is None else ok & in_win
        o_ref[0, block, :] = val if ok is None else jnp.where(ok, val, NEG)


def bias_band(rel_bias, n_heads, head_off, rows, width, c0, dil=0):
    assert rows % BAND_ROWS == 0 and dil & (dil - 1) == 0
    return pl.pallas_call(
        functools.partial(_band_kernel, c0=c0, head_off=head_off, dil=dil),
        grid=(n_heads,),
        in_specs=[pl.BlockSpec(memory_space=pltpu.SMEM)],
        out_specs=pl.BlockSpec((1, rows, width), lambda h: (h, 0, 0)),
        out_shape=jax.ShapeDtypeStruct((n_heads, rows, width), F32),
        compiler_params=_cparams(("parallel",)),
        name=f"bias_band_d{dil}",
    )(rel_bias)


def _band_c0(tk):
    return -(-(FAR_DIST + tk - 1) // LANE) * LANE


def _dsa_kernel(q_ref, qi_ref, wt_ref, k_ref, vt_ref, kidx_ref, band_ref, o_ref,
                hi_ref, lo_ref, cut_ref, qih_ref, s_ref, acc_ref, m_ref, l_ref, *, tq, kc, topk, c0):
    i = pl.program_id(0)
    t0 = i * tq
    nchunk = (t0 + tq + kc - 1) // kc
    half = lax.broadcasted_iota(jnp.int32, (tq, LANE), 1) < IDX_DIM

    for h in range(IDX_HEADS):
        slab = qi_ref[:, (h // 2) * LANE:(h // 2 + 1) * LANE]
        keep = half if h % 2 == 0 else jnp.logical_not(half)
        qih_ref[h] = jnp.where(keep, slab, jnp.zeros_like(slab))
    w = wt_ref[...] * (IDX_HEADS ** -0.5 * IDX_DIM ** -0.5)

    pos_q = t0 + lax.broadcasted_iota(jnp.int32, (kc, tq), 1)
    row = lax.broadcasted_iota(jnp.int32, (kc, tq), 0)

    def score_chunk(c, carry):
        s0 = pl.multiple_of(c * kc, kc)
        kx = kidx_ref[pl.ds(s0, kc), :]
        acc = jnp.zeros((kc, tq), F32)
        for h in range(IDX_HEADS):
            acc = acc + w[h:h + 1, :] * jnp.maximum(_dot_nt(kx, qih_ref[h]), 0.0)
        bits = lax.bitcast_convert_type(acc, jnp.int32)
        key = bits ^ ((bits >> 31) & 0x7FFFFFFF)
        key = jnp.where(s0 + row <= pos_q, key, INT_MIN)
        hi_ref[pl.ds(s0, kc), :] = (key >> 16).astype(jnp.int16)
        lo_ref[pl.ds(s0, kc), :] = ((key & 0xFFFF) - I16_BIAS).astype(jnp.int16)
        return carry

    lax.fori_loop(0, nchunk, score_chunk, 0)

    one16, zero16 = jnp.ones((), jnp.int16), jnp.zeros((), jnp.int16)
    row16 = lax.broadcasted_iota(jnp.int32, (kc, tq), 0).astype(jnp.int16)
    rowc16 = lax.broadcasted_iota(jnp.int32, (CNT_ROWS, tq), 0).astype(jnp.int16)
    n_keys = k_ref.shape[0]

    def to16(v):
        return jnp.clip(v, -I16_BIAS, I16_BIAS - 1).astype(jnp.int16)

    def count(hit):
        def body(c, cnt):
            base = pl.multiple_of(c * kc, kc)
            for g in range(kc // CNT_ROWS):
                cnt = cnt + jnp.where(hit(base + g * CNT_ROWS), one16, zero16)
            return cnt

        cnt = lax.fori_loop(0, nchunk, body, jnp.zeros((CNT_ROWS, tq), jnp.int16))
        return jnp.sum(cnt.astype(jnp.int32).astype(F32), axis=0, keepdims=True)

    def count_ge(ref, cand):
        c16 = cand.astype(jnp.int16)
        return count(lambda r: ref[pl.ds(r, CNT_ROWS), :] >= c16)

    def count_gt(ref, cand):
        c16 = cand.astype(jnp.int16)
        return count(lambda r: ref[pl.ds(r, CNT_ROWS), :] > c16)

    def kth_largest(ref, need):
        t = jnp.where(count_ge(ref, jnp.zeros((1, tq), jnp.int32)) >= need, 0, -I16_BIAS).astype(jnp.int32)

        def bit_body(b, t):
            cand = t + jnp.left_shift(jnp.int32(1), 14 - b)
            return jnp.where(count_ge(ref, cand) >= need, cand, t)

        return lax.fori_loop(0, 15, bit_body, t)

    kf = jnp.full((1, tq), float(topk), F32)
    thi = kth_largest(hi_ref, kf)
    need_lo = kf - count_gt(hi_ref, thi)
    thi16 = thi.astype(jnp.int16)

    def mask_low(c, carry):
        rows = pl.ds(pl.multiple_of(c * kc, kc), kc)
        lo_ref[rows, :] = jnp.where(hi_ref[rows, :] == thi16, lo_ref[rows, :], jnp.int16(-I16_BIAS))
        return carry

    lax.fori_loop(0, nchunk, mask_low, 0)
    tlo = kth_largest(lo_ref, need_lo)
    tlo16 = tlo.astype(jnp.int16)

    def is_eq(r):
        return (hi_ref[pl.ds(r, CNT_ROWS), :] == thi16) & (lo_ref[pl.ds(r, CNT_ROWS), :] == tlo16)

    keep_eq = need_lo - count_gt(lo_ref, tlo)
    excess = (count(is_eq) > keep_eq) & (thi > -I16_BIAS)
    cut_ref[...] = jnp.full(cut_ref.shape, n_keys, jnp.int32)

    @pl.when(jnp.max(jnp.broadcast_to(jnp.where(excess, 1.0, 0.0), (8, tq))) > 0.0)
    def _():
        nbits = n_keys.bit_length()

        def bit_body(b, j):
            cand = j + jnp.left_shift(jnp.int32(1), nbits - 1 - b)
            before = count(lambda r: is_eq(r) & (rowc16 < to16(cand - r)))
            return jnp.where(before < keep_eq, cand, j)

        j = lax.fori_loop(0, nbits, bit_body, jnp.zeros((1, tq), jnp.int32))
        cut_ref[...] = jnp.where(excess, j, n_keys)

    cut = cut_ref[...]
    thi_sel16 = jnp.maximum(thi, 1 - I16_BIAS).astype(jnp.int16)

    m_ref[...] = jnp.full(m_ref.shape, NEG, F32)
    l_ref[...] = jnp.zeros(l_ref.shape, F32)
    acc_ref[...] = jnp.zeros(acc_ref.shape, F32)

    def logits(c):
        s0 = pl.multiple_of(jnp.minimum(c, nchunk - 1) * kc, kc)
        off = pl.multiple_of(c0 - jnp.clip(t0 - c * kc, -kc, c0), LANE)
        hi = hi_ref[pl.ds(s0, kc), :]
        lo = lo_ref[pl.ds(s0, kc), :]
        sel = (hi > thi_sel16) | ((hi == thi_sel16)
                                  & ((lo > tlo16) | ((lo == tlo16) & (row16 <= to16(cut - s0)))))
        drop = jnp.where(sel, jnp.zeros((), jnp.int16), jnp.ones((), jnp.int16))
        mask_bias = drop.astype(jnp.int32).astype(F32) * NEG
        parts = []
        for h in range(A_HEADS):
            hs = slice(h * HEAD_DIM, (h + 1) * HEAD_DIM)
            parts.append(_dot_nt(k_ref[pl.ds(s0, kc), hs], q_ref[:, hs]) + band_ref[h, pl.ds(off, kc), :]
                         + mask_bias)
        return jnp.concatenate(parts, axis=1)

    def consume(c, s):
        s0 = pl.multiple_of(jnp.minimum(c, nchunk - 1) * kc, kc)
        m_prev = m_ref[...]
        m_new = jnp.maximum(m_prev, jnp.max(s, axis=0, keepdims=True))
        alpha = jnp.exp2(m_prev - m_new)
        p = jnp.exp2(s - m_new)
        l_ref[...] = alpha * l_ref[...] + jnp.sum(p, axis=0, keepdims=True)
        p = p.astype(BF16)
        pv = [_dot(vt_ref[h * HEAD_DIM:(h + 1) * HEAD_DIM, pl.ds(s0, kc)], p[:, h * tq:(h + 1) * tq])
              for h in range(A_HEADS)]
        acc_ref[...] = alpha * acc_ref[...] + jnp.concatenate(pv, axis=1)
        m_ref[...] = m_new

    s_ref[0] = logits(0)

    def chunk_pair(cc, carry):
        c = 2 * cc
        s_ref[1] = logits(c + 1)
        consume(c, s_ref[0])
        s_ref[0] = logits(c + 2)
        consume(c + 1, s_ref[1])
        return carry

    lax.fori_loop(0, (nchunk + 1) // 2, chunk_pair, 0)
    o = acc_ref[...] / l_ref[...]
    for h in range(A_HEADS):
        o_ref[:, h * HEAD_DIM:(h + 1) * HEAD_DIM] = o[:, h * tq:(h + 1) * tq].T.astype(o_ref.dtype)


def dsa_attention(qk, vt, pidx, wt, band, tq=256, kc=512):
    S = qk.shape[0]
    topk = min(TOPK_MAX, S // 4)
    aw = A_HEADS * HEAD_DIM
    iw = IDX_HEADS * IDX_DIM
    once = pl.Buffered(1)
    return pl.pallas_call(
        functools.partial(_dsa_kernel, tq=tq, kc=kc, topk=topk, c0=_band_c0(kc)),
        grid=(S // tq,),
        in_specs=[pl.BlockSpec((tq, aw), lambda i: (i, 0)),
                  pl.BlockSpec((tq, iw), lambda i: (i, 0)),
                  pl.BlockSpec((IDX_HEADS, tq), lambda i: (0, i)),
                  pl.BlockSpec((S, aw), lambda i: (0, 1), pipeline_mode=once),
                  pl.BlockSpec((aw, S), lambda i: (0, 0), pipeline_mode=once),
                  pl.BlockSpec((S, LANE), lambda i: (0, iw // LANE), pipeline_mode=once),
                  pl.BlockSpec(band.shape, lambda i: (0, 0, 0), pipeline_mode=once)],
        out_specs=pl.BlockSpec((tq, aw), lambda i: (i, 0)),
        out_shape=jax.ShapeDtypeStruct((S, aw), BF16),
        scratch_shapes=[pltpu.VMEM((S, tq), jnp.int16),
                        pltpu.VMEM((S, tq), jnp.int16),
                        pltpu.VMEM((1, tq), jnp.int32),
                        pltpu.VMEM((IDX_HEADS, tq, LANE), BF16),
                        pltpu.VMEM((2, kc, A_HEADS * tq), F32),
                        pltpu.VMEM((HEAD_DIM, A_HEADS * tq), F32),
                        pltpu.VMEM((1, A_HEADS * tq), F32),
                        pltpu.VMEM((1, A_HEADS * tq), F32)],
        compiler_params=_cparams(("arbitrary",)),
        name="dsa_attention",
    )(qk, pidx, wt, qk, vt, pidx, band)


DIL_ROWS = 384


def _dilated_kernel(q_ref, k_ref, vt_ref, b0_ref, b1_ref, b2_ref, o_ref, s_ref, *, tq):
    t0 = pl.program_id(1) * tq
    q = q_ref[...]
    tiles = []
    row0 = 0
    for band_ref, (_, dil) in zip((b0_ref, b1_ref, b2_ref), DILATED_PAIRS):
        span = DIL_STEPS * dil
        start = jnp.maximum(t0 - span, 0)
        off = span - (t0 - start)
        for j in range((span + tq) // DIL_ROWS):
            ks = pl.multiple_of(start + j * DIL_ROWS, LANE)
            bs = pl.multiple_of(off + j * DIL_ROWS, LANE)
            s_ref[row0:row0 + DIL_ROWS, :] = (_dot_nt(k_ref[pl.ds(ks, DIL_ROWS), :], q)
                                              + band_ref[0, pl.ds(bs, DIL_ROWS), :])
            tiles.append((ks, row0))
            row0 += DIL_ROWS
    m = jnp.full((1, tq), NEG, F32)
    l = jnp.zeros((1, tq), F32)
    acc = jnp.zeros((HEAD_DIM, tq), F32)
    for ks, r0 in tiles:
        s = s_ref[r0:r0 + DIL_ROWS, :]
        m_new = jnp.maximum(m, jnp.max(s, axis=0, keepdims=True))
        alpha = jnp.exp2(m - m_new)
        p = jnp.exp2(s - m_new)
        l = alpha * l + jnp.sum(p, axis=0, keepdims=True)
        acc = alpha * acc + _dot(vt_ref[:, pl.ds(ks, DIL_ROWS)], p.astype(BF16))
        m = m_new
    o_ref[...] = (acc / l).T.astype(o_ref.dtype)


def dilated_attention(qk, vt, bands, vt_row0, tq=256):
    S = qk.shape[0]
    vh0 = vt_row0 // HEAD_DIM
    lengths = [DIL_STEPS * dil + tq for _, dil in DILATED_PAIRS]
    assert all(n % DIL_ROWS == 0 and n <= S for n in lengths)
    return pl.pallas_call(
        functools.partial(_dilated_kernel, tq=tq),
        grid=(B_HEADS, S // tq),
        in_specs=[pl.BlockSpec((tq, LANE), lambda h, i: (i, h)),
                  pl.BlockSpec((S, LANE), lambda h, i: (0, B_HEADS + h)),
                  pl.BlockSpec((HEAD_DIM, S), lambda h, i: (vh0 + h, 0))]
                 + [pl.BlockSpec((1,) + b.shape[1:], lambda h, i: (h, 0, 0)) for b in bands],
        out_specs=pl.BlockSpec((tq, LANE), lambda h, i: (i, h)),
        out_shape=jax.ShapeDtypeStruct((S, B_HEADS * HEAD_DIM), BF16),
        scratch_shapes=[pltpu.VMEM((sum(lengths), tq), F32)],
        compiler_params=_cparams(("parallel", "arbitrary")),
        name="dilated_attention",
    )(qk, qk, vt, *bands)


def _diff_kernel(lam_ref, g_ref, q_ref, k_ref, vt_ref, band_ref, o_ref, acc_ref, m_ref, l_ref, s_ref,
                 *, tq, tk, c0, lam_init):
    i = pl.program_id(1)
    t0 = i * tq
    nchunk = (t0 + tq + tk - 1) // tk
    last_chunk = k_ref.shape[0] // tk - 1
    q = q_ref[...]
    lane = lax.broadcasted_iota(jnp.int32, q.shape, 1)
    zero = jnp.zeros_like(q)
    q2 = jnp.concatenate([jnp.where(lane < C_HALF, q, zero), jnp.where(lane >= C_HALF, q, zero)], axis=0)

    m_ref[...] = jnp.full(m_ref.shape, NEG, F32)
    l_ref[...] = jnp.zeros(l_ref.shape, F32)
    acc_ref[...] = jnp.zeros(acc_ref.shape, F32)

    def logits(c):
        s0 = pl.multiple_of(jnp.minimum(c, last_chunk) * tk, tk)
        off = pl.multiple_of(c0 - jnp.clip(t0 - c * tk, -tk, c0), LANE)
        bias = band_ref[0, pl.ds(off, tk), :]
        return _dot_nt(k_ref[pl.ds(s0, tk), :], q2) + jnp.concatenate([bias, bias], axis=1)

    def consume(c, s):
        s0 = pl.multiple_of(jnp.minimum(c, last_chunk) * tk, tk)
        m_prev = m_ref[...]
        m_new = jnp.maximum(m_prev, jnp.max(s, axis=0, keepdims=True))
        alpha = jnp.exp2(m_prev - m_new)
        p = jnp.exp2(s - m_new)
        l_ref[...] = alpha * l_ref[...] + jnp.sum(p, axis=0, keepdims=True)
        acc_ref[...] = alpha * acc_ref[...] + _dot(vt_ref[:, pl.ds(s0, tk)], p.astype(BF16))
        m_ref[...] = m_new

    s_ref[0] = logits(0)

    def chunk_pair(cc, carry):
        c = 2 * cc
        s_ref[1] = logits(c + 1)
        consume(c, s_ref[0])
        s_ref[0] = logits(c + 2)
        consume(c + 1, s_ref[1])
        return carry

    lax.fori_loop(0, (nchunk + 1) // 2, chunk_pair, 0)

    lp = lam_ref[...]
    lam = (jnp.exp(jnp.sum(lp[0:1] * lp[1:2], axis=1, keepdims=True))
           - jnp.exp(jnp.sum(lp[2:3] * lp[3:4], axis=1, keepdims=True)) + lam_init)
    o = acc_ref[...] / l_ref[...]
    o = o[:, :tq] - lam * o[:, tq:]
    ms = jnp.mean(o * o, axis=0, keepdims=True)
    o = o * lax.rsqrt(ms + RMS_EPS) * (g_ref[...] * (1.0 - lam_init))
    o_ref[...] = o.T.astype(o_ref.dtype)


def diff_attention(qk, vt, band, lam_params, subln_g, lam_init, vt_row0, tq=256, tk=512):
    S = qk.shape[0]
    vh0 = vt_row0 // HEAD_DIM
    return pl.pallas_call(
        functools.partial(_diff_kernel, tq=tq, tk=tk, c0=_band_c0(tk), lam_init=lam_init),
        grid=(C_HEADS, S // tq),
        in_specs=[pl.BlockSpec((4, C_HALF), lambda h, i: (0, 0)),
                  pl.BlockSpec((HEAD_DIM, 1), lambda h, i: (0, 0)),
                  pl.BlockSpec((tq, LANE), lambda h, i: (i, h)),
                  pl.BlockSpec((S, LANE), lambda h, i: (0, C_HEADS + h)),
                  pl.BlockSpec((HEAD_DIM, S), lambda h, i: (vh0 + h, 0)),
                  pl.BlockSpec((1,) + band.shape[1:], lambda h, i: (h, 0, 0))],
        out_specs=pl.BlockSpec((tq, LANE), lambda h, i: (i, h)),
        out_shape=jax.ShapeDtypeStruct((S, C_HEADS * HEAD_DIM), BF16),
        scratch_shapes=[pltpu.VMEM((HEAD_DIM, 2 * tq), F32),
                        pltpu.VMEM((1, 2 * tq), F32),
                        pltpu.VMEM((1, 2 * tq), F32),
                        pltpu.VMEM((2, tk, 2 * tq), F32)],
        compiler_params=_cparams(("parallel", "arbitrary")),
        name="diff_attention",
    )(lam_params, subln_g.reshape(HEAD_DIM, 1), qk, qk, vt, band)


def _outproj_kernel(x_ref, a_ref, b_ref, c_ref, wa_ref, wb_ref, wc_ref, o_ref):
    o_ref[...] = (x_ref[...] + _dot(a_ref[...], wa_ref[...]) + _dot(b_ref[...], wb_ref[...])
                  + _dot(c_ref[...], wc_ref[...]))


def out_projection(x, oa, ob, oc, wa, wb, wc, tm=512, tn=1024):
    S, D = x.shape
    tn = min(tn, D)
    return pl.pallas_call(
        _outproj_kernel,
        grid=(S // tm, D // tn),
        in_specs=[pl.BlockSpec((tm, tn), lambda i, j: (i, j)),
                  pl.BlockSpec((tm, oa.shape[1]), lambda i, j: (i, 0)),
                  pl.BlockSpec((tm, ob.shape[1]), lambda i, j: (i, 0)),
                  pl.BlockSpec((tm, oc.shape[1]), lambda i, j: (i, 0)),
                  pl.BlockSpec((wa.shape[0], tn), lambda i, j: (0, j)),
                  pl.BlockSpec((wb.shape[0], tn), lambda i, j: (0, j)),
                  pl.BlockSpec((wc.shape[0], tn), lambda i, j: (0, j))],
        out_specs=pl.BlockSpec((tm, tn), lambda i, j: (i, j)),
        out_shape=jax.ShapeDtypeStruct((S, D), F32),
        compiler_params=_cparams(("parallel", "arbitrary")),
        name="out_projection",
    )(x, oa, ob, oc, wa, wb, wc)


def _silu(x):
    return x / (1.0 + jnp.exp(-x))


def _ffn_kernel(x_ref, h_ref, wg_ref, wu_ref, wd_ref, o_ref):
    f = pl.program_id(1)

    @pl.when(f == 0)
    def _():
        o_ref[...] = x_ref[...]

    h = h_ref[...]
    a = _silu(_dot(h, wg_ref[...].astype(BF16))) * _dot(h, wu_ref[...].astype(BF16))
    o_ref[...] += _dot(a.astype(BF16), wd_ref[...].astype(BF16))


def ffn_dense(x, h, wg, wu, wd, tm=1024, tf=256):
    S, D = x.shape
    F = wg.shape[1]
    once = pl.Buffered(1)
    return pl.pallas_call(
        _ffn_kernel,
        grid=(S // tm, F // tf),
        in_specs=[pl.BlockSpec((tm, D), lambda i, f: (i, 0), pipeline_mode=once),
                  pl.BlockSpec((tm, D), lambda i, f: (i, 0), pipeline_mode=once),
                  pl.BlockSpec((D, tf), lambda i, f: (0, f)),
                  pl.BlockSpec((D, tf), lambda i, f: (0, f)),
                  pl.BlockSpec((tf, D), lambda i, f: (f, 0))],
        out_specs=pl.BlockSpec((tm, D), lambda i, f: (i, 0)),
        out_shape=jax.ShapeDtypeStruct((S, D), F32),
        compiler_params=_cparams(("parallel", "arbitrary")),
        name="ffn_dense",
    )(x, h, wg, wu, wd)


def _router_kernel(x_ref, g_ref, w_ref, c_ref, sel_ref):
    x = x_ref[...]
    ms = jnp.mean(x * x, axis=-1, keepdims=True)
    h = x * lax.rsqrt(ms + RMS_EPS) * g_ref[...]
    logits = jnp.dot(h, w_ref[...], preferred_element_type=F32, precision=lax.Precision.HIGHEST)
    lane = lax.broadcasted_iota(jnp.int32, logits.shape, 1).astype(F32)
    logits = jnp.where(lane < N_EXPERTS, logits, NEG)
    m1 = jnp.max(logits, axis=1, keepdims=True)
    i1 = jnp.min(jnp.where(logits == m1, lane, float(LANE)), axis=1, keepdims=True)
    rest = jnp.where(lane == i1, NEG, logits)
    m2 = jnp.max(rest, axis=1, keepdims=True)
    i2 = jnp.min(jnp.where(rest == m2, lane, float(LANE)), axis=1, keepdims=True)
    e2 = jnp.exp(m2 - m1)
    g1 = 1.0 / (1.0 + e2)
    g2 = e2 / (1.0 + e2)
    pick1, pick2 = lane == i1, lane == i2
    c_ref[...] = jnp.where(pick1, g1, 0.0) + jnp.where(pick2, g2, 0.0)
    sel_ref[...] = jnp.where(pick1 | pick2, 1.0, 0.0)


def moe_router(x, g, w_router, tm=512):
    S, D = x.shape
    wpad = jnp.zeros((D, LANE), F32).at[:, :N_EXPERTS].set(w_router)
    spec = pl.BlockSpec((tm, LANE), lambda i: (i, 0))
    sds = jax.ShapeDtypeStruct((S, LANE), F32)
    return pl.pallas_call(
        _router_kernel,
        grid=(S // tm,),
        in_specs=[pl.BlockSpec((tm, D), lambda i: (i, 0)),
                  pl.BlockSpec((1, D), lambda i: (0, 0)),
                  pl.BlockSpec((D, LANE), lambda i: (0, 0))],
        out_specs=[spec, spec],
        out_shape=[sds, sds],
        compiler_params=_cparams(("parallel",)),
        name="moe_router",
    )(x, g.reshape(1, D), wpad)


MOE_CH = 256
MOE_TILE = 1024


def _moe_rank_kernel(sel_ref, rm_ref, rmt_ref, cum_ref, tot_ref, carry_ref):
    c = pl.program_id(0)

    @pl.when(c == 0)
    def _():
        carry_ref[...] = jnp.zeros(carry_ref.shape, F32)

    sel = sel_ref[...]
    ch = sel.shape[0]
    before = (lax.broadcasted_iota(jnp.int32, (ch, ch), 1) < lax.broadcasted_iota(jnp.int32, (ch, ch), 0))
    rank = _dot(jnp.where(before, 1.0, 0.0).astype(BF16), sel.astype(BF16)) + carry_ref[...]
    rm = jnp.where(sel > 0.0, rank, -1.0)
    rm_ref[...] = rm
    rmt_ref[...] = rm.T[:N_EXPERTS]
    cum_ref[0] = carry_ref[...]
    carry_ref[...] += jnp.sum(sel, axis=0, keepdims=True)
    tot_ref[...] = carry_ref[...]


def moe_rank(sel):
    S = sel.shape[0]
    nch = S // MOE_CH
    return pl.pallas_call(
        _moe_rank_kernel,
        grid=(nch,),
        in_specs=[pl.BlockSpec((MOE_CH, LANE), lambda c: (c, 0))],
        out_specs=[pl.BlockSpec((MOE_CH, LANE), lambda c: (c, 0)),
                   pl.BlockSpec((N_EXPERTS, MOE_CH), lambda c: (0, c)),
                   pl.BlockSpec((1, 1, LANE), lambda c: (c, 0, 0)),
                   pl.BlockSpec((1, LANE), lambda c: (0, 0))],
        out_shape=[jax.ShapeDtypeStruct((S, LANE), F32),
                   jax.ShapeDtypeStruct((N_EXPERTS, S), F32),
                   jax.ShapeDtypeStruct((nch, 1, LANE), F32),
                   jax.ShapeDtypeStruct((1, LANE), F32)],
        scratch_shapes=[pltpu.VMEM((1, LANE), F32)],
        compiler_params=_cparams(("arbitrary",)),
        name="moe_rank",
    )(sel)


def _moe_expert_kernel(te_ref, tr0_ref, tclo_ref, tchi_ref, tval_ref, cumx_ref,
                       rk_ref, h_hbm, wg_ref, wu_ref, wd_ref, y_ref,
                       hs_ref, acc_ref, hbuf_ref, sem_ref, *, n_experts):
    j = pl.program_id(0)
    f = pl.program_id(1)
    nf = pl.num_programs(1)
    nrows = tval_ref[j]
    tile = hs_ref.shape[0]
    win = MOE_CH + 8

    def chunk_copy(c, slot):
        return pltpu.make_async_copy(h_hbm.at[pl.ds(c * MOE_CH, MOE_CH)], hbuf_ref.at[slot], sem_ref.at[slot])

    @pl.when((nrows > 0) & (f == 0))
    def _():
        clo, chi = tclo_ref[j], tchi_ref[j]
        tr0 = tr0_ref[j]
        wrow = lax.broadcasted_iota(jnp.int32, (win, MOE_CH), 0)
        acc_ref[...] = jnp.zeros(acc_ref.shape, F32)

        @pl.when(chi > clo)
        def _():
            chunk_copy(clo, 0).start()

        def body(c, carry):
            slot = (c - clo) % 2

            @pl.when(c + 1 < chi)
            def _():
                chunk_copy(c + 1, 1 - slot).start()

            chunk_copy(c, slot).wait()
            first = cumx_ref[c * n_experts + te_ref[j]] - tr0
            w0 = pl.multiple_of(jnp.clip((first // 8) * 8, 0, tile - win), 8)
            rk = rk_ref[0, :, pl.ds(pl.multiple_of(c * MOE_CH, MOE_CH), MOE_CH)]
            onehot = jnp.where(rk == (tr0 + w0 + wrow).astype(F32), 1.0, 0.0).astype(BF16)
            acc_ref[pl.ds(w0, win), :] += _dot(onehot, hbuf_ref[slot])
            return carry

        lax.fori_loop(clo, chi, body, 0)
        hs_ref[...] = acc_ref[...].astype(BF16)
        acc_ref[...] = jnp.zeros(acc_ref.shape, F32)

    def swiglu(rows):
        hs = hs_ref[:rows]
        a = _silu(_dot(hs, wg_ref[0].astype(BF16))) * _dot(hs, wu_ref[0].astype(BF16))
        acc_ref[:rows] += _dot(a.astype(BF16), wd_ref[0].astype(BF16))

    sizes = (tile, tile // 2, tile // 4, tile // 8)
    for rows, below in zip(sizes, sizes[1:] + (0,)):
        @pl.when((nrows > below) & (nrows <= rows))
        def _():
            swiglu(rows)

    @pl.when(f == nf - 1)
    def _():
        y_ref[...] = jnp.where(nrows > 0, acc_ref[...], 0.0).astype(y_ref.dtype)


def _moe_combine_kernel(ca_ref, cb_ref, roff_ref, x_ref, rm_ref, cw_ref, ya_ref, yb_ref, o_ref):
    c = pl.program_id(0)
    e = pl.program_id(1)

    @pl.when(e == 0)
    def _():
        o_ref[...] = x_ref[...]

    rm = rm_ref[...]
    lane = lax.broadcasted_iota(jnp.int32, rm.shape, 1)
    mine = lane == e
    pos = jnp.sum(jnp.where(mine, rm, 0.0), axis=1, keepdims=True)
    gate = jnp.sum(jnp.where(mine, cw_ref[...], 0.0), axis=1, keepdims=True)
    row = jnp.where(pos >= 0.0, pos + roff_ref[e].astype(F32), -1.0)
    k = c * pl.num_programs(1) + e
    ca, cb = ca_ref[k], cb_ref[k]
    cr = ya_ref.shape[0]
    col = lax.broadcasted_iota(jnp.int32, (rm.shape[0], cr), 1)
    for y_ref, base in ((ya_ref, ca * cr), (yb_ref, jnp.where(cb != ca, cb * cr, -2 * cr))):
        onehot = jnp.where(row == (base + col).astype(F32), 1.0, 0.0).astype(BF16)
        o_ref[...] += gate * _dot(onehot, y_ref[...])


def ffn_moe(x, h, combine, sel, wg, wu, wd, tf=512):
    S, D = x.shape
    E, _, F = wg.shape
    tf = min(tf, F)
    T = MOE_TILE
    nch = S // MOE_CH
    J = 2 * S // T + E
    rm, rmt, cum, tot = moe_rank(sel)

    cnt = tot[0, :E].astype(jnp.int32)
    ntile = (cnt + T - 1) // T
    tend = jnp.cumsum(ntile)
    tstart = tend - ntile
    jj = jnp.arange(J, dtype=jnp.int32)
    te = jnp.minimum(jnp.searchsorted(tend, jnp.minimum(jj, tend[-1] - 1), side="right"), E - 1).astype(jnp.int32)
    tr0 = (jnp.minimum(jj, tend[-1] - 1) - tstart[te]) * T
    tval = jnp.where(jj < tend[-1], jnp.clip(cnt[te] - tr0, 0, T), 0).astype(jnp.int32)
    cumx = cum[:, 0, :E].astype(jnp.int32)
    cumi = jnp.concatenate([cumx[1:], cnt[None]], axis=0)
    tclo = jnp.sum(cumi[:, te] <= tr0[None, :], axis=0).astype(jnp.int32)
    tchi = jnp.sum(cumx[:, te] < (tr0 + tval)[None, :], axis=0).astype(jnp.int32)
    roff = (tstart * T).astype(jnp.int32)
    nrow_chunks = J * T // MOE_CH
    ca = ((roff[None, :] + cumx) // MOE_CH).astype(jnp.int32)
    cb = jnp.minimum(ca + 1, nrow_chunks - 1)
    ca, cb = ca.reshape(-1), cb.reshape(-1)

    nf = F // tf

    def wmap(j, f, te_r, tr0_r, clo_r, chi_r, val_r, cumx_r):
        return (te_r[j], 0, jnp.where(val_r[j] > 0, f, nf - 1))

    def wdmap(j, f, te_r, tr0_r, clo_r, chi_r, val_r, cumx_r):
        return (te_r[j], jnp.where(val_r[j] > 0, f, nf - 1), 0)

    y = pl.pallas_call(
        functools.partial(_moe_expert_kernel, n_experts=E),
        grid_spec=pltpu.PrefetchScalarGridSpec(
            num_scalar_prefetch=6,
            grid=(J, nf),
            in_specs=[pl.BlockSpec((1, 1, S), lambda j, f, te_r, *_: (te_r[j], 0, 0)),
                      pl.BlockSpec(memory_space=pl.ANY),
                      pl.BlockSpec((1, D, tf), wmap),
                      pl.BlockSpec((1, D, tf), wmap),
                      pl.BlockSpec((1, tf, D), wdmap)],
            out_specs=pl.BlockSpec((T, D), lambda j, f, *_: (j, 0)),
            scratch_shapes=[pltpu.VMEM((T, D), BF16),
                            pltpu.VMEM((T, D), F32),
                            pltpu.VMEM((2, MOE_CH, D), BF16),
                            pltpu.SemaphoreType.DMA((2,))]),
        out_shape=jax.ShapeDtypeStruct((J * T, D), BF16),
        compiler_params=_cparams(("arbitrary", "arbitrary")),
        name="moe_experts",
    )(te, tr0, tclo, tchi, tval, cumx.reshape(-1), rmt.reshape(E, 1, S), h, wg, wu, wd)

    return pl.pallas_call(
        _moe_combine_kernel,
        grid_spec=pltpu.PrefetchScalarGridSpec(
            num_scalar_prefetch=3,
            grid=(nch, E),
            in_specs=[pl.BlockSpec((MOE_CH, D), lambda c, e, *_: (c, 0)),
                      pl.BlockSpec((MOE_CH, LANE), lambda c, e, *_: (c, 0)),
                      pl.BlockSpec((MOE_CH, LANE), lambda c, e, *_: (c, 0)),
                      pl.BlockSpec((MOE_CH, D), lambda c, e, ca_r, cb_r, ro_r: (ca_r[c * E + e], 0)),
                      pl.BlockSpec((MOE_CH, D), lambda c, e, ca_r, cb_r, ro_r: (cb_r[c * E + e], 0))],
            out_specs=pl.BlockSpec((MOE_CH, D), lambda c, e, *_: (c, 0))),
        out_shape=jax.ShapeDtypeStruct((S, D), F32),
        compiler_params=_cparams(("parallel", "arbitrary")),
        name="moe_combine",
    )(ca, cb, roff, x, rm, combine, y, y)


def _tile_gain(g, reps, scale=1.0):
    return jnp.tile(g.astype(F32) * scale, reps)


def kernel(x, w_in, w_out, attn_norm_g, ffn_norm_g, q_norm_a, k_norm_a, q_norm_b, k_norm_b,
           q_norm_c, k_norm_c, lambda_q1, lambda_k1, lambda_q2, lambda_k2, diff_subln_g, rel_bias,
           w_dense_gate, w_dense_up, w_dense_down, w_router, w_moe_gate, w_moe_up, w_moe_down):
    B, S, D = x.shape
    depth = w_in.shape[0]
    assert B == 1 and S % (DIL_STEPS * DILATED_PAIRS[-1][1]) == 0
    aw, bw, cw = A_HEADS * HEAD_DIM, B_HEADS * HEAD_DIM, C_HEADS * HEAD_DIM
    iw = IDX_HEADS * IDX_DIM
    sizes = (aw, aw, aw, iw, IDX_DIM, IDX_HEADS, bw, bw, bw, cw, cw, cw)
    offs = np.concatenate([[0], np.cumsum(sizes)]).tolist()

    def cols(w, *segs):
        return jnp.concatenate([w[:, offs[s]:offs[s + 1]] for s in segs], axis=1)

    tq, tk = 256, 512
    band_a = bias_band(rel_bias, A_HEADS, 0, _band_c0(tk) + 2 * tk, tq, _band_c0(tk))
    band_c = bias_band(rel_bias, C_HEADS, A_HEADS + B_HEADS, _band_c0(tk) + 2 * tk, tq, _band_c0(tk))
    bands_b = [bias_band(rel_bias, B_HEADS, A_HEADS, 2 * DIL_STEPS * dil + tq, tq, DIL_STEPS * dil, dil)
               for _, dil in DILATED_PAIRS]

    xs = x.reshape(S, D)
    for layer in range(depth):
        wl = w_in[layer]
        w_pa = cols(wl, 0, 1).astype(BF16)
        w_pb = cols(wl, 6, 7).astype(BF16)
        w_pc = cols(wl, 9, 10).astype(BF16)
        w_pi = cols(wl, 3, 4, 4).astype(BF16)
        w_vt = cols(wl, 2, 8, 11).T.astype(BF16)
        w_wt = cols(wl, 5).T.astype(BF16)
        qs_ab, qs_c = HEAD_DIM ** -0.5 * LOG2E, C_HALF ** -0.5 * LOG2E
        g_pa = jnp.concatenate([_tile_gain(q_norm_a[layer], A_HEADS, qs_ab), _tile_gain(k_norm_a[layer], A_HEADS)])
        g_pb = jnp.concatenate([_tile_gain(q_norm_b[layer], B_HEADS, qs_ab), _tile_gain(k_norm_b[layer], B_HEADS)])
        g_pc = jnp.concatenate([_tile_gain(q_norm_c[layer], 2 * C_HEADS, qs_c),
                                _tile_gain(k_norm_c[layer], 2 * C_HEADS)])

        h = rmsnorm(xs, attn_norm_g[layer])
        p_a = project(h, w_pa, g_pa, HEAD_DIM, BF16, tn=512)
        p_b = project(h, w_pb, g_pb, HEAD_DIM, BF16, tn=512)
        p_c = project(h, w_pc, g_pc, C_HALF, BF16, tn=512)
        p_i = project(h, w_pi, jnp.ones((w_pi.shape[1],), F32), 0, BF16, tn=384)
        vt = project_t(h, w_vt, BF16)
        wt = project_t(h, w_wt, F32)

        oa = dsa_attention(p_a, vt, p_i, wt, band_a)

        ob = dilated_attention(p_b, vt, bands_b, aw, tq=tq)

        lam_init = 0.8 - 0.6 * math.exp(-0.3 * layer)
        lam_params = jnp.stack([lambda_q1[layer], lambda_k1[layer], lambda_q2[layer], lambda_k2[layer]]).astype(F32)
        oc = diff_attention(p_c, vt, band_c, lam_params, diff_subln_g[layer].astype(F32), lam_init, aw + bw,
                            tq=tq, tk=tk)

        wo = w_out[layer].astype(BF16)
        xs = out_projection(xs, oa, ob, oc, wo[:aw], wo[aw:aw + bw], wo[aw + bw:])

        f = layer // 2
        if layer % 2 == 0:
            h2 = rmsnorm(xs, ffn_norm_g[layer])
            xs = ffn_dense(xs, h2, w_dense_gate[f], w_dense_up[f], w_dense_down[f])
        else:
            h2 = rmsnorm(xs, ffn_norm_g[layer])
            combine, sel = moe_router(xs, ffn_norm_g[layer], w_router[f])
            xs = ffn_moe(xs, h2, combine, sel, w_moe_gate[f], w_moe_up[f], w_moe_down[f])
    return xs.reshape(B, S, D)
```

```python
import functools
import math

import jax
import jax.numpy as jnp
import numpy as np
from jax import lax
from jax.experimental import pallas as pl
from jax.experimental.pallas import tpu as pltpu

F32 = jnp.float32
BF16 = jnp.bfloat16

HEAD_DIM = 128
A_HEADS, B_HEADS, C_HEADS = 4, 6, 6
C_HALF = HEAD_DIM // 2
IDX_HEADS, IDX_DIM = 16, 64
TOPK_MAX = 256
DILATED_PAIRS = ((128, 1), (512, 4), (2048, 16))
DIL_STEPS = 128
N_BUCKETS = 32
MAX_DISTANCE = 2048
N_EXPERTS = 8
RMS_EPS = 1e-6
NEG = -1e30
LOG2E = math.log2(math.e)
INT_MIN = -(2 ** 31)
I16_BIAS = 2 ** 15
CNT_ROWS = 64
LANE = 128
VMEM_LIMIT = 56 * 1024 * 1024

BUCKET_EDGES = tuple(range(16)) + (16,) + tuple(
    int(math.ceil(16.0 * (MAX_DISTANCE / 16.0) ** (k / 16.0))) for k in range(1, 16))
FAR_DIST = BUCKET_EDGES[-1]


def _cparams(sem):
    return pltpu.CompilerParams(dimension_semantics=sem, vmem_limit_bytes=VMEM_LIMIT)


def _dot(a, b):
    return jnp.dot(a, b, preferred_element_type=F32)


def _dot_nt(a, b):
    return lax.dot_general(a, b, (((1,), (1,)), ((), ())), preferred_element_type=F32)


def _rmsnorm_kernel(x_ref, g_ref, o_ref):
    x = x_ref[...]
    ms = jnp.mean(x * x, axis=-1, keepdims=True)
    o_ref[...] = (x * lax.rsqrt(ms + RMS_EPS) * g_ref[...]).astype(o_ref.dtype)


def rmsnorm(x, g, tm=512):
    S, D = x.shape
    return pl.pallas_call(
        _rmsnorm_kernel,
        grid=(S // tm,),
        in_specs=[pl.BlockSpec((tm, D), lambda i: (i, 0)), pl.BlockSpec((1, D), lambda i: (0, 0))],
        out_specs=pl.BlockSpec((tm, D), lambda i: (i, 0)),
        out_shape=jax.ShapeDtypeStruct((S, D), BF16),
        compiler_params=_cparams(("parallel",)),
        name="rmsnorm",
    )(x, g.reshape(1, D))


def _proj_kernel(h_ref, w_ref, g_ref, o_ref, *, group):
    acc = _dot(h_ref[...], w_ref[...])
    tn = acc.shape[1]
    if group == 0:
        o_ref[...] = acc.astype(o_ref.dtype)
        return
    for c in range(tn // LANE):
        a = acc[:, c * LANE:(c + 1) * LANE]
        sq = a * a
        if group == LANE:
            ms = jnp.mean(sq, axis=-1, keepdims=True)
        else:
            lane = lax.broadcasted_iota(jnp.int32, sq.shape, 1)
            lo = jnp.sum(jnp.where(lane < group, sq, 0.0), axis=-1, keepdims=True)
            hi = jnp.sum(jnp.where(lane >= group, sq, 0.0), axis=-1, keepdims=True)
            ms = jnp.where(lane < group, lo, hi) * (1.0 / group)
        y = a * lax.rsqrt(ms + RMS_EPS) * g_ref[:, c * LANE:(c + 1) * LANE]
        o_ref[:, c * LANE:(c + 1) * LANE] = y.astype(o_ref.dtype)


def project(h, w, gains, group, out_dtype, tm=1024, tn=256):
    S, K = h.shape
    N = w.shape[1]
    return pl.pallas_call(
        functools.partial(_proj_kernel, group=group),
        grid=(S // tm, N // tn),
        in_specs=[pl.BlockSpec((tm, K), lambda i, j: (i, 0)),
                  pl.BlockSpec((K, tn), lambda i, j: (0, j)),
                  pl.BlockSpec((1, tn), lambda i, j: (0, j))],
        out_specs=pl.BlockSpec((tm, tn), lambda i, j: (i, j)),
        out_shape=jax.ShapeDtypeStruct((S, N), out_dtype),
        compiler_params=_cparams(("parallel", "arbitrary")),
        name=f"proj_g{group}",
    )(h, w, gains.reshape(1, N))


def _proj_t_kernel(wt_ref, h_ref, o_ref):
    o_ref[...] = _dot_nt(wt_ref[...], h_ref[...]).astype(o_ref.dtype)


def project_t(h, wt, out_dtype, tm=1024, tn=256):
    S, K = h.shape
    N = wt.shape[0]
    tn = min(tn, N)
    return pl.pallas_call(
        _proj_t_kernel,
        grid=(S // tm, N // tn),
        in_specs=[pl.BlockSpec((tn, K), lambda i, j: (j, 0)),
                  pl.BlockSpec((tm, K), lambda i, j: (i, 0))],
        out_specs=pl.BlockSpec((tn, tm), lambda i, j: (j, i)),
        out_shape=jax.ShapeDtypeStruct((N, S), out_dtype),
        compiler_params=_cparams(("parallel", "arbitrary")),
        name="proj_t",
    )(wt, h)


BAND_ROWS = 256


def _band_kernel(tab_ref, o_ref, *, c0, head_off, dil):
    hh = head_off + pl.program_id(0)
    _, rows, width = o_ref.shape
    base = (lax.broadcasted_iota(jnp.int32, (BAND_ROWS, width), 1)
            - lax.broadcasted_iota(jnp.int32, (BAND_ROWS, width), 0))
    hi_valid = DIL_STEPS * dil if dil else None
    for r in range(rows // BAND_ROWS):
        d0 = c0 - r * BAND_ROWS
        lo, hi = d0 - (BAND_ROWS - 1), d0 + width - 1
        if hi_valid is not None:
            hi = min(hi, hi_valid)
        block = pl.ds(r * BAND_ROWS, BAND_ROWS)
        if hi < 0 or hi < lo:
            o_ref[0, block, :] = jnp.full((BAND_ROWS, width), NEG, F32)
            continue
        dist = base + d0
        first = max(b for b in range(N_BUCKETS) if BUCKET_EDGES[b] <= max(lo, 0))
        val = jnp.full((BAND_ROWS, width), tab_ref[first, hh] * LOG2E, F32)
        for b in range(first + 1, N_BUCKETS):
            if BUCKET_EDGES[b] <= hi:
                val = jnp.where(dist >= BUCKET_EDGES[b], tab_ref[b, hh] * LOG2E, val)
        ok = None
        if lo < 0:
            ok = dist >= 0
        if dil:
            in_win = (dist <= hi_valid) & ((dist & (dil - 1)) == 0)
            ok = in_win if ok is None else ok & in_win
        o_ref[0, block, :] = val if ok is None else jnp.where(ok, val, NEG)


def bias_band(rel_bias, n_heads, head_off, rows, width, c0, dil=0):
    assert rows % BAND_ROWS == 0 and dil & (dil - 1) == 0
    return pl.pallas_call(
        functools.partial(_band_kernel, c0=c0, head_off=head_off, dil=dil),
        grid=(n_heads,),
        in_specs=[pl.BlockSpec(memory_space=pltpu.SMEM)],
        out_specs=pl.BlockSpec((1, rows, width), lambda h: (h, 0, 0)),
        out_shape=jax.ShapeDtypeStruct((n_heads, rows, width), F32),
        compiler_params=_cparams(("parallel",)),
        name=f"bias_band_d{dil}",
    )(rel_bias)


def _band_c0(tk):
    return -(-(FAR_DIST + tk - 1) // LANE) * LANE


def _dsa_kernel(q_ref, qi_ref, wt_ref, k_ref, vt_ref, kidx_ref, band_ref, o_ref,
                hi_ref, lo_ref, cut_ref, qih_ref, s_ref, acc_ref, m_ref, l_ref, *, tq, kc, topk, c0):
    i = pl.program_id(0)
    t0 = i * tq
    nchunk = (t0 + tq + kc - 1) // kc
    half = lax.broadcasted_iota(jnp.int32, (tq, LANE), 1) < IDX_DIM

    for h in range(IDX_HEADS):
        slab = qi_ref[:, (h // 2) * LANE:(h // 2 + 1) * LANE]
        keep = half if h % 2 == 0 else jnp.logical_not(half)
        qih_ref[h] = jnp.where(keep, slab, jnp.zeros_like(slab))
    w = wt_ref[...] * (IDX_HEADS ** -0.5 * IDX_DIM ** -0.5)

    pos_q = t0 + lax.broadcasted_iota(jnp.int32, (kc, tq), 1)
    row = lax.broadcasted_iota(jnp.int32, (kc, tq), 0)

    def score_chunk(c, carry):
        s0 = pl.multiple_of(c * kc, kc)
        kx = kidx_ref[pl.ds(s0, kc), :]
        acc = jnp.zeros((kc, tq), F32)
        for h in range(IDX_HEADS):
            acc = acc + w[h:h + 1, :] * jnp.maximum(_dot_nt(kx, qih_ref[h]), 0.0)
        bits = lax.bitcast_convert_type(acc, jnp.int32)
        key = bits ^ ((bits >> 31) & 0x7FFFFFFF)
        key = jnp.where(s0 + row <= pos_q, key, INT_MIN)
        hi_ref[pl.ds(s0, kc), :] = (key >> 16).astype(jnp.int16)
        lo_ref[pl.ds(s0, kc), :] = ((key & 0xFFFF) - I16_BIAS).astype(jnp.int16)
        return carry

    lax.fori_loop(0, nchunk, score_chunk, 0)

    one16, zero16 = jnp.ones((), jnp.int16), jnp.zeros((), jnp.int16)
    row16 = lax.broadcasted_iota(jnp.int32, (kc, tq), 0).astype(jnp.int16)
    rowc16 = lax.broadcasted_iota(jnp.int32, (CNT_ROWS, tq), 0).astype(jnp.int16)
    n_keys = k_ref.shape[0]

    def to16(v):
        return jnp.clip(v, -I16_BIAS, I16_BIAS - 1).astype(jnp.int16)

    def count(hit):
        def body(c, cnt):
            base = pl.multiple_of(c * kc, kc)
            for g in range(kc // CNT_ROWS):
                cnt = cnt + jnp.where(hit(base + g * CNT_ROWS), one16, zero16)
            return cnt

        cnt = lax.fori_loop(0, nchunk, body, jnp.zeros((CNT_ROWS, tq), jnp.int16))
        return jnp.sum(cnt.astype(jnp.int32).astype(F32), axis=0, keepdims=True)

    def count_ge(ref, cand):
        c16 = cand.astype(jnp.int16)
        return count(lambda r: ref[pl.ds(r, CNT_ROWS), :] >= c16)

    def count_gt(ref, cand):
        c16 = cand.astype(jnp.int16)
        return count(lambda r: ref[pl.ds(r, CNT_ROWS), :] > c16)

    def kth_largest(ref, need):
        t = jnp.where(count_ge(ref, jnp.zeros((1, tq), jnp.int32)) >= need, 0, -I16_BIAS).astype(jnp.int32)

        def bit_body(b, t):
            cand = t + jnp.left_shift(jnp.int32(1), 14 - b)
            return jnp.where(count_ge(ref, cand) >= need, cand, t)

        return lax.fori_loop(0, 15, bit_body, t)

    kf = jnp.full((1, tq), float(topk), F32)
    thi = kth_largest(hi_ref, kf)
    need_lo = kf - count_gt(hi_ref, thi)
    thi16 = thi.astype(jnp.int16)

    def mask_low(c, carry):
        rows = pl.ds(pl.multiple_of(c * kc, kc), kc)
        lo_ref[rows, :] = jnp.where(hi_ref[rows, :] == thi16, lo_ref[rows, :], jnp.int16(-I16_BIAS))
        return carry

    lax.fori_loop(0, nchunk, mask_low, 0)
    tlo = kth_largest(lo_ref, need_lo)
    tlo16 = tlo.astype(jnp.int16)

    def is_eq(r):
        return (hi_ref[pl.ds(r, CNT_ROWS), :] == thi16) & (lo_ref[pl.ds(r, CNT_ROWS), :] == tlo16)

    keep_eq = need_lo - count_gt(lo_ref, tlo)
    excess = (count(is_eq) > keep_eq) & (thi > -I16_BIAS)
    cut_ref[...] = jnp.full(cut_ref.shape, n_keys, jnp.int32)

    @pl.when(jnp.max(jnp.broadcast_to(jnp.where(excess, 1.0, 0.0), (8, tq))) > 0.0)
    def _():
        nbits = n_keys.bit_length()

        def bit_body(b, j):
            cand = j + jnp.left_shift(jnp.int32(1), nbits - 1 - b)
            before = count(lambda r: is_eq(r) & (rowc16 < to16(cand - r)))
            return jnp.where(before < keep_eq, cand, j)

        j = lax.fori_loop(0, nbits, bit_body, jnp.zeros((1, tq), jnp.int32))
        cut_ref[...] = jnp.where(excess, j, n_keys)

    cut = cut_ref[...]
    thi_sel16 = jnp.maximum(thi, 1 - I16_BIAS).astype(jnp.int16)

    m_ref[...] = jnp.full(m_ref.shape, NEG, F32)
    l_ref[...] = jnp.zeros(l_ref.shape, F32)
    acc_ref[...] = jnp.zeros(acc_ref.shape, F32)

    def logits(c):
        s0 = pl.multiple_of(jnp.minimum(c, nchunk - 1) * kc, kc)
        off = pl.multiple_of(c0 - jnp.clip(t0 - c * kc, -kc, c0), LANE)
        hi = hi_ref[pl.ds(s0, kc), :]
        lo = lo_ref[pl.ds(s0, kc), :]
        sel = (hi > thi_sel16) | ((hi == thi_sel16)
                                  & ((lo > tlo16) | ((lo == tlo16) & (row16 <= to16(cut - s0)))))
        drop = jnp.where(sel, jnp.zeros((), jnp.int16), jnp.ones((), jnp.int16))
        mask_bias = drop.astype(jnp.int32).astype(F32) * NEG
        parts = []
        for h in range(A_HEADS):
            hs = slice(h * HEAD_DIM, (h + 1) * HEAD_DIM)
            parts.append(_dot_nt(k_ref[pl.ds(s0, kc), hs], q_ref[:, hs]) + band_ref[h, pl.ds(off, kc), :]
                         + mask_bias)
        return jnp.concatenate(parts, axis=1)

    def consume(c, s):
        s0 = pl.multiple_of(jnp.minimum(c, nchunk - 1) * kc, kc)
        m_prev = m_ref[...]
        m_new = jnp.maximum(m_prev, jnp.max(s, axis=0, keepdims=True))
        alpha = jnp.exp2(m_prev - m_new)
        p = jnp.exp2(s - m_new)
        l_ref[...] = alpha * l_ref[...] + jnp.sum(p, axis=0, keepdims=True)
        p = p.astype(BF16)
        pv = [_dot(vt_ref[h * HEAD_DIM:(h + 1) * HEAD_DIM, pl.ds(s0, kc)], p[:, h * tq:(h + 1) * tq])
              for h in range(A_HEADS)]
        acc_ref[...] = alpha * acc_ref[...] + jnp.concatenate(pv, axis=1)
        m_ref[...] = m_new

    s_ref[0] = logits(0)

    def chunk_pair(cc, carry):
        c = 2 * cc
        s_ref[1] = logits(c + 1)
        consume(c, s_ref[0])
        s_ref[0] = logits(c + 2)
        consume(c + 1, s_ref[1])
        return carry

    lax.fori_loop(0, (nchunk + 1) // 2, chunk_pair, 0)
    o = acc_ref[...] / l_ref[...]
    for h in range(A_HEADS):
        o_ref[:, h * HEAD_DIM:(h + 1) * HEAD_DIM] = o[:, h * tq:(h + 1) * tq].T.astype(o_ref.dtype)


def dsa_attention(qk, vt, pidx, wt, band, tq=256, kc=512):
    S = qk.shape[0]
    topk = min(TOPK_MAX, S // 4)
    aw = A_HEADS * HEAD_DIM
    iw = IDX_HEADS * IDX_DIM
    once = pl.Buffered(1)
    return pl.pallas_call(
        functools.partial(_dsa_kernel, tq=tq, kc=kc, topk=topk, c0=_band_c0(kc)),
        grid=(S // tq,),
        in_specs=[pl.BlockSpec((tq, aw), lambda i: (i, 0)),
                  pl.BlockSpec((tq, iw), lambda i: (i, 0)),
                  pl.BlockSpec((IDX_HEADS, tq), lambda i: (0, i)),
                  pl.BlockSpec((S, aw), lambda i: (0, 1), pipeline_mode=once),
                  pl.BlockSpec((aw, S), lambda i: (0, 0), pipeline_mode=once),
                  pl.BlockSpec((S, LANE), lambda i: (0, iw // LANE), pipeline_mode=once),
                  pl.BlockSpec(band.shape, lambda i: (0, 0, 0), pipeline_mode=once)],
        out_specs=pl.BlockSpec((tq, aw), lambda i: (i, 0)),
        out_shape=jax.ShapeDtypeStruct((S, aw), BF16),
        scratch_shapes=[pltpu.VMEM((S, tq), jnp.int16),
                        pltpu.VMEM((S, tq), jnp.int16),
                        pltpu.VMEM((1, tq), jnp.int32),
                        pltpu.VMEM((IDX_HEADS, tq, LANE), BF16),
                        pltpu.VMEM((2, kc, A_HEADS * tq), F32),
                        pltpu.VMEM((HEAD_DIM, A_HEADS * tq), F32),
                        pltpu.VMEM((1, A_HEADS * tq), F32),
                        pltpu.VMEM((1, A_HEADS * tq), F32)],
        compiler_params=_cparams(("arbitrary",)),
        name="dsa_attention",
    )(qk, pidx, wt, qk, vt, pidx, band)


DIL_ROWS = 384


def _dilated_kernel(q_ref, k_ref, vt_ref, b0_ref, b1_ref, b2_ref, o_ref, s_ref, *, tq):
    t0 = pl.program_id(1) * tq
    q = q_ref[...]
    tiles = []
    row0 = 0
    for band_ref, (_, dil) in zip((b0_ref, b1_ref, b2_ref), DILATED_PAIRS):
        span = DIL_STEPS * dil
        start = jnp.maximum(t0 - span, 0)
        off = span - (t0 - start)
        for j in range((span + tq) // DIL_ROWS):
            ks = pl.multiple_of(start + j * DIL_ROWS, LANE)
            bs = pl.multiple_of(off + j * DIL_ROWS, LANE)
            s_ref[row0:row0 + DIL_ROWS, :] = (_dot_nt(k_ref[pl.ds(ks, DIL_ROWS), :], q)
                                              + band_ref[0, pl.ds(bs, DIL_ROWS), :])
            tiles.append((ks, row0))
            row0 += DIL_ROWS
    m = jnp.full((1, tq), NEG, F32)
    l = jnp.zeros((1, tq), F32)
    acc = jnp.zeros((HEAD_DIM, tq), F32)
    for ks, r0 in tiles:
        s = s_ref[r0:r0 + DIL_ROWS, :]
        m_new = jnp.maximum(m, jnp.max(s, axis=0, keepdims=True))
        alpha = jnp.exp2(m - m_new)
        p = jnp.exp2(s - m_new)
        l = alpha * l + jnp.sum(p, axis=0, keepdims=True)
        acc = alpha * acc + _dot(vt_ref[:, pl.ds(ks, DIL_ROWS)], p.astype(BF16))
        m = m_new
    o_ref[...] = (acc / l).T.astype(o_ref.dtype)


def dilated_attention(qk, vt, bands, vt_row0, tq=256):
    S = qk.shape[0]
    vh0 = vt_row0 // HEAD_DIM
    lengths = [DIL_STEPS * dil + tq for _, dil in DILATED_PAIRS]
    assert all(n % DIL_ROWS == 0 and n <= S for n in lengths)
    return pl.pallas_call(
        functools.partial(_dilated_kernel, tq=tq),
        grid=(B_HEADS, S // tq),
        in_specs=[pl.BlockSpec((tq, LANE), lambda h, i: (i, h)),
                  pl.BlockSpec((S, LANE), lambda h, i: (0, B_HEADS + h)),
                  pl.BlockSpec((HEAD_DIM, S), lambda h, i: (vh0 + h, 0))]
                 + [pl.BlockSpec((1,) + b.shape[1:], lambda h, i: (h, 0, 0)) for b in bands],
        out_specs=pl.BlockSpec((tq, LANE), lambda h, i: (i, h)),
        out_shape=jax.ShapeDtypeStruct((S, B_HEADS * HEAD_DIM), BF16),
        scratch_shapes=[pltpu.VMEM((sum(lengths), tq), F32)],
        compiler_params=_cparams(("parallel", "arbitrary")),
        name="dilated_attention",
    )(qk, qk, vt, *bands)


def _diff_kernel(lam_ref, g_ref, q_ref, k_ref, vt_ref, band_ref, o_ref, acc_ref, m_ref, l_ref, s_ref,
                 *, tq, tk, c0, lam_init):
    i = pl.program_id(1)
    t0 = i * tq
    nchunk = (t0 + tq + tk - 1) // tk
    last_chunk = k_ref.shape[0] // tk - 1
    q = q_ref[...]
    lane = lax.broadcasted_iota(jnp.int32, q.shape, 1)
    zero = jnp.zeros_like(q)
    q2 = jnp.concatenate([jnp.where(lane < C_HALF, q, zero), jnp.where(lane >= C_HALF, q, zero)], axis=0)

    m_ref[...] = jnp.full(m_ref.shape, NEG, F32)
    l_ref[...] = jnp.zeros(l_ref.shape, F32)
    acc_ref[...] = jnp.zeros(acc_ref.shape, F32)

    def logits(c):
        s0 = pl.multiple_of(jnp.minimum(c, last_chunk) * tk, tk)
        off = pl.multiple_of(c0 - jnp.clip(t0 - c * tk, -tk, c0), LANE)
        bias = band_ref[0, pl.ds(off, tk), :]
        return _dot_nt(k_ref[pl.ds(s0, tk), :], q2) + jnp.concatenate([bias, bias], axis=1)

    def consume(c, s):
        s0 = pl.multiple_of(jnp.minimum(c, last_chunk) * tk, tk)
        m_prev = m_ref[...]
        m_new = jnp.maximum(m_prev, jnp.max(s, axis=0, keepdims=True))
        alpha = jnp.exp2(m_prev - m_new)
        p = jnp.exp2(s - m_new)
        l_ref[...] = alpha * l_ref[...] + jnp.sum(p, axis=0, keepdims=True)
        acc_ref[...] = alpha * acc_ref[...] + _dot(vt_ref[:, pl.ds(s0, tk)], p.astype(BF16))
        m_ref[...] = m_new

    s_ref[0] = logits(0)

    def chunk_pair(cc, carry):
        c = 2 * cc
        s_ref[1] = logits(c + 1)
        consume(c, s_ref[0])
        s_ref[0] = logits(c + 2)
        consume(c + 1, s_ref[1])
        return carry

    lax.fori_loop(0, (nchunk + 1) // 2, chunk_pair, 0)

    lp = lam_ref[...]
    lam = (jnp.exp(jnp.sum(lp[0:1] * lp[1:2], axis=1, keepdims=True))
           - jnp.exp(jnp.sum(lp[2:3] * lp[3:4], axis=1, keepdims=True)) + lam_init)
    o = acc_ref[...] / l_ref[...]
    o = o[:, :tq] - lam * o[:, tq:]
    ms = jnp.mean(o * o, axis=0, keepdims=True)
    o = o * lax.rsqrt(ms + RMS_EPS) * (g_ref[...] * (1.0 - lam_init))
    o_ref[...] = o.T.astype(o_ref.dtype)


def diff_attention(qk, vt, band, lam_params, subln_g, lam_init, vt_row0, tq=256, tk=512):
    S = qk.shape[0]
    vh0 = vt_row0 // HEAD_DIM
    return pl.pallas_call(
        functools.partial(_diff_kernel, tq=tq, tk=tk, c0=_band_c0(tk), lam_init=lam_init),
        grid=(C_HEADS, S // tq),
        in_specs=[pl.BlockSpec((4, C_HALF), lambda h, i: (0, 0)),
                  pl.BlockSpec((HEAD_DIM, 1), lambda h, i: (0, 0)),
                  pl.BlockSpec((tq, LANE), lambda h, i: (i, h)),
                  pl.BlockSpec((S, LANE), lambda h, i: (0, C_HEADS + h)),
                  pl.BlockSpec((HEAD_DIM, S), lambda h, i: (vh0 + h, 0)),
                  pl.BlockSpec((1,) + band.shape[1:], lambda h, i: (h, 0, 0))],
        out_specs=pl.BlockSpec((tq, LANE), lambda h, i: (i, h)),
        out_shape=jax.ShapeDtypeStruct((S, C_HEADS * HEAD_DIM), BF16),
        scratch_shapes=[pltpu.VMEM((HEAD_DIM, 2 * tq), F32),
                        pltpu.VMEM((1, 2 * tq), F32),
                        pltpu.VMEM((1, 2 * tq), F32),
                        pltpu.VMEM((2, tk, 2 * tq), F32)],
        compiler_params=_cparams(("parallel", "arbitrary")),
        name="diff_attention",
    )(lam_params, subln_g.reshape(HEAD_DIM, 1), qk, qk, vt, band)


def _outproj_kernel(x_ref, a_ref, b_ref, c_ref, wa_ref, wb_ref, wc_ref, o_ref):
    o_ref[...] = (x_ref[...] + _dot(a_ref[...], wa_ref[...]) + _dot(b_ref[...], wb_ref[...])
                  + _dot(c_ref[...], wc_ref[...]))


def out_projection(x, oa, ob, oc, wa, wb, wc, tm=512, tn=1024):
    S, D = x.shape
    tn = min(tn, D)
    return pl.pallas_call(
        _outproj_kernel,
        grid=(S // tm, D // tn),
        in_specs=[pl.BlockSpec((tm, tn), lambda i, j: (i, j)),
                  pl.BlockSpec((tm, oa.shape[1]), lambda i, j: (i, 0)),
                  pl.BlockSpec((tm, ob.shape[1]), lambda i, j: (i, 0)),
                  pl.BlockSpec((tm, oc.shape[1]), lambda i, j: (i, 0)),
                  pl.BlockSpec((wa.shape[0], tn), lambda i, j: (0, j)),
                  pl.BlockSpec((wb.shape[0], tn), lambda i, j: (0, j)),
                  pl.BlockSpec((wc.shape[0], tn), lambda i, j: (0, j))],
        out_specs=pl.BlockSpec((tm, tn), lambda i, j: (i, j)),
        out_shape=jax.ShapeDtypeStruct((S, D), F32),
        compiler_params=_cparams(("parallel", "arbitrary")),
        name="out_projection",
    )(x, oa, ob, oc, wa, wb, wc)


def _silu(x):
    return x / (1.0 + jnp.exp(-x))


def _ffn_kernel(x_ref, h_ref, wg_ref, wu_ref, wd_ref, o_ref):
    f = pl.program_id(1)

    @pl.when(f == 0)
    def _():
        o_ref[...] = x_ref[...]

    h = h_ref[...]
    a = _silu(_dot(h, wg_ref[...].astype(BF16))) * _dot(h, wu_ref[...].astype(BF16))
    o_ref[...] += _dot(a.astype(BF16), wd_ref[...].astype(BF16))


def ffn_dense(x, h, wg, wu, wd, tm=1024, tf=256):
    S, D = x.shape
    F = wg.shape[1]
    once = pl.Buffered(1)
    return pl.pallas_call(
        _ffn_kernel,
        grid=(S // tm, F // tf),
        in_specs=[pl.BlockSpec((tm, D), lambda i, f: (i, 0), pipeline_mode=once),
                  pl.BlockSpec((tm, D), lambda i, f: (i, 0), pipeline_mode=once),
                  pl.BlockSpec((D, tf), lambda i, f: (0, f)),
                  pl.BlockSpec((D, tf), lambda i, f: (0, f)),
                  pl.BlockSpec((tf, D), lambda i, f: (f, 0))],
        out_specs=pl.BlockSpec((tm, D), lambda i, f: (i, 0)),
        out_shape=jax.ShapeDtypeStruct((S, D), F32),
        compiler_params=_cparams(("parallel", "arbitrary")),
        name="ffn_dense",
    )(x, h, wg, wu, wd)


def _router_kernel(x_ref, g_ref, w_ref, c_ref, sel_ref):
    x = x_ref[...]
    ms = jnp.mean(x * x, axis=-1, keepdims=True)
    h = x * lax.rsqrt(ms + RMS_EPS) * g_ref[...]
    logits = jnp.dot(h, w_ref[...], preferred_element_type=F32, precision=lax.Precision.HIGHEST)
    lane = lax.broadcasted_iota(jnp.int32, logits.shape, 1).astype(F32)
    logits = jnp.where(lane < N_EXPERTS, logits, NEG)
    m1 = jnp.max(logits, axis=1, keepdims=True)
    i1 = jnp.min(jnp.where(logits == m1, lane, float(LANE)), axis=1, keepdims=True)
    rest = jnp.where(lane == i1, NEG, logits)
    m2 = jnp.max(rest, axis=1, keepdims=True)
    i2 = jnp.min(jnp.where(rest == m2, lane, float(LANE)), axis=1, keepdims=True)
    e2 = jnp.exp(m2 - m1)
    g1 = 1.0 / (1.0 + e2)
    g2 = e2 / (1.0 + e2)
    pick1, pick2 = lane == i1, lane == i2
    c_ref[...] = jnp.where(pick1, g1, 0.0) + jnp.where(pick2, g2, 0.0)
    sel_ref[...] = jnp.where(pick1 | pick2, 1.0, 0.0)


def moe_router(x, g, w_router, tm=512):
    S, D = x.shape
    wpad = jnp.zeros((D, LANE), F32).at[:, :N_EXPERTS].set(w_router)
    spec = pl.BlockSpec((tm, LANE), lambda i: (i, 0))
    sds = jax.ShapeDtypeStruct((S, LANE), F32)
    return pl.pallas_call(
        _router_kernel,
        grid=(S // tm,),
        in_specs=[pl.BlockSpec((tm, D), lambda i: (i, 0)),
                  pl.BlockSpec((1, D), lambda i: (0, 0)),
                  pl.BlockSpec((D, LANE), lambda i: (0, 0))],
        out_specs=[spec, spec],
        out_shape=[sds, sds],
        compiler_params=_cparams(("parallel",)),
        name="moe_router",
    )(x, g.reshape(1, D), wpad)


MOE_CH = 256
MOE_TILE = 1024


def _moe_rank_kernel(sel_ref, rm_ref, rmt_ref, cum_ref, tot_ref, carry_ref):
    c = pl.program_id(0)

    @pl.when(c == 0)
    def _():
        carry_ref[...] = jnp.zeros(carry_ref.shape, F32)

    sel = sel_ref[...]
    ch = sel.shape[0]
    before = (lax.broadcasted_iota(jnp.int32, (ch, ch), 1) < lax.broadcasted_iota(jnp.int32, (ch, ch), 0))
    rank = _dot(jnp.where(before, 1.0, 0.0).astype(BF16), sel.astype(BF16)) + carry_ref[...]
    rm = jnp.where(sel > 0.0, rank, -1.0)
    rm_ref[...] = rm
    rmt_ref[...] = rm.T[:N_EXPERTS]
    cum_ref[0] = carry_ref[...]
    carry_ref[...] += jnp.sum(sel, axis=0, keepdims=True)
    tot_ref[...] = carry_ref[...]


def moe_rank(sel):
    S = sel.shape[0]
    nch = S // MOE_CH
    return pl.pallas_call(
        _moe_rank_kernel,
        grid=(nch,),
        in_specs=[pl.BlockSpec((MOE_CH, LANE), lambda c: (c, 0))],
        out_specs=[pl.BlockSpec((MOE_CH, LANE), lambda c: (c, 0)),
                   pl.BlockSpec((N_EXPERTS, MOE_CH), lambda c: (0, c)),
                   pl.BlockSpec((1, 1, LANE), lambda c: (c, 0, 0)),
                   pl.BlockSpec((1, LANE), lambda c: (0, 0))],
        out_shape=[jax.ShapeDtypeStruct((S, LANE), F32),
                   jax.ShapeDtypeStruct((N_EXPERTS, S), F32),
                   jax.ShapeDtypeStruct((nch, 1, LANE), F32),
                   jax.ShapeDtypeStruct((1, LANE), F32)],
        scratch_shapes=[pltpu.VMEM((1, LANE), F32)],
        compiler_params=_cparams(("arbitrary",)),
        name="moe_rank",
    )(sel)


def _moe_expert_kernel(te_ref, tr0_ref, tclo_ref, tchi_ref, tval_ref, cumx_ref,
                       rk_ref, h_hbm, wg_ref, wu_ref, wd_ref, y_ref,
                       hs_ref, acc_ref, hbuf_ref, sem_ref, *, n_experts):
    j = pl.program_id(0)
    f = pl.program_id(1)
    nf = pl.num_programs(1)
    nrows = tval_ref[j]
    tile = hs_ref.shape[0]
    win = MOE_CH + 8

    def chunk_copy(c, slot):
        return pltpu.make_async_copy(h_hbm.at[pl.ds(c * MOE_CH, MOE_CH)], hbuf_ref.at[slot], sem_ref.at[slot])

    @pl.when((nrows > 0) & (f == 0))
    def _():
        clo, chi = tclo_ref[j], tchi_ref[j]
        tr0 = tr0_ref[j]
        wrow = lax.broadcasted_iota(jnp.int32, (win, MOE_CH), 0)
        acc_ref[...] = jnp.zeros(acc_ref.shape, F32)

        @pl.when(chi > clo)
        def _():
            chunk_copy(clo, 0).start()

        def body(c, carry):
            slot = (c - clo) % 2

            @pl.when(c + 1 < chi)
            def _():
                chunk_copy(c + 1, 1 - slot).start()

            chunk_copy(c, slot).wait()
            first = cumx_ref[c * n_experts + te_ref[j]] - tr0
            w0 = pl.multiple_of(jnp.clip((first // 8) * 8, 0, tile - win), 8)
            rk = rk_ref[0, :, pl.ds(pl.multiple_of(c * MOE_CH, MOE_CH), MOE_CH)]
            onehot = jnp.where(rk == (tr0 + w0 + wrow).astype(F32), 1.0, 0.0).astype(BF16)
            acc_ref[pl.ds(w0, win), :] += _dot(onehot, hbuf_ref[slot])
            return carry

        lax.fori_loop(clo, chi, body, 0)
        hs_ref[...] = acc_ref[...].astype(BF16)
        acc_ref[...] = jnp.zeros(acc_ref.shape, F32)

    def swiglu(rows):
        hs = hs_ref[:rows]
        a = _silu(_dot(hs, wg_ref[0].astype(BF16))) * _dot(hs, wu_ref[0].astype(BF16))
        acc_ref[:rows] += _dot(a.astype(BF16), wd_ref[0].astype(BF16))

    sizes = (tile, tile // 2, tile // 4, tile // 8)
    for rows, below in zip(sizes, sizes[1:] + (0,)):
        @pl.when((nrows > below) & (nrows <= rows))
        def _():
            swiglu(rows)

    @pl.when(f == nf - 1)
    def _():
        y_ref[...] = jnp.where(nrows > 0, acc_ref[...], 0.0).astype(y_ref.dtype)


def _moe_combine_kernel(ca_ref, cb_ref, roff_ref, x_ref, rm_ref, cw_ref, *refs):
    y_refs, o_ref = refs[:-1], refs[-1]
    n_experts = len(y_refs) // 2
    c = pl.program_id(0)
    rm = rm_ref[...]
    cw = cw_ref[...]
    lane = lax.broadcasted_iota(jnp.int32, rm.shape, 1)
    cr = y_refs[0].shape[0]
    col = lax.broadcasted_iota(jnp.int32, (rm.shape[0], cr), 1)
    out = x_ref[...]
    for e in range(n_experts):
        mine = lane == e
        pos = jnp.sum(jnp.where(mine, rm, 0.0), axis=1, keepdims=True)
        gate = jnp.sum(jnp.where(mine, cw, 0.0), axis=1, keepdims=True)
        row = jnp.where(pos >= 0.0, pos + roff_ref[e].astype(F32), -1.0)
        ca, cb = ca_ref[c * n_experts + e], cb_ref[c * n_experts + e]
        for y_ref, base in ((y_refs[2 * e], ca * cr),
                            (y_refs[2 * e + 1], jnp.where(cb != ca, cb * cr, -2 * cr))):
            onehot = jnp.where(row == (base + col).astype(F32), 1.0, 0.0).astype(BF16)
            out = out + gate * _dot(onehot, y_ref[...])
    o_ref[...] = out


def ffn_moe(x, h, combine, sel, wg, wu, wd, tf=512):
    S, D = x.shape
    E, _, F = wg.shape
    tf = min(tf, F)
    T = MOE_TILE
    nch = S // MOE_CH
    J = 2 * S // T + E
    rm, rmt, cum, tot = moe_rank(sel)

    cnt = tot[0, :E].astype(jnp.int32)
    ntile = (cnt + T - 1) // T
    tend = jnp.cumsum(ntile)
    tstart = tend - ntile
    jj = jnp.arange(J, dtype=jnp.int32)
    te = jnp.minimum(jnp.searchsorted(tend, jnp.minimum(jj, tend[-1] - 1), side="right"), E - 1).astype(jnp.int32)
    tr0 = (jnp.minimum(jj, tend[-1] - 1) - tstart[te]) * T
    tval = jnp.where(jj < tend[-1], jnp.clip(cnt[te] - tr0, 0, T), 0).astype(jnp.int32)
    cumx = cum[:, 0, :E].astype(jnp.int32)
    cumi = jnp.concatenate([cumx[1:], cnt[None]], axis=0)
    tclo = jnp.sum(cumi[:, te] <= tr0[None, :], axis=0).astype(jnp.int32)
    tchi = jnp.sum(cumx[:, te] < (tr0 + tval)[None, :], axis=0).astype(jnp.int32)
    roff = (tstart * T).astype(jnp.int32)
    nrow_chunks = J * T // MOE_CH
    ca = ((roff[None, :] + cumx) // MOE_CH).astype(jnp.int32)
    cb = jnp.minimum(ca + 1, nrow_chunks - 1)
    ca, cb = ca.reshape(-1), cb.reshape(-1)

    nf = F // tf

    def wmap(j, f, te_r, tr0_r, clo_r, chi_r, val_r, cumx_r):
        return (te_r[j], 0, jnp.where(val_r[j] > 0, f, nf - 1))

    def wdmap(j, f, te_r, tr0_r, clo_r, chi_r, val_r, cumx_r):
        return (te_r[j], jnp.where(val_r[j] > 0, f, nf - 1), 0)

    y = pl.pallas_call(
        functools.partial(_moe_expert_kernel, n_experts=E),
        grid_spec=pltpu.PrefetchScalarGridSpec(
            num_scalar_prefetch=6,
            grid=(J, nf),
            in_specs=[pl.BlockSpec((1, 1, S), lambda j, f, te_r, *_: (te_r[j], 0, 0)),
                      pl.BlockSpec(memory_space=pl.ANY),
                      pl.BlockSpec((1, D, tf), wmap),
                      pl.BlockSpec((1, D, tf), wmap),
                      pl.BlockSpec((1, tf, D), wdmap)],
            out_specs=pl.BlockSpec((T, D), lambda j, f, *_: (j, 0)),
            scratch_shapes=[pltpu.VMEM((T, D), BF16),
                            pltpu.VMEM((T, D), F32),
                            pltpu.VMEM((2, MOE_CH, D), BF16),
                            pltpu.SemaphoreType.DMA((2,))]),
        out_shape=jax.ShapeDtypeStruct((J * T, D), BF16),
        compiler_params=_cparams(("arbitrary", "arbitrary")),
        name="moe_experts",
    )(te, tr0, tclo, tchi, tval, cumx.reshape(-1), rmt.reshape(E, 1, S), h, wg, wu, wd)

    def chunk_map(c, ca_r, cb_r, ro_r, *, e, second):
        return ((cb_r if second else ca_r)[c * E + e], 0)

    return pl.pallas_call(
        _moe_combine_kernel,
        grid_spec=pltpu.PrefetchScalarGridSpec(
            num_scalar_prefetch=3,
            grid=(nch,),
            in_specs=[pl.BlockSpec((MOE_CH, D), lambda c, *_: (c, 0)),
                      pl.BlockSpec((MOE_CH, LANE), lambda c, *_: (c, 0)),
                      pl.BlockSpec((MOE_CH, LANE), lambda c, *_: (c, 0))]
                     + [pl.BlockSpec((MOE_CH, D), functools.partial(chunk_map, e=e, second=second))
                        for e in range(E) for second in (False, True)],
            out_specs=pl.BlockSpec((MOE_CH, D), lambda c, *_: (c, 0))),
        out_shape=jax.ShapeDtypeStruct((S, D), F32),
        compiler_params=_cparams(("parallel",)),
        name="moe_combine",
    )(ca, cb, roff, x, rm, combine, *([y] * (2 * E)))


def _tile_gain(g, reps, scale=1.0):
    return jnp.tile(g.astype(F32) * scale, reps)


def kernel(x, w_in, w_out, attn_norm_g, ffn_norm_g, q_norm_a, k_norm_a, q_norm_b, k_norm_b,
           q_norm_c, k_norm_c, lambda_q1, lambda_k1, lambda_q2, lambda_k2, diff_subln_g, rel_bias,
           w_dense_gate, w_dense_up, w_dense_down, w_router, w_moe_gate, w_moe_up, w_moe_down):
    B, S, D = x.shape
    depth = w_in.shape[0]
    assert B == 1 and S % (DIL_STEPS * DILATED_PAIRS[-1][1]) == 0
    aw, bw, cw = A_HEADS * HEAD_DIM, B_HEADS * HEAD_DIM, C_HEADS * HEAD_DIM
    iw = IDX_HEADS * IDX_DIM
    sizes = (aw, aw, aw, iw, IDX_DIM, IDX_HEADS, bw, bw, bw, cw, cw, cw)
    offs = np.concatenate([[0], np.cumsum(sizes)]).tolist()

    def cols(w, *segs):
        return jnp.concatenate([w[:, offs[s]:offs[s + 1]] for s in segs], axis=1)

    tq, tk = 256, 512
    band_a = bias_band(rel_bias, A_HEADS, 0, _band_c0(tk) + 2 * tk, tq, _band_c0(tk))
    band_c = bias_band(rel_bias, C_HEADS, A_HEADS + B_HEADS, _band_c0(tk) + 2 * tk, tq, _band_c0(tk))
    bands_b = [bias_band(rel_bias, B_HEADS, A_HEADS, 2 * DIL_STEPS * dil + tq, tq, DIL_STEPS * dil, dil)
               for _, dil in DILATED_PAIRS]

    xs = x.reshape(S, D)
    for layer in range(depth):
        wl = w_in[layer]
        w_pa = cols(wl, 0, 1).astype(BF16)
        w_pb = cols(wl, 6, 7).astype(BF16)
        w_pc = cols(wl, 9, 10).astype(BF16)
        w_pi = cols(wl, 3, 4, 4).astype(BF16)
        w_vt = cols(wl, 2, 8, 11).T.astype(BF16)
        w_wt = cols(wl, 5).T.astype(BF16)
        qs_ab, qs_c = HEAD_DIM ** -0.5 * LOG2E, C_HALF ** -0.5 * LOG2E
        g_pa = jnp.concatenate([_tile_gain(q_norm_a[layer], A_HEADS, qs_ab), _tile_gain(k_norm_a[layer], A_HEADS)])
        g_pb = jnp.concatenate([_tile_gain(q_norm_b[layer], B_HEADS, qs_ab), _tile_gain(k_norm_b[layer], B_HEADS)])
        g_pc = jnp.concatenate([_tile_gain(q_norm_c[layer], 2 * C_HEADS, qs_c),
                                _tile_gain(k_norm_c[layer], 2 * C_HEADS)])

        h = rmsnorm(xs, attn_norm_g[layer])
        p_a = project(h, w_pa, g_pa, HEAD_DIM, BF16, tn=512)
        p_b = project(h, w_pb, g_pb, HEAD_DIM, BF16, tn=512)
        p_c = project(h, w_pc, g_pc, C_HALF, BF16, tn=512)
        p_i = project(h, w_pi, jnp.ones((w_pi.shape[1],), F32), 0, BF16, tn=384)
        vt = project_t(h, w_vt, BF16)
        wt = project_t(h, w_wt, F32)

        oa = dsa_attention(p_a, vt, p_i, wt, band_a)

        ob = dilated_attention(p_b, vt, bands_b, aw, tq=tq)

        lam_init = 0.8 - 0.6 * math.exp(-0.3 * layer)
        lam_params = jnp.stack([lambda_q1[layer], lambda_k1[layer], lambda_q2[layer], lambda_k2[layer]]).astype(F32)
        oc = diff_attention(p_c, vt, band_c, lam_params, diff_subln_g[layer].astype(F32), lam_init, aw + bw,
                            tq=tq, tk=tk)

        wo = w_out[layer].astype(BF16)
        xs = out_projection(xs, oa, ob, oc, wo[:aw], wo[aw:aw + bw], wo[aw + bw:])

        f = layer // 2
        if layer % 2 == 0:
            h2 = rmsnorm(xs, ffn_norm_g[layer])
            xs = ffn_dense(xs, h2, w_dense_gate[f], w_dense_up[f], w_dense_down[f])
        else:
            h2 = rmsnorm(xs, ffn_norm_g[layer])
            combine, sel = moe_router(xs, ffn_norm_g[layer], w_router[f])
            xs = ffn_moe(xs, h2, combine, sel, w_moe_gate[f], w_moe_up[f], w_moe_down[f])
    return xs.reshape(B, S, D)
```

```python
import functools
import math

import jax
import jax.numpy as jnp
import numpy as np
from jax import lax
from jax.experimental import pallas as pl
from jax.experimental.pallas import tpu as pltpu

F32 = jnp.float32
BF16 = jnp.bfloat16

HEAD_DIM = 128
A_HEADS, B_HEADS, C_HEADS = 4, 6, 6
C_HALF = HEAD_DIM // 2
IDX_HEADS, IDX_DIM = 16, 64
TOPK_MAX = 256
DILATED_PAIRS = ((128, 1), (512, 4), (2048, 16))
DIL_STEPS = 128
N_BUCKETS = 32
MAX_DISTANCE = 2048
N_EXPERTS = 8
RMS_EPS = 1e-6
NEG = -1e30
LOG2E = math.log2(math.e)
INT_MIN = -(2 ** 31)
I16_BIAS = 2 ** 15
CNT_ROWS = 64
LANE = 128
VMEM_LIMIT = 56 * 1024 * 1024

BUCKET_EDGES = tuple(range(16)) + (16,) + tuple(
    int(math.ceil(16.0 * (MAX_DISTANCE / 16.0) ** (k / 16.0))) for k in range(1, 16))
FAR_DIST = BUCKET_EDGES[-1]


def _cparams(sem):
    return pltpu.CompilerParams(dimension_semantics=sem, vmem_limit_bytes=VMEM_LIMIT)


def _dot(a, b):
    return jnp.dot(a, b, preferred_element_type=F32)


def _dot_nt(a, b):
    return lax.dot_general(a, b, (((1,), (1,)), ((), ())), preferred_element_type=F32)


def _rmsnorm_kernel(x_ref, g_ref, o_ref):
    x = x_ref[...]
    ms = jnp.mean(x * x, axis=-1, keepdims=True)
    o_ref[...] = (x * lax.rsqrt(ms + RMS_EPS) * g_ref[...]).astype(o_ref.dtype)


def rmsnorm(x, g, tm=512):
    S, D = x.shape
    return pl.pallas_call(
        _rmsnorm_kernel,
        grid=(S // tm,),
        in_specs=[pl.BlockSpec((tm, D), lambda i: (i, 0)), pl.BlockSpec((1, D), lambda i: (0, 0))],
        out_specs=pl.BlockSpec((tm, D), lambda i: (i, 0)),
        out_shape=jax.ShapeDtypeStruct((S, D), BF16),
        compiler_params=_cparams(("parallel",)),
        name="rmsnorm",
    )(x, g.reshape(1, D))


def _proj_kernel(h_ref, w_ref, g_ref, o_ref, *, group):
    acc = _dot(h_ref[...], w_ref[...])
    tn = acc.shape[1]
    if group == 0:
        o_ref[...] = acc.astype(o_ref.dtype)
        return
    for c in range(tn // LANE):
        a = acc[:, c * LANE:(c + 1) * LANE]
        sq = a * a
        if group == LANE:
            ms = jnp.mean(sq, axis=-1, keepdims=True)
        else:
            lane = lax.broadcasted_iota(jnp.int32, sq.shape, 1)
            lo = jnp.sum(jnp.where(lane < group, sq, 0.0), axis=-1, keepdims=True)
            hi = jnp.sum(jnp.where(lane >= group, sq, 0.0), axis=-1, keepdims=True)
            ms = jnp.where(lane < group, lo, hi) * (1.0 / group)
        y = a * lax.rsqrt(ms + RMS_EPS) * g_ref[:, c * LANE:(c + 1) * LANE]
        o_ref[:, c * LANE:(c + 1) * LANE] = y.astype(o_ref.dtype)


def project(h, w, gains, group, out_dtype, tm=1024, tn=256):
    S, K = h.shape
    N = w.shape[1]
    return pl.pallas_call(
        functools.partial(_proj_kernel, group=group),
        grid=(S // tm, N // tn),
        in_specs=[pl.BlockSpec((tm, K), lambda i, j: (i, 0)),
                  pl.BlockSpec((K, tn), lambda i, j: (0, j)),
                  pl.BlockSpec((1, tn), lambda i, j: (0, j))],
        out_specs=pl.BlockSpec((tm, tn), lambda i, j: (i, j)),
        out_shape=jax.ShapeDtypeStruct((S, N), out_dtype),
        compiler_params=_cparams(("parallel", "arbitrary")),
        name=f"proj_g{group}",
    )(h, w, gains.reshape(1, N))


def _proj_t_kernel(wt_ref, h_ref, o_ref):
    o_ref[...] = _dot_nt(wt_ref[...], h_ref[...]).astype(o_ref.dtype)


def project_t(h, wt, out_dtype, tm=1024, tn=256):
    S, K = h.shape
    N = wt.shape[0]
    tn = min(tn, N)
    return pl.pallas_call(
        _proj_t_kernel,
        grid=(S // tm, N // tn),
        in_specs=[pl.BlockSpec((tn, K), lambda i, j: (j, 0)),
                  pl.BlockSpec((tm, K), lambda i, j: (i, 0))],
        out_specs=pl.BlockSpec((tn, tm), lambda i, j: (j, i)),
        out_shape=jax.ShapeDtypeStruct((N, S), out_dtype),
        compiler_params=_cparams(("parallel", "arbitrary")),
        name="proj_t",
    )(wt, h)


BAND_ROWS = 256


def _band_kernel(tab_ref, o_ref, *, c0, head_off, dil):
    hh = head_off + pl.program_id(0)
    _, rows, width = o_ref.shape
    base = (lax.broadcasted_iota(jnp.int32, (BAND_ROWS, width), 1)
            - lax.broadcasted_iota(jnp.int32, (BAND_ROWS, width), 0))
    hi_valid = DIL_STEPS * dil if dil else None
    for r in range(rows // BAND_ROWS):
        d0 = c0 - r * BAND_ROWS
        lo, hi = d0 - (BAND_ROWS - 1), d0 + width - 1
        if hi_valid is not None:
            hi = min(hi, hi_valid)
        block = pl.ds(r * BAND_ROWS, BAND_ROWS)
        if hi < 0 or hi < lo:
            o_ref[0, block, :] = jnp.full((BAND_ROWS, width), NEG, F32)
            continue
        dist = base + d0
        first = max(b for b in range(N_BUCKETS) if BUCKET_EDGES[b] <= max(lo, 0))
        val = jnp.full((BAND_ROWS, width), tab_ref[first, hh] * LOG2E, F32)
        for b in range(first + 1, N_BUCKETS):
            if BUCKET_EDGES[b] <= hi:
                val = jnp.where(dist >= BUCKET_EDGES[b], tab_ref[b, hh] * LOG2E, val)
        ok = None
        if lo < 0:
            ok = dist >= 0
        if dil:
            in_win = (dist <= hi_valid) & ((dist & (dil - 1)) == 0)
            ok = in_win if ok is None else ok & in_win
        o_ref[0, block, :] = val if ok is None else jnp.where(ok, val, NEG)


def bias_band(rel_bias, n_heads, head_off, rows, width, c0, dil=0):
    assert rows % BAND_ROWS == 0 and dil & (dil - 1) == 0
    return pl.pallas_call(
        functools.partial(_band_kernel, c0=c0, head_off=head_off, dil=dil),
        grid=(n_heads,),
        in_specs=[pl.BlockSpec(memory_space=pltpu.SMEM)],
        out_specs=pl.BlockSpec((1, rows, width), lambda h: (h, 0, 0)),
        out_shape=jax.ShapeDtypeStruct((n_heads, rows, width), F32),
        compiler_params=_cparams(("parallel",)),
        name=f"bias_band_d{dil}",
    )(rel_bias)


def _band_c0(tk):
    return -(-(FAR_DIST + tk - 1) // LANE) * LANE


def _dsa_kernel(q_ref, qi_ref, wt_ref, k_ref, vt_ref, kidx_ref, band_ref, o_ref,
                hi_ref, lo_ref, cut_ref, qih_ref, s_ref, acc_ref, m_ref, l_ref, *, tq, kc, topk, c0):
    i = pl.program_id(0)
    t0 = i * tq
    nchunk = (t0 + tq + kc - 1) // kc
    half = lax.broadcasted_iota(jnp.int32, (tq, LANE), 1) < IDX_DIM

    for h in range(IDX_HEADS):
        slab = qi_ref[:, (h // 2) * LANE:(h // 2 + 1) * LANE]
        keep = half if h % 2 == 0 else jnp.logical_not(half)
        qih_ref[h] = jnp.where(keep, slab, jnp.zeros_like(slab))
    w = wt_ref[...] * (IDX_HEADS ** -0.5 * IDX_DIM ** -0.5)

    pos_q = t0 + lax.broadcasted_iota(jnp.int32, (kc, tq), 1)
    row = lax.broadcasted_iota(jnp.int32, (kc, tq), 0)

    def score_chunk(c, carry):
        s0 = pl.multiple_of(c * kc, kc)
        kx = kidx_ref[pl.ds(s0, kc), :]
        acc = jnp.zeros((kc, tq), F32)
        for h in range(IDX_HEADS):
            acc = acc + w[h:h + 1, :] * jnp.maximum(_dot_nt(kx, qih_ref[h]), 0.0)
        bits = lax.bitcast_convert_type(acc, jnp.int32)
        key = bits ^ ((bits >> 31) & 0x7FFFFFFF)
        key = jnp.where(s0 + row <= pos_q, key, INT_MIN)
        hi_ref[pl.ds(s0, kc), :] = (key >> 16).astype(jnp.int16)
        lo_ref[pl.ds(s0, kc), :] = ((key & 0xFFFF) - I16_BIAS).astype(jnp.int16)
        return carry

    lax.fori_loop(0, nchunk, score_chunk, 0)

    one16, zero16 = jnp.ones((), jnp.int16), jnp.zeros((), jnp.int16)
    row16 = lax.broadcasted_iota(jnp.int32, (kc, tq), 0).astype(jnp.int16)
    rowc16 = lax.broadcasted_iota(jnp.int32, (CNT_ROWS, tq), 0).astype(jnp.int16)
    n_keys = k_ref.shape[0]

    def to16(v):
        return jnp.clip(v, -I16_BIAS, I16_BIAS - 1).astype(jnp.int16)

    def count(hit):
        def body(c, cnt):
            base = pl.multiple_of(c * kc, kc)
            for g in range(kc // CNT_ROWS):
                cnt = cnt + jnp.where(hit(base + g * CNT_ROWS), one16, zero16)
            return cnt

        cnt = lax.fori_loop(0, nchunk, body, jnp.zeros((CNT_ROWS, tq), jnp.int16))
        return jnp.sum(cnt.astype(jnp.int32).astype(F32), axis=0, keepdims=True)

    def count_ge(ref, cand):
        c16 = cand.astype(jnp.int16)
        return count(lambda r: ref[pl.ds(r, CNT_ROWS), :] >= c16)

    def count_gt(ref, cand):
        c16 = cand.astype(jnp.int16)
        return count(lambda r: ref[pl.ds(r, CNT_ROWS), :] > c16)

    def kth_largest(ref, need):
        t = jnp.where(count_ge(ref, jnp.zeros((1, tq), jnp.int32)) >= need, 0, -I16_BIAS).astype(jnp.int32)

        def bit_body(b, t):
            cand = t + jnp.left_shift(jnp.int32(1), 14 - b)
            return jnp.where(count_ge(ref, cand) >= need, cand, t)

        return lax.fori_loop(0, 15, bit_body, t)

    kf = jnp.full((1, tq), float(topk), F32)
    thi = kth_largest(hi_ref, kf)
    need_lo = kf - count_gt(hi_ref, thi)
    thi16 = thi.astype(jnp.int16)

    def mask_low(c, carry):
        rows = pl.ds(pl.multiple_of(c * kc, kc), kc)
        lo_ref[rows, :] = jnp.where(hi_ref[rows, :] == thi16, lo_ref[rows, :], jnp.int16(-I16_BIAS))
        return carry

    lax.fori_loop(0, nchunk, mask_low, 0)
    tlo = kth_largest(lo_ref, need_lo)
    tlo16 = tlo.astype(jnp.int16)

    def is_eq(r):
        return (hi_ref[pl.ds(r, CNT_ROWS), :] == thi16) & (lo_ref[pl.ds(r, CNT_ROWS), :] == tlo16)

    keep_eq = need_lo - count_gt(lo_ref, tlo)
    excess = (count(is_eq) > keep_eq) & (thi > -I16_BIAS)
    cut_ref[...] = jnp.full(cut_ref.shape, n_keys, jnp.int32)

    @pl.when(jnp.max(jnp.broadcast_to(jnp.where(excess, 1.0, 0.0), (8, tq))) > 0.0)
    def _():
        nbits = n_keys.bit_length()

        def bit_body(b, j):
            cand = j + jnp.left_shift(jnp.int32(1), nbits - 1 - b)
            before = count(lambda r: is_eq(r) & (rowc16 < to16(cand - r)))
            return jnp.where(before < keep_eq, cand, j)

        j = lax.fori_loop(0, nbits, bit_body, jnp.zeros((1, tq), jnp.int32))
        cut_ref[...] = jnp.where(excess, j, n_keys)

    cut = cut_ref[...]
    thi_sel16 = jnp.maximum(thi, 1 - I16_BIAS).astype(jnp.int16)

    m_ref[...] = jnp.full(m_ref.shape, NEG, F32)
    l_ref[...] = jnp.zeros(l_ref.shape, F32)
    acc_ref[...] = jnp.zeros(acc_ref.shape, F32)

    def logits(c):
        s0 = pl.multiple_of(jnp.minimum(c, nchunk - 1) * kc, kc)
        off = pl.multiple_of(c0 - jnp.clip(t0 - c * kc, -kc, c0), LANE)
        hi = hi_ref[pl.ds(s0, kc), :]
        lo = lo_ref[pl.ds(s0, kc), :]
        sel = (hi > thi_sel16) | ((hi == thi_sel16)
                                  & ((lo > tlo16) | ((lo == tlo16) & (row16 <= to16(cut - s0)))))
        drop = jnp.where(sel, jnp.zeros((), jnp.int16), jnp.ones((), jnp.int16))
        mask_bias = drop.astype(jnp.int32).astype(F32) * NEG
        parts = []
        for h in range(A_HEADS):
            hs = slice(h * HEAD_DIM, (h + 1) * HEAD_DIM)
            parts.append(_dot_nt(k_ref[pl.ds(s0, kc), hs], q_ref[:, hs]) + band_ref[h, pl.ds(off, kc), :]
                         + mask_bias)
        return jnp.concatenate(parts, axis=1)

    def consume(c, s):
        s0 = pl.multiple_of(jnp.minimum(c, nchunk - 1) * kc, kc)
        m_prev = m_ref[...]
        m_new = jnp.maximum(m_prev, jnp.max(s, axis=0, keepdims=True))
        alpha = jnp.exp2(m_prev - m_new)
        p = jnp.exp2(s - m_new)
        l_ref[...] = alpha * l_ref[...] + jnp.sum(p, axis=0, keepdims=True)
        p = p.astype(BF16)
        pv = [_dot(vt_ref[h * HEAD_DIM:(h + 1) * HEAD_DIM, pl.ds(s0, kc)], p[:, h * tq:(h + 1) * tq])
              for h in range(A_HEADS)]
        acc_ref[...] = alpha * acc_ref[...] + jnp.concatenate(pv, axis=1)
        m_ref[...] = m_new

    s_ref[0] = logits(0)

    def chunk_pair(cc, carry):
        c = 2 * cc
        s_ref[1] = logits(c + 1)
        consume(c, s_ref[0])
        s_ref[0] = logits(c + 2)
        consume(c + 1, s_ref[1])
        return carry

    lax.fori_loop(0, (nchunk + 1) // 2, chunk_pair, 0)
    o = acc_ref[...] / l_ref[...]
    for h in range(A_HEADS):
        o_ref[:, h * HEAD_DIM:(h + 1) * HEAD_DIM] = o[:, h * tq:(h + 1) * tq].T.astype(o_ref.dtype)


def dsa_attention(qk, vt, pidx, wt, band, tq=256, kc=512):
    S = qk.shape[0]
    topk = min(TOPK_MAX, S // 4)
    aw = A_HEADS * HEAD_DIM
    iw = IDX_HEADS * IDX_DIM
    once = pl.Buffered(1)
    return pl.pallas_call(
        functools.partial(_dsa_kernel, tq=tq, kc=kc, topk=topk, c0=_band_c0(kc)),
        grid=(S // tq,),
        in_specs=[pl.BlockSpec((tq, aw), lambda i: (i, 0)),
                  pl.BlockSpec((tq, iw), lambda i: (i, 0)),
                  pl.BlockSpec((IDX_HEADS, tq), lambda i: (0, i)),
                  pl.BlockSpec((S, aw), lambda i: (0, 1), pipeline_mode=once),
                  pl.BlockSpec((aw, S), lambda i: (0, 0), pipeline_mode=once),
                  pl.BlockSpec((S, LANE), lambda i: (0, iw // LANE), pipeline_mode=once),
                  pl.BlockSpec(band.shape, lambda i: (0, 0, 0), pipeline_mode=once)],
        out_specs=pl.BlockSpec((tq, aw), lambda i: (i, 0)),
        out_shape=jax.ShapeDtypeStruct((S, aw), BF16),
        scratch_shapes=[pltpu.VMEM((S, tq), jnp.int16),
                        pltpu.VMEM((S, tq), jnp.int16),
                        pltpu.VMEM((1, tq), jnp.int32),
                        pltpu.VMEM((IDX_HEADS, tq, LANE), BF16),
                        pltpu.VMEM((2, kc, A_HEADS * tq), F32),
                        pltpu.VMEM((HEAD_DIM, A_HEADS * tq), F32),
                        pltpu.VMEM((1, A_HEADS * tq), F32),
                        pltpu.VMEM((1, A_HEADS * tq), F32)],
        compiler_params=_cparams(("arbitrary",)),
        name="dsa_attention",
    )(qk, pidx, wt, qk, vt, pidx, band)


DIL_ROWS = 384


def _dilated_kernel(q_ref, k_ref, vt_ref, b0_ref, b1_ref, b2_ref, o_ref, s_ref, *, tq):
    t0 = pl.program_id(1) * tq
    q = q_ref[...]
    tiles = []
    row0 = 0
    for band_ref, (_, dil) in zip((b0_ref, b1_ref, b2_ref), DILATED_PAIRS):
        span = DIL_STEPS * dil
        start = jnp.maximum(t0 - span, 0)
        off = span - (t0 - start)
        for j in range((span + tq) // DIL_ROWS):
            ks = pl.multiple_of(start + j * DIL_ROWS, LANE)
            bs = pl.multiple_of(off + j * DIL_ROWS, LANE)
            s_ref[row0:row0 + DIL_ROWS, :] = (_dot_nt(k_ref[pl.ds(ks, DIL_ROWS), :], q)
                                              + band_ref[0, pl.ds(bs, DIL_ROWS), :])
            tiles.append((ks, row0))
            row0 += DIL_ROWS
    m = jnp.full((1, tq), NEG, F32)
    l = jnp.zeros((1, tq), F32)
    acc = jnp.zeros((HEAD_DIM, tq), F32)
    for ks, r0 in tiles:
        s = s_ref[r0:r0 + DIL_ROWS, :]
        m_new = jnp.maximum(m, jnp.max(s, axis=0, keepdims=True))
        alpha = jnp.exp2(m - m_new)
        p = jnp.exp2(s - m_new)
        l = alpha * l + jnp.sum(p, axis=0, keepdims=True)
        acc = alpha * acc + _dot(vt_ref[:, pl.ds(ks, DIL_ROWS)], p.astype(BF16))
        m = m_new
    o_ref[...] = (acc / l).T.astype(o_ref.dtype)


def dilated_attention(qk, vt, bands, vt_row0, tq=256):
    S = qk.shape[0]
    vh0 = vt_row0 // HEAD_DIM
    lengths = [DIL_STEPS * dil + tq for _, dil in DILATED_PAIRS]
    assert all(n % DIL_ROWS == 0 and n <= S for n in lengths)
    return pl.pallas_call(
        functools.partial(_dilated_kernel, tq=tq),
        grid=(B_HEADS, S // tq),
        in_specs=[pl.BlockSpec((tq, LANE), lambda h, i: (i, h)),
                  pl.BlockSpec((S, LANE), lambda h, i: (0, B_HEADS + h)),
                  pl.BlockSpec((HEAD_DIM, S), lambda h, i: (vh0 + h, 0))]
                 + [pl.BlockSpec((1,) + b.shape[1:], lambda h, i: (h, 0, 0)) for b in bands],
        out_specs=pl.BlockSpec((tq, LANE), lambda h, i: (i, h)),
        out_shape=jax.ShapeDtypeStruct((S, B_HEADS * HEAD_DIM), BF16),
        scratch_shapes=[pltpu.VMEM((sum(lengths), tq), F32)],
        compiler_params=_cparams(("parallel", "arbitrary")),
        name="dilated_attention",
    )(qk, qk, vt, *bands)


def _diff_kernel(lam_ref, g_ref, q_ref, k_ref, vt_ref, band_ref, o_ref, acc_ref, m_ref, l_ref, s_ref,
                 *, tq, tk, c0, lam_init):
    i = pl.program_id(1)
    t0 = i * tq
    nchunk = (t0 + tq + tk - 1) // tk
    last_chunk = k_ref.shape[0] // tk - 1
    q = q_ref[...]
    lane = lax.broadcasted_iota(jnp.int32, q.shape, 1)
    zero = jnp.zeros_like(q)
    q2 = jnp.concatenate([jnp.where(lane < C_HALF, q, zero), jnp.where(lane >= C_HALF, q, zero)], axis=0)

    m_ref[...] = jnp.full(m_ref.shape, NEG, F32)
    l_ref[...] = jnp.zeros(l_ref.shape, F32)
    acc_ref[...] = jnp.zeros(acc_ref.shape, F32)

    def logits(c):
        s0 = pl.multiple_of(jnp.minimum(c, last_chunk) * tk, tk)
        off = pl.multiple_of(c0 - jnp.clip(t0 - c * tk, -tk, c0), LANE)
        bias = band_ref[0, pl.ds(off, tk), :]
        return _dot_nt(k_ref[pl.ds(s0, tk), :], q2) + jnp.concatenate([bias, bias], axis=1)

    def consume(c, s):
        s0 = pl.multiple_of(jnp.minimum(c, last_chunk) * tk, tk)
        m_prev = m_ref[...]
        m_new = jnp.maximum(m_prev, jnp.max(s, axis=0, keepdims=True))
        alpha = jnp.exp2(m_prev - m_new)
        p = jnp.exp2(s - m_new)
        l_ref[...] = alpha * l_ref[...] + jnp.sum(p, axis=0, keepdims=True)
        acc_ref[...] = alpha * acc_ref[...] + _dot(vt_ref[:, pl.ds(s0, tk)], p.astype(BF16))
        m_ref[...] = m_new

    s_ref[0] = logits(0)

    def chunk_pair(cc, carry):
        c = 2 * cc
        s_ref[1] = logits(c + 1)
        consume(c, s_ref[0])
        s_ref[0] = logits(c + 2)
        consume(c + 1, s_ref[1])
        return carry

    lax.fori_loop(0, (nchunk + 1) // 2, chunk_pair, 0)

    lp = lam_ref[...]
    lam = (jnp.exp(jnp.sum(lp[0:1] * lp[1:2], axis=1, keepdims=True))
           - jnp.exp(jnp.sum(lp[2:3] * lp[3:4], axis=1, keepdims=True)) + lam_init)
    o = acc_ref[...] / l_ref[...]
    o = o[:, :tq] - lam * o[:, tq:]
    ms = jnp.mean(o * o, axis=0, keepdims=True)
    o = o * lax.rsqrt(ms + RMS_EPS) * (g_ref[...] * (1.0 - lam_init))
    o_ref[...] = o.T.astype(o_ref.dtype)


def diff_attention(qk, vt, band, lam_params, subln_g, lam_init, vt_row0, tq=256, tk=512):
    S = qk.shape[0]
    vh0 = vt_row0 // HEAD_DIM
    return pl.pallas_call(
        functools.partial(_diff_kernel, tq=tq, tk=tk, c0=_band_c0(tk), lam_init=lam_init),
        grid=(C_HEADS, S // tq),
        in_specs=[pl.BlockSpec((4, C_HALF), lambda h, i: (0, 0)),
                  pl.BlockSpec((HEAD_DIM, 1), lambda h, i: (0, 0)),
                  pl.BlockSpec((tq, LANE), lambda h, i: (i, h)),
                  pl.BlockSpec((S, LANE), lambda h, i: (0, C_HEADS + h)),
                  pl.BlockSpec((HEAD_DIM, S), lambda h, i: (vh0 + h, 0)),
                  pl.BlockSpec((1,) + band.shape[1:], lambda h, i: (h, 0, 0))],
        out_specs=pl.BlockSpec((tq, LANE), lambda h, i: (i, h)),
        out_shape=jax.ShapeDtypeStruct((S, C_HEADS * HEAD_DIM), BF16),
        scratch_shapes=[pltpu.VMEM((HEAD_DIM, 2 * tq), F32),
                        pltpu.VMEM((1, 2 * tq), F32),
                        pltpu.VMEM((1, 2 * tq), F32),
                        pltpu.VMEM((2, tk, 2 * tq), F32)],
        compiler_params=_cparams(("parallel", "arbitrary")),
        name="diff_attention",
    )(lam_params, subln_g.reshape(HEAD_DIM, 1), qk, qk, vt, band)


def _outproj_kernel(x_ref, a_ref, b_ref, c_ref, wa_ref, wb_ref, wc_ref, g_ref, o_ref, h_ref):
    y = (x_ref[...] + _dot(a_ref[...], wa_ref[...]) + _dot(b_ref[...], wb_ref[...])
         + _dot(c_ref[...], wc_ref[...]))
    o_ref[...] = y
    ms = jnp.mean(y * y, axis=-1, keepdims=True)
    h_ref[...] = (y * lax.rsqrt(ms + RMS_EPS) * g_ref[...]).astype(h_ref.dtype)


def out_projection(x, oa, ob, oc, wa, wb, wc, g, tm=512):
    S, D = x.shape
    once = pl.Buffered(1)
    row = lambda i: (i, 0)
    fixed = lambda i: (0, 0)
    return pl.pallas_call(
        _outproj_kernel,
        grid=(S // tm,),
        in_specs=[pl.BlockSpec((tm, D), row),
                  pl.BlockSpec((tm, oa.shape[1]), row),
                  pl.BlockSpec((tm, ob.shape[1]), row),
                  pl.BlockSpec((tm, oc.shape[1]), row),
                  pl.BlockSpec(wa.shape, fixed, pipeline_mode=once),
                  pl.BlockSpec(wb.shape, fixed, pipeline_mode=once),
                  pl.BlockSpec(wc.shape, fixed, pipeline_mode=once),
                  pl.BlockSpec((1, D), fixed)],
        out_specs=[pl.BlockSpec((tm, D), row), pl.BlockSpec((tm, D), row)],
        out_shape=[jax.ShapeDtypeStruct((S, D), F32), jax.ShapeDtypeStruct((S, D), BF16)],
        compiler_params=_cparams(("parallel",)),
        name="out_projection",
    )(x, oa, ob, oc, wa, wb, wc, g.reshape(1, D))


def _silu(x):
    return x / (1.0 + jnp.exp(-x))


def _ffn_kernel(x_ref, h_ref, wg_ref, wu_ref, wd_ref, g_ref, o_ref, hn_ref):
    f = pl.program_id(1)

    @pl.when(f == 0)
    def _():
        o_ref[...] = x_ref[...]

    h = h_ref[...]
    a = _silu(_dot(h, wg_ref[...].astype(BF16))) * _dot(h, wu_ref[...].astype(BF16))
    o_ref[...] += _dot(a.astype(BF16), wd_ref[...].astype(BF16))

    @pl.when(f == pl.num_programs(1) - 1)
    def _():
        y = o_ref[...]
        ms = jnp.mean(y * y, axis=-1, keepdims=True)
        hn_ref[...] = (y * lax.rsqrt(ms + RMS_EPS) * g_ref[...]).astype(hn_ref.dtype)


def ffn_dense(x, h, wg, wu, wd, g_next, tm=1024, tf=256):
    S, D = x.shape
    F = wg.shape[1]
    once = pl.Buffered(1)
    row = lambda i, f: (i, 0)
    return pl.pallas_call(
        _ffn_kernel,
        grid=(S // tm, F // tf),
        in_specs=[pl.BlockSpec((tm, D), row, pipeline_mode=once),
                  pl.BlockSpec((tm, D), row, pipeline_mode=once),
                  pl.BlockSpec((D, tf), lambda i, f: (0, f)),
                  pl.BlockSpec((D, tf), lambda i, f: (0, f)),
                  pl.BlockSpec((tf, D), lambda i, f: (f, 0)),
                  pl.BlockSpec((1, D), lambda i, f: (0, 0))],
        out_specs=[pl.BlockSpec((tm, D), row), pl.BlockSpec((tm, D), row)],
        out_shape=[jax.ShapeDtypeStruct((S, D), F32), jax.ShapeDtypeStruct((S, D), BF16)],
        compiler_params=_cparams(("parallel", "arbitrary")),
        name="ffn_dense",
    )(x, h, wg, wu, wd, g_next.reshape(1, D))


def _router_kernel(x_ref, g_ref, w_ref, c_ref, sel_ref):
    x = x_ref[...]
    ms = jnp.mean(x * x, axis=-1, keepdims=True)
    h = x * lax.rsqrt(ms + RMS_EPS) * g_ref[...]
    logits = jnp.dot(h, w_ref[...], preferred_element_type=F32, precision=lax.Precision.HIGHEST)
    lane = lax.broadcasted_iota(jnp.int32, logits.shape, 1).astype(F32)
    logits = jnp.where(lane < N_EXPERTS, logits, NEG)
    m1 = jnp.max(logits, axis=1, keepdims=True)
    i1 = jnp.min(jnp.where(logits == m1, lane, float(LANE)), axis=1, keepdims=True)
    rest = jnp.where(lane == i1, NEG, logits)
    m2 = jnp.max(rest, axis=1, keepdims=True)
    i2 = jnp.min(jnp.where(rest == m2, lane, float(LANE)), axis=1, keepdims=True)
    e2 = jnp.exp(m2 - m1)
    g1 = 1.0 / (1.0 + e2)
    g2 = e2 / (1.0 + e2)
    pick1, pick2 = lane == i1, lane == i2
    c_ref[...] = jnp.where(pick1, g1, 0.0) + jnp.where(pick2, g2, 0.0)
    sel_ref[...] = jnp.where(pick1 | pick2, 1.0, 0.0)


def moe_router(x, g, w_router, tm=512):
    S, D = x.shape
    wpad = jnp.zeros((D, LANE), F32).at[:, :N_EXPERTS].set(w_router)
    spec = pl.BlockSpec((tm, LANE), lambda i: (i, 0))
    sds = jax.ShapeDtypeStruct((S, LANE), F32)
    return pl.pallas_call(
        _router_kernel,
        grid=(S // tm,),
        in_specs=[pl.BlockSpec((tm, D), lambda i: (i, 0)),
                  pl.BlockSpec((1, D), lambda i: (0, 0)),
                  pl.BlockSpec((D, LANE), lambda i: (0, 0))],
        out_specs=[spec, spec],
        out_shape=[sds, sds],
        compiler_params=_cparams(("parallel",)),
        name="moe_router",
    )(x, g.reshape(1, D), wpad)


MOE_CH = 256
MOE_TILE = 1024


def _moe_rank_kernel(sel_ref, rm_ref, rmt_ref, cum_ref, tot_ref, carry_ref):
    c = pl.program_id(0)

    @pl.when(c == 0)
    def _():
        carry_ref[...] = jnp.zeros(carry_ref.shape, F32)

    sel = sel_ref[...]
    ch = sel.shape[0]
    before = (lax.broadcasted_iota(jnp.int32, (ch, ch), 1) < lax.broadcasted_iota(jnp.int32, (ch, ch), 0))
    rank = _dot(jnp.where(before, 1.0, 0.0).astype(BF16), sel.astype(BF16)) + carry_ref[...]
    rm = jnp.where(sel > 0.0, rank, -1.0)
    rm_ref[...] = rm
    rmt_ref[...] = rm.T[:N_EXPERTS]
    cum_ref[0] = carry_ref[...]
    carry_ref[...] += jnp.sum(sel, axis=0, keepdims=True)
    tot_ref[...] = carry_ref[...]


def moe_rank(sel):
    S = sel.shape[0]
    nch = S // MOE_CH
    return pl.pallas_call(
        _moe_rank_kernel,
        grid=(nch,),
        in_specs=[pl.BlockSpec((MOE_CH, LANE), lambda c: (c, 0))],
        out_specs=[pl.BlockSpec((MOE_CH, LANE), lambda c: (c, 0)),
                   pl.BlockSpec((N_EXPERTS, MOE_CH), lambda c: (0, c)),
                   pl.BlockSpec((1, 1, LANE), lambda c: (c, 0, 0)),
                   pl.BlockSpec((1, LANE), lambda c: (0, 0))],
        out_shape=[jax.ShapeDtypeStruct((S, LANE), F32),
                   jax.ShapeDtypeStruct((N_EXPERTS, S), F32),
                   jax.ShapeDtypeStruct((nch, 1, LANE), F32),
                   jax.ShapeDtypeStruct((1, LANE), F32)],
        scratch_shapes=[pltpu.VMEM((1, LANE), F32)],
        compiler_params=_cparams(("arbitrary",)),
        name="moe_rank",
    )(sel)


def _moe_expert_kernel(te_ref, tr0_ref, tclo_ref, tchi_ref, tval_ref, cumx_ref,
                       rk_ref, h_hbm, wg_ref, wu_ref, wd_ref, y_ref,
                       hs_ref, acc_ref, hbuf_ref, sem_ref, *, n_experts):
    j = pl.program_id(0)
    f = pl.program_id(1)
    nf = pl.num_programs(1)
    nrows = tval_ref[j]
    tile = hs_ref.shape[0]
    win = MOE_CH + 8

    def chunk_copy(c, slot):
        return pltpu.make_async_copy(h_hbm.at[pl.ds(c * MOE_CH, MOE_CH)], hbuf_ref.at[slot], sem_ref.at[slot])

    @pl.when((nrows > 0) & (f == 0))
    def _():
        clo, chi = tclo_ref[j], tchi_ref[j]
        tr0 = tr0_ref[j]
        wrow = lax.broadcasted_iota(jnp.int32, (win, MOE_CH), 0)
        acc_ref[...] = jnp.zeros(acc_ref.shape, F32)

        @pl.when(chi > clo)
        def _():
            chunk_copy(clo, 0).start()

        def body(c, carry):
            slot = (c - clo) % 2

            @pl.when(c + 1 < chi)
            def _():
                chunk_copy(c + 1, 1 - slot).start()

            chunk_copy(c, slot).wait()
            first = cumx_ref[c * n_experts + te_ref[j]] - tr0
            w0 = pl.multiple_of(jnp.clip((first // 8) * 8, 0, tile - win), 8)
            rk = rk_ref[0, :, pl.ds(pl.multiple_of(c * MOE_CH, MOE_CH), MOE_CH)]
            onehot = jnp.where(rk == (tr0 + w0 + wrow).astype(F32), 1.0, 0.0).astype(BF16)
            acc_ref[pl.ds(w0, win), :] += _dot(onehot, hbuf_ref[slot])
            return carry

        lax.fori_loop(clo, chi, body, 0)
        hs_ref[...] = acc_ref[...].astype(BF16)
        acc_ref[...] = jnp.zeros(acc_ref.shape, F32)

    def swiglu(rows):
        hs = hs_ref[:rows]
        a = _silu(_dot(hs, wg_ref[0].astype(BF16))) * _dot(hs, wu_ref[0].astype(BF16))
        acc_ref[:rows] += _dot(a.astype(BF16), wd_ref[0].astype(BF16))

    sizes = (tile, tile // 2, tile // 4, tile // 8)
    for rows, below in zip(sizes, sizes[1:] + (0,)):
        @pl.when((nrows > below) & (nrows <= rows))
        def _():
            swiglu(rows)

    @pl.when(f == nf - 1)
    def _():
        y_ref[...] = jnp.where(nrows > 0, acc_ref[...], 0.0).astype(y_ref.dtype)


def _moe_combine_kernel(ca_ref, cb_ref, roff_ref, x_ref, rm_ref, cw_ref, *refs):
    y_refs, o_ref = refs[:-1], refs[-1]
    n_experts = len(y_refs) // 2
    c = pl.program_id(0)
    rm = rm_ref[...]
    cw = cw_ref[...]
    lane = lax.broadcasted_iota(jnp.int32, rm.shape, 1)
    cr = y_refs[0].shape[0]
    col = lax.broadcasted_iota(jnp.int32, (rm.shape[0], cr), 1)
    out = x_ref[...]
    for e in range(n_experts):
        mine = lane == e
        pos = jnp.sum(jnp.where(mine, rm, 0.0), axis=1, keepdims=True)
        gate = jnp.sum(jnp.where(mine, cw, 0.0), axis=1, keepdims=True)
        row = jnp.where(pos >= 0.0, pos + roff_ref[e].astype(F32), -1.0)
        ca, cb = ca_ref[c * n_experts + e], cb_ref[c * n_experts + e]
        for y_ref, base in ((y_refs[2 * e], ca * cr),
                            (y_refs[2 * e + 1], jnp.where(cb != ca, cb * cr, -2 * cr))):
            onehot = jnp.where(row == (base + col).astype(F32), 1.0, 0.0).astype(BF16)
            out = out + gate * _dot(onehot, y_ref[...])
    o_ref[...] = out


def ffn_moe(x, h, combine, sel, wg, wu, wd, tf=512):
    S, D = x.shape
    E, _, F = wg.shape
    tf = min(tf, F)
    T = MOE_TILE
    nch = S // MOE_CH
    J = 2 * S // T + E
    rm, rmt, cum, tot = moe_rank(sel)

    cnt = tot[0, :E].astype(jnp.int32)
    ntile = (cnt + T - 1) // T
    tend = jnp.cumsum(ntile)
    tstart = tend - ntile
    jj = jnp.arange(J, dtype=jnp.int32)
    te = jnp.minimum(jnp.searchsorted(tend, jnp.minimum(jj, tend[-1] - 1), side="right"), E - 1).astype(jnp.int32)
    tr0 = (jnp.minimum(jj, tend[-1] - 1) - tstart[te]) * T
    tval = jnp.where(jj < tend[-1], jnp.clip(cnt[te] - tr0, 0, T), 0).astype(jnp.int32)
    cumx = cum[:, 0, :E].astype(jnp.int32)
    cumi = jnp.concatenate([cumx[1:], cnt[None]], axis=0)
    tclo = jnp.sum(cumi[:, te] <= tr0[None, :], axis=0).astype(jnp.int32)
    tchi = jnp.sum(cumx[:, te] < (tr0 + tval)[None, :], axis=0).astype(jnp.int32)
    roff = (tstart * T).astype(jnp.int32)
    nrow_chunks = J * T // MOE_CH
    ca = ((roff[None, :] + cumx) // MOE_CH).astype(jnp.int32)
    cb = jnp.minimum(ca + 1, nrow_chunks - 1)
    ca, cb = ca.reshape(-1), cb.reshape(-1)

    nf = F // tf

    def wmap(j, f, te_r, tr0_r, clo_r, chi_r, val_r, cumx_r):
        return (te_r[j], 0, jnp.where(val_r[j] > 0, f, nf - 1))

    def wdmap(j, f, te_r, tr0_r, clo_r, chi_r, val_r, cumx_r):
        return (te_r[j], jnp.where(val_r[j] > 0, f, nf - 1), 0)

    y = pl.pallas_call(
        functools.partial(_moe_expert_kernel, n_experts=E),
        grid_spec=pltpu.PrefetchScalarGridSpec(
            num_scalar_prefetch=6,
            grid=(J, nf),
            in_specs=[pl.BlockSpec((1, 1, S), lambda j, f, te_r, *_: (te_r[j], 0, 0)),
                      pl.BlockSpec(memory_space=pl.ANY),
                      pl.BlockSpec((1, D, tf), wmap),
                      pl.BlockSpec((1, D, tf), wmap),
                      pl.BlockSpec((1, tf, D), wdmap)],
            out_specs=pl.BlockSpec((T, D), lambda j, f, *_: (j, 0)),
            scratch_shapes=[pltpu.VMEM((T, D), BF16),
                            pltpu.VMEM((T, D), F32),
                            pltpu.VMEM((2, MOE_CH, D), BF16),
                            pltpu.SemaphoreType.DMA((2,))]),
        out_shape=jax.ShapeDtypeStruct((J * T, D), BF16),
        compiler_params=_cparams(("arbitrary", "arbitrary")),
        name="moe_experts",
    )(te, tr0, tclo, tchi, tval, cumx.reshape(-1), rmt.reshape(E, 1, S), h, wg, wu, wd)

    def chunk_map(c, ca_r, cb_r, ro_r, *, e, second):
        return ((cb_r if second else ca_r)[c * E + e], 0)

    return pl.pallas_call(
        _moe_combine_kernel,
        grid_spec=pltpu.PrefetchScalarGridSpec(
            num_scalar_prefetch=3,
            grid=(nch,),
            in_specs=[pl.BlockSpec((MOE_CH, D), lambda c, *_: (c, 0)),
                      pl.BlockSpec((MOE_CH, LANE), lambda c, *_: (c, 0)),
                      pl.BlockSpec((MOE_CH, LANE), lambda c, *_: (c, 0))]
                     + [pl.BlockSpec((MOE_CH, D), functools.partial(chunk_map, e=e, second=second))
                        for e in range(E) for second in (False, True)],
            out_specs=pl.BlockSpec((MOE_CH, D), lambda c, *_: (c, 0))),
        out_shape=jax.ShapeDtypeStruct((S, D), F32),
        compiler_params=_cparams(("parallel",)),
        name="moe_combine",
    )(ca, cb, roff, x, rm, combine, *([y] * (2 * E)))


def _tile_gain(g, reps, scale=1.0):
    return jnp.tile(g.astype(F32) * scale, reps)


def kernel(x, w_in, w_out, attn_norm_g, ffn_norm_g, q_norm_a, k_norm_a, q_norm_b, k_norm_b,
           q_norm_c, k_norm_c, lambda_q1, lambda_k1, lambda_q2, lambda_k2, diff_subln_g, rel_bias,
           w_dense_gate, w_dense_up, w_dense_down, w_router, w_moe_gate, w_moe_up, w_moe_down):
    B, S, D = x.shape
    depth = w_in.shape[0]
    assert B == 1 and S % (DIL_STEPS * DILATED_PAIRS[-1][1]) == 0
    aw, bw, cw = A_HEADS * HEAD_DIM, B_HEADS * HEAD_DIM, C_HEADS * HEAD_DIM
    iw = IDX_HEADS * IDX_DIM
    sizes = (aw, aw, aw, iw, IDX_DIM, IDX_HEADS, bw, bw, bw, cw, cw, cw)
    offs = np.concatenate([[0], np.cumsum(sizes)]).tolist()

    def cols(w, *segs):
        return jnp.concatenate([w[:, offs[s]:offs[s + 1]] for s in segs], axis=1)

    tq, tk = 256, 512
    band_a = bias_band(rel_bias, A_HEADS, 0, _band_c0(tk) + 2 * tk, tq, _band_c0(tk))
    band_c = bias_band(rel_bias, C_HEADS, A_HEADS + B_HEADS, _band_c0(tk) + 2 * tk, tq, _band_c0(tk))
    bands_b = [bias_band(rel_bias, B_HEADS, A_HEADS, 2 * DIL_STEPS * dil + tq, tq, DIL_STEPS * dil, dil)
               for _, dil in DILATED_PAIRS]

    xs = x.reshape(S, D)
    h = None
    for layer in range(depth):
        wl = w_in[layer]
        w_pa = cols(wl, 0, 1).astype(BF16)
        w_pb = cols(wl, 6, 7).astype(BF16)
        w_pc = cols(wl, 9, 10).astype(BF16)
        w_pi = cols(wl, 3, 4, 4).astype(BF16)
        w_vt = cols(wl, 2, 8, 11).T.astype(BF16)
        w_wt = cols(wl, 5).T.astype(BF16)
        qs_ab, qs_c = HEAD_DIM ** -0.5 * LOG2E, C_HALF ** -0.5 * LOG2E
        g_pa = jnp.concatenate([_tile_gain(q_norm_a[layer], A_HEADS, qs_ab), _tile_gain(k_norm_a[layer], A_HEADS)])
        g_pb = jnp.concatenate([_tile_gain(q_norm_b[layer], B_HEADS, qs_ab), _tile_gain(k_norm_b[layer], B_HEADS)])
        g_pc = jnp.concatenate([_tile_gain(q_norm_c[layer], 2 * C_HEADS, qs_c),
                                _tile_gain(k_norm_c[layer], 2 * C_HEADS)])

        if h is None:
            h = rmsnorm(xs, attn_norm_g[layer])
        p_a = project(h, w_pa, g_pa, HEAD_DIM, BF16, tn=512)
        p_b = project(h, w_pb, g_pb, HEAD_DIM, BF16, tn=512)
        p_c = project(h, w_pc, g_pc, C_HALF, BF16, tn=512)
        p_i = project(h, w_pi, jnp.ones((w_pi.shape[1],), F32), 0, BF16, tn=w_pi.shape[1])
        vt = project_t(h, w_vt, BF16, tn=512)
        wt = project_t(h, w_wt, F32)

        oa = dsa_attention(p_a, vt, p_i, wt, band_a)

        ob = dilated_attention(p_b, vt, bands_b, aw, tq=tq)

        lam_init = 0.8 - 0.6 * math.exp(-0.3 * layer)
        lam_params = jnp.stack([lambda_q1[layer], lambda_k1[layer], lambda_q2[layer], lambda_k2[layer]]).astype(F32)
        oc = diff_attention(p_c, vt, band_c, lam_params, diff_subln_g[layer].astype(F32), lam_init, aw + bw,
                            tq=tq, tk=tk)

        wo = w_out[layer].astype(BF16)
        xs, h2 = out_projection(xs, oa, ob, oc, wo[:aw], wo[aw:aw + bw], wo[aw + bw:], ffn_norm_g[layer])

        f = layer // 2
        if layer % 2 == 0:
            g_next = attn_norm_g[min(layer + 1, depth - 1)]
            xs, h = ffn_dense(xs, h2, w_dense_gate[f], w_dense_up[f], w_dense_down[f], g_next)
        else:
            combine, sel = moe_router(xs, ffn_norm_g[layer], w_router[f])
            xs = ffn_moe(xs, h2, combine, sel, w_moe_gate[f], w_moe_up[f], w_moe_down[f])
            h = None
    return xs.reshape(B, S, D)
```

```python
import functools
import math

import jax
import jax.numpy as jnp
import numpy as np
from jax import lax
from jax.experimental import pallas as pl
from jax.experimental.pallas import tpu as pltpu

F32 = jnp.float32
BF16 = jnp.bfloat16

HEAD_DIM = 128
A_HEADS, B_HEADS, C_HEADS = 4, 6, 6
C_HALF = HEAD_DIM // 2
IDX_HEADS, IDX_DIM = 16, 64
TOPK_MAX = 256
DILATED_PAIRS = ((128, 1), (512, 4), (2048, 16))
DIL_STEPS = 128
N_BUCKETS = 32
MAX_DISTANCE = 2048
N_EXPERTS = 8
RMS_EPS = 1e-6
NEG = -1e30
LOG2E = math.log2(math.e)
INT_MIN = -(2 ** 31)
I16_BIAS = 2 ** 15
CNT_ROWS = 64
LANE = 128
VMEM_LIMIT = 56 * 1024 * 1024

BUCKET_EDGES = tuple(range(16)) + (16,) + tuple(
    int(math.ceil(16.0 * (MAX_DISTANCE / 16.0) ** (k / 16.0))) for k in range(1, 16))
FAR_DIST = BUCKET_EDGES[-1]


def _cparams(sem):
    return pltpu.CompilerParams(dimension_semantics=sem, vmem_limit_bytes=VMEM_LIMIT)


def _dot(a, b):
    return jnp.dot(a, b, preferred_element_type=F32)


def _dot_nt(a, b):
    return lax.dot_general(a, b, (((1,), (1,)), ((), ())), preferred_element_type=F32)


def _rmsnorm_kernel(x_ref, g_ref, o_ref):
    x = x_ref[...]
    ms = jnp.mean(x * x, axis=-1, keepdims=True)
    o_ref[...] = (x * lax.rsqrt(ms + RMS_EPS) * g_ref[...]).astype(o_ref.dtype)


def rmsnorm(x, g, tm=512):
    S, D = x.shape
    return pl.pallas_call(
        _rmsnorm_kernel,
        grid=(S // tm,),
        in_specs=[pl.BlockSpec((tm, D), lambda i: (i, 0)), pl.BlockSpec((1, D), lambda i: (0, 0))],
        out_specs=pl.BlockSpec((tm, D), lambda i: (i, 0)),
        out_shape=jax.ShapeDtypeStruct((S, D), BF16),
        compiler_params=_cparams(("parallel",)),
        name="rmsnorm",
    )(x, g.reshape(1, D))


def _proj_kernel(h_ref, w_ref, g_ref, o_ref, *, group):
    acc = _dot(h_ref[...], w_ref[...])
    tn = acc.shape[1]
    if group == 0:
        o_ref[...] = acc.astype(o_ref.dtype)
        return
    for c in range(tn // LANE):
        a = acc[:, c * LANE:(c + 1) * LANE]
        sq = a * a
        if group == LANE:
            ms = jnp.mean(sq, axis=-1, keepdims=True)
        else:
            lane = lax.broadcasted_iota(jnp.int32, sq.shape, 1)
            lo = jnp.sum(jnp.where(lane < group, sq, 0.0), axis=-1, keepdims=True)
            hi = jnp.sum(jnp.where(lane >= group, sq, 0.0), axis=-1, keepdims=True)
            ms = jnp.where(lane < group, lo, hi) * (1.0 / group)
        y = a * lax.rsqrt(ms + RMS_EPS) * g_ref[:, c * LANE:(c + 1) * LANE]
        o_ref[:, c * LANE:(c + 1) * LANE] = y.astype(o_ref.dtype)


def project(h, w, gains, group, out_dtype, tm=1024, tn=256):
    S, K = h.shape
    N = w.shape[1]
    return pl.pallas_call(
        functools.partial(_proj_kernel, group=group),
        grid=(S // tm, N // tn),
        in_specs=[pl.BlockSpec((tm, K), lambda i, j: (i, 0)),
                  pl.BlockSpec((K, tn), lambda i, j: (0, j)),
                  pl.BlockSpec((1, tn), lambda i, j: (0, j))],
        out_specs=pl.BlockSpec((tm, tn), lambda i, j: (i, j)),
        out_shape=jax.ShapeDtypeStruct((S, N), out_dtype),
        compiler_params=_cparams(("parallel", "arbitrary")),
        name=f"proj_g{group}",
    )(h, w, gains.reshape(1, N))


def _proj_t_kernel(wt_ref, h_ref, o_ref):
    o_ref[...] = _dot_nt(wt_ref[...], h_ref[...]).astype(o_ref.dtype)


def project_t(h, wt, out_dtype, tm=1024, tn=256):
    S, K = h.shape
    N = wt.shape[0]
    tn = min(tn, N)
    return pl.pallas_call(
        _proj_t_kernel,
        grid=(S // tm, N // tn),
        in_specs=[pl.BlockSpec((tn, K), lambda i, j: (j, 0)),
                  pl.BlockSpec((tm, K), lambda i, j: (i, 0))],
        out_specs=pl.BlockSpec((tn, tm), lambda i, j: (j, i)),
        out_shape=jax.ShapeDtypeStruct((N, S), out_dtype),
        compiler_params=_cparams(("parallel", "arbitrary")),
        name="proj_t",
    )(wt, h)


BAND_ROWS = 256


def _band_kernel(tab_ref, o_ref, *, c0, head_off, dil):
    hh = head_off + pl.program_id(0)
    _, rows, width = o_ref.shape
    base = (lax.broadcasted_iota(jnp.int32, (BAND_ROWS, width), 1)
            - lax.broadcasted_iota(jnp.int32, (BAND_ROWS, width), 0))
    hi_valid = DIL_STEPS * dil if dil else None
    for r in range(rows // BAND_ROWS):
        d0 = c0 - r * BAND_ROWS
        lo, hi = d0 - (BAND_ROWS - 1), d0 + width - 1
        if hi_valid is not None:
            hi = min(hi, hi_valid)
        block = pl.ds(r * BAND_ROWS, BAND_ROWS)
        if hi < 0 or hi < lo:
            o_ref[0, block, :] = jnp.full((BAND_ROWS, width), NEG, F32)
            continue
        dist = base + d0
        first = max(b for b in range(N_BUCKETS) if BUCKET_EDGES[b] <= max(lo, 0))
        val = jnp.full((BAND_ROWS, width), tab_ref[first, hh] * LOG2E, F32)
        for b in range(first + 1, N_BUCKETS):
            if BUCKET_EDGES[b] <= hi:
                val = jnp.where(dist >= BUCKET_EDGES[b], tab_ref[b, hh] * LOG2E, val)
        ok = None
        if lo < 0:
            ok = dist >= 0
        if dil:
            in_win = (dist <= hi_valid) & ((dist & (dil - 1)) == 0)
            ok = in_win if ok is None else ok & in_win
        o_ref[0, block, :] = val if ok is None else jnp.where(ok, val, NEG)


def bias_band(rel_bias, n_heads, head_off, rows, width, c0, dil=0):
    assert rows % BAND_ROWS == 0 and dil & (dil - 1) == 0
    return pl.pallas_call(
        functools.partial(_band_kernel, c0=c0, head_off=head_off, dil=dil),
        grid=(n_heads,),
        in_specs=[pl.BlockSpec(memory_space=pltpu.SMEM)],
        out_specs=pl.BlockSpec((1, rows, width), lambda h: (h, 0, 0)),
        out_shape=jax.ShapeDtypeStruct((n_heads, rows, width), F32),
        compiler_params=_cparams(("parallel",)),
        name=f"bias_band_d{dil}",
    )(rel_bias)


def _band_c0(tk):
    return -(-(FAR_DIST + tk - 1) // LANE) * LANE


def _dsa_kernel(q_ref, qi_ref, wt_ref, k_ref, vt_ref, kidx_ref, band_ref, o_ref,
                hi_ref, lo_ref, cut_ref, qih_ref, s_ref, acc_ref, m_ref, l_ref, *, tq, kc, topk, c0):
    i = pl.program_id(0)
    t0 = i * tq
    nchunk = (t0 + tq + kc - 1) // kc
    half = lax.broadcasted_iota(jnp.int32, (tq, LANE), 1) < IDX_DIM

    for h in range(IDX_HEADS):
        slab = qi_ref[:, (h // 2) * LANE:(h // 2 + 1) * LANE]
        keep = half if h % 2 == 0 else jnp.logical_not(half)
        qih_ref[h] = jnp.where(keep, slab, jnp.zeros_like(slab))
    w = wt_ref[...] * (IDX_HEADS ** -0.5 * IDX_DIM ** -0.5)

    pos_q = t0 + lax.broadcasted_iota(jnp.int32, (kc, tq), 1)
    row = lax.broadcasted_iota(jnp.int32, (kc, tq), 0)

    def score_chunk(c, carry):
        s0 = pl.multiple_of(c * kc, kc)
        kx = kidx_ref[pl.ds(s0, kc), :]
        acc = jnp.zeros((kc, tq), F32)
        for h in range(IDX_HEADS):
            acc = acc + w[h:h + 1, :] * jnp.maximum(_dot_nt(kx, qih_ref[h]), 0.0)
        bits = lax.bitcast_convert_type(acc, jnp.int32)
        key = bits ^ ((bits >> 31) & 0x7FFFFFFF)
        key = jnp.where(s0 + row <= pos_q, key, INT_MIN)
        hi_ref[pl.ds(s0, kc), :] = (key >> 16).astype(jnp.int16)
        lo_ref[pl.ds(s0, kc), :] = ((key & 0xFFFF) - I16_BIAS).astype(jnp.int16)
        return carry

    lax.fori_loop(0, nchunk, score_chunk, 0)

    one16, zero16 = jnp.ones((), jnp.int16), jnp.zeros((), jnp.int16)
    row16 = lax.broadcasted_iota(jnp.int32, (kc, tq), 0).astype(jnp.int16)
    rowc16 = lax.broadcasted_iota(jnp.int32, (CNT_ROWS, tq), 0).astype(jnp.int16)
    n_keys = k_ref.shape[0]

    def to16(v):
        return jnp.clip(v, -I16_BIAS, I16_BIAS - 1).astype(jnp.int16)

    def count(hit):
        def body(c, cnt):
            base = pl.multiple_of(c * kc, kc)
            for g in range(kc // CNT_ROWS):
                cnt = cnt + jnp.where(hit(base + g * CNT_ROWS), one16, zero16)
            return cnt

        cnt = lax.fori_loop(0, nchunk, body, jnp.zeros((CNT_ROWS, tq), jnp.int16))
        return jnp.sum(cnt.astype(jnp.int32).astype(F32), axis=0, keepdims=True)

    def count_ge(ref, cand):
        c16 = cand.astype(jnp.int16)
        return count(lambda r: ref[pl.ds(r, CNT_ROWS), :] >= c16)

    def count_gt(ref, cand):
        c16 = cand.astype(jnp.int16)
        return count(lambda r: ref[pl.ds(r, CNT_ROWS), :] > c16)

    def kth_largest(ref, need):
        t = jnp.where(count_ge(ref, jnp.zeros((1, tq), jnp.int32)) >= need, 0, -I16_BIAS).astype(jnp.int32)

        def bit_body(b, t):
            cand = t + jnp.left_shift(jnp.int32(1), 14 - b)
            return jnp.where(count_ge(ref, cand) >= need, cand, t)

        return lax.fori_loop(0, 15, bit_body, t)

    kf = jnp.full((1, tq), float(topk), F32)
    thi = kth_largest(hi_ref, kf)
    need_lo = kf - count_gt(hi_ref, thi)
    thi16 = thi.astype(jnp.int16)

    def mask_low(c, carry):
        rows = pl.ds(pl.multiple_of(c * kc, kc), kc)
        lo_ref[rows, :] = jnp.where(hi_ref[rows, :] == thi16, lo_ref[rows, :], jnp.int16(-I16_BIAS))
        return carry

    lax.fori_loop(0, nchunk, mask_low, 0)
    tlo = kth_largest(lo_ref, need_lo)
    tlo16 = tlo.astype(jnp.int16)

    def is_eq(r):
        return (hi_ref[pl.ds(r, CNT_ROWS), :] == thi16) & (lo_ref[pl.ds(r, CNT_ROWS), :] == tlo16)

    keep_eq = need_lo - count_gt(lo_ref, tlo)
    excess = (count(is_eq) > keep_eq) & (thi > -I16_BIAS)
    cut_ref[...] = jnp.full(cut_ref.shape, n_keys, jnp.int32)

    @pl.when(jnp.max(jnp.broadcast_to(jnp.where(excess, 1.0, 0.0), (8, tq))) > 0.0)
    def _():
        nbits = n_keys.bit_length()

        def bit_body(b, j):
            cand = j + jnp.left_shift(jnp.int32(1), nbits - 1 - b)
            before = count(lambda r: is_eq(r) & (rowc16 < to16(cand - r)))
            return jnp.where(before < keep_eq, cand, j)

        j = lax.fori_loop(0, nbits, bit_body, jnp.zeros((1, tq), jnp.int32))
        cut_ref[...] = jnp.where(excess, j, n_keys)

    cut = cut_ref[...]
    thi_sel16 = jnp.maximum(thi, 1 - I16_BIAS).astype(jnp.int16)

    m_ref[...] = jnp.full(m_ref.shape, NEG, F32)
    l_ref[...] = jnp.zeros(l_ref.shape, F32)
    acc_ref[...] = jnp.zeros(acc_ref.shape, F32)

    def logits(c):
        s0 = pl.multiple_of(jnp.minimum(c, nchunk - 1) * kc, kc)
        off = pl.multiple_of(c0 - jnp.clip(t0 - c * kc, -kc, c0), LANE)
        hi = hi_ref[pl.ds(s0, kc), :]
        lo = lo_ref[pl.ds(s0, kc), :]
        sel = (hi > thi_sel16) | ((hi == thi_sel16)
                                  & ((lo > tlo16) | ((lo == tlo16) & (row16 <= to16(cut - s0)))))
        drop = jnp.where(sel, jnp.zeros((), jnp.int16), jnp.ones((), jnp.int16))
        mask_bias = drop.astype(jnp.int32).astype(F32) * NEG
        parts = []
        for h in range(A_HEADS):
            hs = slice(h * HEAD_DIM, (h + 1) * HEAD_DIM)
            parts.append(_dot_nt(k_ref[pl.ds(s0, kc), hs], q_ref[:, hs]) + band_ref[h, pl.ds(off, kc), :]
                         + mask_bias)
        return jnp.concatenate(parts, axis=1)

    def consume(c, s):
        s0 = pl.multiple_of(jnp.minimum(c, nchunk - 1) * kc, kc)
        m_prev = m_ref[...]
        m_new = jnp.maximum(m_prev, jnp.max(s, axis=0, keepdims=True))
        alpha = jnp.exp2(m_prev - m_new)
        p = jnp.exp2(s - m_new)
        l_ref[...] = alpha * l_ref[...] + jnp.sum(p, axis=0, keepdims=True)
        p = p.astype(BF16)
        pv = [_dot(vt_ref[h * HEAD_DIM:(h + 1) * HEAD_DIM, pl.ds(s0, kc)], p[:, h * tq:(h + 1) * tq])
              for h in range(A_HEADS)]
        acc_ref[...] = alpha * acc_ref[...] + jnp.concatenate(pv, axis=1)
        m_ref[...] = m_new

    s_ref[0] = logits(0)

    def chunk_pair(cc, carry):
        c = 2 * cc
        s_ref[1] = logits(c + 1)
        consume(c, s_ref[0])
        s_ref[0] = logits(c + 2)
        consume(c + 1, s_ref[1])
        return carry

    lax.fori_loop(0, (nchunk + 1) // 2, chunk_pair, 0)
    o = acc_ref[...] / l_ref[...]
    for h in range(A_HEADS):
        o_ref[:, h * HEAD_DIM:(h + 1) * HEAD_DIM] = o[:, h * tq:(h + 1) * tq].T.astype(o_ref.dtype)


def dsa_attention(qk, vt, pidx, wt, band, tq=256, kc=512):
    S = qk.shape[0]
    topk = min(TOPK_MAX, S // 4)
    aw = A_HEADS * HEAD_DIM
    iw = IDX_HEADS * IDX_DIM
    once = pl.Buffered(1)
    return pl.pallas_call(
        functools.partial(_dsa_kernel, tq=tq, kc=kc, topk=topk, c0=_band_c0(kc)),
        grid=(S // tq,),
        in_specs=[pl.BlockSpec((tq, aw), lambda i: (i, 0)),
                  pl.BlockSpec((tq, iw), lambda i: (i, 0)),
                  pl.BlockSpec((IDX_HEADS, tq), lambda i: (0, i)),
                  pl.BlockSpec((S, aw), lambda i: (0, 1), pipeline_mode=once),
                  pl.BlockSpec((aw, S), lambda i: (0, 0), pipeline_mode=once),
                  pl.BlockSpec((S, LANE), lambda i: (0, iw // LANE), pipeline_mode=once),
                  pl.BlockSpec(band.shape, lambda i: (0, 0, 0), pipeline_mode=once)],
        out_specs=pl.BlockSpec((tq, aw), lambda i: (i, 0)),
        out_shape=jax.ShapeDtypeStruct((S, aw), BF16),
        scratch_shapes=[pltpu.VMEM((S, tq), jnp.int16),
                        pltpu.VMEM((S, tq), jnp.int16),
                        pltpu.VMEM((1, tq), jnp.int32),
                        pltpu.VMEM((IDX_HEADS, tq, LANE), BF16),
                        pltpu.VMEM((2, kc, A_HEADS * tq), F32),
                        pltpu.VMEM((HEAD_DIM, A_HEADS * tq), F32),
                        pltpu.VMEM((1, A_HEADS * tq), F32),
                        pltpu.VMEM((1, A_HEADS * tq), F32)],
        compiler_params=_cparams(("arbitrary",)),
        name="dsa_attention",
    )(qk, pidx, wt, qk, vt, pidx, band)


DIL_ROWS = 384


def _dilated_kernel(q_ref, k_ref, vt_ref, b0_ref, b1_ref, b2_ref, o_ref, s_ref, *, tq):
    t0 = pl.program_id(1) * tq
    q = q_ref[...]
    tiles = []
    row0 = 0
    for band_ref, (_, dil) in zip((b0_ref, b1_ref, b2_ref), DILATED_PAIRS):
        span = DIL_STEPS * dil
        start = jnp.maximum(t0 - span, 0)
        off = span - (t0 - start)
        for j in range((span + tq) // DIL_ROWS):
            ks = pl.multiple_of(start + j * DIL_ROWS, LANE)
            bs = pl.multiple_of(off + j * DIL_ROWS, LANE)
            s_ref[row0:row0 + DIL_ROWS, :] = (_dot_nt(k_ref[pl.ds(ks, DIL_ROWS), :], q)
                                              + band_ref[0, pl.ds(bs, DIL_ROWS), :])
            tiles.append((ks, row0))
            row0 += DIL_ROWS
    m = jnp.full((1, tq), NEG, F32)
    l = jnp.zeros((1, tq), F32)
    acc = jnp.zeros((HEAD_DIM, tq), F32)
    for ks, r0 in tiles:
        s = s_ref[r0:r0 + DIL_ROWS, :]
        m_new = jnp.maximum(m, jnp.max(s, axis=0, keepdims=True))
        alpha = jnp.exp2(m - m_new)
        p = jnp.exp2(s - m_new)
        l = alpha * l + jnp.sum(p, axis=0, keepdims=True)
        acc = alpha * acc + _dot(vt_ref[:, pl.ds(ks, DIL_ROWS)], p.astype(BF16))
        m = m_new
    o_ref[...] = (acc / l).T.astype(o_ref.dtype)


def dilated_attention(qk, vt, bands, vt_row0, tq=256):
    S = qk.shape[0]
    vh0 = vt_row0 // HEAD_DIM
    lengths = [DIL_STEPS * dil + tq for _, dil in DILATED_PAIRS]
    assert all(n % DIL_ROWS == 0 and n <= S for n in lengths)
    return pl.pallas_call(
        functools.partial(_dilated_kernel, tq=tq),
        grid=(B_HEADS, S // tq),
        in_specs=[pl.BlockSpec((tq, LANE), lambda h, i: (i, h)),
                  pl.BlockSpec((S, LANE), lambda h, i: (0, B_HEADS + h)),
                  pl.BlockSpec((HEAD_DIM, S), lambda h, i: (vh0 + h, 0))]
                 + [pl.BlockSpec((1,) + b.shape[1:], lambda h, i: (h, 0, 0)) for b in bands],
        out_specs=pl.BlockSpec((tq, LANE), lambda h, i: (i, h)),
        out_shape=jax.ShapeDtypeStruct((S, B_HEADS * HEAD_DIM), BF16),
        scratch_shapes=[pltpu.VMEM((sum(lengths), tq), F32)],
        compiler_params=_cparams(("parallel", "arbitrary")),
        name="dilated_attention",
    )(qk, qk, vt, *bands)


def _diff_kernel(lam_ref, g_ref, q_ref, k_ref, vt_ref, band_ref, o_ref, acc_ref, m_ref, l_ref, s_ref,
                 *, tq, tk, c0, lam_init):
    i = pl.program_id(1)
    t0 = i * tq
    nchunk = (t0 + tq + tk - 1) // tk
    last_chunk = k_ref.shape[0] // tk - 1
    q = q_ref[...]
    lane = lax.broadcasted_iota(jnp.int32, q.shape, 1)
    zero = jnp.zeros_like(q)
    q2 = jnp.concatenate([jnp.where(lane < C_HALF, q, zero), jnp.where(lane >= C_HALF, q, zero)], axis=0)

    m_ref[...] = jnp.full(m_ref.shape, NEG, F32)
    l_ref[...] = jnp.zeros(l_ref.shape, F32)
    acc_ref[...] = jnp.zeros(acc_ref.shape, F32)

    def logits(c):
        s0 = pl.multiple_of(jnp.minimum(c, last_chunk) * tk, tk)
        off = pl.multiple_of(c0 - jnp.clip(t0 - c * tk, -tk, c0), LANE)
        bias = band_ref[0, pl.ds(off, tk), :]
        return _dot_nt(k_ref[pl.ds(s0, tk), :], q2) + jnp.concatenate([bias, bias], axis=1)

    def consume(c, s):
        s0 = pl.multiple_of(jnp.minimum(c, last_chunk) * tk, tk)
        m_prev = m_ref[...]
        m_new = jnp.maximum(m_prev, jnp.max(s, axis=0, keepdims=True))
        alpha = jnp.exp2(m_prev - m_new)
        p = jnp.exp2(s - m_new)
        l_ref[...] = alpha * l_ref[...] + jnp.sum(p, axis=0, keepdims=True)
        acc_ref[...] = alpha * acc_ref[...] + _dot(vt_ref[:, pl.ds(s0, tk)], p.astype(BF16))
        m_ref[...] = m_new

    s_ref[0] = logits(0)

    def chunk_pair(cc, carry):
        c = 2 * cc
        s_ref[1] = logits(c + 1)
        consume(c, s_ref[0])
        s_ref[0] = logits(c + 2)
        consume(c + 1, s_ref[1])
        return carry

    lax.fori_loop(0, (nchunk + 1) // 2, chunk_pair, 0)

    lp = lam_ref[...]
    lam = (jnp.exp(jnp.sum(lp[0:1] * lp[1:2], axis=1, keepdims=True))
           - jnp.exp(jnp.sum(lp[2:3] * lp[3:4], axis=1, keepdims=True)) + lam_init)
    o = acc_ref[...] / l_ref[...]
    o = o[:, :tq] - lam * o[:, tq:]
    ms = jnp.mean(o * o, axis=0, keepdims=True)
    o = o * lax.rsqrt(ms + RMS_EPS) * (g_ref[...] * (1.0 - lam_init))
    o_ref[...] = o.T.astype(o_ref.dtype)


def diff_attention(qk, vt, band, lam_params, subln_g, lam_init, vt_row0, tq=256, tk=512):
    S = qk.shape[0]
    vh0 = vt_row0 // HEAD_DIM
    return pl.pallas_call(
        functools.partial(_diff_kernel, tq=tq, tk=tk, c0=_band_c0(tk), lam_init=lam_init),
        grid=(C_HEADS, S // tq),
        in_specs=[pl.BlockSpec((4, C_HALF), lambda h, i: (0, 0)),
                  pl.BlockSpec((HEAD_DIM, 1), lambda h, i: (0, 0)),
                  pl.BlockSpec((tq, LANE), lambda h, i: (i, h)),
                  pl.BlockSpec((S, LANE), lambda h, i: (0, C_HEADS + h)),
                  pl.BlockSpec((HEAD_DIM, S), lambda h, i: (vh0 + h, 0)),
                  pl.BlockSpec((1,) + band.shape[1:], lambda h, i: (h, 0, 0))],
        out_specs=pl.BlockSpec((tq, LANE), lambda h, i: (i, h)),
        out_shape=jax.ShapeDtypeStruct((S, C_HEADS * HEAD_DIM), BF16),
        scratch_shapes=[pltpu.VMEM((HEAD_DIM, 2 * tq), F32),
                        pltpu.VMEM((1, 2 * tq), F32),
                        pltpu.VMEM((1, 2 * tq), F32),
                        pltpu.VMEM((2, tk, 2 * tq), F32)],
        compiler_params=_cparams(("parallel", "arbitrary")),
        name="diff_attention",
    )(lam_params, subln_g.reshape(HEAD_DIM, 1), qk, qk, vt, band)


def _outproj_kernel(x_ref, a_ref, b_ref, c_ref, wa_ref, wb_ref, wc_ref, g_ref, o_ref, h_ref):
    y = (x_ref[...] + _dot(a_ref[...], wa_ref[...]) + _dot(b_ref[...], wb_ref[...])
         + _dot(c_ref[...], wc_ref[...]))
    o_ref[...] = y
    ms = jnp.mean(y * y, axis=-1, keepdims=True)
    h_ref[...] = (y * lax.rsqrt(ms + RMS_EPS) * g_ref[...]).astype(h_ref.dtype)


def out_projection(x, oa, ob, oc, wa, wb, wc, g, tm=512):
    S, D = x.shape
    once = pl.Buffered(1)
    row = lambda i: (i, 0)
    fixed = lambda i: (0, 0)
    return pl.pallas_call(
        _outproj_kernel,
        grid=(S // tm,),
        in_specs=[pl.BlockSpec((tm, D), row),
                  pl.BlockSpec((tm, oa.shape[1]), row),
                  pl.BlockSpec((tm, ob.shape[1]), row),
                  pl.BlockSpec((tm, oc.shape[1]), row),
                  pl.BlockSpec(wa.shape, fixed, pipeline_mode=once),
                  pl.BlockSpec(wb.shape, fixed, pipeline_mode=once),
                  pl.BlockSpec(wc.shape, fixed, pipeline_mode=once),
                  pl.BlockSpec((1, D), fixed)],
        out_specs=[pl.BlockSpec((tm, D), row), pl.BlockSpec((tm, D), row)],
        out_shape=[jax.ShapeDtypeStruct((S, D), F32), jax.ShapeDtypeStruct((S, D), BF16)],
        compiler_params=_cparams(("parallel",)),
        name="out_projection",
    )(x, oa, ob, oc, wa, wb, wc, g.reshape(1, D))


def _silu(x):
    return x / (1.0 + jnp.exp(-x))


def _ffn_kernel(x_ref, h_ref, wg_ref, wu_ref, wd_ref, g_ref, o_ref, hn_ref):
    f = pl.program_id(1)

    @pl.when(f == 0)
    def _():
        o_ref[...] = x_ref[...]

    h = h_ref[...]
    a = _silu(_dot(h, wg_ref[...].astype(BF16))) * _dot(h, wu_ref[...].astype(BF16))
    o_ref[...] += _dot(a.astype(BF16), wd_ref[...].astype(BF16))

    @pl.when(f == pl.num_programs(1) - 1)
    def _():
        y = o_ref[...]
        ms = jnp.mean(y * y, axis=-1, keepdims=True)
        hn_ref[...] = (y * lax.rsqrt(ms + RMS_EPS) * g_ref[...]).astype(hn_ref.dtype)


def ffn_dense(x, h, wg, wu, wd, g_next, tm=1024, tf=256):
    S, D = x.shape
    F = wg.shape[1]
    once = pl.Buffered(1)
    row = lambda i, f: (i, 0)
    return pl.pallas_call(
        _ffn_kernel,
        grid=(S // tm, F // tf),
        in_specs=[pl.BlockSpec((tm, D), row, pipeline_mode=once),
                  pl.BlockSpec((tm, D), row, pipeline_mode=once),
                  pl.BlockSpec((D, tf), lambda i, f: (0, f)),
                  pl.BlockSpec((D, tf), lambda i, f: (0, f)),
                  pl.BlockSpec((tf, D), lambda i, f: (f, 0)),
                  pl.BlockSpec((1, D), lambda i, f: (0, 0))],
        out_specs=[pl.BlockSpec((tm, D), row), pl.BlockSpec((tm, D), row)],
        out_shape=[jax.ShapeDtypeStruct((S, D), F32), jax.ShapeDtypeStruct((S, D), BF16)],
        compiler_params=_cparams(("parallel", "arbitrary")),
        name="ffn_dense",
    )(x, h, wg, wu, wd, g_next.reshape(1, D))


def _router_kernel(h_ref, w_ref, c_ref, sel_ref):
    logits = _dot(h_ref[...], w_ref[...])
    lane = lax.broadcasted_iota(jnp.int32, logits.shape, 1).astype(F32)
    logits = jnp.where(lane < N_EXPERTS, logits, NEG)
    m1 = jnp.max(logits, axis=1, keepdims=True)
    i1 = jnp.min(jnp.where(logits == m1, lane, float(LANE)), axis=1, keepdims=True)
    rest = jnp.where(lane == i1, NEG, logits)
    m2 = jnp.max(rest, axis=1, keepdims=True)
    i2 = jnp.min(jnp.where(rest == m2, lane, float(LANE)), axis=1, keepdims=True)
    e2 = jnp.exp(m2 - m1)
    g1 = 1.0 / (1.0 + e2)
    g2 = e2 / (1.0 + e2)
    pick1, pick2 = lane == i1, lane == i2
    c_ref[...] = jnp.where(pick1, g1, 0.0) + jnp.where(pick2, g2, 0.0)
    sel_ref[...] = jnp.where(pick1 | pick2, 1.0, 0.0)


def moe_router(h, w_router, tm=1024):
    S, D = h.shape
    wpad = jnp.zeros((D, LANE), BF16).at[:, :N_EXPERTS].set(w_router.astype(BF16))
    spec = pl.BlockSpec((tm, LANE), lambda i: (i, 0))
    sds = jax.ShapeDtypeStruct((S, LANE), F32)
    return pl.pallas_call(
        _router_kernel,
        grid=(S // tm,),
        in_specs=[pl.BlockSpec((tm, D), lambda i: (i, 0)),
                  pl.BlockSpec((D, LANE), lambda i: (0, 0))],
        out_specs=[spec, spec],
        out_shape=[sds, sds],
        compiler_params=_cparams(("parallel",)),
        name="moe_router",
    )(h, wpad)


MOE_CH = 256
MOE_TILE = 1024


def _moe_rank_kernel(sel_ref, rm_ref, rmt_ref, cum_ref, tot_ref, carry_ref):
    c = pl.program_id(0)

    @pl.when(c == 0)
    def _():
        carry_ref[...] = jnp.zeros(carry_ref.shape, F32)

    sel = sel_ref[...]
    ch = sel.shape[0]
    before = (lax.broadcasted_iota(jnp.int32, (ch, ch), 1) < lax.broadcasted_iota(jnp.int32, (ch, ch), 0))
    rank = _dot(jnp.where(before, 1.0, 0.0).astype(BF16), sel.astype(BF16)) + carry_ref[...]
    rm = jnp.where(sel > 0.0, rank, -1.0)
    rm_ref[...] = rm
    rmt_ref[...] = rm.T[:N_EXPERTS]
    cum_ref[0] = carry_ref[...]
    carry_ref[...] += jnp.sum(sel, axis=0, keepdims=True)
    tot_ref[...] = carry_ref[...]


def moe_rank(sel):
    S = sel.shape[0]
    nch = S // MOE_CH
    return pl.pallas_call(
        _moe_rank_kernel,
        grid=(nch,),
        in_specs=[pl.BlockSpec((MOE_CH, LANE), lambda c: (c, 0))],
        out_specs=[pl.BlockSpec((MOE_CH, LANE), lambda c: (c, 0)),
                   pl.BlockSpec((N_EXPERTS, MOE_CH), lambda c: (0, c)),
                   pl.BlockSpec((1, 1, LANE), lambda c: (c, 0, 0)),
                   pl.BlockSpec((1, LANE), lambda c: (0, 0))],
        out_shape=[jax.ShapeDtypeStruct((S, LANE), F32),
                   jax.ShapeDtypeStruct((N_EXPERTS, S), F32),
                   jax.ShapeDtypeStruct((nch, 1, LANE), F32),
                   jax.ShapeDtypeStruct((1, LANE), F32)],
        scratch_shapes=[pltpu.VMEM((1, LANE), F32)],
        compiler_params=_cparams(("arbitrary",)),
        name="moe_rank",
    )(sel)


def _moe_expert_kernel(te_ref, tr0_ref, tclo_ref, tchi_ref, tval_ref, cumx_ref,
                       rk_ref, h_hbm, wg_ref, wu_ref, wd_ref, y_ref,
                       hs_ref, acc_ref, hbuf_ref, sem_ref, *, n_experts):
    j = pl.program_id(0)
    f = pl.program_id(1)
    nf = pl.num_programs(1)
    nrows = tval_ref[j]
    tile = hs_ref.shape[0]
    win = MOE_CH + 8

    def chunk_copy(c, slot):
        return pltpu.make_async_copy(h_hbm.at[pl.ds(c * MOE_CH, MOE_CH)], hbuf_ref.at[slot], sem_ref.at[slot])

    @pl.when((nrows > 0) & (f == 0))
    def _():
        clo, chi = tclo_ref[j], tchi_ref[j]
        tr0 = tr0_ref[j]
        wrow = lax.broadcasted_iota(jnp.int32, (win, MOE_CH), 0)
        acc_ref[...] = jnp.zeros(acc_ref.shape, F32)

        @pl.when(chi > clo)
        def _():
            chunk_copy(clo, 0).start()

        def body(c, carry):
            slot = (c - clo) % 2

            @pl.when(c + 1 < chi)
            def _():
                chunk_copy(c + 1, 1 - slot).start()

            chunk_copy(c, slot).wait()
            first = cumx_ref[c * n_experts + te_ref[j]] - tr0
            w0 = pl.multiple_of(jnp.clip((first // 8) * 8, 0, tile - win), 8)
            rk = rk_ref[0, :, pl.ds(pl.multiple_of(c * MOE_CH, MOE_CH), MOE_CH)]
            onehot = jnp.where(rk == (tr0 + w0 + wrow).astype(F32), 1.0, 0.0).astype(BF16)
            acc_ref[pl.ds(w0, win), :] += _dot(onehot, hbuf_ref[slot])
            return carry

        lax.fori_loop(clo, chi, body, 0)
        hs_ref[...] = acc_ref[...].astype(BF16)
        acc_ref[...] = jnp.zeros(acc_ref.shape, F32)

    def swiglu(rows):
        hs = hs_ref[:rows]
        a = _silu(_dot(hs, wg_ref[0].astype(BF16))) * _dot(hs, wu_ref[0].astype(BF16))
        acc_ref[:rows] += _dot(a.astype(BF16), wd_ref[0].astype(BF16))

    sizes = (tile, tile // 2, tile // 4, tile // 8)
    for rows, below in zip(sizes, sizes[1:] + (0,)):
        @pl.when((nrows > below) & (nrows <= rows))
        def _():
            swiglu(rows)

    @pl.when(f == nf - 1)
    def _():
        y_ref[...] = jnp.where(nrows > 0, acc_ref[...], 0.0).astype(y_ref.dtype)


def _moe_combine_kernel(ca_ref, cb_ref, roff_ref, x_ref, rm_ref, cw_ref, *refs):
    y_refs, o_ref = refs[:-1], refs[-1]
    n_experts = len(y_refs) // 2
    c = pl.program_id(0)
    rm = rm_ref[...]
    cw = cw_ref[...]
    lane = lax.broadcasted_iota(jnp.int32, rm.shape, 1)
    cr = y_refs[0].shape[0]
    col = lax.broadcasted_iota(jnp.int32, (rm.shape[0], cr), 1)
    out = x_ref[...]
    for e in range(n_experts):
        mine = lane == e
        pos = jnp.sum(jnp.where(mine, rm, 0.0), axis=1, keepdims=True)
        gate = jnp.sum(jnp.where(mine, cw, 0.0), axis=1, keepdims=True)
        row = jnp.where(pos >= 0.0, pos + roff_ref[e].astype(F32), -1.0)
        ca, cb = ca_ref[c * n_experts + e], cb_ref[c * n_experts + e]
        for y_ref, base in ((y_refs[2 * e], ca * cr),
                            (y_refs[2 * e + 1], jnp.where(cb != ca, cb * cr, -2 * cr))):
            onehot = jnp.where(row == (base + col).astype(F32), 1.0, 0.0).astype(BF16)
            out = out + gate * _dot(onehot, y_ref[...])
    o_ref[...] = out


def ffn_moe(x, h, combine, sel, wg, wu, wd, tf=512):
    S, D = x.shape
    E, _, F = wg.shape
    tf = min(tf, F)
    T = MOE_TILE
    nch = S // MOE_CH
    J = 2 * S // T + E
    rm, rmt, cum, tot = moe_rank(sel)

    cnt = tot[0, :E].astype(jnp.int32)
    ntile = (cnt + T - 1) // T
    tend = jnp.cumsum(ntile)
    tstart = tend - ntile
    jj = jnp.arange(J, dtype=jnp.int32)
    te = jnp.minimum(jnp.searchsorted(tend, jnp.minimum(jj, tend[-1] - 1), side="right"), E - 1).astype(jnp.int32)
    tr0 = (jnp.minimum(jj, tend[-1] - 1) - tstart[te]) * T
    tval = jnp.where(jj < tend[-1], jnp.clip(cnt[te] - tr0, 0, T), 0).astype(jnp.int32)
    cumx = cum[:, 0, :E].astype(jnp.int32)
    cumi = jnp.concatenate([cumx[1:], cnt[None]], axis=0)
    tclo = jnp.sum(cumi[:, te] <= tr0[None, :], axis=0).astype(jnp.int32)
    tchi = jnp.sum(cumx[:, te] < (tr0 + tval)[None, :], axis=0).astype(jnp.int32)
    roff = (tstart * T).astype(jnp.int32)
    nrow_chunks = J * T // MOE_CH
    ca = ((roff[None, :] + cumx) // MOE_CH).astype(jnp.int32)
    cb = jnp.minimum(ca + 1, nrow_chunks - 1)
    ca, cb = ca.reshape(-1), cb.reshape(-1)

    nf = F // tf

    def wmap(j, f, te_r, tr0_r, clo_r, chi_r, val_r, cumx_r):
        return (te_r[j], 0, jnp.where(val_r[j] > 0, f, nf - 1))

    def wdmap(j, f, te_r, tr0_r, clo_r, chi_r, val_r, cumx_r):
        return (te_r[j], jnp.where(val_r[j] > 0, f, nf - 1), 0)

    y = pl.pallas_call(
        functools.partial(_moe_expert_kernel, n_experts=E),
        grid_spec=pltpu.PrefetchScalarGridSpec(
            num_scalar_prefetch=6,
            grid=(J, nf),
            in_specs=[pl.BlockSpec((1, 1, S), lambda j, f, te_r, *_: (te_r[j], 0, 0)),
                      pl.BlockSpec(memory_space=pl.ANY),
                      pl.BlockSpec((1, D, tf), wmap),
                      pl.BlockSpec((1, D, tf), wmap),
                      pl.BlockSpec((1, tf, D), wdmap)],
            out_specs=pl.BlockSpec((T, D), lambda j, f, *_: (j, 0)),
            scratch_shapes=[pltpu.VMEM((T, D), BF16),
                            pltpu.VMEM((T, D), F32),
                            pltpu.VMEM((2, MOE_CH, D), BF16),
                            pltpu.SemaphoreType.DMA((2,))]),
        out_shape=jax.ShapeDtypeStruct((J * T, D), BF16),
        compiler_params=_cparams(("arbitrary", "arbitrary")),
        name="moe_experts",
    )(te, tr0, tclo, tchi, tval, cumx.reshape(-1), rmt.reshape(E, 1, S), h, wg, wu, wd)

    def chunk_map(c, ca_r, cb_r, ro_r, *, e, second):
        return ((cb_r if second else ca_r)[c * E + e], 0)

    return pl.pallas_call(
        _moe_combine_kernel,
        grid_spec=pltpu.PrefetchScalarGridSpec(
            num_scalar_prefetch=3,
            grid=(nch,),
            in_specs=[pl.BlockSpec((MOE_CH, D), lambda c, *_: (c, 0)),
                      pl.BlockSpec((MOE_CH, LANE), lambda c, *_: (c, 0)),
                      pl.BlockSpec((MOE_CH, LANE), lambda c, *_: (c, 0))]
                     + [pl.BlockSpec((MOE_CH, D), functools.partial(chunk_map, e=e, second=second))
                        for e in range(E) for second in (False, True)],
            out_specs=pl.BlockSpec((MOE_CH, D), lambda c, *_: (c, 0))),
        out_shape=jax.ShapeDtypeStruct((S, D), F32),
        compiler_params=_cparams(("parallel",)),
        name="moe_combine",
    )(ca, cb, roff, x, rm, combine, *([y] * (2 * E)))


def _tile_gain(g, reps, scale=1.0):
    return jnp.tile(g.astype(F32) * scale, reps)


def kernel(x, w_in, w_out, attn_norm_g, ffn_norm_g, q_norm_a, k_norm_a, q_norm_b, k_norm_b,
           q_norm_c, k_norm_c, lambda_q1, lambda_k1, lambda_q2, lambda_k2, diff_subln_g, rel_bias,
           w_dense_gate, w_dense_up, w_dense_down, w_router, w_moe_gate, w_moe_up, w_moe_down):
    B, S, D = x.shape
    depth = w_in.shape[0]
    assert B == 1 and S % (DIL_STEPS * DILATED_PAIRS[-1][1]) == 0
    aw, bw, cw = A_HEADS * HEAD_DIM, B_HEADS * HEAD_DIM, C_HEADS * HEAD_DIM
    iw = IDX_HEADS * IDX_DIM
    sizes = (aw, aw, aw, iw, IDX_DIM, IDX_HEADS, bw, bw, bw, cw, cw, cw)
    offs = np.concatenate([[0], np.cumsum(sizes)]).tolist()

    def cols(w, *segs):
        return jnp.concatenate([w[:, offs[s]:offs[s + 1]] for s in segs], axis=1)

    tq, tk = 256, 512
    band_a = bias_band(rel_bias, A_HEADS, 0, _band_c0(tk) + 2 * tk, tq, _band_c0(tk))
    band_c = bias_band(rel_bias, C_HEADS, A_HEADS + B_HEADS, _band_c0(tk) + 2 * tk, tq, _band_c0(tk))
    bands_b = [bias_band(rel_bias, B_HEADS, A_HEADS, 2 * DIL_STEPS * dil + tq, tq, DIL_STEPS * dil, dil)
               for _, dil in DILATED_PAIRS]

    xs = x.reshape(S, D)
    h = None
    for layer in range(depth):
        wl = w_in[layer]
        w_pa = cols(wl, 0, 1).astype(BF16)
        w_pb = cols(wl, 6, 7).astype(BF16)
        w_pc = cols(wl, 9, 10).astype(BF16)
        w_pi = cols(wl, 3, 4, 4).astype(BF16)
        w_vt = cols(wl, 2, 8, 11).T.astype(BF16)
        w_wt = cols(wl, 5).T.astype(BF16)
        qs_ab, qs_c = HEAD_DIM ** -0.5 * LOG2E, C_HALF ** -0.5 * LOG2E
        g_pa = jnp.concatenate([_tile_gain(q_norm_a[layer], A_HEADS, qs_ab), _tile_gain(k_norm_a[layer], A_HEADS)])
        g_pb = jnp.concatenate([_tile_gain(q_norm_b[layer], B_HEADS, qs_ab), _tile_gain(k_norm_b[layer], B_HEADS)])
        g_pc = jnp.concatenate([_tile_gain(q_norm_c[layer], 2 * C_HEADS, qs_c),
                                _tile_gain(k_norm_c[layer], 2 * C_HEADS)])

        if h is None:
            h = rmsnorm(xs, attn_norm_g[layer])
        p_a = project(h, w_pa, g_pa, HEAD_DIM, BF16, tn=512)
        p_b = project(h, w_pb, g_pb, HEAD_DIM, BF16, tn=512)
        p_c = project(h, w_pc, g_pc, C_HALF, BF16, tn=512)
        p_i = project(h, w_pi, jnp.ones((w_pi.shape[1],), F32), 0, BF16, tn=w_pi.shape[1])
        vt = project_t(h, w_vt, BF16, tn=512)
        wt = project_t(h, w_wt, F32)

        oa = dsa_attention(p_a, vt, p_i, wt, band_a)

        ob = dilated_attention(p_b, vt, bands_b, aw, tq=tq)

        lam_init = 0.8 - 0.6 * math.exp(-0.3 * layer)
        lam_params = jnp.stack([lambda_q1[layer], lambda_k1[layer], lambda_q2[layer], lambda_k2[layer]]).astype(F32)
        oc = diff_attention(p_c, vt, band_c, lam_params, diff_subln_g[layer].astype(F32), lam_init, aw + bw,
                            tq=tq, tk=tk)

        wo = w_out[layer].astype(BF16)
        xs, h2 = out_projection(xs, oa, ob, oc, wo[:aw], wo[aw:aw + bw], wo[aw + bw:], ffn_norm_g[layer])

        f = layer // 2
        if layer % 2 == 0:
            g_next = attn_norm_g[min(layer + 1, depth - 1)]
            xs, h = ffn_dense(xs, h2, w_dense_gate[f], w_dense_up[f], w_dense_down[f], g_next)
        else:
            combine, sel = moe_router(h2, w_router[f])
            xs = ffn_moe(xs, h2, combine, sel, w_moe_gate[f], w_moe_up[f], w_moe_down[f])
            h = None
    return xs.reshape(B, S, D)
```

```python
import functools
import math

import jax
import jax.numpy as jnp
import numpy as np
from jax import lax
from jax.experimental import pallas as pl
from jax.experimental.pallas import tpu as pltpu

F32 = jnp.float32
BF16 = jnp.bfloat16

HEAD_DIM = 128
A_HEADS, B_HEADS, C_HEADS = 4, 6, 6
C_HALF = HEAD_DIM // 2
IDX_HEADS, IDX_DIM = 16, 64
TOPK_MAX = 256
DILATED_PAIRS = ((128, 1), (512, 4), (2048, 16))
DIL_STEPS = 128
N_BUCKETS = 32
MAX_DISTANCE = 2048
N_EXPERTS = 8
RMS_EPS = 1e-6
NEG = -1e30
LOG2E = math.log2(math.e)
INT_MIN = -(2 ** 31)
I16_BIAS = 2 ** 15
CNT_ROWS = 64
ONES_ROWS = 16
LANE = 128
VMEM_LIMIT = 56 * 1024 * 1024

BUCKET_EDGES = tuple(range(16)) + (16,) + tuple(
    int(math.ceil(16.0 * (MAX_DISTANCE / 16.0) ** (k / 16.0))) for k in range(1, 16))
FAR_DIST = BUCKET_EDGES[-1]


def _cparams(sem):
    return pltpu.CompilerParams(dimension_semantics=sem, vmem_limit_bytes=VMEM_LIMIT)


def _dot(a, b):
    return jnp.dot(a, b, preferred_element_type=F32)


def _dot_nt(a, b):
    return lax.dot_general(a, b, (((1,), (1,)), ((), ())), preferred_element_type=F32)


def _rmsnorm_kernel(x_ref, g_ref, o_ref):
    x = x_ref[...]
    ms = jnp.mean(x * x, axis=-1, keepdims=True)
    o_ref[...] = (x * lax.rsqrt(ms + RMS_EPS) * g_ref[...]).astype(o_ref.dtype)


def rmsnorm(x, g, tm=512):
    S, D = x.shape
    return pl.pallas_call(
        _rmsnorm_kernel,
        grid=(S // tm,),
        in_specs=[pl.BlockSpec((tm, D), lambda i: (i, 0)), pl.BlockSpec((1, D), lambda i: (0, 0))],
        out_specs=pl.BlockSpec((tm, D), lambda i: (i, 0)),
        out_shape=jax.ShapeDtypeStruct((S, D), BF16),
        compiler_params=_cparams(("parallel",)),
        name="rmsnorm",
    )(x, g.reshape(1, D))


def _proj_kernel(h_ref, w_ref, g_ref, o_ref, *, group):
    acc = _dot(h_ref[...], w_ref[...])
    tn = acc.shape[1]
    if group == 0:
        o_ref[...] = acc.astype(o_ref.dtype)
        return
    for c in range(tn // LANE):
        a = acc[:, c * LANE:(c + 1) * LANE]
        sq = a * a
        if group == LANE:
            ms = jnp.mean(sq, axis=-1, keepdims=True)
        else:
            lane = lax.broadcasted_iota(jnp.int32, sq.shape, 1)
            lo = jnp.sum(jnp.where(lane < group, sq, 0.0), axis=-1, keepdims=True)
            hi = jnp.sum(jnp.where(lane >= group, sq, 0.0), axis=-1, keepdims=True)
            ms = jnp.where(lane < group, lo, hi) * (1.0 / group)
        y = a * lax.rsqrt(ms + RMS_EPS) * g_ref[:, c * LANE:(c + 1) * LANE]
        o_ref[:, c * LANE:(c + 1) * LANE] = y.astype(o_ref.dtype)


def project(h, w, gains, group, out_dtype, tm=1024, tn=256):
    S, K = h.shape
    N = w.shape[1]
    return pl.pallas_call(
        functools.partial(_proj_kernel, group=group),
        grid=(S // tm, N // tn),
        in_specs=[pl.BlockSpec((tm, K), lambda i, j: (i, 0)),
                  pl.BlockSpec((K, tn), lambda i, j: (0, j)),
                  pl.BlockSpec((1, tn), lambda i, j: (0, j))],
        out_specs=pl.BlockSpec((tm, tn), lambda i, j: (i, j)),
        out_shape=jax.ShapeDtypeStruct((S, N), out_dtype),
        compiler_params=_cparams(("parallel", "arbitrary")),
        name=f"proj_g{group}",
    )(h, w, gains.reshape(1, N))


def _proj_t_kernel(wt_ref, h_ref, o_ref):
    o_ref[...] = _dot_nt(wt_ref[...], h_ref[...]).astype(o_ref.dtype)


def project_t(h, wt, out_dtype, tm=1024, tn=256):
    S, K = h.shape
    N = wt.shape[0]
    tn = min(tn, N)
    return pl.pallas_call(
        _proj_t_kernel,
        grid=(S // tm, N // tn),
        in_specs=[pl.BlockSpec((tn, K), lambda i, j: (j, 0)),
                  pl.BlockSpec((tm, K), lambda i, j: (i, 0))],
        out_specs=pl.BlockSpec((tn, tm), lambda i, j: (j, i)),
        out_shape=jax.ShapeDtypeStruct((N, S), out_dtype),
        compiler_params=_cparams(("parallel", "arbitrary")),
        name="proj_t",
    )(wt, h)


BAND_ROWS = 256


def _band_kernel(tab_ref, o_ref, *, c0, head_off, dil):
    hh = head_off + pl.program_id(0)
    _, rows, width = o_ref.shape
    base = (lax.broadcasted_iota(jnp.int32, (BAND_ROWS, width), 1)
            - lax.broadcasted_iota(jnp.int32, (BAND_ROWS, width), 0))
    hi_valid = DIL_STEPS * dil if dil else None
    for r in range(rows // BAND_ROWS):
        d0 = c0 - r * BAND_ROWS
        lo, hi = d0 - (BAND_ROWS - 1), d0 + width - 1
        if hi_valid is not None:
            hi = min(hi, hi_valid)
        block = pl.ds(r * BAND_ROWS, BAND_ROWS)
        if hi < 0 or hi < lo:
            o_ref[0, block, :] = jnp.full((BAND_ROWS, width), NEG, F32)
            continue
        dist = base + d0
        first = max(b for b in range(N_BUCKETS) if BUCKET_EDGES[b] <= max(lo, 0))
        val = jnp.full((BAND_ROWS, width), tab_ref[first, hh] * LOG2E, F32)
        for b in range(first + 1, N_BUCKETS):
            if BUCKET_EDGES[b] <= hi:
                val = jnp.where(dist >= BUCKET_EDGES[b], tab_ref[b, hh] * LOG2E, val)
        ok = None
        if lo < 0:
            ok = dist >= 0
        if dil:
            in_win = (dist <= hi_valid) & ((dist & (dil - 1)) == 0)
            ok = in_win if ok is None else ok & in_win
        o_ref[0, block, :] = val if ok is None else jnp.where(ok, val, NEG)


def bias_band(rel_bias, n_heads, head_off, rows, width, c0, dil=0):
    assert rows % BAND_ROWS == 0 and dil & (dil - 1) == 0
    return pl.pallas_call(
        functools.partial(_band_kernel, c0=c0, head_off=head_off, dil=dil),
        grid=(n_heads,),
        in_specs=[pl.BlockSpec(memory_space=pltpu.SMEM)],
        out_specs=pl.BlockSpec((1, rows, width), lambda h: (h, 0, 0)),
        out_shape=jax.ShapeDtypeStruct((n_heads, rows, width), F32),
        compiler_params=_cparams(("parallel",)),
        name=f"bias_band_d{dil}",
    )(rel_bias)


def _band_c0(tk):
    return -(-(FAR_DIST + tk - 1) // LANE) * LANE


def _dsa_kernel(q_ref, qi_ref, wt_ref, k_ref, vt_ref, kidx_ref, band_ref, o_ref,
                hi_ref, lo_ref, cut_ref, qih_ref, s_ref, acc_ref, m_ref, l_ref, *, tq, kc, topk, c0):
    i = pl.program_id(0)
    t0 = i * tq
    nchunk = (t0 + tq + kc - 1) // kc
    half = lax.broadcasted_iota(jnp.int32, (tq, LANE), 1) < IDX_DIM

    for h in range(IDX_HEADS):
        slab = qi_ref[:, (h // 2) * LANE:(h // 2 + 1) * LANE]
        keep = half if h % 2 == 0 else jnp.logical_not(half)
        qih_ref[h] = jnp.where(keep, slab, jnp.zeros_like(slab))
    w = wt_ref[...] * (IDX_HEADS ** -0.5 * IDX_DIM ** -0.5)

    pos_q = t0 + lax.broadcasted_iota(jnp.int32, (kc, tq), 1)
    row = lax.broadcasted_iota(jnp.int32, (kc, tq), 0)

    def score_chunk(c, carry):
        s0 = pl.multiple_of(c * kc, kc)
        kx = kidx_ref[pl.ds(s0, kc), :]
        acc = jnp.zeros((kc, tq), F32)
        for h in range(IDX_HEADS):
            acc = acc + w[h:h + 1, :] * jnp.maximum(_dot_nt(kx, qih_ref[h]), 0.0)
        bits = lax.bitcast_convert_type(acc, jnp.int32)
        key = bits ^ ((bits >> 31) & 0x7FFFFFFF)
        key = jnp.where(s0 + row <= pos_q, key, INT_MIN)
        hi_ref[pl.ds(s0, kc), :] = (key >> 16).astype(jnp.int16)
        lo_ref[pl.ds(s0, kc), :] = ((key & 0xFFFF) - I16_BIAS).astype(jnp.int16)
        return carry

    lax.fori_loop(0, nchunk, score_chunk, 0)

    one16, zero16 = jnp.ones((), jnp.int16), jnp.zeros((), jnp.int16)
    row16 = lax.broadcasted_iota(jnp.int32, (kc, tq), 0).astype(jnp.int16)
    rowc16 = lax.broadcasted_iota(jnp.int32, (CNT_ROWS, tq), 0).astype(jnp.int16)
    n_keys = k_ref.shape[0]

    def to16(v):
        return jnp.clip(v, -I16_BIAS, I16_BIAS - 1).astype(jnp.int16)

    def count(hit):
        def body(c, cnt):
            base = pl.multiple_of(c * kc, kc)
            for g in range(kc // CNT_ROWS):
                cnt = cnt + jnp.where(hit(base + g * CNT_ROWS), one16, zero16)
            return cnt

        cnt = lax.fori_loop(0, nchunk, body, jnp.zeros((CNT_ROWS, tq), jnp.int16))
        return jnp.sum(cnt.astype(jnp.int32).astype(F32), axis=0, keepdims=True)

    def count_ge(ref, cand):
        c16 = cand.astype(jnp.int16)
        return count(lambda r: ref[pl.ds(r, CNT_ROWS), :] >= c16)

    def count_gt(ref, cand):
        c16 = cand.astype(jnp.int16)
        return count(lambda r: ref[pl.ds(r, CNT_ROWS), :] > c16)

    def kth_largest(ref, need):
        t = jnp.where(count_ge(ref, jnp.zeros((1, tq), jnp.int32)) >= need, 0, -I16_BIAS).astype(jnp.int32)

        def bit_body(b, t):
            cand = t + jnp.left_shift(jnp.int32(1), 14 - b)
            return jnp.where(count_ge(ref, cand) >= need, cand, t)

        return lax.fori_loop(0, 15, bit_body, t)

    kf = jnp.full((1, tq), float(topk), F32)
    thi = kth_largest(hi_ref, kf)
    need_lo = kf - count_gt(hi_ref, thi)
    thi16 = thi.astype(jnp.int16)

    def mask_low(c, carry):
        rows = pl.ds(pl.multiple_of(c * kc, kc), kc)
        lo_ref[rows, :] = jnp.where(hi_ref[rows, :] == thi16, lo_ref[rows, :], jnp.int16(-I16_BIAS))
        return carry

    lax.fori_loop(0, nchunk, mask_low, 0)
    tlo = kth_largest(lo_ref, need_lo)
    tlo16 = tlo.astype(jnp.int16)

    def is_eq(r):
        return (hi_ref[pl.ds(r, CNT_ROWS), :] == thi16) & (lo_ref[pl.ds(r, CNT_ROWS), :] == tlo16)

    keep_eq = need_lo - count_gt(lo_ref, tlo)
    excess = (count(is_eq) > keep_eq) & (thi > -I16_BIAS)
    cut_ref[...] = jnp.full(cut_ref.shape, n_keys, jnp.int32)

    @pl.when(jnp.max(jnp.broadcast_to(jnp.where(excess, 1.0, 0.0), (8, tq))) > 0.0)
    def _():
        nbits = n_keys.bit_length()

        def bit_body(b, j):
            cand = j + jnp.left_shift(jnp.int32(1), nbits - 1 - b)
            before = count(lambda r: is_eq(r) & (rowc16 < to16(cand - r)))
            return jnp.where(before < keep_eq, cand, j)

        j = lax.fori_loop(0, nbits, bit_body, jnp.zeros((1, tq), jnp.int32))
        cut_ref[...] = jnp.where(excess, j, n_keys)

    cut = cut_ref[...]
    thi_sel16 = jnp.maximum(thi, 1 - I16_BIAS).astype(jnp.int16)

    m_ref[...] = jnp.full(m_ref.shape, NEG, F32)
    l_ref[...] = jnp.zeros(l_ref.shape, F32)
    acc_ref[...] = jnp.zeros(acc_ref.shape, F32)
    ones_rows = jnp.ones((ONES_ROWS, kc), BF16)

    def logits(c):
        s0 = pl.multiple_of(jnp.minimum(c, nchunk - 1) * kc, kc)
        off = pl.multiple_of(c0 - jnp.clip(t0 - c * kc, -kc, c0), LANE)
        hi = hi_ref[pl.ds(s0, kc), :]
        lo = lo_ref[pl.ds(s0, kc), :]
        sel = (hi > thi_sel16) | ((hi == thi_sel16)
                                  & ((lo > tlo16) | ((lo == tlo16) & (row16 <= to16(cut - s0)))))
        drop = jnp.where(sel, jnp.zeros((), jnp.int16), jnp.ones((), jnp.int16))
        mask_bias = drop.astype(jnp.int32).astype(F32) * NEG
        parts = []
        for h in range(A_HEADS):
            hs = slice(h * HEAD_DIM, (h + 1) * HEAD_DIM)
            parts.append(_dot_nt(k_ref[pl.ds(s0, kc), hs], q_ref[:, hs]) + band_ref[h, pl.ds(off, kc), :]
                         + mask_bias)
        return jnp.concatenate(parts, axis=1)

    def consume(c, s):
        s0 = pl.multiple_of(jnp.minimum(c, nchunk - 1) * kc, kc)
        m_prev = m_ref[...]
        m_new = jnp.maximum(m_prev, jnp.max(s, axis=0, keepdims=True))
        alpha = jnp.exp2(m_prev - m_new)
        p = jnp.exp2(s - m_new).astype(BF16)
        pv = [_dot(jnp.concatenate([vt_ref[h * HEAD_DIM:(h + 1) * HEAD_DIM, pl.ds(s0, kc)], ones_rows], axis=0),
                   p[:, h * tq:(h + 1) * tq]) for h in range(A_HEADS)]
        pv = jnp.concatenate(pv, axis=1)
        l_ref[...] = alpha * l_ref[...] + pv[HEAD_DIM:HEAD_DIM + 1]
        acc_ref[...] = alpha * acc_ref[...] + pv[:HEAD_DIM]
        m_ref[...] = m_new

    s_ref[0] = logits(0)

    def chunk_pair(cc, carry):
        c = 2 * cc
        s_ref[1] = logits(c + 1)
        consume(c, s_ref[0])
        s_ref[0] = logits(c + 2)
        consume(c + 1, s_ref[1])
        return carry

    lax.fori_loop(0, (nchunk + 1) // 2, chunk_pair, 0)
    o = acc_ref[...] / l_ref[...]
    for h in range(A_HEADS):
        o_ref[:, h * HEAD_DIM:(h + 1) * HEAD_DIM] = o[:, h * tq:(h + 1) * tq].T.astype(o_ref.dtype)


def dsa_attention(qk, vt, pidx, wt, band, tq=256, kc=512):
    S = qk.shape[0]
    topk = min(TOPK_MAX, S // 4)
    aw = A_HEADS * HEAD_DIM
    iw = IDX_HEADS * IDX_DIM
    once = pl.Buffered(1)
    return pl.pallas_call(
        functools.partial(_dsa_kernel, tq=tq, kc=kc, topk=topk, c0=_band_c0(kc)),
        grid=(S // tq,),
        in_specs=[pl.BlockSpec((tq, aw), lambda i: (i, 0)),
                  pl.BlockSpec((tq, iw), lambda i: (i, 0)),
                  pl.BlockSpec((IDX_HEADS, tq), lambda i: (0, i)),
                  pl.BlockSpec((S, aw), lambda i: (0, 1), pipeline_mode=once),
                  pl.BlockSpec((aw, S), lambda i: (0, 0), pipeline_mode=once),
                  pl.BlockSpec((S, LANE), lambda i: (0, iw // LANE), pipeline_mode=once),
                  pl.BlockSpec(band.shape, lambda i: (0, 0, 0), pipeline_mode=once)],
        out_specs=pl.BlockSpec((tq, aw), lambda i: (i, 0)),
        out_shape=jax.ShapeDtypeStruct((S, aw), BF16),
        scratch_shapes=[pltpu.VMEM((S, tq), jnp.int16),
                        pltpu.VMEM((S, tq), jnp.int16),
                        pltpu.VMEM((1, tq), jnp.int32),
                        pltpu.VMEM((IDX_HEADS, tq, LANE), BF16),
                        pltpu.VMEM((2, kc, A_HEADS * tq), F32),
                        pltpu.VMEM((HEAD_DIM, A_HEADS * tq), F32),
                        pltpu.VMEM((1, A_HEADS * tq), F32),
                        pltpu.VMEM((1, A_HEADS * tq), F32)],
        compiler_params=_cparams(("arbitrary",)),
        name="dsa_attention",
    )(qk, pidx, wt, qk, vt, pidx, band)


DIL_ROWS = 384


def _dilated_kernel(q_ref, k_ref, vt_ref, b0_ref, b1_ref, b2_ref, o_ref, s_ref, *, tq):
    t0 = pl.program_id(1) * tq
    q = q_ref[...]
    tiles = []
    row0 = 0
    for band_ref, (_, dil) in zip((b0_ref, b1_ref, b2_ref), DILATED_PAIRS):
        span = DIL_STEPS * dil
        start = jnp.maximum(t0 - span, 0)
        off = span - (t0 - start)
        for j in range((span + tq) // DIL_ROWS):
            ks = pl.multiple_of(start + j * DIL_ROWS, LANE)
            bs = pl.multiple_of(off + j * DIL_ROWS, LANE)
            s_ref[row0:row0 + DIL_ROWS, :] = (_dot_nt(k_ref[pl.ds(ks, DIL_ROWS), :], q)
                                              + band_ref[0, pl.ds(bs, DIL_ROWS), :])
            tiles.append((ks, row0))
            row0 += DIL_ROWS
    m = jnp.full((1, tq), NEG, F32)
    l = jnp.zeros((1, tq), F32)
    acc = jnp.zeros((HEAD_DIM, tq), F32)
    for ks, r0 in tiles:
        s = s_ref[r0:r0 + DIL_ROWS, :]
        m_new = jnp.maximum(m, jnp.max(s, axis=0, keepdims=True))
        alpha = jnp.exp2(m - m_new)
        p = jnp.exp2(s - m_new)
        l = alpha * l + jnp.sum(p, axis=0, keepdims=True)
        acc = alpha * acc + _dot(vt_ref[:, pl.ds(ks, DIL_ROWS)], p.astype(BF16))
        m = m_new
    o_ref[...] = (acc / l).T.astype(o_ref.dtype)


def dilated_attention(qk, vt, bands, vt_row0, tq=256):
    S = qk.shape[0]
    vh0 = vt_row0 // HEAD_DIM
    lengths = [DIL_STEPS * dil + tq for _, dil in DILATED_PAIRS]
    assert all(n % DIL_ROWS == 0 and n <= S for n in lengths)
    return pl.pallas_call(
        functools.partial(_dilated_kernel, tq=tq),
        grid=(B_HEADS, S // tq),
        in_specs=[pl.BlockSpec((tq, LANE), lambda h, i: (i, h)),
                  pl.BlockSpec((S, LANE), lambda h, i: (0, B_HEADS + h)),
                  pl.BlockSpec((HEAD_DIM, S), lambda h, i: (vh0 + h, 0))]
                 + [pl.BlockSpec((1,) + b.shape[1:], lambda h, i: (h, 0, 0)) for b in bands],
        out_specs=pl.BlockSpec((tq, LANE), lambda h, i: (i, h)),
        out_shape=jax.ShapeDtypeStruct((S, B_HEADS * HEAD_DIM), BF16),
        scratch_shapes=[pltpu.VMEM((sum(lengths), tq), F32)],
        compiler_params=_cparams(("parallel", "arbitrary")),
        name="dilated_attention",
    )(qk, qk, vt, *bands)


def _diff_kernel(lam_ref, g_ref, q_ref, k_ref, vt_ref, band_ref, o_ref, acc_ref, m_ref, l_ref, s_ref,
                 *, tq, tk, c0, lam_init):
    i = pl.program_id(1)
    t0 = i * tq
    nchunk = (t0 + tq + tk - 1) // tk
    last_chunk = k_ref.shape[0] // tk - 1
    q = q_ref[...]
    lane = lax.broadcasted_iota(jnp.int32, q.shape, 1)
    zero = jnp.zeros_like(q)
    q2 = jnp.concatenate([jnp.where(lane < C_HALF, q, zero), jnp.where(lane >= C_HALF, q, zero)], axis=0)
    ones_rows = jnp.ones((ONES_ROWS, tk), BF16)

    m_ref[...] = jnp.full(m_ref.shape, NEG, F32)
    l_ref[...] = jnp.zeros(l_ref.shape, F32)
    acc_ref[...] = jnp.zeros(acc_ref.shape, F32)

    def logits(c):
        s0 = pl.multiple_of(jnp.minimum(c, last_chunk) * tk, tk)
        off = pl.multiple_of(c0 - jnp.clip(t0 - c * tk, -tk, c0), LANE)
        bias = band_ref[0, pl.ds(off, tk), :]
        return _dot_nt(k_ref[pl.ds(s0, tk), :], q2) + jnp.concatenate([bias, bias], axis=1)

    def consume(c, s):
        s0 = pl.multiple_of(jnp.minimum(c, last_chunk) * tk, tk)
        m_prev = m_ref[...]
        m_new = jnp.maximum(m_prev, jnp.max(s, axis=0, keepdims=True))
        alpha = jnp.exp2(m_prev - m_new)
        p = jnp.exp2(s - m_new).astype(BF16)
        pv = _dot(jnp.concatenate([vt_ref[:, pl.ds(s0, tk)], ones_rows], axis=0), p)
        l_ref[...] = alpha * l_ref[...] + pv[HEAD_DIM:HEAD_DIM + 1]
        acc_ref[...] = alpha * acc_ref[...] + pv[:HEAD_DIM]
        m_ref[...] = m_new

    s_ref[0] = logits(0)

    def chunk_pair(cc, carry):
        c = 2 * cc
        s_ref[1] = logits(c + 1)
        consume(c, s_ref[0])
        s_ref[0] = logits(c + 2)
        consume(c + 1, s_ref[1])
        return carry

    lax.fori_loop(0, (nchunk + 1) // 2, chunk_pair, 0)

    lp = lam_ref[...]
    lam = (jnp.exp(jnp.sum(lp[0:1] * lp[1:2], axis=1, keepdims=True))
           - jnp.exp(jnp.sum(lp[2:3] * lp[3:4], axis=1, keepdims=True)) + lam_init)
    o = acc_ref[...] / l_ref[...]
    o = o[:, :tq] - lam * o[:, tq:]
    ms = jnp.mean(o * o, axis=0, keepdims=True)
    o = o * lax.rsqrt(ms + RMS_EPS) * (g_ref[...] * (1.0 - lam_init))
    o_ref[...] = o.T.astype(o_ref.dtype)


def diff_attention(qk, vt, band, lam_params, subln_g, lam_init, vt_row0, tq=256, tk=512):
    S = qk.shape[0]
    vh0 = vt_row0 // HEAD_DIM
    return pl.pallas_call(
        functools.partial(_diff_kernel, tq=tq, tk=tk, c0=_band_c0(tk), lam_init=lam_init),
        grid=(C_HEADS, S // tq),
        in_specs=[pl.BlockSpec((4, C_HALF), lambda h, i: (0, 0)),
                  pl.BlockSpec((HEAD_DIM, 1), lambda h, i: (0, 0)),
                  pl.BlockSpec((tq, LANE), lambda h, i: (i, h)),
                  pl.BlockSpec((S, LANE), lambda h, i: (0, C_HEADS + h)),
                  pl.BlockSpec((HEAD_DIM, S), lambda h, i: (vh0 + h, 0)),
                  pl.BlockSpec((1,) + band.shape[1:], lambda h, i: (h, 0, 0))],
        out_specs=pl.BlockSpec((tq, LANE), lambda h, i: (i, h)),
        out_shape=jax.ShapeDtypeStruct((S, C_HEADS * HEAD_DIM), BF16),
        scratch_shapes=[pltpu.VMEM((HEAD_DIM, 2 * tq), F32),
                        pltpu.VMEM((1, 2 * tq), F32),
                        pltpu.VMEM((1, 2 * tq), F32),
                        pltpu.VMEM((2, tk, 2 * tq), F32)],
        compiler_params=_cparams(("parallel", "arbitrary")),
        name="diff_attention",
    )(lam_params, subln_g.reshape(HEAD_DIM, 1), qk, qk, vt, band)


def _outproj_kernel(x_ref, a_ref, b_ref, c_ref, wa_ref, wb_ref, wc_ref, g_ref, o_ref, h_ref):
    y = (x_ref[...] + _dot(a_ref[...], wa_ref[...]) + _dot(b_ref[...], wb_ref[...])
         + _dot(c_ref[...], wc_ref[...]))
    o_ref[...] = y
    ms = jnp.mean(y * y, axis=-1, keepdims=True)
    h_ref[...] = (y * lax.rsqrt(ms + RMS_EPS) * g_ref[...]).astype(h_ref.dtype)


def out_projection(x, oa, ob, oc, wa, wb, wc, g, tm=512):
    S, D = x.shape
    once = pl.Buffered(1)
    row = lambda i: (i, 0)
    fixed = lambda i: (0, 0)
    return pl.pallas_call(
        _outproj_kernel,
        grid=(S // tm,),
        in_specs=[pl.BlockSpec((tm, D), row),
                  pl.BlockSpec((tm, oa.shape[1]), row),
                  pl.BlockSpec((tm, ob.shape[1]), row),
                  pl.BlockSpec((tm, oc.shape[1]), row),
                  pl.BlockSpec(wa.shape, fixed, pipeline_mode=once),
                  pl.BlockSpec(wb.shape, fixed, pipeline_mode=once),
                  pl.BlockSpec(wc.shape, fixed, pipeline_mode=once),
                  pl.BlockSpec((1, D), fixed)],
        out_specs=[pl.BlockSpec((tm, D), row), pl.BlockSpec((tm, D), row)],
        out_shape=[jax.ShapeDtypeStruct((S, D), F32), jax.ShapeDtypeStruct((S, D), BF16)],
        compiler_params=_cparams(("parallel",)),
        name="out_projection",
    )(x, oa, ob, oc, wa, wb, wc, g.reshape(1, D))


def _silu(x):
    return x / (1.0 + jnp.exp(-x))


def _ffn_kernel(x_ref, h_ref, wg_ref, wu_ref, wd_ref, g_ref, o_ref, hn_ref):
    f = pl.program_id(1)

    @pl.when(f == 0)
    def _():
        o_ref[...] = x_ref[...]

    h = h_ref[...]
    a = _silu(_dot(h, wg_ref[...].astype(BF16))) * _dot(h, wu_ref[...].astype(BF16))
    o_ref[...] += _dot(a.astype(BF16), wd_ref[...].astype(BF16))

    @pl.when(f == pl.num_programs(1) - 1)
    def _():
        y = o_ref[...]
        ms = jnp.mean(y * y, axis=-1, keepdims=True)
        hn_ref[...] = (y * lax.rsqrt(ms + RMS_EPS) * g_ref[...]).astype(hn_ref.dtype)


def ffn_dense(x, h, wg, wu, wd, g_next, tm=1024, tf=256):
    S, D = x.shape
    F = wg.shape[1]
    once = pl.Buffered(1)
    row = lambda i, f: (i, 0)
    return pl.pallas_call(
        _ffn_kernel,
        grid=(S // tm, F // tf),
        in_specs=[pl.BlockSpec((tm, D), row, pipeline_mode=once),
                  pl.BlockSpec((tm, D), row, pipeline_mode=once),
                  pl.BlockSpec((D, tf), lambda i, f: (0, f)),
                  pl.BlockSpec((D, tf), lambda i, f: (0, f)),
                  pl.BlockSpec((tf, D), lambda i, f: (f, 0)),
                  pl.BlockSpec((1, D), lambda i, f: (0, 0))],
        out_specs=[pl.BlockSpec((tm, D), row), pl.BlockSpec((tm, D), row)],
        out_shape=[jax.ShapeDtypeStruct((S, D), F32), jax.ShapeDtypeStruct((S, D), BF16)],
        compiler_params=_cparams(("parallel", "arbitrary")),
        name="ffn_dense",
    )(x, h, wg, wu, wd, g_next.reshape(1, D))


def _router_kernel(h_ref, w_ref, c_ref, sel_ref):
    logits = _dot(h_ref[...], w_ref[...])
    lane = lax.broadcasted_iota(jnp.int32, logits.shape, 1).astype(F32)
    logits = jnp.where(lane < N_EXPERTS, logits, NEG)
    m1 = jnp.max(logits, axis=1, keepdims=True)
    i1 = jnp.min(jnp.where(logits == m1, lane, float(LANE)), axis=1, keepdims=True)
    rest = jnp.where(lane == i1, NEG, logits)
    m2 = jnp.max(rest, axis=1, keepdims=True)
    i2 = jnp.min(jnp.where(rest == m2, lane, float(LANE)), axis=1, keepdims=True)
    e2 = jnp.exp(m2 - m1)
    g1 = 1.0 / (1.0 + e2)
    g2 = e2 / (1.0 + e2)
    pick1, pick2 = lane == i1, lane == i2
    c_ref[...] = jnp.where(pick1, g1, 0.0) + jnp.where(pick2, g2, 0.0)
    sel_ref[...] = jnp.where(pick1 | pick2, 1.0, 0.0)


def moe_router(h, w_router, tm=1024):
    S, D = h.shape
    wpad = jnp.zeros((D, LANE), BF16).at[:, :N_EXPERTS].set(w_router.astype(BF16))
    spec = pl.BlockSpec((tm, LANE), lambda i: (i, 0))
    sds = jax.ShapeDtypeStruct((S, LANE), F32)
    return pl.pallas_call(
        _router_kernel,
        grid=(S // tm,),
        in_specs=[pl.BlockSpec((tm, D), lambda i: (i, 0)),
                  pl.BlockSpec((D, LANE), lambda i: (0, 0))],
        out_specs=[spec, spec],
        out_shape=[sds, sds],
        compiler_params=_cparams(("parallel",)),
        name="moe_router",
    )(h, wpad)


MOE_CH = 256
MOE_TILE = 1024


def _moe_rank_kernel(sel_ref, rm_ref, rmt_ref, cum_ref, tot_ref, carry_ref):
    c = pl.program_id(0)

    @pl.when(c == 0)
    def _():
        carry_ref[...] = jnp.zeros(carry_ref.shape, F32)

    sel = sel_ref[...]
    ch = sel.shape[0]
    before = (lax.broadcasted_iota(jnp.int32, (ch, ch), 1) < lax.broadcasted_iota(jnp.int32, (ch, ch), 0))
    rank = _dot(jnp.where(before, 1.0, 0.0).astype(BF16), sel.astype(BF16)) + carry_ref[...]
    rm = jnp.where(sel > 0.0, rank, -1.0)
    rm_ref[...] = rm
    rmt_ref[...] = rm.T[:N_EXPERTS]
    cum_ref[0] = carry_ref[...]
    carry_ref[...] += jnp.sum(sel, axis=0, keepdims=True)
    tot_ref[...] = carry_ref[...]


def moe_rank(sel):
    S = sel.shape[0]
    nch = S // MOE_CH
    return pl.pallas_call(
        _moe_rank_kernel,
        grid=(nch,),
        in_specs=[pl.BlockSpec((MOE_CH, LANE), lambda c: (c, 0))],
        out_specs=[pl.BlockSpec((MOE_CH, LANE), lambda c: (c, 0)),
                   pl.BlockSpec((N_EXPERTS, MOE_CH), lambda c: (0, c)),
                   pl.BlockSpec((1, 1, LANE), lambda c: (c, 0, 0)),
                   pl.BlockSpec((1, LANE), lambda c: (0, 0))],
        out_shape=[jax.ShapeDtypeStruct((S, LANE), F32),
                   jax.ShapeDtypeStruct((N_EXPERTS, S), F32),
                   jax.ShapeDtypeStruct((nch, 1, LANE), F32),
                   jax.ShapeDtypeStruct((1, LANE), F32)],
        scratch_shapes=[pltpu.VMEM((1, LANE), F32)],
        compiler_params=_cparams(("arbitrary",)),
        name="moe_rank",
    )(sel)


def _moe_expert_kernel(te_ref, tr0_ref, tclo_ref, tchi_ref, tval_ref, cumx_ref,
                       rk_ref, h_hbm, wg_ref, wu_ref, wd_ref, y_ref,
                       hs_ref, acc_ref, hbuf_ref, sem_ref, *, n_experts):
    j = pl.program_id(0)
    f = pl.program_id(1)
    nf = pl.num_programs(1)
    nrows = tval_ref[j]
    tile = hs_ref.shape[0]
    win = MOE_CH + 8

    def chunk_copy(c, slot):
        return pltpu.make_async_copy(h_hbm.at[pl.ds(c * MOE_CH, MOE_CH)], hbuf_ref.at[slot], sem_ref.at[slot])

    @pl.when((nrows > 0) & (f == 0))
    def _():
        clo, chi = tclo_ref[j], tchi_ref[j]
        tr0 = tr0_ref[j]
        wrow = lax.broadcasted_iota(jnp.int32, (win, MOE_CH), 0)
        acc_ref[...] = jnp.zeros(acc_ref.shape, F32)

        @pl.when(chi > clo)
        def _():
            chunk_copy(clo, 0).start()

        def body(c, carry):
            slot = (c - clo) % 2

            @pl.when(c + 1 < chi)
            def _():
                chunk_copy(c + 1, 1 - slot).start()

            chunk_copy(c, slot).wait()
            first = cumx_ref[c * n_experts + te_ref[j]] - tr0
            w0 = pl.multiple_of(jnp.clip((first // 8) * 8, 0, tile - win), 8)
            rk = rk_ref[0, :, pl.ds(pl.multiple_of(c * MOE_CH, MOE_CH), MOE_CH)]
            onehot = jnp.where(rk == (tr0 + w0 + wrow).astype(F32), 1.0, 0.0).astype(BF16)
            acc_ref[pl.ds(w0, win), :] += _dot(onehot, hbuf_ref[slot])
            return carry

        lax.fori_loop(clo, chi, body, 0)
        hs_ref[...] = acc_ref[...].astype(BF16)
        acc_ref[...] = jnp.zeros(acc_ref.shape, F32)

    def swiglu(rows):
        hs = hs_ref[:rows]
        a = _silu(_dot(hs, wg_ref[0].astype(BF16))) * _dot(hs, wu_ref[0].astype(BF16))
        acc_ref[:rows] += _dot(a.astype(BF16), wd_ref[0].astype(BF16))

    sizes = (tile, tile // 2, tile // 4, tile // 8)
    for rows, below in zip(sizes, sizes[1:] + (0,)):
        @pl.when((nrows > below) & (nrows <= rows))
        def _():
            swiglu(rows)

    @pl.when(f == nf - 1)
    def _():
        y_ref[...] = jnp.where(nrows > 0, acc_ref[...], 0.0).astype(y_ref.dtype)


def _moe_combine_kernel(ca_ref, cb_ref, roff_ref, x_ref, rm_ref, cw_ref, *refs):
    y_refs, o_ref = refs[:-1], refs[-1]
    n_experts = len(y_refs) // 2
    c = pl.program_id(0)
    rm = rm_ref[...]
    cw = cw_ref[...]
    lane = lax.broadcasted_iota(jnp.int32, rm.shape, 1)
    cr = y_refs[0].shape[0]
    col = lax.broadcasted_iota(jnp.int32, (rm.shape[0], cr), 1)
    out = x_ref[...]
    for e in range(n_experts):
        mine = lane == e
        pos = jnp.sum(jnp.where(mine, rm, 0.0), axis=1, keepdims=True)
        gate = jnp.sum(jnp.where(mine, cw, 0.0), axis=1, keepdims=True)
        row = jnp.where(pos >= 0.0, pos + roff_ref[e].astype(F32), -1.0)
        ca, cb = ca_ref[c * n_experts + e], cb_ref[c * n_experts + e]
        for y_ref, base in ((y_refs[2 * e], ca * cr),
                            (y_refs[2 * e + 1], jnp.where(cb != ca, cb * cr, -2 * cr))):
            onehot = jnp.where(row == (base + col).astype(F32), 1.0, 0.0).astype(BF16)
            out = out + gate * _dot(onehot, y_ref[...])
    o_ref[...] = out


def ffn_moe(x, h, combine, sel, wg, wu, wd, tf=512):
    S, D = x.shape
    E, _, F = wg.shape
    tf = min(tf, F)
    T = MOE_TILE
    nch = S // MOE_CH
    J = 2 * S // T + E
    rm, rmt, cum, tot = moe_rank(sel)

    cnt = tot[0, :E].astype(jnp.int32)
    ntile = (cnt + T - 1) // T
    tend = jnp.cumsum(ntile)
    tstart = tend - ntile
    jj = jnp.arange(J, dtype=jnp.int32)
    te = jnp.minimum(jnp.searchsorted(tend, jnp.minimum(jj, tend[-1] - 1), side="right"), E - 1).astype(jnp.int32)
    tr0 = (jnp.minimum(jj, tend[-1] - 1) - tstart[te]) * T
    tval = jnp.where(jj < tend[-1], jnp.clip(cnt[te] - tr0, 0, T), 0).astype(jnp.int32)
    cumx = cum[:, 0, :E].astype(jnp.int32)
    cumi = jnp.concatenate([cumx[1:], cnt[None]], axis=0)
    tclo = jnp.sum(cumi[:, te] <= tr0[None, :], axis=0).astype(jnp.int32)
    tchi = jnp.sum(cumx[:, te] < (tr0 + tval)[None, :], axis=0).astype(jnp.int32)
    roff = (tstart * T).astype(jnp.int32)
    nrow_chunks = J * T // MOE_CH
    ca = ((roff[None, :] + cumx) // MOE_CH).astype(jnp.int32)
    cb = jnp.minimum(ca + 1, nrow_chunks - 1)
    ca, cb = ca.reshape(-1), cb.reshape(-1)

    nf = F // tf

    def wmap(j, f, te_r, tr0_r, clo_r, chi_r, val_r, cumx_r):
        return (te_r[j], 0, jnp.where(val_r[j] > 0, f, nf - 1))

    def wdmap(j, f, te_r, tr0_r, clo_r, chi_r, val_r, cumx_r):
        return (te_r[j], jnp.where(val_r[j] > 0, f, nf - 1), 0)

    y = pl.pallas_call(
        functools.partial(_moe_expert_kernel, n_experts=E),
        grid_spec=pltpu.PrefetchScalarGridSpec(
            num_scalar_prefetch=6,
            grid=(J, nf),
            in_specs=[pl.BlockSpec((1, 1, S), lambda j, f, te_r, *_: (te_r[j], 0, 0)),
                      pl.BlockSpec(memory_space=pl.ANY),
                      pl.BlockSpec((1, D, tf), wmap),
                      pl.BlockSpec((1, D, tf), wmap),
                      pl.BlockSpec((1, tf, D), wdmap)],
            out_specs=pl.BlockSpec((T, D), lambda j, f, *_: (j, 0)),
            scratch_shapes=[pltpu.VMEM((T, D), BF16),
                            pltpu.VMEM((T, D), F32),
                            pltpu.VMEM((2, MOE_CH, D), BF16),
                            pltpu.SemaphoreType.DMA((2,))]),
        out_shape=jax.ShapeDtypeStruct((J * T, D), BF16),
        compiler_params=_cparams(("arbitrary", "arbitrary")),
        name="moe_experts",
    )(te, tr0, tclo, tchi, tval, cumx.reshape(-1), rmt.reshape(E, 1, S), h, wg, wu, wd)

    def chunk_map(c, ca_r, cb_r, ro_r, *, e, second):
        return ((cb_r if second else ca_r)[c * E + e], 0)

    return pl.pallas_call(
        _moe_combine_kernel,
        grid_spec=pltpu.PrefetchScalarGridSpec(
            num_scalar_prefetch=3,
            grid=(nch,),
            in_specs=[pl.BlockSpec((MOE_CH, D), lambda c, *_: (c, 0)),
                      pl.BlockSpec((MOE_CH, LANE), lambda c, *_: (c, 0)),
                      pl.BlockSpec((MOE_CH, LANE), lambda c, *_: (c, 0))]
                     + [pl.BlockSpec((MOE_CH, D), functools.partial(chunk_map, e=e, second=second))
                        for e in range(E) for second in (False, True)],
            out_specs=pl.BlockSpec((MOE_CH, D), lambda c, *_: (c, 0))),
        out_shape=jax.ShapeDtypeStruct((S, D), F32),
        compiler_params=_cparams(("parallel",)),
        name="moe_combine",
    )(ca, cb, roff, x, rm, combine, *([y] * (2 * E)))


def _tile_gain(g, reps, scale=1.0):
    return jnp.tile(g.astype(F32) * scale, reps)


def kernel(x, w_in, w_out, attn_norm_g, ffn_norm_g, q_norm_a, k_norm_a, q_norm_b, k_norm_b,
           q_norm_c, k_norm_c, lambda_q1, lambda_k1, lambda_q2, lambda_k2, diff_subln_g, rel_bias,
           w_dense_gate, w_dense_up, w_dense_down, w_router, w_moe_gate, w_moe_up, w_moe_down):
    B, S, D = x.shape
    depth = w_in.shape[0]
    assert B == 1 and S % (DIL_STEPS * DILATED_PAIRS[-1][1]) == 0
    aw, bw, cw = A_HEADS * HEAD_DIM, B_HEADS * HEAD_DIM, C_HEADS * HEAD_DIM
    iw = IDX_HEADS * IDX_DIM
    sizes = (aw, aw, aw, iw, IDX_DIM, IDX_HEADS, bw, bw, bw, cw, cw, cw)
    offs = np.concatenate([[0], np.cumsum(sizes)]).tolist()

    def cols(w, *segs):
        return jnp.concatenate([w[:, offs[s]:offs[s + 1]] for s in segs], axis=1)

    tq, tk = 256, 512
    band_a = bias_band(rel_bias, A_HEADS, 0, _band_c0(tk) + 2 * tk, tq, _band_c0(tk))
    band_c = bias_band(rel_bias, C_HEADS, A_HEADS + B_HEADS, _band_c0(tk) + 2 * tk, tq, _band_c0(tk))
    bands_b = [bias_band(rel_bias, B_HEADS, A_HEADS, 2 * DIL_STEPS * dil + tq, tq, DIL_STEPS * dil, dil)
               for _, dil in DILATED_PAIRS]

    xs = x.reshape(S, D)
    h = None
    for layer in range(depth):
        wl = w_in[layer]
        w_pa = cols(wl, 0, 1).astype(BF16)
        w_pb = cols(wl, 6, 7).astype(BF16)
        w_pc = cols(wl, 9, 10).astype(BF16)
        w_pi = cols(wl, 3, 4, 4).astype(BF16)
        w_vt = cols(wl, 2, 8, 11).T.astype(BF16)
        w_wt = cols(wl, 5).T.astype(BF16)
        qs_ab, qs_c = HEAD_DIM ** -0.5 * LOG2E, C_HALF ** -0.5 * LOG2E
        g_pa = jnp.concatenate([_tile_gain(q_norm_a[layer], A_HEADS, qs_ab), _tile_gain(k_norm_a[layer], A_HEADS)])
        g_pb = jnp.concatenate([_tile_gain(q_norm_b[layer], B_HEADS, qs_ab), _tile_gain(k_norm_b[layer], B_HEADS)])
        g_pc = jnp.concatenate([_tile_gain(q_norm_c[layer], 2 * C_HEADS, qs_c),
                                _tile_gain(k_norm_c[layer], 2 * C_HEADS)])

        if h is None:
            h = rmsnorm(xs, attn_norm_g[layer])
        p_a = project(h, w_pa, g_pa, HEAD_DIM, BF16, tn=512)
        p_b = project(h, w_pb, g_pb, HEAD_DIM, BF16, tn=512)
        p_c = project(h, w_pc, g_pc, C_HALF, BF16, tn=512)
        p_i = project(h, w_pi, jnp.ones((w_pi.shape[1],), F32), 0, BF16, tn=w_pi.shape[1])
        vt = project_t(h, w_vt, BF16, tn=512)
        wt = project_t(h, w_wt, F32)

        oa = dsa_attention(p_a, vt, p_i, wt, band_a)

        ob = dilated_attention(p_b, vt, bands_b, aw, tq=tq)

        lam_init = 0.8 - 0.6 * math.exp(-0.3 * layer)
        lam_params = jnp.stack([lambda_q1[layer], lambda_k1[layer], lambda_q2[layer], lambda_k2[layer]]).astype(F32)
        oc = diff_attention(p_c, vt, band_c, lam_params, diff_subln_g[layer].astype(F32), lam_init, aw + bw,
                            tq=tq, tk=tk)

        wo = w_out[layer].astype(BF16)
        xs, h2 = out_projection(xs, oa, ob, oc, wo[:aw], wo[aw:aw + bw], wo[aw + bw:], ffn_norm_g[layer])

        f = layer // 2
        if layer % 2 == 0:
            g_next = attn_norm_g[min(layer + 1, depth - 1)]
            xs, h = ffn_dense(xs, h2, w_dense_gate[f], w_dense_up[f], w_dense_down[f], g_next)
        else:
            combine, sel = moe_router(h2, w_router[f])
            xs = ffn_moe(xs, h2, combine, sel, w_moe_gate[f], w_moe_up[f], w_moe_down[f])
            h = None
    return xs.reshape(B, S, D)
```

```python
import functools
import math

import jax
import jax.numpy as jnp
import numpy as np
from jax import lax
from jax.experimental import pallas as pl
from jax.experimental.pallas import tpu as pltpu

F32 = jnp.float32
BF16 = jnp.bfloat16

HEAD_DIM = 128
A_HEADS, B_HEADS, C_HEADS = 4, 6, 6
C_HALF = HEAD_DIM // 2
IDX_HEADS, IDX_DIM = 16, 64
TOPK_MAX = 256
DILATED_PAIRS = ((128, 1), (512, 4), (2048, 16))
DIL_STEPS = 128
N_BUCKETS = 32
MAX_DISTANCE = 2048
N_EXPERTS = 8
RMS_EPS = 1e-6
NEG = -1e30
LOG2E = math.log2(math.e)
INT_MIN = -(2 ** 31)
I16_BIAS = 2 ** 15
CNT_ROWS = 64
ONES_ROWS = 16
LANE = 128
VMEM_LIMIT = 56 * 1024 * 1024

BUCKET_EDGES = tuple(range(16)) + (16,) + tuple(
    int(math.ceil(16.0 * (MAX_DISTANCE / 16.0) ** (k / 16.0))) for k in range(1, 16))
FAR_DIST = BUCKET_EDGES[-1]


def _cparams(sem):
    return pltpu.CompilerParams(dimension_semantics=sem, vmem_limit_bytes=VMEM_LIMIT)


def _dot(a, b):
    return jnp.dot(a, b, preferred_element_type=F32)


def _dot_nt(a, b):
    return lax.dot_general(a, b, (((1,), (1,)), ((), ())), preferred_element_type=F32)


def _rmsnorm_kernel(x_ref, g_ref, o_ref):
    x = x_ref[...]
    ms = jnp.mean(x * x, axis=-1, keepdims=True)
    o_ref[...] = (x * lax.rsqrt(ms + RMS_EPS) * g_ref[...]).astype(o_ref.dtype)


def rmsnorm(x, g, tm=512):
    S, D = x.shape
    return pl.pallas_call(
        _rmsnorm_kernel,
        grid=(S // tm,),
        in_specs=[pl.BlockSpec((tm, D), lambda i: (i, 0)), pl.BlockSpec((1, D), lambda i: (0, 0))],
        out_specs=pl.BlockSpec((tm, D), lambda i: (i, 0)),
        out_shape=jax.ShapeDtypeStruct((S, D), BF16),
        compiler_params=_cparams(("parallel",)),
        name="rmsnorm",
    )(x, g.reshape(1, D))


def _proj_kernel(h_ref, w_ref, g_ref, o_ref, *, group):
    acc = _dot(h_ref[...], w_ref[...])
    tn = acc.shape[1]
    if group == 0:
        o_ref[...] = acc.astype(o_ref.dtype)
        return
    for c in range(tn // LANE):
        a = acc[:, c * LANE:(c + 1) * LANE]
        sq = a * a
        if group == LANE:
            ms = jnp.mean(sq, axis=-1, keepdims=True)
        else:
            lane = lax.broadcasted_iota(jnp.int32, sq.shape, 1)
            lo = jnp.sum(jnp.where(lane < group, sq, 0.0), axis=-1, keepdims=True)
            hi = jnp.sum(jnp.where(lane >= group, sq, 0.0), axis=-1, keepdims=True)
            ms = jnp.where(lane < group, lo, hi) * (1.0 / group)
        y = a * lax.rsqrt(ms + RMS_EPS) * g_ref[:, c * LANE:(c + 1) * LANE]
        o_ref[:, c * LANE:(c + 1) * LANE] = y.astype(o_ref.dtype)


def project(h, w, gains, group, out_dtype, tm=1024, tn=256):
    S, K = h.shape
    N = w.shape[1]
    return pl.pallas_call(
        functools.partial(_proj_kernel, group=group),
        grid=(S // tm, N // tn),
        in_specs=[pl.BlockSpec((tm, K), lambda i, j: (i, 0)),
                  pl.BlockSpec((K, tn), lambda i, j: (0, j)),
                  pl.BlockSpec((1, tn), lambda i, j: (0, j))],
        out_specs=pl.BlockSpec((tm, tn), lambda i, j: (i, j)),
        out_shape=jax.ShapeDtypeStruct((S, N), out_dtype),
        compiler_params=_cparams(("parallel", "arbitrary")),
        name=f"proj_g{group}",
    )(h, w, gains.reshape(1, N))


def _proj_t_kernel(wt_ref, h_ref, o_ref):
    o_ref[...] = _dot_nt(wt_ref[...], h_ref[...]).astype(o_ref.dtype)


def project_t(h, wt, out_dtype, tm=1024, tn=256):
    S, K = h.shape
    N = wt.shape[0]
    tn = min(tn, N)
    return pl.pallas_call(
        _proj_t_kernel,
        grid=(S // tm, N // tn),
        in_specs=[pl.BlockSpec((tn, K), lambda i, j: (j, 0)),
                  pl.BlockSpec((tm, K), lambda i, j: (i, 0))],
        out_specs=pl.BlockSpec((tn, tm), lambda i, j: (j, i)),
        out_shape=jax.ShapeDtypeStruct((N, S), out_dtype),
        compiler_params=_cparams(("parallel", "arbitrary")),
        name="proj_t",
    )(wt, h)


BAND_ROWS = 256


def _band_kernel(tab_ref, o_ref, *, c0, head_off, dil):
    hh = head_off + pl.program_id(0)
    _, rows, width = o_ref.shape
    base = (lax.broadcasted_iota(jnp.int32, (BAND_ROWS, width), 1)
            - lax.broadcasted_iota(jnp.int32, (BAND_ROWS, width), 0))
    hi_valid = DIL_STEPS * dil if dil else None
    for r in range(rows // BAND_ROWS):
        d0 = c0 - r * BAND_ROWS
        lo, hi = d0 - (BAND_ROWS - 1), d0 + width - 1
        if hi_valid is not None:
            hi = min(hi, hi_valid)
        block = pl.ds(r * BAND_ROWS, BAND_ROWS)
        if hi < 0 or hi < lo:
            o_ref[0, block, :] = jnp.full((BAND_ROWS, width), NEG, F32)
            continue
        dist = base + d0
        first = max(b for b in range(N_BUCKETS) if BUCKET_EDGES[b] <= max(lo, 0))
        val = jnp.full((BAND_ROWS, width), tab_ref[first, hh] * LOG2E, F32)
        for b in range(first + 1, N_BUCKETS):
            if BUCKET_EDGES[b] <= hi:
                val = jnp.where(dist >= BUCKET_EDGES[b], tab_ref[b, hh] * LOG2E, val)
        ok = None
        if lo < 0:
            ok = dist >= 0
        if dil:
            in_win = (dist <= hi_valid) & ((dist & (dil - 1)) == 0)
            ok = in_win if ok is None else ok & in_win
        o_ref[0, block, :] = val if ok is None else jnp.where(ok, val, NEG)


def bias_band(rel_bias, n_heads, head_off, rows, width, c0, dil=0):
    assert rows % BAND_ROWS == 0 and dil & (dil - 1) == 0
    return pl.pallas_call(
        functools.partial(_band_kernel, c0=c0, head_off=head_off, dil=dil),
        grid=(n_heads,),
        in_specs=[pl.BlockSpec(memory_space=pltpu.SMEM)],
        out_specs=pl.BlockSpec((1, rows, width), lambda h: (h, 0, 0)),
        out_shape=jax.ShapeDtypeStruct((n_heads, rows, width), F32),
        compiler_params=_cparams(("parallel",)),
        name=f"bias_band_d{dil}",
    )(rel_bias)


def _band_c0(tk):
    return -(-(FAR_DIST + tk - 1) // LANE) * LANE


def _dsa_kernel(q_ref, qi_ref, wt_ref, k_ref, vt_ref, kidx_ref, band_ref, o_ref,
                hi_ref, lo_ref, cut_ref, qih_ref, s_ref, acc_ref, m_ref, l_ref, *, tq, kc, topk, c0):
    i = pl.program_id(0)
    t0 = i * tq
    nchunk = (t0 + tq + kc - 1) // kc
    half = lax.broadcasted_iota(jnp.int32, (tq, LANE), 1) < IDX_DIM

    for h in range(IDX_HEADS):
        slab = qi_ref[:, (h // 2) * LANE:(h // 2 + 1) * LANE]
        keep = half if h % 2 == 0 else jnp.logical_not(half)
        qih_ref[h] = jnp.where(keep, slab, jnp.zeros_like(slab))
    w = wt_ref[...] * (IDX_HEADS ** -0.5 * IDX_DIM ** -0.5)

    pos_q = t0 + lax.broadcasted_iota(jnp.int32, (kc, tq), 1)
    row = lax.broadcasted_iota(jnp.int32, (kc, tq), 0)

    def score_chunk(c, carry):
        s0 = pl.multiple_of(c * kc, kc)
        kx = kidx_ref[pl.ds(s0, kc), :]
        acc = jnp.zeros((kc, tq), F32)
        for h in range(IDX_HEADS):
            acc = acc + w[h:h + 1, :] * jnp.maximum(_dot_nt(kx, qih_ref[h]), 0.0)
        bits = lax.bitcast_convert_type(acc, jnp.int32)
        key = bits ^ ((bits >> 31) & 0x7FFFFFFF)
        key = jnp.where(s0 + row <= pos_q, key, INT_MIN)
        hi_ref[pl.ds(s0, kc), :] = (key >> 16).astype(jnp.int16)
        lo_ref[pl.ds(s0, kc), :] = ((key & 0xFFFF) - I16_BIAS).astype(jnp.int16)
        return carry

    lax.fori_loop(0, nchunk, score_chunk, 0)

    one16, zero16 = jnp.ones((), jnp.int16), jnp.zeros((), jnp.int16)
    row16 = lax.broadcasted_iota(jnp.int32, (kc, tq), 0).astype(jnp.int16)
    rowc16 = lax.broadcasted_iota(jnp.int32, (CNT_ROWS, tq), 0).astype(jnp.int16)
    n_keys = k_ref.shape[0]

    def to16(v):
        return jnp.clip(v, -I16_BIAS, I16_BIAS - 1).astype(jnp.int16)

    def count(hit):
        def body(c, cnt):
            base = pl.multiple_of(c * kc, kc)
            for g in range(kc // CNT_ROWS):
                cnt = cnt + jnp.where(hit(base + g * CNT_ROWS), one16, zero16)
            return cnt

        cnt = lax.fori_loop(0, nchunk, body, jnp.zeros((CNT_ROWS, tq), jnp.int16))
        return jnp.sum(cnt.astype(jnp.int32).astype(F32), axis=0, keepdims=True)

    def count_ge(ref, cand):
        c16 = cand.astype(jnp.int16)
        return count(lambda r: ref[pl.ds(r, CNT_ROWS), :] >= c16)

    def count_gt(ref, cand):
        c16 = cand.astype(jnp.int16)
        return count(lambda r: ref[pl.ds(r, CNT_ROWS), :] > c16)

    def kth_largest(ref, need):
        t = jnp.where(count_ge(ref, jnp.zeros((1, tq), jnp.int32)) >= need, 0, -I16_BIAS).astype(jnp.int32)

        def bit_body(b, t):
            cand = t + jnp.left_shift(jnp.int32(1), 14 - b)
            return jnp.where(count_ge(ref, cand) >= need, cand, t)

        return lax.fori_loop(0, 15, bit_body, t)

    kf = jnp.full((1, tq), float(topk), F32)
    thi = kth_largest(hi_ref, kf)
    need_lo = kf - count_gt(hi_ref, thi)
    thi16 = thi.astype(jnp.int16)

    def mask_low(c, carry):
        rows = pl.ds(pl.multiple_of(c * kc, kc), kc)
        lo_ref[rows, :] = jnp.where(hi_ref[rows, :] == thi16, lo_ref[rows, :], jnp.int16(-I16_BIAS))
        return carry

    lax.fori_loop(0, nchunk, mask_low, 0)
    tlo = kth_largest(lo_ref, need_lo)
    tlo16 = tlo.astype(jnp.int16)

    def is_eq(r):
        return (hi_ref[pl.ds(r, CNT_ROWS), :] == thi16) & (lo_ref[pl.ds(r, CNT_ROWS), :] == tlo16)

    keep_eq = need_lo - count_gt(lo_ref, tlo)
    excess = (count(is_eq) > keep_eq) & (thi > -I16_BIAS)
    cut_ref[...] = jnp.full(cut_ref.shape, n_keys, jnp.int32)

    @pl.when(jnp.max(jnp.broadcast_to(jnp.where(excess, 1.0, 0.0), (8, tq))) > 0.0)
    def _():
        nbits = n_keys.bit_length()

        def bit_body(b, j):
            cand = j + jnp.left_shift(jnp.int32(1), nbits - 1 - b)
            before = count(lambda r: is_eq(r) & (rowc16 < to16(cand - r)))
            return jnp.where(before < keep_eq, cand, j)

        j = lax.fori_loop(0, nbits, bit_body, jnp.zeros((1, tq), jnp.int32))
        cut_ref[...] = jnp.where(excess, j, n_keys)

    cut = cut_ref[...]
    thi_sel16 = jnp.maximum(thi, 1 - I16_BIAS).astype(jnp.int16)

    m_ref[...] = jnp.full(m_ref.shape, NEG, F32)
    l_ref[...] = jnp.zeros(l_ref.shape, F32)
    acc_ref[...] = jnp.zeros(acc_ref.shape, F32)
    ones_rows = jnp.ones((ONES_ROWS, kc), BF16)

    def logits(c):
        s0 = pl.multiple_of(jnp.minimum(c, nchunk - 1) * kc, kc)
        off = pl.multiple_of(c0 - jnp.clip(t0 - c * kc, -kc, c0), LANE)
        hi = hi_ref[pl.ds(s0, kc), :]
        lo = lo_ref[pl.ds(s0, kc), :]
        sel = (hi > thi_sel16) | ((hi == thi_sel16)
                                  & ((lo > tlo16) | ((lo == tlo16) & (row16 <= to16(cut - s0)))))
        drop = jnp.where(sel, jnp.zeros((), jnp.int16), jnp.ones((), jnp.int16))
        mask_bias = drop.astype(jnp.int32).astype(F32) * NEG
        parts = []
        for h in range(A_HEADS):
            hs = slice(h * HEAD_DIM, (h + 1) * HEAD_DIM)
            parts.append(_dot_nt(k_ref[pl.ds(s0, kc), hs], q_ref[:, hs]) + band_ref[h, pl.ds(off, kc), :]
                         + mask_bias)
        return jnp.concatenate(parts, axis=1)

    def consume(c, s):
        s0 = pl.multiple_of(jnp.minimum(c, nchunk - 1) * kc, kc)
        m_prev = m_ref[...]
        m_new = jnp.maximum(m_prev, jnp.max(s, axis=0, keepdims=True))
        alpha = jnp.exp2(m_prev - m_new)
        p = jnp.exp2(s - m_new).astype(BF16)
        pv = [_dot(jnp.concatenate([vt_ref[h * HEAD_DIM:(h + 1) * HEAD_DIM, pl.ds(s0, kc)], ones_rows], axis=0),
                   p[:, h * tq:(h + 1) * tq]) for h in range(A_HEADS)]
        pv = jnp.concatenate(pv, axis=1)
        l_ref[...] = alpha * l_ref[...] + pv[HEAD_DIM:HEAD_DIM + 1]
        acc_ref[...] = alpha * acc_ref[...] + pv[:HEAD_DIM]
        m_ref[...] = m_new

    s_ref[0] = logits(0)

    def chunk_pair(cc, carry):
        c = 2 * cc
        s_ref[1] = logits(c + 1)
        consume(c, s_ref[0])
        s_ref[0] = logits(c + 2)
        consume(c + 1, s_ref[1])
        return carry

    lax.fori_loop(0, (nchunk + 1) // 2, chunk_pair, 0)
    o = acc_ref[...] / l_ref[...]
    for h in range(A_HEADS):
        o_ref[:, h * HEAD_DIM:(h + 1) * HEAD_DIM] = o[:, h * tq:(h + 1) * tq].T.astype(o_ref.dtype)


def dsa_attention(qk, vt, pidx, wt, band, tq=256, kc=512):
    S = qk.shape[0]
    topk = min(TOPK_MAX, S // 4)
    aw = A_HEADS * HEAD_DIM
    iw = IDX_HEADS * IDX_DIM
    once = pl.Buffered(1)
    return pl.pallas_call(
        functools.partial(_dsa_kernel, tq=tq, kc=kc, topk=topk, c0=_band_c0(kc)),
        grid=(S // tq,),
        in_specs=[pl.BlockSpec((tq, aw), lambda i: (i, 0)),
                  pl.BlockSpec((tq, iw), lambda i: (i, 0)),
                  pl.BlockSpec((IDX_HEADS, tq), lambda i: (0, i)),
                  pl.BlockSpec((S, aw), lambda i: (0, 1), pipeline_mode=once),
                  pl.BlockSpec((aw, S), lambda i: (0, 0), pipeline_mode=once),
                  pl.BlockSpec((S, LANE), lambda i: (0, iw // LANE), pipeline_mode=once),
                  pl.BlockSpec(band.shape, lambda i: (0, 0, 0), pipeline_mode=once)],
        out_specs=pl.BlockSpec((tq, aw), lambda i: (i, 0)),
        out_shape=jax.ShapeDtypeStruct((S, aw), BF16),
        scratch_shapes=[pltpu.VMEM((S, tq), jnp.int16),
                        pltpu.VMEM((S, tq), jnp.int16),
                        pltpu.VMEM((1, tq), jnp.int32),
                        pltpu.VMEM((IDX_HEADS, tq, LANE), BF16),
                        pltpu.VMEM((2, kc, A_HEADS * tq), F32),
                        pltpu.VMEM((HEAD_DIM, A_HEADS * tq), F32),
                        pltpu.VMEM((1, A_HEADS * tq), F32),
                        pltpu.VMEM((1, A_HEADS * tq), F32)],
        compiler_params=_cparams(("arbitrary",)),
        name="dsa_attention",
    )(qk, pidx, wt, qk, vt, pidx, band)


DIL_ROWS = 384


def _dilated_kernel(q_ref, k_ref, vt_ref, b0_ref, b1_ref, b2_ref, o_ref, s_ref, *, tq):
    t0 = pl.program_id(1) * tq
    q = q_ref[...]
    tiles = []
    row0 = 0
    for band_ref, (_, dil) in zip((b0_ref, b1_ref, b2_ref), DILATED_PAIRS):
        span = DIL_STEPS * dil
        start = jnp.maximum(t0 - span, 0)
        off = span - (t0 - start)
        for j in range((span + tq) // DIL_ROWS):
            ks = pl.multiple_of(start + j * DIL_ROWS, LANE)
            bs = pl.multiple_of(off + j * DIL_ROWS, LANE)
            s_ref[row0:row0 + DIL_ROWS, :] = (_dot_nt(k_ref[pl.ds(ks, DIL_ROWS), :], q)
                                              + band_ref[0, pl.ds(bs, DIL_ROWS), :])
            tiles.append((ks, row0))
            row0 += DIL_ROWS
    m = jnp.full((1, tq), NEG, F32)
    l = jnp.zeros((1, tq), F32)
    acc = jnp.zeros((HEAD_DIM, tq), F32)
    ones_rows = jnp.ones((ONES_ROWS, DIL_ROWS), BF16)
    for ks, r0 in tiles:
        s = s_ref[r0:r0 + DIL_ROWS, :]
        m_new = jnp.maximum(m, jnp.max(s, axis=0, keepdims=True))
        alpha = jnp.exp2(m - m_new)
        p = jnp.exp2(s - m_new).astype(BF16)
        pv = _dot(jnp.concatenate([vt_ref[:, pl.ds(ks, DIL_ROWS)], ones_rows], axis=0), p)
        l = alpha * l + pv[HEAD_DIM:HEAD_DIM + 1]
        acc = alpha * acc + pv[:HEAD_DIM]
        m = m_new
    o_ref[...] = (acc / l).T.astype(o_ref.dtype)


def dilated_attention(qk, vt, bands, vt_row0, tq=256):
    S = qk.shape[0]
    vh0 = vt_row0 // HEAD_DIM
    lengths = [DIL_STEPS * dil + tq for _, dil in DILATED_PAIRS]
    assert all(n % DIL_ROWS == 0 and n <= S for n in lengths)
    return pl.pallas_call(
        functools.partial(_dilated_kernel, tq=tq),
        grid=(B_HEADS, S // tq),
        in_specs=[pl.BlockSpec((tq, LANE), lambda h, i: (i, h)),
                  pl.BlockSpec((S, LANE), lambda h, i: (0, B_HEADS + h)),
                  pl.BlockSpec((HEAD_DIM, S), lambda h, i: (vh0 + h, 0))]
                 + [pl.BlockSpec((1,) + b.shape[1:], lambda h, i: (h, 0, 0)) for b in bands],
        out_specs=pl.BlockSpec((tq, LANE), lambda h, i: (i, h)),
        out_shape=jax.ShapeDtypeStruct((S, B_HEADS * HEAD_DIM), BF16),
        scratch_shapes=[pltpu.VMEM((sum(lengths), tq), F32)],
        compiler_params=_cparams(("parallel", "arbitrary")),
        name="dilated_attention",
    )(qk, qk, vt, *bands)


def _diff_kernel(lam_ref, g_ref, q_ref, k_ref, vt_ref, band_ref, o_ref, acc_ref, m_ref, l_ref, s_ref,
                 *, tq, tk, c0, lam_init):
    i = pl.program_id(1)
    t0 = i * tq
    nchunk = (t0 + tq + tk - 1) // tk
    last_chunk = k_ref.shape[0] // tk - 1
    q = q_ref[...]
    lane = lax.broadcasted_iota(jnp.int32, q.shape, 1)
    zero = jnp.zeros_like(q)
    q2 = jnp.concatenate([jnp.where(lane < C_HALF, q, zero), jnp.where(lane >= C_HALF, q, zero)], axis=0)
    ones_rows = jnp.ones((ONES_ROWS, tk), BF16)

    m_ref[...] = jnp.full(m_ref.shape, NEG, F32)
    l_ref[...] = jnp.zeros(l_ref.shape, F32)
    acc_ref[...] = jnp.zeros(acc_ref.shape, F32)

    def logits(c):
        s0 = pl.multiple_of(jnp.minimum(c, last_chunk) * tk, tk)
        off = pl.multiple_of(c0 - jnp.clip(t0 - c * tk, -tk, c0), LANE)
        bias = band_ref[0, pl.ds(off, tk), :]
        return _dot_nt(k_ref[pl.ds(s0, tk), :], q2) + jnp.concatenate([bias, bias], axis=1)

    def consume(c, s):
        s0 = pl.multiple_of(jnp.minimum(c, last_chunk) * tk, tk)
        m_prev = m_ref[...]
        m_new = jnp.maximum(m_prev, jnp.max(s, axis=0, keepdims=True))
        alpha = jnp.exp2(m_prev - m_new)
        p = jnp.exp2(s - m_new).astype(BF16)
        pv = _dot(jnp.concatenate([vt_ref[:, pl.ds(s0, tk)], ones_rows], axis=0), p)
        l_ref[...] = alpha * l_ref[...] + pv[HEAD_DIM:HEAD_DIM + 1]
        acc_ref[...] = alpha * acc_ref[...] + pv[:HEAD_DIM]
        m_ref[...] = m_new

    s_ref[0] = logits(0)

    def chunk_pair(cc, carry):
        c = 2 * cc
        s_ref[1] = logits(c + 1)
        consume(c, s_ref[0])
        s_ref[0] = logits(c + 2)
        consume(c + 1, s_ref[1])
        return carry

    lax.fori_loop(0, (nchunk + 1) // 2, chunk_pair, 0)

    lp = lam_ref[...]
    lam = (jnp.exp(jnp.sum(lp[0:1] * lp[1:2], axis=1, keepdims=True))
           - jnp.exp(jnp.sum(lp[2:3] * lp[3:4], axis=1, keepdims=True)) + lam_init)
    o = acc_ref[...] / l_ref[...]
    o = o[:, :tq] - lam * o[:, tq:]
    ms = jnp.mean(o * o, axis=0, keepdims=True)
    o = o * lax.rsqrt(ms + RMS_EPS) * (g_ref[...] * (1.0 - lam_init))
    o_ref[...] = o.T.astype(o_ref.dtype)


def diff_attention(qk, vt, band, lam_params, subln_g, lam_init, vt_row0, tq=256, tk=512):
    S = qk.shape[0]
    vh0 = vt_row0 // HEAD_DIM
    return pl.pallas_call(
        functools.partial(_diff_kernel, tq=tq, tk=tk, c0=_band_c0(tk), lam_init=lam_init),
        grid=(C_HEADS, S // tq),
        in_specs=[pl.BlockSpec((4, C_HALF), lambda h, i: (0, 0)),
                  pl.BlockSpec((HEAD_DIM, 1), lambda h, i: (0, 0)),
                  pl.BlockSpec((tq, LANE), lambda h, i: (i, h)),
                  pl.BlockSpec((S, LANE), lambda h, i: (0, C_HEADS + h)),
                  pl.BlockSpec((HEAD_DIM, S), lambda h, i: (vh0 + h, 0)),
                  pl.BlockSpec((1,) + band.shape[1:], lambda h, i: (h, 0, 0))],
        out_specs=pl.BlockSpec((tq, LANE), lambda h, i: (i, h)),
        out_shape=jax.ShapeDtypeStruct((S, C_HEADS * HEAD_DIM), BF16),
        scratch_shapes=[pltpu.VMEM((HEAD_DIM, 2 * tq), F32),
                        pltpu.VMEM((1, 2 * tq), F32),
                        pltpu.VMEM((1, 2 * tq), F32),
                        pltpu.VMEM((2, tk, 2 * tq), F32)],
        compiler_params=_cparams(("parallel", "arbitrary")),
        name="diff_attention",
    )(lam_params, subln_g.reshape(HEAD_DIM, 1), qk, qk, vt, band)


def _outproj_kernel(x_ref, a_ref, b_ref, c_ref, wa_ref, wb_ref, wc_ref, g_ref, o_ref, h_ref):
    y = (x_ref[...] + _dot(a_ref[...], wa_ref[...]) + _dot(b_ref[...], wb_ref[...])
         + _dot(c_ref[...], wc_ref[...]))
    o_ref[...] = y
    ms = jnp.mean(y * y, axis=-1, keepdims=True)
    h_ref[...] = (y * lax.rsqrt(ms + RMS_EPS) * g_ref[...]).astype(h_ref.dtype)


def out_projection(x, oa, ob, oc, wa, wb, wc, g, tm=512):
    S, D = x.shape
    once = pl.Buffered(1)
    row = lambda i: (i, 0)
    fixed = lambda i: (0, 0)
    return pl.pallas_call(
        _outproj_kernel,
        grid=(S // tm,),
        in_specs=[pl.BlockSpec((tm, D), row),
                  pl.BlockSpec((tm, oa.shape[1]), row),
                  pl.BlockSpec((tm, ob.shape[1]), row),
                  pl.BlockSpec((tm, oc.shape[1]), row),
                  pl.BlockSpec(wa.shape, fixed, pipeline_mode=once),
                  pl.BlockSpec(wb.shape, fixed, pipeline_mode=once),
                  pl.BlockSpec(wc.shape, fixed, pipeline_mode=once),
                  pl.BlockSpec((1, D), fixed)],
        out_specs=[pl.BlockSpec((tm, D), row), pl.BlockSpec((tm, D), row)],
        out_shape=[jax.ShapeDtypeStruct((S, D), F32), jax.ShapeDtypeStruct((S, D), BF16)],
        compiler_params=_cparams(("parallel",)),
        name="out_projection",
    )(x, oa, ob, oc, wa, wb, wc, g.reshape(1, D))


def _silu(x):
    return x / (1.0 + jnp.exp(-x))


def _ffn_kernel(x_ref, h_ref, wg_ref, wu_ref, wd_ref, g_ref, o_ref, hn_ref):
    f = pl.program_id(1)

    @pl.when(f == 0)
    def _():
        o_ref[...] = x_ref[...]

    h = h_ref[...]
    a = _silu(_dot(h, wg_ref[...].astype(BF16))) * _dot(h, wu_ref[...].astype(BF16))
    o_ref[...] += _dot(a.astype(BF16), wd_ref[...].astype(BF16))

    @pl.when(f == pl.num_programs(1) - 1)
    def _():
        y = o_ref[...]
        ms = jnp.mean(y * y, axis=-1, keepdims=True)
        hn_ref[...] = (y * lax.rsqrt(ms + RMS_EPS) * g_ref[...]).astype(hn_ref.dtype)


def ffn_dense(x, h, wg, wu, wd, g_next, tm=1024, tf=256):
    S, D = x.shape
    F = wg.shape[1]
    once = pl.Buffered(1)
    row = lambda i, f: (i, 0)
    return pl.pallas_call(
        _ffn_kernel,
        grid=(S // tm, F // tf),
        in_specs=[pl.BlockSpec((tm, D), row, pipeline_mode=once),
                  pl.BlockSpec((tm, D), row, pipeline_mode=once),
                  pl.BlockSpec((D, tf), lambda i, f: (0, f)),
                  pl.BlockSpec((D, tf), lambda i, f: (0, f)),
                  pl.BlockSpec((tf, D), lambda i, f: (f, 0)),
                  pl.BlockSpec((1, D), lambda i, f: (0, 0))],
        out_specs=[pl.BlockSpec((tm, D), row), pl.BlockSpec((tm, D), row)],
        out_shape=[jax.ShapeDtypeStruct((S, D), F32), jax.ShapeDtypeStruct((S, D), BF16)],
        compiler_params=_cparams(("parallel", "arbitrary")),
        name="ffn_dense",
    )(x, h, wg, wu, wd, g_next.reshape(1, D))


def _router_kernel(h_ref, w_ref, c_ref, sel_ref):
    logits = _dot(h_ref[...], w_ref[...])
    lane = lax.broadcasted_iota(jnp.int32, logits.shape, 1).astype(F32)
    logits = jnp.where(lane < N_EXPERTS, logits, NEG)
    m1 = jnp.max(logits, axis=1, keepdims=True)
    i1 = jnp.min(jnp.where(logits == m1, lane, float(LANE)), axis=1, keepdims=True)
    rest = jnp.where(lane == i1, NEG, logits)
    m2 = jnp.max(rest, axis=1, keepdims=True)
    i2 = jnp.min(jnp.where(rest == m2, lane, float(LANE)), axis=1, keepdims=True)
    e2 = jnp.exp(m2 - m1)
    g1 = 1.0 / (1.0 + e2)
    g2 = e2 / (1.0 + e2)
    pick1, pick2 = lane == i1, lane == i2
    c_ref[...] = jnp.where(pick1, g1, 0.0) + jnp.where(pick2, g2, 0.0)
    sel_ref[...] = jnp.where(pick1 | pick2, 1.0, 0.0)


def moe_router(h, w_router, tm=1024):
    S, D = h.shape
    wpad = jnp.zeros((D, LANE), BF16).at[:, :N_EXPERTS].set(w_router.astype(BF16))
    spec = pl.BlockSpec((tm, LANE), lambda i: (i, 0))
    sds = jax.ShapeDtypeStruct((S, LANE), F32)
    return pl.pallas_call(
        _router_kernel,
        grid=(S // tm,),
        in_specs=[pl.BlockSpec((tm, D), lambda i: (i, 0)),
                  pl.BlockSpec((D, LANE), lambda i: (0, 0))],
        out_specs=[spec, spec],
        out_shape=[sds, sds],
        compiler_params=_cparams(("parallel",)),
        name="moe_router",
    )(h, wpad)


MOE_CH = 256
MOE_TILE = 1024


def _moe_rank_kernel(sel_ref, rm_ref, rmt_ref, cum_ref, tot_ref, carry_ref):
    c = pl.program_id(0)

    @pl.when(c == 0)
    def _():
        carry_ref[...] = jnp.zeros(carry_ref.shape, F32)

    sel = sel_ref[...]
    ch = sel.shape[0]
    before = (lax.broadcasted_iota(jnp.int32, (ch, ch), 1) < lax.broadcasted_iota(jnp.int32, (ch, ch), 0))
    rank = _dot(jnp.where(before, 1.0, 0.0).astype(BF16), sel.astype(BF16)) + carry_ref[...]
    rm = jnp.where(sel > 0.0, rank, -1.0)
    rm_ref[...] = rm
    rmt_ref[...] = rm.T[:N_EXPERTS]
    cum_ref[0] = carry_ref[...]
    carry_ref[...] += jnp.sum(sel, axis=0, keepdims=True)
    tot_ref[...] = carry_ref[...]


def moe_rank(sel):
    S = sel.shape[0]
    nch = S // MOE_CH
    return pl.pallas_call(
        _moe_rank_kernel,
        grid=(nch,),
        in_specs=[pl.BlockSpec((MOE_CH, LANE), lambda c: (c, 0))],
        out_specs=[pl.BlockSpec((MOE_CH, LANE), lambda c: (c, 0)),
                   pl.BlockSpec((N_EXPERTS, MOE_CH), lambda c: (0, c)),
                   pl.BlockSpec((1, 1, LANE), lambda c: (c, 0, 0)),
                   pl.BlockSpec((1, LANE), lambda c: (0, 0))],
        out_shape=[jax.ShapeDtypeStruct((S, LANE), F32),
                   jax.ShapeDtypeStruct((N_EXPERTS, S), F32),
                   jax.ShapeDtypeStruct((nch, 1, LANE), F32),
                   jax.ShapeDtypeStruct((1, LANE), F32)],
        scratch_shapes=[pltpu.VMEM((1, LANE), F32)],
        compiler_params=_cparams(("arbitrary",)),
        name="moe_rank",
    )(sel)


def _moe_expert_kernel(te_ref, tr0_ref, tclo_ref, tchi_ref, tval_ref, cumx_ref,
                       rk_ref, h_hbm, wg_ref, wu_ref, wd_ref, y_ref,
                       hs_ref, acc_ref, hbuf_ref, sem_ref, *, n_experts):
    j = pl.program_id(0)
    f = pl.program_id(1)
    nf = pl.num_programs(1)
    nrows = tval_ref[j]
    tile = hs_ref.shape[0]
    win = MOE_CH + 8

    def chunk_copy(c, slot):
        return pltpu.make_async_copy(h_hbm.at[pl.ds(c * MOE_CH, MOE_CH)], hbuf_ref.at[slot], sem_ref.at[slot])

    @pl.when((nrows > 0) & (f == 0))
    def _():
        clo, chi = tclo_ref[j], tchi_ref[j]
        tr0 = tr0_ref[j]
        wrow = lax.broadcasted_iota(jnp.int32, (win, MOE_CH), 0)
        acc_ref[...] = jnp.zeros(acc_ref.shape, F32)

        @pl.when(chi > clo)
        def _():
            chunk_copy(clo, 0).start()

        def body(c, carry):
            slot = (c - clo) % 2

            @pl.when(c + 1 < chi)
            def _():
                chunk_copy(c + 1, 1 - slot).start()

            chunk_copy(c, slot).wait()
            first = cumx_ref[c * n_experts + te_ref[j]] - tr0
            w0 = pl.multiple_of(jnp.clip((first // 8) * 8, 0, tile - win), 8)
            rk = rk_ref[0, :, pl.ds(pl.multiple_of(c * MOE_CH, MOE_CH), MOE_CH)]
            onehot = jnp.where(rk == (tr0 + w0 + wrow).astype(F32), 1.0, 0.0).astype(BF16)
            acc_ref[pl.ds(w0, win), :] += _dot(onehot, hbuf_ref[slot])
            return carry

        lax.fori_loop(clo, chi, body, 0)
        hs_ref[...] = acc_ref[...].astype(BF16)
        acc_ref[...] = jnp.zeros(acc_ref.shape, F32)

    def swiglu(rows):
        hs = hs_ref[:rows]
        a = _silu(_dot(hs, wg_ref[0].astype(BF16))) * _dot(hs, wu_ref[0].astype(BF16))
        acc_ref[:rows] += _dot(a.astype(BF16), wd_ref[0].astype(BF16))

    sizes = (tile, tile // 2, tile // 4, tile // 8)
    for rows, below in zip(sizes, sizes[1:] + (0,)):
        @pl.when((nrows > below) & (nrows <= rows))
        def _():
            swiglu(rows)

    @pl.when(f == nf - 1)
    def _():
        y_ref[...] = jnp.where(nrows > 0, acc_ref[...], 0.0).astype(y_ref.dtype)


def _moe_combine_kernel(ca_ref, cb_ref, roff_ref, x_ref, rm_ref, cw_ref, *refs):
    y_refs, o_ref = refs[:-1], refs[-1]
    n_experts = len(y_refs) // 2
    c = pl.program_id(0)
    rm = rm_ref[...]
    cw = cw_ref[...]
    lane = lax.broadcasted_iota(jnp.int32, rm.shape, 1)
    cr = y_refs[0].shape[0]
    col = lax.broadcasted_iota(jnp.int32, (rm.shape[0], cr), 1)
    out = x_ref[...]
    for e in range(n_experts):
        mine = lane == e
        pos = jnp.sum(jnp.where(mine, rm, 0.0), axis=1, keepdims=True)
        gate = jnp.sum(jnp.where(mine, cw, 0.0), axis=1, keepdims=True)
        row = jnp.where(pos >= 0.0, pos + roff_ref[e].astype(F32), -1.0)
        ca, cb = ca_ref[c * n_experts + e], cb_ref[c * n_experts + e]
        for y_ref, base in ((y_refs[2 * e], ca * cr),
                            (y_refs[2 * e + 1], jnp.where(cb != ca, cb * cr, -2 * cr))):
            onehot = jnp.where(row == (base + col).astype(F32), 1.0, 0.0).astype(BF16)
            out = out + gate * _dot(onehot, y_ref[...])
    o_ref[...] = out


def ffn_moe(x, h, combine, sel, wg, wu, wd, tf=512):
    S, D = x.shape
    E, _, F = wg.shape
    tf = min(tf, F)
    T = MOE_TILE
    nch = S // MOE_CH
    J = 2 * S // T + E
    rm, rmt, cum, tot = moe_rank(sel)

    cnt = tot[0, :E].astype(jnp.int32)
    ntile = (cnt + T - 1) // T
    tend = jnp.cumsum(ntile)
    tstart = tend - ntile
    jj = jnp.arange(J, dtype=jnp.int32)
    te = jnp.minimum(jnp.searchsorted(tend, jnp.minimum(jj, tend[-1] - 1), side="right"), E - 1).astype(jnp.int32)
    tr0 = (jnp.minimum(jj, tend[-1] - 1) - tstart[te]) * T
    tval = jnp.where(jj < tend[-1], jnp.clip(cnt[te] - tr0, 0, T), 0).astype(jnp.int32)
    cumx = cum[:, 0, :E].astype(jnp.int32)
    cumi = jnp.concatenate([cumx[1:], cnt[None]], axis=0)
    tclo = jnp.sum(cumi[:, te] <= tr0[None, :], axis=0).astype(jnp.int32)
    tchi = jnp.sum(cumx[:, te] < (tr0 + tval)[None, :], axis=0).astype(jnp.int32)
    roff = (tstart * T).astype(jnp.int32)
    nrow_chunks = J * T // MOE_CH
    ca = ((roff[None, :] + cumx) // MOE_CH).astype(jnp.int32)
    cb = jnp.minimum(ca + 1, nrow_chunks - 1)
    ca, cb = ca.reshape(-1), cb.reshape(-1)

    nf = F // tf

    def wmap(j, f, te_r, tr0_r, clo_r, chi_r, val_r, cumx_r):
        return (te_r[j], 0, jnp.where(val_r[j] > 0, f, nf - 1))

    def wdmap(j, f, te_r, tr0_r, clo_r, chi_r, val_r, cumx_r):
        return (te_r[j], jnp.where(val_r[j] > 0, f, nf - 1), 0)

    y = pl.pallas_call(
        functools.partial(_moe_expert_kernel, n_experts=E),
        grid_spec=pltpu.PrefetchScalarGridSpec(
            num_scalar_prefetch=6,
            grid=(J, nf),
            in_specs=[pl.BlockSpec((1, 1, S), lambda j, f, te_r, *_: (te_r[j], 0, 0)),
                      pl.BlockSpec(memory_space=pl.ANY),
                      pl.BlockSpec((1, D, tf), wmap),
                      pl.BlockSpec((1, D, tf), wmap),
                      pl.BlockSpec((1, tf, D), wdmap)],
            out_specs=pl.BlockSpec((T, D), lambda j, f, *_: (j, 0)),
            scratch_shapes=[pltpu.VMEM((T, D), BF16),
                            pltpu.VMEM((T, D), F32),
                            pltpu.VMEM((2, MOE_CH, D), BF16),
                            pltpu.SemaphoreType.DMA((2,))]),
        out_shape=jax.ShapeDtypeStruct((J * T, D), BF16),
        compiler_params=_cparams(("arbitrary", "arbitrary")),
        name="moe_experts",
    )(te, tr0, tclo, tchi, tval, cumx.reshape(-1), rmt.reshape(E, 1, S), h, wg, wu, wd)

    def chunk_map(c, ca_r, cb_r, ro_r, *, e, second):
        return ((cb_r if second else ca_r)[c * E + e], 0)

    return pl.pallas_call(
        _moe_combine_kernel,
        grid_spec=pltpu.PrefetchScalarGridSpec(
            num_scalar_prefetch=3,
            grid=(nch,),
            in_specs=[pl.BlockSpec((MOE_CH, D), lambda c, *_: (c, 0)),
                      pl.BlockSpec((MOE_CH, LANE), lambda c, *_: (c, 0)),
                      pl.BlockSpec((MOE_CH, LANE), lambda c, *_: (c, 0))]
                     + [pl.BlockSpec((MOE_CH, D), functools.partial(chunk_map, e=e, second=second))
                        for e in range(E) for second in (False, True)],
            out_specs=pl.BlockSpec((MOE_CH, D), lambda c, *_: (c, 0))),
        out_shape=jax.ShapeDtypeStruct((S, D), F32),
        compiler_params=_cparams(("parallel",)),
        name="moe_combine",
    )(ca, cb, roff, x, rm, combine, *([y] * (2 * E)))


def _tile_gain(g, reps, scale=1.0):
    return jnp.tile(g.astype(F32) * scale, reps)


def kernel(x, w_in, w_out, attn_norm_g, ffn_norm_g, q_norm_a, k_norm_a, q_norm_b, k_norm_b,
           q_norm_c, k_norm_c, lambda_q1, lambda_k1, lambda_q2, lambda_k2, diff_subln_g, rel_bias,
           w_dense_gate, w_dense_up, w_dense_down, w_router, w_moe_gate, w_moe_up, w_moe_down):
    B, S, D = x.shape
    depth = w_in.shape[0]
    assert B == 1 and S % (DIL_STEPS * DILATED_PAIRS[-1][1]) == 0
    aw, bw, cw = A_HEADS * HEAD_DIM, B_HEADS * HEAD_DIM, C_HEADS * HEAD_DIM
    iw = IDX_HEADS * IDX_DIM
    sizes = (aw, aw, aw, iw, IDX_DIM, IDX_HEADS, bw, bw, bw, cw, cw, cw)
    offs = np.concatenate([[0], np.cumsum(sizes)]).tolist()

    def cols(w, *segs):
        return jnp.concatenate([w[:, offs[s]:offs[s + 1]] for s in segs], axis=1)

    tq, tk = 256, 512
    band_a = bias_band(rel_bias, A_HEADS, 0, _band_c0(tk) + 2 * tk, tq, _band_c0(tk))
    band_c = bias_band(rel_bias, C_HEADS, A_HEADS + B_HEADS, _band_c0(tk) + 2 * tk, tq, _band_c0(tk))
    bands_b = [bias_band(rel_bias, B_HEADS, A_HEADS, 2 * DIL_STEPS * dil + tq, tq, DIL_STEPS * dil, dil)
               for _, dil in DILATED_PAIRS]

    xs = x.reshape(S, D)
    h = None
    for layer in range(depth):
        wl = w_in[layer]
        w_pa = cols(wl, 0, 1).astype(BF16)
        w_pb = cols(wl, 6, 7).astype(BF16)
        w_pc = cols(wl, 9, 10).astype(BF16)
        w_pi = cols(wl, 3, 4, 4).astype(BF16)
        w_vt = cols(wl, 2, 8, 11).T.astype(BF16)
        w_wt = cols(wl, 5).T.astype(BF16)
        qs_ab, qs_c = HEAD_DIM ** -0.5 * LOG2E, C_HALF ** -0.5 * LOG2E
        g_pa = jnp.concatenate([_tile_gain(q_norm_a[layer], A_HEADS, qs_ab), _tile_gain(k_norm_a[layer], A_HEADS)])
        g_pb = jnp.concatenate([_tile_gain(q_norm_b[layer], B_HEADS, qs_ab), _tile_gain(k_norm_b[layer], B_HEADS)])
        g_pc = jnp.concatenate([_tile_gain(q_norm_c[layer], 2 * C_HEADS, qs_c),
                                _tile_gain(k_norm_c[layer], 2 * C_HEADS)])

        if h is None:
            h = rmsnorm(xs, attn_norm_g[layer])
        p_a = project(h, w_pa, g_pa, HEAD_DIM, BF16, tn=512)
        p_b = project(h, w_pb, g_pb, HEAD_DIM, BF16, tn=512)
        p_c = project(h, w_pc, g_pc, C_HALF, BF16, tn=512)
        p_i = project(h, w_pi, jnp.ones((w_pi.shape[1],), F32), 0, BF16, tn=w_pi.shape[1])
        vt = project_t(h, w_vt, BF16, tn=512)
        wt = project_t(h, w_wt, F32)

        oa = dsa_attention(p_a, vt, p_i, wt, band_a)

        ob = dilated_attention(p_b, vt, bands_b, aw, tq=tq)

        lam_init = 0.8 - 0.6 * math.exp(-0.3 * layer)
        lam_params = jnp.stack([lambda_q1[layer], lambda_k1[layer], lambda_q2[layer], lambda_k2[layer]]).astype(F32)
        oc = diff_attention(p_c, vt, band_c, lam_params, diff_subln_g[layer].astype(F32), lam_init, aw + bw,
                            tq=tq, tk=tk)

        wo = w_out[layer].astype(BF16)
        xs, h2 = out_projection(xs, oa, ob, oc, wo[:aw], wo[aw:aw + bw], wo[aw + bw:], ffn_norm_g[layer])

        f = layer // 2
        if layer % 2 == 0:
            g_next = attn_norm_g[min(layer + 1, depth - 1)]
            xs, h = ffn_dense(xs, h2, w_dense_gate[f], w_dense_up[f], w_dense_down[f], g_next)
        else:
            combine, sel = moe_router(h2, w_router[f])
            xs = ffn_moe(xs, h2, combine, sel, w_moe_gate[f], w_moe_up[f], w_moe_down[f])
            h = None
    return xs.reshape(B, S, D)
```
